```python
import math
import jax
import jax.numpy as jnp
from jax import lax
import numpy as np

D_MODEL = 2048
BATCH = 4
SEQ = 2048
DEPTH = 1
DEC_BATCH = 128
DEC_SEQ = 1
PAST_LEN = 16384
PAGE_SIZE = 128

D_MIX = 2 * D_MODEL
D_A = D_MIX // 2
HA_DK = 128
HA_DV = 128
H_A = D_A // HA_DK
D_B = D_MIX - D_A
B_HEADDIM = 64
H_B = D_B // B_HEADDIM
B_NGROUPS = 8
HEADS_PER_GROUP = H_B // B_NGROUPS
B_DSTATE = 128
CONV_W = 4
CONV_DIM = D_B + 2 * B_NGROUPS * B_DSTATE
D_IN_PROJ = 4 * D_A + D_B + CONV_DIM + H_B
D_FF = 4 * D_MODEL
N_MOD = 6
CHUNK_A = 16
CHUNK_B = 64
EPS = 1e-6

kernel_name = 'hymba_hgrn2_mamba2_sandwich_adaln_step'


def _rms(x):
    xf = x.astype(jnp.float32)
    return (xf * lax.rsqrt(jnp.mean(xf * xf, axis=-1, keepdims=True) + EPS)).astype(x.dtype)


def _pad_time(a, n_pad):
    return jnp.pad(a, [(0, 0), (0, n_pad)] + [(0, 0)] * (a.ndim - 2))


def _to_chunks(a, chunk):
    b, t = a.shape[:2]
    return jnp.moveaxis(a.reshape((b, t // chunk, chunk) + a.shape[2:]), 1, 0)


def _from_chunks(a, t):
    a = jnp.moveaxis(a, 0, 1)
    return a.reshape((a.shape[0], a.shape[1] * a.shape[2]) + a.shape[3:])[:, :t]


def hgrn2_recurrence(q, log_f, k, v, s0):
    t = q.shape[1]
    chunk = min(CHUNK_A, t)
    n_pad = (-t) % chunk
    q, log_f, k, v = [_to_chunks(_pad_time(a.astype(jnp.float32), n_pad), chunk) for a in (q, log_f, k, v)]
    causal = jnp.tril(jnp.ones((chunk, chunk), dtype=bool))

    def step(s, inp):
        qc, lfc, kc, vc = inp
        b = jnp.cumsum(lfc, axis=1)
        qg = qc * jnp.exp(b)
        kg = kc * jnp.exp(-b)
        scores = jnp.where(causal, jnp.einsum('bthk,bshk->bhts', qg, kg), 0.0)
        o = jnp.einsum('bhts,bshv->bthv', scores, vc) + jnp.einsum('bthk,bhkv->bthv', qg, s)
        b_end = b[:, -1]
        k_end = kc * jnp.exp(b_end[:, None] - b)
        s_new = jnp.exp(b_end)[..., None] * s + jnp.einsum('bshk,bshv->bhkv', k_end, vc)
        return s_new, o

    s_fin, o = lax.scan(step, s0.astype(jnp.float32), (q, log_f, k, v))
    return _from_chunks(o, t), s_fin


def ssd_recurrence(x, dt, a, b_in, c_in, h0):
    t = x.shape[1]
    chunk = min(CHUNK_B, t)
    n_pad = (-t) % chunk
    x, dt, b_in, c_in = [_to_chunks(_pad_time(u.astype(jnp.float32), n_pad), chunk) for u in (x, dt, b_in, c_in)]
    causal = jnp.tril(jnp.ones((chunk, chunk), dtype=bool))

    def step(h, inp):
        xc, dtc, bc, cc = inp
        bh = jnp.repeat(bc, HEADS_PER_GROUP, axis=2)
        ch = jnp.repeat(cc, HEADS_PER_GROUP, axis=2)
        cum = jnp.cumsum(dtc * a, axis=1)
        seg = cum[:, :, None, :] - cum[:, None, :, :]
        decay = jnp.exp(jnp.where(causal[None, :, :, None], seg, -jnp.inf))
        scores = jnp.einsum('bthn,bshn->btsh', ch, bh) * decay * dtc[:, None, :, :]
        y = (jnp.einsum('btsh,bshp->bthp', scores, xc)
             + jnp.einsum('bthn,bhpn->bthp', ch, h) * jnp.exp(cum)[..., None])
        w_end = jnp.exp(cum[:, -1:] - cum) * dtc
        h_new = (jnp.exp(cum[:, -1])[:, :, None, None] * h
                 + jnp.einsum('bshn,bshp->bhpn', bh * w_end[..., None], xc))
        return h_new, y

    h_fin, y = lax.scan(step, h0.astype(jnp.float32), (x, dt, b_in, c_in))
    return _from_chunks(y, t), h_fin


def causal_conv(u, buf, w, b):
    t = u.shape[1]
    full = jnp.concatenate([buf.astype(u.dtype), u], axis=1)
    out = b
    for i in range(CONV_W):
        out = out + full[:, i:i + t] * w[i]
    return jax.nn.silu(out), full[:, full.shape[1] - (CONV_W - 1):]


def _layer(x, c, s_hgrn, s_ssm, s_conv, w_ada, b_ada, g_pre_mix, g_post_mix, g_pre_mlp, g_post_mlp,
           w_in, lb, hgrn_norm, conv_w, conv_b, dt_bias, a_log, d_skip, ssd_norm, w_out, w_up, w_down):
    bsz, t, _ = x.shape
    mod = jnp.einsum('bd,de->be', jax.nn.silu(c), w_ada) + b_ada
    sh1, sc1, gt1, sh2, sc2, gt2 = jnp.split(mod[:, None, :], N_MOD, axis=-1)

    h = _rms(x) * g_pre_mix * (1 + sc1) + sh1
    proj = jnp.einsum('btd,de->bte', h, w_in)
    bounds = np.cumsum([D_A, D_A, D_A, D_A, D_B, CONV_DIM]).tolist()
    q, f_logit, i_in, g_out, z, xbc, dt_raw = jnp.split(proj, bounds, axis=-1)

    f = lb + (1.0 - lb) * jax.nn.sigmoid(f_logit.astype(jnp.float32))
    qa = jax.nn.silu(q).reshape(bsz, t, H_A, HA_DK)
    log_f = jnp.log(f).reshape(bsz, t, H_A, HA_DK)
    ka = (1.0 - f).reshape(bsz, t, H_A, HA_DK)
    va = i_in.reshape(bsz, t, H_A, HA_DV)
    o_a, s_hgrn_new = hgrn2_recurrence(qa, log_f, ka, va, s_hgrn)
    o_a = _rms(o_a.astype(x.dtype)).reshape(bsz, t, D_A) * hgrn_norm * jax.nn.silu(g_out)

    xbc, s_conv_new = causal_conv(xbc, s_conv, conv_w, conv_b)
    xs, b_in, c_in = jnp.split(xbc, [D_B, D_B + B_NGROUPS * B_DSTATE], axis=-1)
    xs = xs.reshape(bsz, t, H_B, B_HEADDIM)
    dt = jax.nn.softplus(dt_raw.astype(jnp.float32) + dt_bias)
    a = -jnp.exp(a_log.astype(jnp.float32))
    y, s_ssm_new = ssd_recurrence(xs, dt, a, b_in.reshape(bsz, t, B_NGROUPS, B_DSTATE),
                                  c_in.reshape(bsz, t, B_NGROUPS, B_DSTATE), s_ssm)
    y = (y.astype(x.dtype) + d_skip[:, None] * xs).reshape(bsz, t, D_B)
    yz = (y * jax.nn.silu(z)).reshape(bsz, t, B_NGROUPS, D_B // B_NGROUPS)
    o_b = _rms(yz).reshape(bsz, t, D_B) * ssd_norm

    mix = jnp.einsum('bte,ed->btd', jnp.concatenate([o_a, o_b], axis=-1), w_out)
    x = x + gt1 * (_rms(mix) * g_post_mix)

    h = _rms(x) * g_pre_mlp * (1 + sc2) + sh2
    m = jnp.einsum('btf,fd->btd', jnp.square(jax.nn.relu(jnp.einsum('btd,df->btf', h, w_up))), w_down)
    x = x + gt2 * (_rms(m) * g_post_mlp)
    return x, s_hgrn_new.astype(x.dtype), s_ssm_new.astype(x.dtype), s_conv_new


def setup_inputs(seed: int = 0) -> dict:
    key = jax.random.key(seed)
    ks = jax.random.split(key, 28)
    f32 = jnp.float32

    def nrm(k, shape, scale):
        return jax.random.normal(k, shape, f32) * scale

    dt0 = jnp.exp(jax.random.uniform(ks[18], (DEPTH, H_B), f32, math.log(1e-3), math.log(1e-1)))
    return {
        'x_prompt': nrm(ks[0], (BATCH, SEQ, D_MODEL), 1.0),
        'x_sample': nrm(ks[1], (DEC_BATCH, DEC_SEQ, D_MODEL), 1.0),
        'c_prompt': nrm(ks[2], (BATCH, D_MODEL), 1.0),
        'c_sample': nrm(ks[3], (DEC_BATCH, D_MODEL), 1.0),
        'state_hgrn': nrm(ks[4], (DEPTH, DEC_BATCH, H_A, HA_DK, HA_DV), 0.3),
        'state_ssm': nrm(ks[5], (DEPTH, DEC_BATCH, H_B, B_HEADDIM, B_DSTATE), 0.1),
        'state_conv': nrm(ks[6], (DEPTH, DEC_BATCH, CONV_W - 1, CONV_DIM), 1.0),
        'w_ada': nrm(ks[7], (DEPTH, D_MODEL, N_MOD * D_MODEL), 0.5 * D_MODEL ** -0.5),
        'b_ada': nrm(ks[8], (DEPTH, N_MOD * D_MODEL), 0.02),
        'norm_pre_mix': 1.0 + nrm(ks[9], (DEPTH, D_MODEL), 0.02),
        'norm_post_mix': 1.0 + nrm(ks[10], (DEPTH, D_MODEL), 0.02),
        'norm_pre_mlp': 1.0 + nrm(ks[11], (DEPTH, D_MODEL), 0.02),
        'norm_post_mlp': 1.0 + nrm(ks[12], (DEPTH, D_MODEL), 0.02),
        'w_in': nrm(ks[13], (DEPTH, D_MODEL, D_IN_PROJ), D_MODEL ** -0.5),
        'hgrn_lb_logits': nrm(ks[14], (DEPTH + 1, D_A), 0.1),
        'hgrn_norm': 1.0 + nrm(ks[15], (DEPTH, D_A), 0.02),
        'conv_w': nrm(ks[16], (DEPTH, CONV_W, CONV_DIM), CONV_W ** -0.5),
        'conv_b': nrm(ks[17], (DEPTH, CONV_DIM), 0.02),
        'dt_bias': dt0 + jnp.log(-jnp.expm1(-dt0)),
        'a_log': jnp.log(jax.random.uniform(ks[19], (DEPTH, H_B), f32, 1.0, 16.0)),
        'd_skip': 1.0 + nrm(ks[20], (DEPTH, H_B), 0.1),
        'ssd_norm': 1.0 + nrm(ks[21], (DEPTH, D_B), 0.02),
        'w_out': nrm(ks[22], (DEPTH, D_MIX, D_MODEL), D_MIX ** -0.5),
        'w_up': nrm(ks[23], (DEPTH, D_MODEL, D_FF), D_MODEL ** -0.5),
        'w_down': nrm(ks[24], (DEPTH, D_FF, D_MODEL), D_FF ** -0.5),
    }


def reference(x_prompt, x_sample, c_prompt, c_sample, state_hgrn, state_ssm, state_conv,
              w_ada, b_ada, norm_pre_mix, norm_post_mix, norm_pre_mlp, norm_post_mlp,
              w_in, hgrn_lb_logits, hgrn_norm, conv_w, conv_b, dt_bias, a_log, d_skip, ssd_norm,
              w_out, w_up, w_down):
    lb_all = jnp.cumsum(jax.nn.softmax(hgrn_lb_logits.astype(jnp.float32), axis=0), axis=0)
    dtp = x_prompt.dtype
    zero_hgrn = jnp.zeros((BATCH, H_A, HA_DK, HA_DV), dtp)
    zero_ssm = jnp.zeros((BATCH, H_B, B_HEADDIM, B_DSTATE), dtp)
    zero_conv = jnp.zeros((BATCH, CONV_W - 1, CONV_DIM), dtp)
    xp, xs = x_prompt, x_sample
    hp_l, sp_l, cp_l, hs_l, ss_l, cs_l = [], [], [], [], [], []
    for l in range(DEPTH):
        lp = (w_ada[l], b_ada[l], norm_pre_mix[l], norm_post_mix[l], norm_pre_mlp[l], norm_post_mlp[l],
              w_in[l], lb_all[l], hgrn_norm[l], conv_w[l], conv_b[l], dt_bias[l], a_log[l], d_skip[l],
              ssd_norm[l], w_out[l], w_up[l], w_down[l])
        xp, hp, sp, cp = _layer(xp, c_prompt, zero_hgrn, zero_ssm, zero_conv, *lp)
        xs, hs, ss, cs = _layer(xs, c_sample, state_hgrn[l], state_ssm[l], state_conv[l], *lp)
        hp_l.append(hp); sp_l.append(sp); cp_l.append(cp)
        hs_l.append(hs); ss_l.append(ss); cs_l.append(cs)
    return (xp, xs, jnp.stack(hp_l), jnp.stack(sp_l), jnp.stack(cp_l),
            jnp.stack(hs_l), jnp.stack(ss_l), jnp.stack(cs_l))
```

```python
import functools

import jax
import jax.numpy as jnp
from jax import lax
from jax.experimental import pallas as pl
from jax.experimental.pallas import tpu as pltpu

F32 = jnp.float32
BF16 = jnp.bfloat16

D_MODEL = 2048
D_A = 2048
HA_D = 128
H_A = D_A // HA_D
D_B = 2048
B_P = 64
H_B = D_B // B_P
B_G = 8
HPG = H_B // B_G
GW = HPG * B_P
B_N = 128
CONV_W = 4
CONV_DIM = D_B + 2 * B_G * B_N
D_MAIN = 4 * D_A + D_B + CONV_DIM
D_FF = 4 * D_MODEL
N_MOD = 6
EPS = 1e-6

OFF_Q, OFF_F, OFF_I, OFF_G = 0, D_A, 2 * D_A, 3 * D_A
OFF_Z = 4 * D_A
OFF_X = OFF_Z + D_B
OFF_B = OFF_X + D_B
OFF_C = OFF_B + B_G * B_N

LANE = 128
SUBLANE = 8
VMEM_LIMIT = 56 * 1024 * 1024

CHUNK_A = 128
SUB_A = 16
CHUNK_B = 128


def _sigmoid(x):
    return 1.0 / (1.0 + jnp.exp(-x))


def _silu(x):
    return x * _sigmoid(x)


def _softplus(x):
    return jnp.maximum(x, 0.0) + jnp.log1p(jnp.exp(-jnp.abs(x)))


def _idiv(x, d):
    assert d & (d - 1) == 0
    return jnp.right_shift(x, d.bit_length() - 1)


def _rms(x):
    return x * lax.rsqrt(jnp.mean(x * x, axis=-1, keepdims=True) + EPS)


def _cumsum_rows(x, n, row):
    s = 1
    while s < n:
        x = x + jnp.where(row >= s, pltpu.roll(x, s, 0), 0.0)
        s *= 2
    return x


def _cumsum_lanes(x, n, lane):
    s = 1
    while s < n:
        x = x + jnp.where(lane >= s, pltpu.roll(x, s, 1), 0.0)
        s *= 2
    return x


def _dot(a, b):
    return jnp.dot(a.astype(BF16), b.astype(BF16), preferred_element_type=F32)


def _dot_nt(a, b):
    return lax.dot_general(a.astype(BF16), b.astype(BF16), (((1,), (1,)), ((), ())),
                           preferred_element_type=F32)


def _dot_tn(a, b):
    return lax.dot_general(a.astype(BF16), b.astype(BF16), (((0,), (0,)), ((), ())),
                           preferred_element_type=F32)


def _params(*sem):
    return pltpu.CompilerParams(dimension_semantics=sem, vmem_limit_bytes=VMEM_LIMIT)


def _ada_kernel(c_ref, w_ref, b_ref, o_ref):
    a = _silu(c_ref[...])
    o_ref[...] = _dot(a, w_ref[...]) + b_ref[...]


def _ada(c_all, w_ada, b_ada):
    m = c_all.shape[0]
    n = w_ada.shape[1]
    tn = 1024
    return pl.pallas_call(
        _ada_kernel,
        grid=(n // tn,),
        in_specs=[pl.BlockSpec((m, D_MODEL), lambda j: (0, 0)),
                  pl.BlockSpec((D_MODEL, tn), lambda j: (0, j)),
                  pl.BlockSpec((1, tn), lambda j: (0, j))],
        out_specs=pl.BlockSpec((m, tn), lambda j: (0, j)),
        out_shape=jax.ShapeDtypeStruct((m, n), F32),
        compiler_params=_params("parallel"),
        name="ada_mod",
    )(c_all, w_ada, b_ada)


def _mod_spec(per_row, rows_per_seq, tm, col):
    if per_row:
        return pl.BlockSpec((tm, D_MODEL), lambda i, j: (i, col))
    tiles_per_seq = rows_per_seq // tm
    return pl.BlockSpec((None, 1, D_MODEL), lambda i, j: (i // tiles_per_seq, 0, col))


def _inproj_kernel(x_ref, sh_ref, sc_ref, g_ref, w_ref, wdt_ref, o_ref, odt_ref, h_scr):
    @pl.when(pl.program_id(1) == 0)
    def _():
        h = _rms(x_ref[...]) * g_ref[...] * (1.0 + sc_ref[...]) + sh_ref[...]
        hb = h.astype(BF16)
        h_scr[...] = hb
        odt_ref[...] = jnp.dot(hb, wdt_ref[...], preferred_element_type=F32)

    o_ref[...] = jnp.dot(h_scr[...], w_ref[...], preferred_element_type=F32)


def _inproj(x, mod, per_row, rows_per_seq, g_pre, w_main, w_dt, tm, tn):
    m = x.shape[0]
    return pl.pallas_call(
        _inproj_kernel,
        grid=(m // tm, D_MAIN // tn),
        in_specs=[pl.BlockSpec((tm, D_MODEL), lambda i, j: (i, 0)),
                  _mod_spec(per_row, rows_per_seq, tm, 0),
                  _mod_spec(per_row, rows_per_seq, tm, 1),
                  pl.BlockSpec((1, D_MODEL), lambda i, j: (0, 0)),
                  pl.BlockSpec((D_MODEL, tn), lambda i, j: (0, j)),
                  pl.BlockSpec((D_MODEL, LANE), lambda i, j: (0, 0))],
        out_specs=[pl.BlockSpec((tm, tn), lambda i, j: (i, j)),
                   pl.BlockSpec((tm, LANE), lambda i, j: (i, 0))],
        out_shape=[jax.ShapeDtypeStruct((m, D_MAIN), F32),
                   jax.ShapeDtypeStruct((m, LANE), F32)],
        scratch_shapes=[pltpu.VMEM((tm, D_MODEL), BF16)],
        compiler_params=_params("parallel", "arbitrary"),
        name="in_proj",
    )(x, mod, mod, g_pre, w_main, w_dt)


def _lower_bound(logits):
    e = jnp.exp(logits - jnp.max(logits, axis=0, keepdims=True))
    return e[0:1] / jnp.sum(e, axis=0, keepdims=True)


def _hgrn_prompt_kernel(q_ref, f_ref, i_ref, g_ref, lbl_ref, nw_ref, o_ref, s_ref, *, n_chunks):
    c_len = CHUNK_A
    n_sub = c_len // SUB_A
    lb = _lower_bound(lbl_ref[...])
    nw = nw_ref[...]
    row = lax.broadcasted_iota(jnp.int32, (c_len, HA_D), 0)
    row_sub = _idiv(row, SUB_A)
    a_row = lax.broadcasted_iota(jnp.int32, (SUB_A, c_len), 0)
    a_col = lax.broadcasted_iota(jnp.int32, (SUB_A, c_len), 1)

    def chunk(c, st):
        sl = pl.ds(pl.multiple_of(c * c_len, c_len), c_len)
        f = lb + (1.0 - lb) * _sigmoid(f_ref[sl, :])
        k = 1.0 - f
        qs = _silu(q_ref[sl, :])
        v = i_ref[sl, :]
        b = _cumsum_rows(jnp.log(f), c_len, row)
        blocks = []
        for i in range(n_sub):
            lo = i * SUB_A
            ref_pt = b[lo - 1:lo, :] if i > 0 else jnp.zeros((1, HA_D), F32)
            qg = qs[lo:lo + SUB_A, :] * jnp.exp(b[lo:lo + SUB_A, :] - ref_pt)
            kg = k * jnp.exp(jnp.where(row_sub <= i, ref_pt - b, 0.0))
            a_i = _dot_nt(qg, kg)
            blocks.append(jnp.where(a_col <= a_row + lo, a_i, 0.0))
        scores = jnp.concatenate(blocks, axis=0)
        b_end = b[c_len - 1:c_len, :]
        o = _dot(scores, v) + _dot_nt(qs * jnp.exp(b), st)
        st = jnp.exp(b_end) * st + _dot_tn(v, k * jnp.exp(b_end - b))
        o_ref[sl, :] = (_rms(o) * nw * _silu(g_ref[sl, :])).astype(o_ref.dtype)
        return st

    st = lax.fori_loop(0, n_chunks, chunk, jnp.zeros((HA_D, HA_D), F32))
    s_ref[...] = st.T


def _hgrn_prompt(proj, lb_logits, hgrn_norm, n_seq, t):
    m = proj.shape[0]
    cb = lambda off: off // HA_D
    col = lambda off: pl.BlockSpec((t, HA_D), lambda b, h: (b, cb(off) + h))
    return pl.pallas_call(
        functools.partial(_hgrn_prompt_kernel, n_chunks=t // CHUNK_A),
        grid=(n_seq, H_A),
        in_specs=[col(OFF_Q), col(OFF_F), col(OFF_I), col(OFF_G),
                  pl.BlockSpec((lb_logits.shape[0], HA_D), lambda b, h: (0, h)),
                  pl.BlockSpec((1, HA_D), lambda b, h: (0, h))],
        out_specs=[pl.BlockSpec((t, HA_D), lambda b, h: (b, h)),
                   pl.BlockSpec((None, None, HA_D, HA_D), lambda b, h: (b, h, 0, 0))],
        out_shape=[jax.ShapeDtypeStruct((m, D_A), BF16),
                   jax.ShapeDtypeStruct((n_seq, H_A, HA_D, HA_D), F32)],
        compiler_params=_params("parallel", "parallel"),
        name="hgrn_prompt",
    )(proj, proj, proj, proj, lb_logits, hgrn_norm)


def _conv_chunk(tail, u, w, bias, n):
    full = jnp.concatenate([tail, u], axis=0)
    out = bias + w[CONV_W - 1:CONV_W, :] * u
    for i in range(CONV_W - 1):
        shifted = pltpu.roll(full, CONV_W - 1 - i, 0)[SUBLANE:SUBLANE + n, :]
        out = out + w[i:i + 1, :] * shifted
    return _silu(out)


def _ssd_prompt_kernel(z_ref, x_ref, b_ref, c_ref, dtc_ref, dtr_ref,
                       wx_ref, wb_ref, wc_ref, bx_ref, bb_ref, bc_ref,
                       dtbc_ref, alc_ref, dtbr_ref, alr_ref, dsk_ref, nw_ref,
                       o_ref, h_ref, *, n_chunks):
    n = CHUNK_B
    wx, wb, wc = wx_ref[...], wb_ref[...], wc_ref[...]
    bx, bb, bc = bx_ref[...], bb_ref[...], bc_ref[...]
    a_col = -jnp.exp(alc_ref[...])
    a_row = -jnp.exp(alr_ref[...])
    dtb_col, dtb_row = dtbc_ref[...], dtbr_ref[...]
    nw = nw_ref[...]
    row4 = lax.broadcasted_iota(jnp.int32, (n, HPG), 0)
    lane4 = lax.broadcasted_iota(jnp.int32, (HPG, n), 1)
    tri = (lax.broadcasted_iota(jnp.int32, (n, n), 0)
           >= lax.broadcasted_iota(jnp.int32, (n, n), 1))
    lane_head = _idiv(lax.broadcasted_iota(jnp.int32, (1, GW), 1), B_P)
    row_head = _idiv(lax.broadcasted_iota(jnp.int32, (GW, 1), 0), B_P)
    dsk = dsk_ref[...]
    dskip = jnp.zeros((1, GW), F32)
    for j in range(HPG):
        dskip = jnp.where(lane_head == j, dsk[:, j:j + 1], dskip)

    def chunk(c, carry):
        h, tx, tb, tc = carry
        sl = pl.ds(pl.multiple_of(c * n, n), n)
        xr, br, cr = x_ref[sl, :], b_ref[sl, :], c_ref[sl, :]
        xc = _conv_chunk(tx, xr, wx, bx, n)
        bcv = _conv_chunk(tb, br, wb, bb, n)
        ccv = _conv_chunk(tc, cr, wc, bc, n)
        dt_col = _softplus(dtc_ref[sl, :] + dtb_col)
        cum_col = _cumsum_rows(dt_col * a_col, n, row4)
        dt_row = _softplus(dtr_ref[c] + dtb_row)
        cum_row = _cumsum_lanes(dt_row * a_row, n, lane4)
        g = _dot_nt(ccv, bcv)
        ch = _dot_nt(ccv, h)
        y = jnp.zeros((n, GW), F32)
        e_cum = jnp.zeros((n, GW), F32)
        w_end = jnp.zeros((n, GW), F32)
        h_dec = jnp.zeros((GW, 1), F32)
        for j in range(HPG):
            cc = cum_col[:, j:j + 1]
            seg = cc - cum_row[j:j + 1, :]
            s = g * jnp.exp(jnp.where(tri, seg, -jnp.inf)) * dt_row[j:j + 1, :]
            y = y + _dot(s, jnp.where(lane_head == j, xc, 0.0))
            e_cum = jnp.where(lane_head == j, jnp.exp(cc), e_cum)
            c_end = cum_col[n - 1:n, j:j + 1]
            w_end = jnp.where(lane_head == j, jnp.exp(c_end - cc) * dt_col[:, j:j + 1], w_end)
            h_dec = jnp.where(row_head == j, jnp.exp(c_end), h_dec)
        y = y + ch * e_cum + dskip * xc
        yz = y * _silu(z_ref[sl, :])
        o_ref[sl, :] = (_rms(yz) * nw).astype(o_ref.dtype)
        h = h_dec * h + _dot_tn(xc * w_end, bcv)
        return h, xr[n - SUBLANE:, :], br[n - SUBLANE:, :], cr[n - SUBLANE:, :]

    init = (jnp.zeros((GW, B_N), F32), jnp.zeros((SUBLANE, GW), F32),
            jnp.zeros((SUBLANE, B_N), F32), jnp.zeros((SUBLANE, B_N), F32))
    h, _, _, _ = lax.fori_loop(0, n_chunks, chunk, init)
    h_ref[...] = h.reshape(HPG, B_P, B_N)


def _ssd_prompt(proj, dt_col, dt_row, conv_w, conv_b, dtb_col, al_col, dtb_row, al_row,
                dsk_col, ssd_norm, n_seq, t):
    m = proj.shape[0]
    n_chunks = t // CHUNK_B
    wide = lambda off: pl.BlockSpec((t, GW), lambda b, g: (b, off // GW + g))
    narrow = lambda off: pl.BlockSpec((t, B_N), lambda b, g: (b, off // B_N + g))
    cw = lambda off, w: pl.BlockSpec((CONV_W, w), lambda b, g: (0, off // w + g))
    cbias = lambda off, w: pl.BlockSpec((1, w), lambda b, g: (0, off // w + g))
    pc = pl.BlockSpec((None, 1, HPG), lambda b, g: (g, 0, 0))
    pr = pl.BlockSpec((None, HPG, 1), lambda b, g: (g, 0, 0))
    return pl.pallas_call(
        functools.partial(_ssd_prompt_kernel, n_chunks=n_chunks),
        grid=(n_seq, B_G),
        in_specs=[wide(OFF_Z), wide(OFF_X), narrow(OFF_B), narrow(OFF_C),
                  pl.BlockSpec((None, t, HPG), lambda b, g: (g, b, 0)),
                  pl.BlockSpec((None, None, n_chunks, HPG, CHUNK_B), lambda b, g: (g, b, 0, 0, 0)),
                  cw(0, GW), cw(D_B, B_N), cw(D_B + B_G * B_N, B_N),
                  cbias(0, GW), cbias(D_B, B_N), cbias(D_B + B_G * B_N, B_N),
                  pc, pc, pr, pr, pc,
                  pl.BlockSpec((1, GW), lambda b, g: (0, g))],
        out_specs=[pl.BlockSpec((t, GW), lambda b, g: (b, g)),
                   pl.BlockSpec((None, HPG, B_P, B_N), lambda b, g: (b, g, 0, 0))],
        out_shape=[jax.ShapeDtypeStruct((m, D_B), BF16),
                   jax.ShapeDtypeStruct((n_seq, H_B, B_P, B_N), F32)],
        compiler_params=_params("parallel", "parallel"),
        name="ssd_prompt",
    )(proj, proj, proj, proj, dt_col, dt_row, conv_w, conv_w, conv_w, conv_b, conv_b, conv_b,
      dtb_col, al_col, dtb_row, al_row, dsk_col, ssd_norm)


def _hgrn_step_kernel(qt_ref, ft_ref, v_ref, g_ref, lblt_ref, nw_ref, s_ref, o_ref, so_ref, *, bt):
    lbt = _lower_bound_t(lblt_ref[...])
    nw = nw_ref[...]

    def seq(b, carry):
        ft = lbt + (1.0 - lbt) * _sigmoid(ft_ref[b])
        kt = 1.0 - ft
        qt = _silu(qt_ref[b])
        v = v_ref[b]
        rows = []
        for h in range(H_A):
            s_new = ft[:, h:h + 1] * s_ref[b, h] + kt[:, h:h + 1] * v[h:h + 1, :]
            so_ref[b, h] = s_new
            rows.append(jnp.sum(qt[:, h:h + 1] * s_new, axis=0, keepdims=True))
        o = jnp.concatenate(rows, axis=0)
        o_ref[b] = (_rms(o) * nw * _silu(g_ref[b])).astype(o_ref.dtype)
        return carry

    lax.fori_loop(0, bt, seq, 0)


def _lower_bound_t(logits):
    e = jnp.exp(logits - jnp.max(logits, axis=0, keepdims=True))
    return e[0] / jnp.sum(e, axis=0)


def _hgrn_step(qt, ft, v, g, lbl_t, nw, state, bt):
    nb = qt.shape[0]
    tr = pl.BlockSpec((bt, HA_D, H_A), lambda i: (i, 0, 0))
    rw = pl.BlockSpec((bt, H_A, HA_D), lambda i: (i, 0, 0))
    st = pl.BlockSpec((bt, H_A, HA_D, HA_D), lambda i: (i, 0, 0, 0))
    return pl.pallas_call(
        functools.partial(_hgrn_step_kernel, bt=bt),
        grid=(nb // bt,),
        in_specs=[tr, tr, rw, rw,
                  pl.BlockSpec(lbl_t.shape, lambda i: (0, 0, 0)),
                  pl.BlockSpec((H_A, HA_D), lambda i: (0, 0)),
                  st],
        out_specs=[rw, st],
        out_shape=[jax.ShapeDtypeStruct((nb, H_A, HA_D), BF16),
                   jax.ShapeDtypeStruct(state.shape, F32)],
        compiler_params=_params("parallel"),
        name="hgrn_step",
    )(qt, ft, v, g, lbl_t, nw, state)


def _conv_step(buf_ref, b, u, w, bias):
    out = bias + w[CONV_W - 1] * u
    for i in range(CONV_W - 1):
        out = out + w[i] * buf_ref[b, i]
    return _silu(out)


def _ssd_step_kernel(xt_ref, xr_ref, bcr_ref, z_ref, dt_ref,
                     cxt_ref, cxr_ref, cbc_ref,
                     wxt_ref, wxr_ref, wbc_ref, bxt_ref, bxr_ref, bbc_ref,
                     dtb_ref, al_ref, dskr_ref, nw_ref, h_ref,
                     o_ref, ho_ref, y_scr, xc_scr, *, bt):
    wxt, wxr, wbc = wxt_ref[...], wxr_ref[...], wbc_ref[...]
    bxt, bxr, bbc = bxt_ref[...], bxr_ref[...], bbc_ref[...]
    a = -jnp.exp(al_ref[...])
    dtb = dtb_ref[...]

    def seq(b, carry):
        xt = _conv_step(cxt_ref, b, xt_ref[b], wxt, bxt)
        xc_scr[pl.ds(b, 1), :] = _conv_step(cxr_ref, b, xr_ref[b], wxr, bxr)
        bcv = _conv_step(cbc_ref, b, bcr_ref[b], wbc, bbc)
        dt = _softplus(dt_ref[b] + dtb)
        d_a = jnp.exp(dt * a)
        xdt = xt * dt
        ys = []
        for g in range(B_G):
            b_row = bcv[g:g + 1, :]
            c_row = bcv[B_G + g:B_G + g + 1, :]
            hs = []
            for j in range(HPG):
                hh = g * HPG + j
                h_new = d_a[:, hh:hh + 1] * h_ref[b, hh] + xdt[:, hh:hh + 1] * b_row
                ho_ref[b, hh] = h_new
                hs.append(h_new)
            hg = jnp.concatenate(hs, axis=0)
            c8 = jnp.broadcast_to(c_row, (SUBLANE, B_N))
            ys.append(_dot_nt(c8, hg)[0:1, :])
        y_scr[pl.ds(b, 1), :] = jnp.concatenate(ys, axis=1)
        return carry

    lax.fori_loop(0, bt, seq, 0)
    xc = xc_scr[...]
    yz = (y_scr[...] + dskr_ref[...] * xc) * _silu(z_ref[...])
    nw = nw_ref[...]
    for g in range(B_G):
        sl = slice(g * GW, (g + 1) * GW)
        o_ref[:, sl] = (_rms(yz[:, sl]) * nw[:, sl]).astype(o_ref.dtype)


def _ssd_step(xt, xr, bcr, z, dt, cxt, cxr, cbc, wxt, wxr, wbc, bxt, bxr, bbc,
              dtb, al, dskr, nw, state, bt):
    nb = xt.shape[0]
    full = lambda a: pl.BlockSpec(a.shape, lambda i: (0,) * a.ndim)
    lead = lambda a: pl.BlockSpec((bt,) + a.shape[1:], lambda i: (i,) + (0,) * (a.ndim - 1))
    return pl.pallas_call(
        functools.partial(_ssd_step_kernel, bt=bt),
        grid=(nb // bt,),
        in_specs=[lead(xt), lead(xr), lead(bcr), lead(z), lead(dt),
                  lead(cxt), lead(cxr), lead(cbc),
                  full(wxt), full(wxr), full(wbc), full(bxt), full(bxr), full(bbc),
                  full(dtb), full(al), full(dskr), full(nw), lead(state)],
        out_specs=[pl.BlockSpec((bt, D_B), lambda i: (i, 0)), lead(state)],
        out_shape=[jax.ShapeDtypeStruct((nb, D_B), F32),
                   jax.ShapeDtypeStruct(state.shape, F32)],
        scratch_shapes=[pltpu.VMEM((bt, D_B), F32), pltpu.VMEM((bt, D_B), F32)],
        compiler_params=_params("parallel"),
        name="ssd_step",
    )(xt, xr, bcr, z, dt, cxt, cxr, cbc, wxt, wxr, wbc, bxt, bxr, bbc, dtb, al, dskr, nw, state)


def _outproj_kernel(oa_ref, ob_ref, wa_ref, wb_ref, x_ref, gt_ref, gp_ref, y_ref):
    mix = _dot(oa_ref[...], wa_ref[...]) + _dot(ob_ref[...], wb_ref[...])
    y_ref[...] = x_ref[...] + gt_ref[...] * (_rms(mix) * gp_ref[...])


def _outproj(o_a, o_b, w_out, x, mod, per_row, rows_per_seq, g_post, tm):
    m = x.shape[0]
    row = lambda w: pl.BlockSpec((tm, w), lambda i, j: (i, 0))
    return pl.pallas_call(
        _outproj_kernel,
        grid=(m // tm, 1),
        in_specs=[row(D_A), row(D_B),
                  pl.BlockSpec((D_A, D_MODEL), lambda i, j: (0, 0)),
                  pl.BlockSpec((D_B, D_MODEL), lambda i, j: (1, 0)),
                  row(D_MODEL),
                  _mod_spec(per_row, rows_per_seq, tm, 2),
                  pl.BlockSpec((1, D_MODEL), lambda i, j: (0, 0))],
        out_specs=row(D_MODEL),
        out_shape=jax.ShapeDtypeStruct((m, D_MODEL), F32),
        compiler_params=_params("parallel", "arbitrary"),
        name="out_proj",
    )(o_a, o_b, w_out, w_out, x, mod, g_post)


def _mlp_kernel(x_ref, sh_ref, sc_ref, gt_ref, gpre_ref, gpost_ref, wu_ref, wd_ref, y_ref,
                h_scr, acc_scr):
    f = pl.program_id(1)

    @pl.when(f == 0)
    def _():
        h = _rms(x_ref[...]) * gpre_ref[...] * (1.0 + sc_ref[...]) + sh_ref[...]
        h_scr[...] = h.astype(BF16)
        acc_scr[...] = jnp.zeros_like(acc_scr)

    u = jnp.maximum(jnp.dot(h_scr[...], wu_ref[...], preferred_element_type=F32), 0.0)
    acc_scr[...] += jnp.dot((u * u).astype(BF16), wd_ref[...], preferred_element_type=F32)

    @pl.when(f == pl.num_programs(1) - 1)
    def _():
        y_ref[...] = x_ref[...] + gt_ref[...] * (_rms(acc_scr[...]) * gpost_ref[...])


def _mlp(x, mod, per_row, rows_per_seq, g_pre, g_post, w_up, w_down, tm, tf):
    m = x.shape[0]
    row = pl.BlockSpec((tm, D_MODEL), lambda i, j: (i, 0))
    vec = pl.BlockSpec((1, D_MODEL), lambda i, j: (0, 0))
    return pl.pallas_call(
        _mlp_kernel,
        grid=(m // tm, D_FF // tf),
        in_specs=[row,
                  _mod_spec(per_row, rows_per_seq, tm, 3),
                  _mod_spec(per_row, rows_per_seq, tm, 4),
                  _mod_spec(per_row, rows_per_seq, tm, 5),
                  vec, vec,
                  pl.BlockSpec((D_MODEL, tf), lambda i, j: (0, j)),
                  pl.BlockSpec((tf, D_MODEL), lambda i, j: (j, 0))],
        out_specs=row,
        out_shape=jax.ShapeDtypeStruct((m, D_MODEL), F32),
        scratch_shapes=[pltpu.VMEM((tm, D_MODEL), BF16), pltpu.VMEM((tm, D_MODEL), F32)],
        compiler_params=_params("parallel", "arbitrary"),
        name="mlp",
    )(x, mod, mod, mod, g_pre, g_post, w_up, w_down)


def _pick_tile(m, target):
    t = min(m, target)
    while m % t:
        t //= 2
    return t


def kernel(x_prompt, x_sample, c_prompt, c_sample, state_hgrn, state_ssm, state_conv, w_ada, b_ada, norm_pre_mix, norm_post_mix, norm_pre_mlp, norm_post_mlp, w_in, hgrn_lb_logits, hgrn_norm, conv_w, conv_b, dt_bias, a_log, d_skip, ssd_norm, w_out, w_up, w_down):
    n_seq, t, _ = x_prompt.shape
    n_dec = x_sample.shape[0]
    assert x_sample.shape[1] == 1 and t % CHUNK_A == 0 and t % CHUNK_B == 0
    assert w_ada.shape[0] == 1, "one layer"
    l = 0

    w_main = w_in[l][:, :D_MAIN].astype(BF16)
    w_dt = jnp.pad(w_in[l][:, D_MAIN:], ((0, 0), (0, LANE - H_B))).astype(BF16)
    w_out_b = w_out[l].astype(BF16)
    w_up_b = w_up[l].astype(BF16)
    w_down_b = w_down[l].astype(BF16)
    g_pre_mix, g_post_mix = norm_pre_mix[l][None], norm_post_mix[l][None]
    g_pre_mlp, g_post_mlp = norm_pre_mlp[l][None], norm_post_mlp[l][None]
    cw, cb = conv_w[l], conv_b[l][None]
    hn = hgrn_norm[l][None]
    sn = ssd_norm[l][None]

    n_c = n_seq + n_dec
    pad = (-n_c) % SUBLANE
    c_all = jnp.concatenate([c_prompt, c_sample, jnp.zeros((pad, D_MODEL), F32)], axis=0)
    mod = _ada(c_all, w_ada[l], b_ada[l][None])
    mod_p = mod[:n_seq].reshape(n_seq, 1, N_MOD * D_MODEL)
    mod_s = mod[n_seq:n_c]

    xp = x_prompt.reshape(n_seq * t, D_MODEL)
    tm_p = _pick_tile(t, 1024)
    proj_p, dt_p = _inproj(xp, mod_p, False, t, g_pre_mix, w_main, w_dt, tm_p, 512)
    o_a, s_hgrn_p = _hgrn_prompt(proj_p, hgrn_lb_logits, hn, n_seq, t)

    dt_g = dt_p[:, :H_B].reshape(n_seq, t, B_G, HPG)
    dt_col = dt_g.transpose(2, 0, 1, 3).reshape(B_G, n_seq * t, HPG)
    dt_row = dt_g.reshape(n_seq, t // CHUNK_B, CHUNK_B, B_G, HPG).transpose(3, 0, 1, 4, 2)
    pcol = lambda p: p.reshape(B_G, 1, HPG)
    prow = lambda p: p.reshape(B_G, HPG, 1)
    o_b, s_ssm_p = _ssd_prompt(proj_p, dt_col, dt_row, cw, cb,
                               pcol(dt_bias[l]), pcol(a_log[l]), prow(dt_bias[l]), prow(a_log[l]),
                               pcol(d_skip[l]), sn, n_seq, t)
    x1 = _outproj(o_a, o_b, w_out_b, xp, mod_p, False, t, g_post_mix, _pick_tile(t, 256))
    y_p = _mlp(x1, mod_p, False, t, g_pre_mlp, g_post_mlp, w_up_b, w_down_b, _pick_tile(t, 512), 512)
    conv_p = proj_p.reshape(n_seq, t, D_MAIN)[:, t - (CONV_W - 1):, OFF_X:]

    xs = x_sample.reshape(n_dec, D_MODEL)
    proj_s, dt_s = _inproj(xs, mod_s, True, 1, g_pre_mix, w_main, w_dt, n_dec, 512)
    bt = _pick_tile(n_dec, SUBLANE)
    to_t = lambda a, h, d: a.reshape(a.shape[:-1] + (h, d)).swapaxes(-1, -2)
    qt = to_t(proj_s[:, OFF_Q:OFF_Q + D_A], H_A, HA_D)
    ft = to_t(proj_s[:, OFF_F:OFF_F + D_A], H_A, HA_D)
    v_s = proj_s[:, OFF_I:OFF_I + D_A].reshape(n_dec, H_A, HA_D)
    g_s = proj_s[:, OFF_G:OFF_G + D_A].reshape(n_dec, H_A, HA_D)
    lbl_t = to_t(hgrn_lb_logits, H_A, HA_D)
    o_a_s, s_hgrn_s = _hgrn_step(qt, ft, v_s, g_s, lbl_t, hn.reshape(H_A, HA_D), state_hgrn[l], bt)

    xbc_s = proj_s[:, OFF_X:]
    cst = state_conv[l]
    xt = to_t(xbc_s[:, :D_B], H_B, B_P)
    xr = xbc_s[:, None, :D_B]
    bcr = xbc_s[:, D_B:].reshape(n_dec, 2 * B_G, B_N)
    cxt = to_t(cst[:, :, :D_B], H_B, B_P)
    cxr = cst[:, :, None, :D_B]
    cbc = cst[:, :, D_B:].reshape(n_dec, CONV_W - 1, 2 * B_G, B_N)
    wxt = to_t(cw[:, :D_B], H_B, B_P)
    wxr = cw[:, None, :D_B]
    wbc = cw[:, D_B:].reshape(CONV_W, 2 * B_G, B_N)
    bxt = to_t(cb[:, :D_B], H_B, B_P)[0]
    bxr = cb[:, :D_B]
    bbc = cb[0, D_B:].reshape(2 * B_G, B_N)
    o_b_s, s_ssm_s = _ssd_step(
        xt, xr, bcr, proj_s[:, OFF_Z:OFF_Z + D_B], dt_s[:, None, :H_B],
        cxt, cxr, cbc, wxt, wxr, wbc, bxt, bxr, bbc,
        dt_bias[l][None], a_log[l][None], jnp.repeat(d_skip[l], B_P)[None], sn,
        state_ssm[l], bt)
    x1_s = _outproj(o_a_s.reshape(n_dec, D_A), o_b_s, w_out_b, xs, mod_s, True, 1, g_post_mix, n_dec)
    y_s = _mlp(x1_s, mod_s, True, 1, g_pre_mlp, g_post_mlp, w_up_b, w_down_b, n_dec, 512)
    conv_s = jnp.concatenate([cst[:, 1:], xbc_s[:, None, :]], axis=1)

    return (y_p.reshape(n_seq, t, D_MODEL), y_s.reshape(n_dec, 1, D_MODEL),
            s_hgrn_p[None], s_ssm_p[None], conv_p[None],
            s_hgrn_s[None], s_ssm_s[None], conv_s[None])
```

```python
import functools

import jax
import jax.numpy as jnp
from jax import lax
from jax.experimental import pallas as pl
from jax.experimental.pallas import tpu as pltpu

F32 = jnp.float32
BF16 = jnp.bfloat16

D_MODEL = 2048
D_A = 2048
HA_D = 128
H_A = D_A // HA_D
D_B = 2048
B_P = 64
H_B = D_B // B_P
B_G = 8
HPG = H_B // B_G
GW = HPG * B_P
B_N = 128
CONV_W = 4
CONV_DIM = D_B + 2 * B_G * B_N
D_MAIN = 4 * D_A + D_B + CONV_DIM
D_FF = 4 * D_MODEL
N_MOD = 6
EPS = 1e-6

OFF_Q, OFF_F, OFF_I, OFF_G = 0, D_A, 2 * D_A, 3 * D_A
OFF_Z = 4 * D_A
OFF_X = OFF_Z + D_B
OFF_B = OFF_X + D_B
OFF_C = OFF_B + B_G * B_N

LANE = 128
SUBLANE = 8
VMEM_LIMIT = 56 * 1024 * 1024

CHUNK_A = 128
SUB_A = 16
CHUNK_B = 128


def _sigmoid(x):
    return 0.5 * jnp.tanh(0.5 * x) + 0.5


def _silu(x):
    hx = 0.5 * x
    return hx * jnp.tanh(hx) + hx


def _softplus(x):
    return jnp.maximum(x, 0.0) + jnp.log1p(jnp.exp(-jnp.abs(x)))


def _idiv(x, d):
    assert d & (d - 1) == 0
    return jnp.right_shift(x, d.bit_length() - 1)


def _rms(x):
    return x * lax.rsqrt(jnp.mean(x * x, axis=-1, keepdims=True) + EPS)


def _cumsum_rows(x, n, row):
    s = 1
    while s < n:
        x = x + jnp.where(row >= s, pltpu.roll(x, s, 0), 0.0)
        s *= 2
    return x


def _cumsum_lanes(x, n, lane):
    s = 1
    while s < n:
        x = x + jnp.where(lane >= s, pltpu.roll(x, s, 1), 0.0)
        s *= 2
    return x


def _dot(a, b):
    return jnp.dot(a.astype(BF16), b.astype(BF16), preferred_element_type=F32)


def _dot_nt(a, b):
    return lax.dot_general(a.astype(BF16), b.astype(BF16), (((1,), (1,)), ((), ())),
                           preferred_element_type=F32)


def _dot_tn(a, b):
    return lax.dot_general(a.astype(BF16), b.astype(BF16), (((0,), (0,)), ((), ())),
                           preferred_element_type=F32)


def _params(*sem):
    return pltpu.CompilerParams(dimension_semantics=sem, vmem_limit_bytes=VMEM_LIMIT)


def _ada_kernel(c_ref, w_ref, b_ref, o_ref):
    a = _silu(c_ref[...])
    o_ref[...] = _dot(a, w_ref[...]) + b_ref[...]


def _ada(c_all, w_ada, b_ada):
    m = c_all.shape[0]
    n = w_ada.shape[1]
    tn = 1024
    return pl.pallas_call(
        _ada_kernel,
        grid=(n // tn,),
        in_specs=[pl.BlockSpec((m, D_MODEL), lambda j: (0, 0)),
                  pl.BlockSpec((D_MODEL, tn), lambda j: (0, j)),
                  pl.BlockSpec((1, tn), lambda j: (0, j))],
        out_specs=pl.BlockSpec((m, tn), lambda j: (0, j)),
        out_shape=jax.ShapeDtypeStruct((m, n), F32),
        compiler_params=_params("parallel"),
        name="ada_mod",
    )(c_all, w_ada, b_ada)


def _mod_spec(per_row, rows_per_seq, tm, col):
    if per_row:
        return pl.BlockSpec((tm, D_MODEL), lambda i, j: (i, col))
    tiles_per_seq = rows_per_seq // tm
    return pl.BlockSpec((None, 1, D_MODEL), lambda i, j: (i // tiles_per_seq, 0, col))


def _inproj_kernel(x_ref, sh_ref, sc_ref, g_ref, w_ref, wdt_ref, o_ref, odt_ref, h_scr):
    @pl.when(pl.program_id(1) == 0)
    def _():
        h = _rms(x_ref[...]) * g_ref[...] * (1.0 + sc_ref[...]) + sh_ref[...]
        hb = h.astype(BF16)
        h_scr[...] = hb
        odt_ref[...] = jnp.dot(hb, wdt_ref[...], preferred_element_type=F32)

    o_ref[...] = jnp.dot(h_scr[...], w_ref[...], preferred_element_type=F32)


def _inproj(x, mod, per_row, rows_per_seq, g_pre, w_main, w_dt, tm, tn):
    m = x.shape[0]
    return pl.pallas_call(
        _inproj_kernel,
        grid=(m // tm, D_MAIN // tn),
        in_specs=[pl.BlockSpec((tm, D_MODEL), lambda i, j: (i, 0)),
                  _mod_spec(per_row, rows_per_seq, tm, 0),
                  _mod_spec(per_row, rows_per_seq, tm, 1),
                  pl.BlockSpec((1, D_MODEL), lambda i, j: (0, 0)),
                  pl.BlockSpec((D_MODEL, tn), lambda i, j: (0, j)),
                  pl.BlockSpec((D_MODEL, LANE), lambda i, j: (0, 0))],
        out_specs=[pl.BlockSpec((tm, tn), lambda i, j: (i, j)),
                   pl.BlockSpec((tm, LANE), lambda i, j: (i, 0))],
        out_shape=[jax.ShapeDtypeStruct((m, D_MAIN), F32),
                   jax.ShapeDtypeStruct((m, LANE), F32)],
        scratch_shapes=[pltpu.VMEM((tm, D_MODEL), BF16)],
        compiler_params=_params("parallel", "arbitrary"),
        name="in_proj",
    )(x, mod, mod, g_pre, w_main, w_dt)


def _lower_bound(logits):
    e = jnp.exp(logits - jnp.max(logits, axis=0, keepdims=True))
    return e[0:1] / jnp.sum(e, axis=0, keepdims=True)


def _hgrn_scores(q, fl, lb, row, a_row, a_col):
    c_len = CHUNK_A
    n_sub = c_len // SUB_A
    half = 0.5 * (1.0 - lb)
    f = (lb + half) + half * jnp.tanh(0.5 * fl)
    k = 1.0 - f
    qs = _silu(q)
    b = _cumsum_rows(jnp.log(f), c_len, row)
    k_raw = [k[j * SUB_A:(j + 1) * SUB_A, :].astype(BF16) for j in range(n_sub)]
    k_acc = []
    blocks, q_dec = [], []
    for i in range(n_sub):
        lo = i * SUB_A
        start = b[lo - 1:lo, :] if i > 0 else jnp.zeros((1, HA_D), F32)
        end = b[lo + SUB_A - 1:lo + SUB_A, :]
        b_i = b[lo:lo + SUB_A, :]
        k_i = k[lo:lo + SUB_A, :]
        qg = qs[lo:lo + SUB_A, :] * jnp.exp(b_i - start)
        k_diag = k_i * jnp.exp(start - b_i)
        rhs = jnp.concatenate([p.astype(BF16) for p in k_acc] + [k_diag.astype(BF16)] + k_raw[i + 1:],
                              axis=0)
        a_i = lax.dot_general(qg.astype(BF16), rhs, (((1,), (1,)), ((), ())),
                              preferred_element_type=F32)
        blocks.append(jnp.where(a_col <= a_row + lo, a_i, 0.0).astype(BF16))
        q_dec.append((qg * jnp.exp(start)).astype(BF16))
        sub_decay = jnp.exp(end - start)
        k_acc = [p * sub_decay for p in k_acc] + [k_i * jnp.exp(end - b_i)]
    scores = jnp.concatenate(blocks, axis=0)
    k_end = jnp.concatenate([p.astype(BF16) for p in k_acc], axis=0)
    chunk_decay = jnp.exp(b[c_len - 1:c_len, :])
    return scores, jnp.concatenate(q_dec, axis=0), k_end, chunk_decay


def _hgrn_apply(pend, v, g, nw, st):
    scores, q_dec, k_end, chunk_decay = pend
    vb = v.astype(BF16)
    o = (jnp.dot(scores, vb, preferred_element_type=F32)
         + lax.dot_general(q_dec, st.astype(BF16), (((1,), (1,)), ((), ())),
                           preferred_element_type=F32))
    st = chunk_decay * st + lax.dot_general(vb, k_end, (((0,), (0,)), ((), ())),
                                            preferred_element_type=F32)
    return (_rms(o) * nw * _silu(g)).astype(BF16), st


def _hgrn_prompt_kernel(q_ref, f_ref, i_ref, g_ref, lbl_ref, nw_ref, o_ref, s_ref, *, n_chunks, n_heads):
    c_len = CHUNK_A
    lb = _lower_bound(lbl_ref[...])
    nw = nw_ref[...]
    row = lax.broadcasted_iota(jnp.int32, (c_len, HA_D), 0)
    a_row = lax.broadcasted_iota(jnp.int32, (SUB_A, c_len), 0)
    a_col = lax.broadcasted_iota(jnp.int32, (SUB_A, c_len), 1)
    heads = [slice(h * HA_D, (h + 1) * HA_D) for h in range(n_heads)]

    def rows(c):
        return pl.ds(pl.multiple_of(c * c_len, c_len), c_len)

    def scores_of(c):
        sl = rows(c)
        return tuple(_hgrn_scores(q_ref[sl, hs], f_ref[sl, hs], lb[:, hs], row, a_row, a_col)
                     for hs in heads)

    def apply_to(c, pends, sts):
        sl = rows(c)
        new = []
        for hs, pend, st in zip(heads, pends, sts):
            out, st = _hgrn_apply(pend, i_ref[sl, hs], g_ref[sl, hs], nw[:, hs], st)
            o_ref[sl, hs] = out
            new.append(st)
        return tuple(new)

    def step(c, carry):
        pends, sts = carry
        sts = apply_to(c - 1, pends, sts)
        return scores_of(c), sts

    zeros = tuple(jnp.zeros((HA_D, HA_D), F32) for _ in range(n_heads))
    pends, sts = lax.fori_loop(1, n_chunks, step, (scores_of(0), zeros))
    sts = apply_to(n_chunks - 1, pends, sts)
    for h in range(n_heads):
        s_ref[h] = sts[h].T


HGRN_HEADS_PER_STEP = 4


def _hgrn_prompt(proj, lb_logits, hgrn_norm, n_seq, t):
    m = proj.shape[0]
    nh = HGRN_HEADS_PER_STEP
    w = nh * HA_D
    col = lambda off: pl.BlockSpec((t, w), lambda b, h: (b, off // w + h))
    return pl.pallas_call(
        functools.partial(_hgrn_prompt_kernel, n_chunks=t // CHUNK_A, n_heads=nh),
        grid=(n_seq, H_A // nh),
        in_specs=[col(OFF_Q), col(OFF_F), col(OFF_I), col(OFF_G),
                  pl.BlockSpec((lb_logits.shape[0], w), lambda b, h: (0, h)),
                  pl.BlockSpec((1, w), lambda b, h: (0, h))],
        out_specs=[pl.BlockSpec((t, w), lambda b, h: (b, h)),
                   pl.BlockSpec((None, nh, HA_D, HA_D), lambda b, h: (b, h, 0, 0))],
        out_shape=[jax.ShapeDtypeStruct((m, D_A), BF16),
                   jax.ShapeDtypeStruct((n_seq, H_A, HA_D, HA_D), F32)],
        compiler_params=_params("parallel", "parallel"),
        name="hgrn_prompt",
    )(proj, proj, proj, proj, lb_logits, hgrn_norm)


def _conv_chunk(tail, u, w, bias, n):
    full = jnp.concatenate([tail, u], axis=0)
    out = bias + w[CONV_W - 1:CONV_W, :] * u
    for i in range(CONV_W - 1):
        shifted = pltpu.roll(full, CONV_W - 1 - i, 0)[SUBLANE:SUBLANE + n, :]
        out = out + w[i:i + 1, :] * shifted
    return _silu(out)


def _ssd_prompt_kernel(z_ref, x_ref, b_ref, c_ref, dtr_ref,
                       wx_ref, wb_ref, wc_ref, bx_ref, bb_ref, bc_ref,
                       dtbr_ref, alr_ref, dsk_ref, nw_ref,
                       o_ref, h_ref, row_scr, col_scr, *, n_chunks):
    n = CHUNK_B
    wx, wb, wc = wx_ref[...], wb_ref[...], wc_ref[...]
    bx, bb, bc = bx_ref[...], bb_ref[...], bc_ref[...]
    nw = nw_ref[...]
    tri = (lax.broadcasted_iota(jnp.int32, (n, n), 0)
           >= lax.broadcasted_iota(jnp.int32, (n, n), 1))
    lane_head = _idiv(lax.broadcasted_iota(jnp.int32, (1, GW), 1), B_P)
    row_head = _idiv(lax.broadcasted_iota(jnp.int32, (GW, 1), 0), B_P)
    dsk = dsk_ref[...]
    dskip = jnp.zeros((1, GW), F32)
    for j in range(HPG):
        dskip = jnp.where(lane_head == j, dsk[:, j:j + 1], dskip)

    nr = n_chunks * SUBLANE
    dt_all = _softplus(dtr_ref[...] + dtbr_ref[...])
    cum_all = _cumsum_lanes(dt_all * -jnp.exp(alr_ref[...]), n,
                            lax.broadcasted_iota(jnp.int32, (nr, n), 1))
    is_dt = (lax.broadcasted_iota(jnp.int32, (nr, n), 0) & (SUBLANE - 1)) < HPG
    both = jnp.where(is_dt, dt_all, cum_all)
    for c in range(n_chunks):
        m8 = both[c * SUBLANE:(c + 1) * SUBLANE, :]
        row_scr[c] = m8
        col_scr[c] = jnp.concatenate([m8, jnp.zeros((n - SUBLANE, n), F32)], axis=0).T

    def rows(c):
        return pl.ds(pl.multiple_of(c * n, n), n)

    def scan_free(c):
        sl = rows(c)
        if isinstance(c, int) and c == 0:
            tail = lambda ref: jnp.zeros((SUBLANE, ref.shape[1]), F32)
        else:
            prev = pl.ds(pl.multiple_of(c * n - SUBLANE, SUBLANE), SUBLANE)
            tail = lambda ref: ref[prev, :]
        xc = _conv_chunk(tail(x_ref), x_ref[sl, :], wx, bx, n)
        bcv = _conv_chunk(tail(b_ref), b_ref[sl, :], wb, bb, n)
        ccv = _conv_chunk(tail(c_ref), c_ref[sl, :], wc, bc, n)
        r8 = row_scr[c]
        cols = col_scr[c]
        g = _dot_nt(ccv, bcv)
        y = dskip * xc
        e_cum = jnp.zeros((n, GW), F32)
        w_end = jnp.zeros((n, GW), F32)
        for j in range(HPG):
            cc = cols[:, HPG + j:HPG + j + 1]
            seg = cc - r8[HPG + j:HPG + j + 1, :]
            s = g * jnp.exp(jnp.where(tri, seg, -jnp.inf)) * r8[j:j + 1, :]
            y = y + _dot(s, jnp.where(lane_head == j, xc, 0.0))
            e_cum = jnp.where(lane_head == j, jnp.exp(cc), e_cum)
            c_end = r8[HPG + j:HPG + j + 1, n - 1:n]
            w_end = jnp.where(lane_head == j, jnp.exp(c_end - cc) * cols[:, j:j + 1], w_end)
        return y, e_cum, ccv.astype(BF16), (xc * w_end).astype(BF16), bcv.astype(BF16), r8

    def apply_state(c, pend, h):
        y, e_cum, ccb, xwb, bcb, r8 = pend
        sl = rows(c)
        ch = lax.dot_general(ccb, h.astype(BF16), (((1,), (1,)), ((), ())),
                             preferred_element_type=F32)
        yz = (y + ch * e_cum) * _silu(z_ref[sl, :])
        o_ref[sl, :] = (_rms(yz) * nw).astype(o_ref.dtype)
        h_dec = jnp.zeros((GW, 1), F32)
        for j in range(HPG):
            h_dec = jnp.where(row_head == j, jnp.exp(r8[HPG + j:HPG + j + 1, n - 1:n]), h_dec)
        return h_dec * h + lax.dot_general(xwb, bcb, (((0,), (0,)), ((), ())),
                                           preferred_element_type=F32)

    def step(c, carry):
        pend, h = carry
        h = apply_state(c - 1, pend, h)
        return scan_free(c), h

    pend, h = lax.fori_loop(1, n_chunks, step, (scan_free(0), jnp.zeros((GW, B_N), F32)))
    h = apply_state(n_chunks - 1, pend, h)
    h_ref[...] = h.reshape(HPG, B_P, B_N)


def _ssd_prompt(proj, dt_rows, conv_w, conv_b, dtb_rows, al_rows, dsk_col, ssd_norm, n_seq, t):
    m = proj.shape[0]
    n_chunks = t // CHUNK_B
    nr = n_chunks * SUBLANE
    wide = lambda off: pl.BlockSpec((t, GW), lambda b, g: (b, off // GW + g))
    narrow = lambda off: pl.BlockSpec((t, B_N), lambda b, g: (b, off // B_N + g))
    cw = lambda off, w: pl.BlockSpec((CONV_W, w), lambda b, g: (0, off // w + g))
    cbias = lambda off, w: pl.BlockSpec((1, w), lambda b, g: (0, off // w + g))
    pr = pl.BlockSpec((None, nr, 1), lambda b, g: (g, 0, 0))
    return pl.pallas_call(
        functools.partial(_ssd_prompt_kernel, n_chunks=n_chunks),
        grid=(n_seq, B_G),
        in_specs=[wide(OFF_Z), wide(OFF_X), narrow(OFF_B), narrow(OFF_C),
                  pl.BlockSpec((None, None, nr, CHUNK_B), lambda b, g: (g, b, 0, 0)),
                  cw(0, GW), cw(D_B, B_N), cw(D_B + B_G * B_N, B_N),
                  cbias(0, GW), cbias(D_B, B_N), cbias(D_B + B_G * B_N, B_N),
                  pr, pr,
                  pl.BlockSpec((None, 1, HPG), lambda b, g: (g, 0, 0)),
                  pl.BlockSpec((1, GW), lambda b, g: (0, g))],
        out_specs=[pl.BlockSpec((t, GW), lambda b, g: (b, g)),
                   pl.BlockSpec((None, HPG, B_P, B_N), lambda b, g: (b, g, 0, 0))],
        out_shape=[jax.ShapeDtypeStruct((m, D_B), BF16),
                   jax.ShapeDtypeStruct((n_seq, H_B, B_P, B_N), F32)],
        scratch_shapes=[pltpu.VMEM((n_chunks, SUBLANE, CHUNK_B), F32),
                        pltpu.VMEM((n_chunks, CHUNK_B, CHUNK_B), F32)],
        compiler_params=_params("parallel", "parallel"),
        name="ssd_prompt",
    )(proj, proj, proj, proj, dt_rows, conv_w, conv_w, conv_w, conv_b, conv_b, conv_b,
      dtb_rows, al_rows, dsk_col, ssd_norm)


def _hgrn_step_kernel(qt_ref, ft_ref, v_ref, g_ref, lblt_ref, nw_ref, s_ref, o_ref, so_ref, *, bt):
    lbt = _lower_bound_t(lblt_ref[...])
    nw = nw_ref[...]

    def seq(b, carry):
        ft = lbt + (1.0 - lbt) * _sigmoid(ft_ref[b])
        kt = 1.0 - ft
        qt = _silu(qt_ref[b])
        v = v_ref[b]
        rows = []
        for h in range(H_A):
            s_new = ft[:, h:h + 1] * s_ref[b, h] + kt[:, h:h + 1] * v[h:h + 1, :]
            so_ref[b, h] = s_new
            rows.append(jnp.sum(qt[:, h:h + 1] * s_new, axis=0, keepdims=True))
        o = jnp.concatenate(rows, axis=0)
        o_ref[b] = (_rms(o) * nw * _silu(g_ref[b])).astype(o_ref.dtype)
        return carry

    lax.fori_loop(0, bt, seq, 0)


def _lower_bound_t(logits):
    e = jnp.exp(logits - jnp.max(logits, axis=0, keepdims=True))
    return e[0] / jnp.sum(e, axis=0)


def _hgrn_step(qt, ft, v, g, lbl_t, nw, state, bt):
    nb = qt.shape[0]
    tr = pl.BlockSpec((bt, HA_D, H_A), lambda i: (i, 0, 0))
    rw = pl.BlockSpec((bt, H_A, HA_D), lambda i: (i, 0, 0))
    st = pl.BlockSpec((bt, H_A, HA_D, HA_D), lambda i: (i, 0, 0, 0))
    return pl.pallas_call(
        functools.partial(_hgrn_step_kernel, bt=bt),
        grid=(nb // bt,),
        in_specs=[tr, tr, rw, rw,
                  pl.BlockSpec(lbl_t.shape, lambda i: (0, 0, 0)),
                  pl.BlockSpec((H_A, HA_D), lambda i: (0, 0)),
                  st],
        out_specs=[rw, st],
        out_shape=[jax.ShapeDtypeStruct((nb, H_A, HA_D), BF16),
                   jax.ShapeDtypeStruct(state.shape, F32)],
        compiler_params=_params("parallel"),
        name="hgrn_step",
    )(qt, ft, v, g, lbl_t, nw, state)


def _conv_step(buf_ref, b, u, w, bias):
    out = bias + w[CONV_W - 1] * u
    for i in range(CONV_W - 1):
        out = out + w[i] * buf_ref[b, i]
    return _silu(out)


def _ssd_step_kernel(xt_ref, xr_ref, bcr_ref, z_ref, dt_ref,
                     cxt_ref, cxr_ref, cbc_ref,
                     wxt_ref, wxr_ref, wbc_ref, bxt_ref, bxr_ref, bbc_ref,
                     dtb_ref, al_ref, dskr_ref, nw_ref, h_ref,
                     o_ref, ho_ref, y_scr, xc_scr, *, bt):
    wxt, wxr, wbc = wxt_ref[...], wxr_ref[...], wbc_ref[...]
    bxt, bxr, bbc = bxt_ref[...], bxr_ref[...], bbc_ref[...]
    a = -jnp.exp(al_ref[...])
    dtb = dtb_ref[...]

    def seq(b, carry):
        xt = _conv_step(cxt_ref, b, xt_ref[b], wxt, bxt)
        xc_scr[pl.ds(b, 1), :] = _conv_step(cxr_ref, b, xr_ref[b], wxr, bxr)
        bcv = _conv_step(cbc_ref, b, bcr_ref[b], wbc, bbc)
        dt = _softplus(dt_ref[b] + dtb)
        d_a = jnp.exp(dt * a)
        xdt = xt * dt
        ys = []
        for g in range(B_G):
            b_row = bcv[g:g + 1, :]
            c_row = bcv[B_G + g:B_G + g + 1, :]
            hs = []
            for j in range(HPG):
                hh = g * HPG + j
                h_new = d_a[:, hh:hh + 1] * h_ref[b, hh] + xdt[:, hh:hh + 1] * b_row
                ho_ref[b, hh] = h_new
                hs.append(h_new)
            hg = jnp.concatenate(hs, axis=0)
            c8 = jnp.broadcast_to(c_row, (SUBLANE, B_N))
            ys.append(_dot_nt(c8, hg)[0:1, :])
        y_scr[pl.ds(b, 1), :] = jnp.concatenate(ys, axis=1)
        return carry

    lax.fori_loop(0, bt, seq, 0)
    xc = xc_scr[...]
    yz = (y_scr[...] + dskr_ref[...] * xc) * _silu(z_ref[...])
    nw = nw_ref[...]
    for g in range(B_G):
        sl = slice(g * GW, (g + 1) * GW)
        o_ref[:, sl] = (_rms(yz[:, sl]) * nw[:, sl]).astype(o_ref.dtype)


def _ssd_step(xt, xr, bcr, z, dt, cxt, cxr, cbc, wxt, wxr, wbc, bxt, bxr, bbc,
              dtb, al, dskr, nw, state, bt):
    nb = xt.shape[0]
    full = lambda a: pl.BlockSpec(a.shape, lambda i: (0,) * a.ndim)
    lead = lambda a: pl.BlockSpec((bt,) + a.shape[1:], lambda i: (i,) + (0,) * (a.ndim - 1))
    return pl.pallas_call(
        functools.partial(_ssd_step_kernel, bt=bt),
        grid=(nb // bt,),
        in_specs=[lead(xt), lead(xr), lead(bcr), lead(z), lead(dt),
                  lead(cxt), lead(cxr), lead(cbc),
                  full(wxt), full(wxr), full(wbc), full(bxt), full(bxr), full(bbc),
                  full(dtb), full(al), full(dskr), full(nw), lead(state)],
        out_specs=[pl.BlockSpec((bt, D_B), lambda i: (i, 0)), lead(state)],
        out_shape=[jax.ShapeDtypeStruct((nb, D_B), F32),
                   jax.ShapeDtypeStruct(state.shape, F32)],
        scratch_shapes=[pltpu.VMEM((bt, D_B), F32), pltpu.VMEM((bt, D_B), F32)],
        compiler_params=_params("parallel"),
        name="ssd_step",
    )(xt, xr, bcr, z, dt, cxt, cxr, cbc, wxt, wxr, wbc, bxt, bxr, bbc, dtb, al, dskr, nw, state)


def _outproj_kernel(oa_ref, ob_ref, wa_ref, wb_ref, x_ref, gt_ref, gp_ref, y_ref):
    mix = _dot(oa_ref[...], wa_ref[...]) + _dot(ob_ref[...], wb_ref[...])
    y_ref[...] = x_ref[...] + gt_ref[...] * (_rms(mix) * gp_ref[...])


def _outproj(o_a, o_b, w_out, x, mod, per_row, rows_per_seq, g_post, tm):
    m = x.shape[0]
    row = lambda w: pl.BlockSpec((tm, w), lambda i, j: (i, 0))
    return pl.pallas_call(
        _outproj_kernel,
        grid=(m // tm, 1),
        in_specs=[row(D_A), row(D_B),
                  pl.BlockSpec((D_A, D_MODEL), lambda i, j: (0, 0)),
                  pl.BlockSpec((D_B, D_MODEL), lambda i, j: (1, 0)),
                  row(D_MODEL),
                  _mod_spec(per_row, rows_per_seq, tm, 2),
                  pl.BlockSpec((1, D_MODEL), lambda i, j: (0, 0))],
        out_specs=row(D_MODEL),
        out_shape=jax.ShapeDtypeStruct((m, D_MODEL), F32),
        compiler_params=_params("parallel", "arbitrary"),
        name="out_proj",
    )(o_a, o_b, w_out, w_out, x, mod, g_post)


def _mlp_kernel(x_ref, sh_ref, sc_ref, gt_ref, gpre_ref, gpost_ref, wu_ref, wd_ref, y_ref,
                h_scr, acc_scr):
    f = pl.program_id(1)

    @pl.when(f == 0)
    def _():
        h = _rms(x_ref[...]) * gpre_ref[...] * (1.0 + sc_ref[...]) + sh_ref[...]
        h_scr[...] = h.astype(BF16)
        acc_scr[...] = jnp.zeros_like(acc_scr)

    u = jnp.maximum(jnp.dot(h_scr[...], wu_ref[...], preferred_element_type=F32), 0.0)
    acc_scr[...] += jnp.dot((u * u).astype(BF16), wd_ref[...], preferred_element_type=F32)

    @pl.when(f == pl.num_programs(1) - 1)
    def _():
        y_ref[...] = x_ref[...] + gt_ref[...] * (_rms(acc_scr[...]) * gpost_ref[...])


def _mlp(x, mod, per_row, rows_per_seq, g_pre, g_post, w_up, w_down, tm, tf):
    m = x.shape[0]
    row = pl.BlockSpec((tm, D_MODEL), lambda i, j: (i, 0))
    vec = pl.BlockSpec((1, D_MODEL), lambda i, j: (0, 0))
    return pl.pallas_call(
        _mlp_kernel,
        grid=(m // tm, D_FF // tf),
        in_specs=[row,
                  _mod_spec(per_row, rows_per_seq, tm, 3),
                  _mod_spec(per_row, rows_per_seq, tm, 4),
                  _mod_spec(per_row, rows_per_seq, tm, 5),
                  vec, vec,
                  pl.BlockSpec((D_MODEL, tf), lambda i, j: (0, j)),
                  pl.BlockSpec((tf, D_MODEL), lambda i, j: (j, 0))],
        out_specs=row,
        out_shape=jax.ShapeDtypeStruct((m, D_MODEL), F32),
        scratch_shapes=[pltpu.VMEM((tm, D_MODEL), BF16), pltpu.VMEM((tm, D_MODEL), F32)],
        compiler_params=_params("parallel", "arbitrary"),
        name="mlp",
    )(x, mod, mod, mod, g_pre, g_post, w_up, w_down)


def _pick_tile(m, target):
    t = min(m, target)
    while m % t:
        t //= 2
    return t


def kernel(x_prompt, x_sample, c_prompt, c_sample, state_hgrn, state_ssm, state_conv, w_ada, b_ada, norm_pre_mix, norm_post_mix, norm_pre_mlp, norm_post_mlp, w_in, hgrn_lb_logits, hgrn_norm, conv_w, conv_b, dt_bias, a_log, d_skip, ssd_norm, w_out, w_up, w_down):
    n_seq, t, _ = x_prompt.shape
    n_dec = x_sample.shape[0]
    assert x_sample.shape[1] == 1 and t % CHUNK_A == 0 and t % CHUNK_B == 0
    assert w_ada.shape[0] == 1, "one layer"
    l = 0

    w_main = w_in[l][:, :D_MAIN].astype(BF16)
    w_dt = jnp.pad(w_in[l][:, D_MAIN:], ((0, 0), (0, LANE - H_B))).astype(BF16)
    w_out_b = w_out[l].astype(BF16)
    w_up_b = w_up[l].astype(BF16)
    w_down_b = w_down[l].astype(BF16)
    g_pre_mix, g_post_mix = norm_pre_mix[l][None], norm_post_mix[l][None]
    g_pre_mlp, g_post_mlp = norm_pre_mlp[l][None], norm_post_mlp[l][None]
    cw, cb = conv_w[l], conv_b[l][None]
    hn = hgrn_norm[l][None]
    sn = ssd_norm[l][None]

    n_c = n_seq + n_dec
    pad = (-n_c) % SUBLANE
    c_all = jnp.concatenate([c_prompt, c_sample, jnp.zeros((pad, D_MODEL), F32)], axis=0)
    mod = _ada(c_all, w_ada[l], b_ada[l][None])
    mod_p = mod[:n_seq].reshape(n_seq, 1, N_MOD * D_MODEL)
    mod_s = mod[n_seq:n_c]

    xp = x_prompt.reshape(n_seq * t, D_MODEL)
    tm_p = _pick_tile(t, 1024)
    proj_p, dt_p = _inproj(xp, mod_p, False, t, g_pre_mix, w_main, w_dt, tm_p, 512)
    o_a, s_hgrn_p = _hgrn_prompt(proj_p, hgrn_lb_logits, hn, n_seq, t)

    n_chunks = t // CHUNK_B
    rep = SUBLANE // HPG
    dt_rows = dt_p[:, :H_B].reshape(n_seq, n_chunks, CHUNK_B, B_G, 1, HPG).transpose(3, 0, 1, 4, 5, 2)
    dt_rows = jnp.broadcast_to(dt_rows, (B_G, n_seq, n_chunks, rep, HPG, CHUNK_B))
    dt_rows = dt_rows.reshape(B_G, n_seq, n_chunks * SUBLANE, CHUNK_B)
    prow = lambda p: jnp.tile(p.reshape(B_G, 1, HPG), (1, n_chunks * rep, 1)).reshape(B_G, n_chunks * SUBLANE, 1)
    o_b, s_ssm_p = _ssd_prompt(proj_p, dt_rows, cw, cb, prow(dt_bias[l]), prow(a_log[l]),
                               d_skip[l].reshape(B_G, 1, HPG), sn, n_seq, t)
    x1 = _outproj(o_a, o_b, w_out_b, xp, mod_p, False, t, g_post_mix, _pick_tile(t, 256))
    y_p = _mlp(x1, mod_p, False, t, g_pre_mlp, g_post_mlp, w_up_b, w_down_b, _pick_tile(t, 512), 512)
    conv_p = proj_p.reshape(n_seq, t, D_MAIN)[:, t - (CONV_W - 1):, OFF_X:]

    xs = x_sample.reshape(n_dec, D_MODEL)
    proj_s, dt_s = _inproj(xs, mod_s, True, 1, g_pre_mix, w_main, w_dt, n_dec, 512)
    bt = _pick_tile(n_dec, SUBLANE)
    to_t = lambda a, h, d: a.reshape(a.shape[:-1] + (h, d)).swapaxes(-1, -2)
    qt = to_t(proj_s[:, OFF_Q:OFF_Q + D_A], H_A, HA_D)
    ft = to_t(proj_s[:, OFF_F:OFF_F + D_A], H_A, HA_D)
    v_s = proj_s[:, OFF_I:OFF_I + D_A].reshape(n_dec, H_A, HA_D)
    g_s = proj_s[:, OFF_G:OFF_G + D_A].reshape(n_dec, H_A, HA_D)
    lbl_t = to_t(hgrn_lb_logits, H_A, HA_D)
    o_a_s, s_hgrn_s = _hgrn_step(qt, ft, v_s, g_s, lbl_t, hn.reshape(H_A, HA_D), state_hgrn[l], bt)

    xbc_s = proj_s[:, OFF_X:]
    cst = state_conv[l]
    xt = to_t(xbc_s[:, :D_B], H_B, B_P)
    xr = xbc_s[:, None, :D_B]
    bcr = xbc_s[:, D_B:].reshape(n_dec, 2 * B_G, B_N)
    cxt = to_t(cst[:, :, :D_B], H_B, B_P)
    cxr = cst[:, :, None, :D_B]
    cbc = cst[:, :, D_B:].reshape(n_dec, CONV_W - 1, 2 * B_G, B_N)
    wxt = to_t(cw[:, :D_B], H_B, B_P)
    wxr = cw[:, None, :D_B]
    wbc = cw[:, D_B:].reshape(CONV_W, 2 * B_G, B_N)
    bxt = to_t(cb[:, :D_B], H_B, B_P)[0]
    bxr = cb[:, :D_B]
    bbc = cb[0, D_B:].reshape(2 * B_G, B_N)
    o_b_s, s_ssm_s = _ssd_step(
        xt, xr, bcr, proj_s[:, OFF_Z:OFF_Z + D_B], dt_s[:, None, :H_B],
        cxt, cxr, cbc, wxt, wxr, wbc, bxt, bxr, bbc,
        dt_bias[l][None], a_log[l][None], jnp.repeat(d_skip[l], B_P)[None], sn,
        state_ssm[l], bt)
    x1_s = _outproj(o_a_s.reshape(n_dec, D_A), o_b_s, w_out_b, xs, mod_s, True, 1, g_post_mix, n_dec)
    y_s = _mlp(x1_s, mod_s, True, 1, g_pre_mlp, g_post_mlp, w_up_b, w_down_b, n_dec, 512)
    conv_s = jnp.concatenate([cst[:, 1:], xbc_s[:, None, :]], axis=1)

    return (y_p.reshape(n_seq, t, D_MODEL), y_s.reshape(n_dec, 1, D_MODEL),
            s_hgrn_p[None], s_ssm_p[None], conv_p[None],
            s_hgrn_s[None], s_ssm_s[None], conv_s[None])
```

```python
import functools

import jax
import jax.numpy as jnp
from jax import lax
from jax.experimental import pallas as pl
from jax.experimental.pallas import tpu as pltpu

F32 = jnp.float32
BF16 = jnp.bfloat16

D_MODEL = 2048
D_A = 2048
HA_D = 128
H_A = D_A // HA_D
D_B = 2048
B_P = 64
H_B = D_B // B_P
B_G = 8
HPG = H_B // B_G
GW = HPG * B_P
B_N = 128
CONV_W = 4
CONV_DIM = D_B + 2 * B_G * B_N
D_MAIN = 4 * D_A + D_B + CONV_DIM
D_FF = 4 * D_MODEL
N_MOD = 6
EPS = 1e-6

OFF_Q, OFF_F, OFF_I, OFF_G = 0, D_A, 2 * D_A, 3 * D_A
OFF_Z = 4 * D_A
OFF_X = OFF_Z + D_B
OFF_B = OFF_X + D_B
OFF_C = OFF_B + B_G * B_N

LANE = 128
SUBLANE = 8
VMEM_LIMIT = 56 * 1024 * 1024

CHUNK_A = 128
SUB_A = 16
CHUNK_B = 128


def _sigmoid(x):
    return 0.5 * jnp.tanh(0.5 * x) + 0.5


def _silu(x):
    hx = 0.5 * x
    return hx * jnp.tanh(hx) + hx


def _softplus(x):
    return jnp.maximum(x, 0.0) + jnp.log1p(jnp.exp(-jnp.abs(x)))


def _idiv(x, d):
    assert d & (d - 1) == 0
    return jnp.right_shift(x, d.bit_length() - 1)


def _rms(x):
    return x * lax.rsqrt(jnp.mean(x * x, axis=-1, keepdims=True) + EPS)


def _cumsum_rows(x, n, row):
    s = 1
    while s < n:
        x = x + jnp.where(row >= s, pltpu.roll(x, s, 0), 0.0)
        s *= 2
    return x


def _cumsum_lanes(x, n, lane):
    s = 1
    while s < n:
        x = x + jnp.where(lane >= s, pltpu.roll(x, s, 1), 0.0)
        s *= 2
    return x


def _dot(a, b):
    return jnp.dot(a.astype(BF16), b.astype(BF16), preferred_element_type=F32)


def _dot_nt(a, b):
    return lax.dot_general(a.astype(BF16), b.astype(BF16), (((1,), (1,)), ((), ())),
                           preferred_element_type=F32)


def _dot_tn(a, b):
    return lax.dot_general(a.astype(BF16), b.astype(BF16), (((0,), (0,)), ((), ())),
                           preferred_element_type=F32)


def _params(*sem):
    return pltpu.CompilerParams(dimension_semantics=sem, vmem_limit_bytes=VMEM_LIMIT)


def _ada_kernel(c_ref, w_ref, b_ref, o_ref):
    a = _silu(c_ref[...])
    o_ref[...] = _dot(a, w_ref[...]) + b_ref[...]


def _ada(c_all, w_ada, b_ada):
    m = c_all.shape[0]
    n = w_ada.shape[1]
    tn = 1024
    return pl.pallas_call(
        _ada_kernel,
        grid=(n // tn,),
        in_specs=[pl.BlockSpec((m, D_MODEL), lambda j: (0, 0)),
                  pl.BlockSpec((D_MODEL, tn), lambda j: (0, j)),
                  pl.BlockSpec((1, tn), lambda j: (0, j))],
        out_specs=pl.BlockSpec((m, tn), lambda j: (0, j)),
        out_shape=jax.ShapeDtypeStruct((m, n), F32),
        compiler_params=_params("parallel"),
        name="ada_mod",
    )(c_all, w_ada, b_ada)


def _mod_spec(per_row, rows_per_seq, tm, col):
    if per_row:
        return pl.BlockSpec((tm, D_MODEL), lambda i, j: (i, col))
    tiles_per_seq = rows_per_seq // tm
    return pl.BlockSpec((None, 1, D_MODEL), lambda i, j: (i // tiles_per_seq, 0, col))


def _inproj_kernel(x_ref, sh_ref, sc_ref, g_ref, w_ref, wdt_ref, o_ref, odt_ref, h_scr):
    @pl.when(pl.program_id(1) == 0)
    def _():
        h = _rms(x_ref[...]) * g_ref[...] * (1.0 + sc_ref[...]) + sh_ref[...]
        hb = h.astype(BF16)
        h_scr[...] = hb
        odt_ref[...] = jnp.dot(hb, wdt_ref[...], preferred_element_type=F32)

    o_ref[...] = jnp.dot(h_scr[...], w_ref[...], preferred_element_type=F32)


def _inproj(x, mod, per_row, rows_per_seq, g_pre, w_all, w_dt, tm, tn):
    m = x.shape[0]
    assert D_MAIN % tn == 0
    return pl.pallas_call(
        _inproj_kernel,
        grid=(m // tm, D_MAIN // tn),
        in_specs=[pl.BlockSpec((tm, D_MODEL), lambda i, j: (i, 0)),
                  _mod_spec(per_row, rows_per_seq, tm, 0),
                  _mod_spec(per_row, rows_per_seq, tm, 1),
                  pl.BlockSpec((1, D_MODEL), lambda i, j: (0, 0)),
                  pl.BlockSpec((D_MODEL, tn), lambda i, j: (0, j)),
                  pl.BlockSpec((D_MODEL, LANE), lambda i, j: (0, 0))],
        out_specs=[pl.BlockSpec((tm, tn), lambda i, j: (i, j)),
                   pl.BlockSpec((tm, LANE), lambda i, j: (i, 0))],
        out_shape=[jax.ShapeDtypeStruct((m, D_MAIN), F32),
                   jax.ShapeDtypeStruct((m, LANE), F32)],
        scratch_shapes=[pltpu.VMEM((tm, D_MODEL), BF16)],
        compiler_params=_params("parallel", "arbitrary"),
        name="in_proj",
    )(x, mod, mod, g_pre, w_all, w_dt)


def _lower_bound(logits):
    e = jnp.exp(logits - jnp.max(logits, axis=0, keepdims=True))
    return e[0:1] / jnp.sum(e, axis=0, keepdims=True)


def _hgrn_scores(q, fl, lb, row, a_row, a_col):
    c_len = CHUNK_A
    n_sub = c_len // SUB_A
    half = 0.5 * (1.0 - lb)
    f = (lb + half) + half * jnp.tanh(0.5 * fl)
    k = 1.0 - f
    qs = _silu(q)
    b = _cumsum_rows(jnp.log(f), c_len, row)
    k_raw = [k[j * SUB_A:(j + 1) * SUB_A, :].astype(BF16) for j in range(n_sub)]
    k_acc = []
    blocks, q_dec = [], []
    for i in range(n_sub):
        lo = i * SUB_A
        start = b[lo - 1:lo, :] if i > 0 else jnp.zeros((1, HA_D), F32)
        end = b[lo + SUB_A - 1:lo + SUB_A, :]
        b_i = b[lo:lo + SUB_A, :]
        k_i = k[lo:lo + SUB_A, :]
        qg = qs[lo:lo + SUB_A, :] * jnp.exp(b_i - start)
        k_diag = k_i * jnp.exp(start - b_i)
        rhs = jnp.concatenate([p.astype(BF16) for p in k_acc] + [k_diag.astype(BF16)] + k_raw[i + 1:],
                              axis=0)
        a_i = lax.dot_general(qg.astype(BF16), rhs, (((1,), (1,)), ((), ())),
                              preferred_element_type=F32)
        blocks.append(jnp.where(a_col <= a_row + lo, a_i, 0.0).astype(BF16))
        q_dec.append((qg * jnp.exp(start)).astype(BF16))
        sub_decay = jnp.exp(end - start)
        k_acc = [p * sub_decay for p in k_acc] + [k_i * jnp.exp(end - b_i)]
    scores = jnp.concatenate(blocks, axis=0)
    k_end = jnp.concatenate([p.astype(BF16) for p in k_acc], axis=0)
    chunk_decay = jnp.exp(b[c_len - 1:c_len, :])
    return scores, jnp.concatenate(q_dec, axis=0), k_end, chunk_decay


def _hgrn_apply(pend, v, g, nw, st):
    scores, q_dec, k_end, chunk_decay = pend
    vb = v.astype(BF16)
    o = (jnp.dot(scores, vb, preferred_element_type=F32)
         + lax.dot_general(q_dec, st.astype(BF16), (((1,), (1,)), ((), ())),
                           preferred_element_type=F32))
    st = chunk_decay * st + lax.dot_general(vb, k_end, (((0,), (0,)), ((), ())),
                                            preferred_element_type=F32)
    return (_rms(o) * nw * _silu(g)).astype(BF16), st


def _hgrn_prompt_kernel(q_ref, f_ref, i_ref, g_ref, lbl_ref, nw_ref, o_ref, s_ref, *, n_chunks, n_heads):
    c_len = CHUNK_A
    lb = _lower_bound(lbl_ref[...])
    nw = nw_ref[...]
    row = lax.broadcasted_iota(jnp.int32, (c_len, HA_D), 0)
    a_row = lax.broadcasted_iota(jnp.int32, (SUB_A, c_len), 0)
    a_col = lax.broadcasted_iota(jnp.int32, (SUB_A, c_len), 1)
    heads = [slice(h * HA_D, (h + 1) * HA_D) for h in range(n_heads)]

    def rows(c):
        return pl.ds(pl.multiple_of(c * c_len, c_len), c_len)

    def scores_of(c):
        sl = rows(c)
        return tuple(_hgrn_scores(q_ref[sl, hs], f_ref[sl, hs], lb[:, hs], row, a_row, a_col)
                     for hs in heads)

    def apply_to(c, pends, sts):
        sl = rows(c)
        new = []
        for hs, pend, st in zip(heads, pends, sts):
            out, st = _hgrn_apply(pend, i_ref[sl, hs], g_ref[sl, hs], nw[:, hs], st)
            o_ref[sl, hs] = out
            new.append(st)
        return tuple(new)

    def step(c, carry):
        pends, sts = carry
        sts = apply_to(c - 1, pends, sts)
        return scores_of(c), sts

    zeros = tuple(jnp.zeros((HA_D, HA_D), F32) for _ in range(n_heads))
    pends, sts = lax.fori_loop(1, n_chunks, step, (scores_of(0), zeros))
    sts = apply_to(n_chunks - 1, pends, sts)
    for h in range(n_heads):
        s_ref[h] = sts[h].T


HGRN_HEADS_PER_STEP = 4


def _hgrn_prompt(proj, lb_logits, hgrn_norm, n_seq, t):
    m = proj.shape[0]
    nh = HGRN_HEADS_PER_STEP
    w = nh * HA_D
    col = lambda off: pl.BlockSpec((t, w), lambda b, h: (b, off // w + h))
    return pl.pallas_call(
        functools.partial(_hgrn_prompt_kernel, n_chunks=t // CHUNK_A, n_heads=nh),
        grid=(n_seq, H_A // nh),
        in_specs=[col(OFF_Q), col(OFF_F), col(OFF_I), col(OFF_G),
                  pl.BlockSpec((lb_logits.shape[0], w), lambda b, h: (0, h)),
                  pl.BlockSpec((1, w), lambda b, h: (0, h))],
        out_specs=[pl.BlockSpec((t, w), lambda b, h: (b, h)),
                   pl.BlockSpec((None, nh, HA_D, HA_D), lambda b, h: (b, h, 0, 0))],
        out_shape=[jax.ShapeDtypeStruct((m, D_A), BF16),
                   jax.ShapeDtypeStruct((n_seq, H_A, HA_D, HA_D), F32)],
        compiler_params=_params("parallel", "parallel"),
        name="hgrn_prompt",
    )(proj, proj, proj, proj, lb_logits, hgrn_norm)


def _conv_chunk(tail, u, w, bias, n):
    full = jnp.concatenate([tail, u], axis=0)
    out = bias + w[CONV_W - 1:CONV_W, :] * u
    for i in range(CONV_W - 1):
        shifted = pltpu.roll(full, CONV_W - 1 - i, 0)[SUBLANE:SUBLANE + n, :]
        out = out + w[i:i + 1, :] * shifted
    return _silu(out)


def _ssd_prompt_kernel_old(z_ref, x_ref, b_ref, c_ref, dtr_ref,
                       wx_ref, wb_ref, wc_ref, bx_ref, bb_ref, bc_ref,
                       dtbr_ref, alr_ref, dsk_ref, nw_ref,
                       o_ref, h_ref, row_scr, col_scr, *, n_chunks):
    n = CHUNK_B
    wx, wb, wc = wx_ref[...], wb_ref[...], wc_ref[...]
    bx, bb, bc = bx_ref[...], bb_ref[...], bc_ref[...]
    nw = nw_ref[...]
    tri = (lax.broadcasted_iota(jnp.int32, (n, n), 0)
           >= lax.broadcasted_iota(jnp.int32, (n, n), 1))
    lane_head = _idiv(lax.broadcasted_iota(jnp.int32, (1, GW), 1), B_P)
    row_head = _idiv(lax.broadcasted_iota(jnp.int32, (GW, 1), 0), B_P)
    dsk = dsk_ref[...]
    dskip = jnp.zeros((1, GW), F32)
    for j in range(HPG):
        dskip = jnp.where(lane_head == j, dsk[:, j:j + 1], dskip)

    nr = n_chunks * SUBLANE
    dt_all = _softplus(dtr_ref[...] + dtbr_ref[...])
    cum_all = _cumsum_lanes(dt_all * -jnp.exp(alr_ref[...]), n,
                            lax.broadcasted_iota(jnp.int32, (nr, n), 1))
    is_dt = (lax.broadcasted_iota(jnp.int32, (nr, n), 0) & (SUBLANE - 1)) < HPG
    both = jnp.where(is_dt, dt_all, cum_all)
    for c in range(n_chunks):
        m8 = both[c * SUBLANE:(c + 1) * SUBLANE, :]
        row_scr[c] = m8
        col_scr[c] = jnp.concatenate([m8, jnp.zeros((n - SUBLANE, n), F32)], axis=0).T

    def rows(c):
        return pl.ds(pl.multiple_of(c * n, n), n)

    def scan_free(c):
        sl = rows(c)
        if isinstance(c, int) and c == 0:
            tail = lambda ref: jnp.zeros((SUBLANE, ref.shape[1]), F32)
        else:
            prev = pl.ds(pl.multiple_of(c * n - SUBLANE, SUBLANE), SUBLANE)
            tail = lambda ref: ref[prev, :]
        xc = _conv_chunk(tail(x_ref), x_ref[sl, :], wx, bx, n)
        bcv = _conv_chunk(tail(b_ref), b_ref[sl, :], wb, bb, n)
        ccv = _conv_chunk(tail(c_ref), c_ref[sl, :], wc, bc, n)
        r8 = row_scr[c]
        cols = col_scr[c]
        g = _dot_nt(ccv, bcv)
        y = dskip * xc
        e_cum = jnp.zeros((n, GW), F32)
        w_end = jnp.zeros((n, GW), F32)
        for j in range(HPG):
            cc = cols[:, HPG + j:HPG + j + 1]
            seg = cc - r8[HPG + j:HPG + j + 1, :]
            s = g * jnp.exp(jnp.where(tri, seg, -jnp.inf)) * r8[j:j + 1, :]
            y = y + _dot(s, jnp.where(lane_head == j, xc, 0.0))
            e_cum = jnp.where(lane_head == j, jnp.exp(cc), e_cum)
            c_end = r8[HPG + j:HPG + j + 1, n - 1:n]
            w_end = jnp.where(lane_head == j, jnp.exp(c_end - cc) * cols[:, j:j + 1], w_end)
        return y, e_cum, ccv.astype(BF16), (xc * w_end).astype(BF16), bcv.astype(BF16), r8

    def apply_state(c, pend, h):
        y, e_cum, ccb, xwb, bcb, r8 = pend
        sl = rows(c)
        ch = lax.dot_general(ccb, h.astype(BF16), (((1,), (1,)), ((), ())),
                             preferred_element_type=F32)
        yz = (y + ch * e_cum) * _silu(z_ref[sl, :])
        o_ref[sl, :] = (_rms(yz) * nw).astype(o_ref.dtype)
        h_dec = jnp.zeros((GW, 1), F32)
        for j in range(HPG):
            h_dec = jnp.where(row_head == j, jnp.exp(r8[HPG + j:HPG + j + 1, n - 1:n]), h_dec)
        return h_dec * h + lax.dot_general(xwb, bcb, (((0,), (0,)), ((), ())),
                                           preferred_element_type=F32)

    def step(c, carry):
        pend, h = carry
        h = apply_state(c - 1, pend, h)
        return scan_free(c), h

    pend, h = lax.fori_loop(1, n_chunks, step, (scan_free(0), jnp.zeros((GW, B_N), F32)))
    h = apply_state(n_chunks - 1, pend, h)
    h_ref[...] = h.reshape(HPG, B_P, B_N)


def _ssd_prompt_old(proj, dt_rows, conv_w, conv_b, dtb_rows, al_rows, dsk_col, ssd_norm, n_seq, t):
    m = proj.shape[0]
    n_chunks = t // CHUNK_B
    nr = n_chunks * SUBLANE
    wide = lambda off: pl.BlockSpec((t, GW), lambda b, g: (b, off // GW + g))
    narrow = lambda off: pl.BlockSpec((t, B_N), lambda b, g: (b, off // B_N + g))
    cw = lambda off, w: pl.BlockSpec((CONV_W, w), lambda b, g: (0, off // w + g))
    cbias = lambda off, w: pl.BlockSpec((1, w), lambda b, g: (0, off // w + g))
    pr = pl.BlockSpec((None, nr, 1), lambda b, g: (g, 0, 0))
    return pl.pallas_call(
        functools.partial(_ssd_prompt_kernel, n_chunks=n_chunks),
        grid=(n_seq, B_G),
        in_specs=[wide(OFF_Z), wide(OFF_X), narrow(OFF_B), narrow(OFF_C),
                  pl.BlockSpec((None, None, nr, CHUNK_B), lambda b, g: (g, b, 0, 0)),
                  cw(0, GW), cw(D_B, B_N), cw(D_B + B_G * B_N, B_N),
                  cbias(0, GW), cbias(D_B, B_N), cbias(D_B + B_G * B_N, B_N),
                  pr, pr,
                  pl.BlockSpec((None, 1, HPG), lambda b, g: (g, 0, 0)),
                  pl.BlockSpec((1, GW), lambda b, g: (0, g))],
        out_specs=[pl.BlockSpec((t, GW), lambda b, g: (b, g)),
                   pl.BlockSpec((None, HPG, B_P, B_N), lambda b, g: (b, g, 0, 0))],
        out_shape=[jax.ShapeDtypeStruct((m, D_B), BF16),
                   jax.ShapeDtypeStruct((n_seq, H_B, B_P, B_N), F32)],
        scratch_shapes=[pltpu.VMEM((n_chunks, SUBLANE, CHUNK_B), F32),
                        pltpu.VMEM((n_chunks, CHUNK_B, CHUNK_B), F32)],
        compiler_params=_params("parallel", "parallel"),
        name="ssd_prompt",
    )(proj, proj, proj, proj, dt_rows, conv_w, conv_w, conv_w, conv_b, conv_b, conv_b,
      dtb_rows, al_rows, dsk_col, ssd_norm)


SSD_GROUPS_PER_STEP = 2


def _ssd_prompt_kernel(z_ref, x_ref, b_ref, c_ref, dtr_ref,
                       wx_ref, wb_ref, wc_ref, bx_ref, bb_ref, bc_ref,
                       dtbr_ref, alr_ref, dsk_ref, nw_ref,
                       o_ref, h_ref, row_scr, *, n_chunks, n_groups):
    n = CHUNK_B
    assert n == LANE
    tri = (lax.broadcasted_iota(jnp.int32, (n, n), 0)
           >= lax.broadcasted_iota(jnp.int32, (n, n), 1))
    lane_head = _idiv(lax.broadcasted_iota(jnp.int32, (1, GW), 1), B_P)
    row_head = _idiv(lax.broadcasted_iota(jnp.int32, (GW, 1), 0), B_P)
    nr = n_chunks * SUBLANE
    lane_nr = lax.broadcasted_iota(jnp.int32, (nr, n), 1)
    is_dt = (lax.broadcasted_iota(jnp.int32, (nr, n), 0) & (SUBLANE - 1)) < HPG
    is_dt8 = lax.broadcasted_iota(jnp.int32, (SUBLANE, n), 0) < HPG
    n_src = 2 * SUBLANE
    sel_shape = (4 * n_src, HPG * n + 2 * GW)
    sel_k = lax.broadcasted_iota(jnp.int32, sel_shape, 0)
    sel_l = lax.broadcasted_iota(jnp.int32, sel_shape, 1)
    want = jnp.where(sel_l < HPG * n, HPG + _idiv(sel_l, n),
                     2 * HPG + _idiv(sel_l - HPG * n, B_P))
    bcast_sel = jnp.where(((sel_k & (n_src - 1)) == want) & (sel_k < 3 * n_src), 1.0, 0.0).astype(BF16)

    groups = []
    for gi in range(n_groups):
        xs = slice(gi * GW, (gi + 1) * GW)
        bs = slice(gi * B_N, (gi + 1) * B_N)
        dsk = dsk_ref[gi]
        dskip = jnp.zeros((1, GW), F32)
        for j in range(HPG):
            dskip = jnp.where(lane_head == j, dsk[:, j:j + 1], dskip)
        dt_all = _softplus(dtr_ref[gi] + dtbr_ref[gi])
        cum_all = _cumsum_lanes(dt_all * -jnp.exp(alr_ref[gi]), n, lane_nr)
        both = jnp.where(is_dt, dt_all, cum_all)
        row_scr[gi] = both.reshape(n_chunks, SUBLANE, n)
        groups.append((gi, xs, bs, dskip, wx_ref[:, xs], wb_ref[:, bs], wc_ref[:, bs],
                       bx_ref[:, xs], bb_ref[:, bs], bc_ref[:, bs], nw_ref[:, xs]))

    def rows(c):
        return pl.ds(pl.multiple_of(c * n, n), n)

    def scan_free(c, grp):
        gi, xs, bs, dskip, wx, wb, wc, bx, bb, bc, _ = grp
        sl = rows(c)
        if isinstance(c, int) and c == 0:
            tail = lambda ref, lanes: jnp.zeros((SUBLANE, lanes.stop - lanes.start), F32)
        else:
            prev = pl.ds(pl.multiple_of(c * n - SUBLANE, SUBLANE), SUBLANE)
            tail = lambda ref, lanes: ref[prev, lanes]
        xc = _conv_chunk(tail(x_ref, xs), x_ref[sl, xs], wx, bx, n)
        bcv = _conv_chunk(tail(b_ref, bs), b_ref[sl, bs], wb, bb, n)
        ccv = _conv_chunk(tail(c_ref, bs), c_ref[sl, bs], wc, bc, n)
        r8 = row_scr[gi, c]
        swapped = pltpu.roll(r8, HPG, 0)
        cum8 = jnp.where(is_dt8, swapped, r8)
        dt8 = jnp.where(is_dt8, r8, swapped)
        e8 = jnp.where(is_dt8, jnp.exp(cum8), jnp.exp(cum8[:, n - 1:n] - cum8) * dt8)
        both = jnp.concatenate([r8, e8], axis=0)
        hi = both.astype(BF16).astype(F32)
        mid = (both - hi).astype(BF16).astype(F32)
        lo = both - hi - mid
        terms = jnp.concatenate([hi, mid, lo, jnp.zeros_like(hi)], axis=0).astype(BF16)
        cols = lax.dot_general(terms, bcast_sel, (((0,), (0,)), ((), ())),
                               preferred_element_type=F32)
        e_cum = cols[:, HPG * n:HPG * n + GW]
        w_end = cols[:, HPG * n + GW:]
        g = _dot_nt(ccv, bcv)
        xb = xc.astype(BF16)
        s_all, x_all = [], []
        for j in range(HPG):
            cc = cols[:, j * n:(j + 1) * n]
            seg = cc - r8[HPG + j:HPG + j + 1, :]
            s = g * jnp.exp(jnp.where(tri, seg, -jnp.inf)) * r8[j:j + 1, :]
            s_all.append(s.astype(BF16))
            x_all.append(jnp.where(lane_head == j, xb, jnp.zeros_like(xb)))
        y = dskip * xc + jnp.dot(jnp.concatenate(s_all, axis=1), jnp.concatenate(x_all, axis=0),
                                 preferred_element_type=F32)
        return y, e_cum, ccv.astype(BF16), (xc * w_end).astype(BF16), bcv.astype(BF16), r8

    def apply_state(c, grp, pend, h):
        xs, nw = grp[1], grp[-1]
        y, e_cum, ccb, xwb, bcb, r8 = pend
        sl = rows(c)
        ch = lax.dot_general(ccb, h.astype(BF16), (((1,), (1,)), ((), ())),
                             preferred_element_type=F32)
        yz = (y + ch * e_cum) * _silu(z_ref[sl, xs])
        o_ref[sl, xs] = (_rms(yz) * nw).astype(o_ref.dtype)
        h_dec = jnp.zeros((GW, 1), F32)
        for j in range(HPG):
            h_dec = jnp.where(row_head == j, jnp.exp(r8[HPG + j:HPG + j + 1, n - 1:n]), h_dec)
        return h_dec * h + lax.dot_general(xwb, bcb, (((0,), (0,)), ((), ())),
                                           preferred_element_type=F32)

    def step(c, carry):
        pends, hs = carry
        hs = tuple(apply_state(c - 1, grp, pend, h) for grp, pend, h in zip(groups, pends, hs))
        return tuple(scan_free(c, grp) for grp in groups), hs

    init = (tuple(scan_free(0, grp) for grp in groups),
            tuple(jnp.zeros((GW, B_N), F32) for _ in groups))
    pends, hs = lax.fori_loop(1, n_chunks, step, init)
    for gi, (grp, pend, h) in enumerate(zip(groups, pends, hs)):
        h = apply_state(n_chunks - 1, grp, pend, h)
        h_ref[gi * HPG:(gi + 1) * HPG] = h.reshape(HPG, B_P, B_N)


def _ssd_prompt(proj, dt_rows, conv_w, conv_b, dtb_rows, al_rows, dsk_col, ssd_norm, n_seq, t):
    m = proj.shape[0]
    n_chunks = t // CHUNK_B
    nr = n_chunks * SUBLANE
    ng = SSD_GROUPS_PER_STEP
    xw, bw = ng * GW, ng * B_N
    wide = lambda off: pl.BlockSpec((t, xw), lambda b, g: (b, off // xw + g))
    narrow = lambda off: pl.BlockSpec((t, bw), lambda b, g: (b, off // bw + g))
    cw = lambda off, w: pl.BlockSpec((CONV_W, w), lambda b, g: (0, off // w + g))
    cbias = lambda off, w: pl.BlockSpec((1, w), lambda b, g: (0, off // w + g))
    pr = pl.BlockSpec((ng, nr, 1), lambda b, g: (g, 0, 0))
    return pl.pallas_call(
        functools.partial(_ssd_prompt_kernel, n_chunks=n_chunks, n_groups=ng),
        grid=(n_seq, B_G // ng),
        in_specs=[wide(OFF_Z), wide(OFF_X), narrow(OFF_B), narrow(OFF_C),
                  pl.BlockSpec((ng, None, nr, CHUNK_B), lambda b, g: (g, b, 0, 0)),
                  cw(0, xw), cw(D_B, bw), cw(D_B + B_G * B_N, bw),
                  cbias(0, xw), cbias(D_B, bw), cbias(D_B + B_G * B_N, bw),
                  pr, pr,
                  pl.BlockSpec((ng, 1, HPG), lambda b, g: (g, 0, 0)),
                  pl.BlockSpec((1, xw), lambda b, g: (0, g))],
        out_specs=[pl.BlockSpec((t, xw), lambda b, g: (b, g)),
                   pl.BlockSpec((None, ng * HPG, B_P, B_N), lambda b, g: (b, g, 0, 0))],
        out_shape=[jax.ShapeDtypeStruct((m, D_B), BF16),
                   jax.ShapeDtypeStruct((n_seq, H_B, B_P, B_N), F32)],
        scratch_shapes=[pltpu.VMEM((ng, n_chunks, SUBLANE, CHUNK_B), F32)],
        compiler_params=_params("parallel", "parallel"),
        name="ssd_prompt",
    )(proj, proj, proj, proj, dt_rows, conv_w, conv_w, conv_w, conv_b, conv_b, conv_b,
      dtb_rows, al_rows, dsk_col, ssd_norm)


def _hgrn_step_kernel(qt_ref, ft_ref, v_ref, g_ref, lblt_ref, nw_ref, s_ref, o_ref, so_ref, *, bt):
    lbt = _lower_bound_t(lblt_ref[...])
    nw = nw_ref[...]

    def seq(b, carry):
        ft = lbt + (1.0 - lbt) * _sigmoid(ft_ref[b])
        kt = 1.0 - ft
        qt = _silu(qt_ref[b])
        v = v_ref[b]
        rows = []
        for h in range(H_A):
            s_new = ft[:, h:h + 1] * s_ref[b, h] + kt[:, h:h + 1] * v[h:h + 1, :]
            so_ref[b, h] = s_new
            rows.append(jnp.sum(qt[:, h:h + 1] * s_new, axis=0, keepdims=True))
        o = jnp.concatenate(rows, axis=0)
        o_ref[b] = (_rms(o) * nw * _silu(g_ref[b])).astype(o_ref.dtype)
        return carry

    lax.fori_loop(0, bt, seq, 0)


def _lower_bound_t(logits):
    e = jnp.exp(logits - jnp.max(logits, axis=0, keepdims=True))
    return e[0] / jnp.sum(e, axis=0)


def _hgrn_step(qt, ft, v, g, lbl_t, nw, state, bt):
    nb = qt.shape[0]
    tr = pl.BlockSpec((bt, HA_D, H_A), lambda i: (i, 0, 0))
    rw = pl.BlockSpec((bt, H_A, HA_D), lambda i: (i, 0, 0))
    st = pl.BlockSpec((bt, H_A, HA_D, HA_D), lambda i: (i, 0, 0, 0))
    return pl.pallas_call(
        functools.partial(_hgrn_step_kernel, bt=bt),
        grid=(nb // bt,),
        in_specs=[tr, tr, rw, rw,
                  pl.BlockSpec(lbl_t.shape, lambda i: (0, 0, 0)),
                  pl.BlockSpec((H_A, HA_D), lambda i: (0, 0)),
                  st],
        out_specs=[rw, st],
        out_shape=[jax.ShapeDtypeStruct((nb, H_A, HA_D), BF16),
                   jax.ShapeDtypeStruct(state.shape, F32)],
        compiler_params=_params("parallel"),
        name="hgrn_step",
    )(qt, ft, v, g, lbl_t, nw, state)


def _conv_step(buf_ref, b, u, w, bias):
    out = bias + w[CONV_W - 1] * u
    for i in range(CONV_W - 1):
        out = out + w[i] * buf_ref[b, i]
    return _silu(out)


def _ssd_step_kernel(xt_ref, xr_ref, bcr_ref, z_ref, dt_ref,
                     cxt_ref, cxr_ref, cbc_ref,
                     wxt_ref, wxr_ref, wbc_ref, bxt_ref, bxr_ref, bbc_ref,
                     dtb_ref, al_ref, dskr_ref, nw_ref, h_ref,
                     o_ref, ho_ref, y_scr, xc_scr, *, bt):
    wxt, wxr, wbc = wxt_ref[...], wxr_ref[...], wbc_ref[...]
    bxt, bxr, bbc = bxt_ref[...], bxr_ref[...], bbc_ref[...]
    a = -jnp.exp(al_ref[...])
    dtb = dtb_ref[...]

    def seq(b, carry):
        xt = _conv_step(cxt_ref, b, xt_ref[b], wxt, bxt)
        xc_scr[pl.ds(b, 1), :] = _conv_step(cxr_ref, b, xr_ref[b], wxr, bxr)
        bcv = _conv_step(cbc_ref, b, bcr_ref[b], wbc, bbc)
        dt = _softplus(dt_ref[b] + dtb)
        d_a = jnp.exp(dt * a)
        xdt = xt * dt
        ys = []
        for g in range(B_G):
            b_row = bcv[g:g + 1, :]
            c_row = bcv[B_G + g:B_G + g + 1, :]
            hs = []
            for j in range(HPG):
                hh = g * HPG + j
                h_new = d_a[:, hh:hh + 1] * h_ref[b, hh] + xdt[:, hh:hh + 1] * b_row
                ho_ref[b, hh] = h_new
                hs.append(h_new)
            hg = jnp.concatenate(hs, axis=0)
            c8 = jnp.broadcast_to(c_row, (SUBLANE, B_N))
            ys.append(_dot_nt(c8, hg)[0:1, :])
        y_scr[pl.ds(b, 1), :] = jnp.concatenate(ys, axis=1)
        return carry

    lax.fori_loop(0, bt, seq, 0)
    xc = xc_scr[...]
    yz = (y_scr[...] + dskr_ref[...] * xc) * _silu(z_ref[...])
    nw = nw_ref[...]
    for g in range(B_G):
        sl = slice(g * GW, (g + 1) * GW)
        o_ref[:, sl] = (_rms(yz[:, sl]) * nw[:, sl]).astype(o_ref.dtype)


def _ssd_step(xt, xr, bcr, z, dt, cxt, cxr, cbc, wxt, wxr, wbc, bxt, bxr, bbc,
              dtb, al, dskr, nw, state, bt):
    nb = xt.shape[0]
    full = lambda a: pl.BlockSpec(a.shape, lambda i: (0,) * a.ndim)
    lead = lambda a: pl.BlockSpec((bt,) + a.shape[1:], lambda i: (i,) + (0,) * (a.ndim - 1))
    return pl.pallas_call(
        functools.partial(_ssd_step_kernel, bt=bt),
        grid=(nb // bt,),
        in_specs=[lead(xt), lead(xr), lead(bcr), lead(z), lead(dt),
                  lead(cxt), lead(cxr), lead(cbc),
                  full(wxt), full(wxr), full(wbc), full(bxt), full(bxr), full(bbc),
                  full(dtb), full(al), full(dskr), full(nw), lead(state)],
        out_specs=[pl.BlockSpec((bt, D_B), lambda i: (i, 0)), lead(state)],
        out_shape=[jax.ShapeDtypeStruct((nb, D_B), F32),
                   jax.ShapeDtypeStruct(state.shape, F32)],
        scratch_shapes=[pltpu.VMEM((bt, D_B), F32), pltpu.VMEM((bt, D_B), F32)],
        compiler_params=_params("parallel"),
        name="ssd_step",
    )(xt, xr, bcr, z, dt, cxt, cxr, cbc, wxt, wxr, wbc, bxt, bxr, bbc, dtb, al, dskr, nw, state)


def _outproj_kernel(oa_ref, ob_ref, wa_ref, wb_ref, x_ref, gt_ref, gp_ref, y_ref):
    mix = _dot(oa_ref[...], wa_ref[...]) + _dot(ob_ref[...], wb_ref[...])
    y_ref[...] = x_ref[...] + gt_ref[...] * (_rms(mix) * gp_ref[...])


def _outproj(o_a, o_b, w_out, x, mod, per_row, rows_per_seq, g_post, tm):
    m = x.shape[0]
    row = lambda w: pl.BlockSpec((tm, w), lambda i, j: (i, 0))
    return pl.pallas_call(
        _outproj_kernel,
        grid=(m // tm, 1),
        in_specs=[row(D_A), row(D_B),
                  pl.BlockSpec((D_A, D_MODEL), lambda i, j: (0, 0)),
                  pl.BlockSpec((D_B, D_MODEL), lambda i, j: (1, 0)),
                  row(D_MODEL),
                  _mod_spec(per_row, rows_per_seq, tm, 2),
                  pl.BlockSpec((1, D_MODEL), lambda i, j: (0, 0))],
        out_specs=row(D_MODEL),
        out_shape=jax.ShapeDtypeStruct((m, D_MODEL), F32),
        compiler_params=_params("parallel", "arbitrary"),
        name="out_proj",
    )(o_a, o_b, w_out, w_out, x, mod, g_post)


def _mlp_kernel(x_ref, sh_ref, sc_ref, gt_ref, gpre_ref, gpost_ref, wu_ref, wd_ref, y_ref,
                h_scr, acc_scr):
    f = pl.program_id(1)

    @pl.when(f == 0)
    def _():
        h = _rms(x_ref[...]) * gpre_ref[...] * (1.0 + sc_ref[...]) + sh_ref[...]
        h_scr[...] = h.astype(BF16)
        acc_scr[...] = jnp.zeros_like(acc_scr)

    u = jnp.maximum(jnp.dot(h_scr[...], wu_ref[...], preferred_element_type=F32), 0.0)
    acc_scr[...] += jnp.dot((u * u).astype(BF16), wd_ref[...], preferred_element_type=F32)

    @pl.when(f == pl.num_programs(1) - 1)
    def _():
        y_ref[...] = x_ref[...] + gt_ref[...] * (_rms(acc_scr[...]) * gpost_ref[...])


def _mlp(x, mod, per_row, rows_per_seq, g_pre, g_post, w_up, w_down, tm, tf):
    m = x.shape[0]
    row = pl.BlockSpec((tm, D_MODEL), lambda i, j: (i, 0))
    vec = pl.BlockSpec((1, D_MODEL), lambda i, j: (0, 0))
    return pl.pallas_call(
        _mlp_kernel,
        grid=(m // tm, D_FF // tf),
        in_specs=[row,
                  _mod_spec(per_row, rows_per_seq, tm, 3),
                  _mod_spec(per_row, rows_per_seq, tm, 4),
                  _mod_spec(per_row, rows_per_seq, tm, 5),
                  vec, vec,
                  pl.BlockSpec((D_MODEL, tf), lambda i, j: (0, j)),
                  pl.BlockSpec((tf, D_MODEL), lambda i, j: (j, 0))],
        out_specs=row,
        out_shape=jax.ShapeDtypeStruct((m, D_MODEL), F32),
        scratch_shapes=[pltpu.VMEM((tm, D_MODEL), BF16), pltpu.VMEM((tm, D_MODEL), F32)],
        compiler_params=_params("parallel", "arbitrary"),
        name="mlp",
    )(x, mod, mod, mod, g_pre, g_post, w_up, w_down)


def _pick_tile(m, target):
    t = min(m, target)
    while m % t:
        t //= 2
    return t


def kernel(x_prompt, x_sample, c_prompt, c_sample, state_hgrn, state_ssm, state_conv, w_ada, b_ada, norm_pre_mix, norm_post_mix, norm_pre_mlp, norm_post_mlp, w_in, hgrn_lb_logits, hgrn_norm, conv_w, conv_b, dt_bias, a_log, d_skip, ssd_norm, w_out, w_up, w_down):
    n_seq, t, _ = x_prompt.shape
    n_dec = x_sample.shape[0]
    assert x_sample.shape[1] == 1 and t % CHUNK_A == 0 and t % CHUNK_B == 0
    assert w_ada.shape[0] == 1, "one layer"
    l = 0

    w_in_b = w_in[l].astype(BF16)
    w_dt = jnp.pad(w_in[l][:, D_MAIN:], ((0, 0), (0, LANE - H_B))).astype(BF16)
    w_out_b = w_out[l].astype(BF16)
    w_up_b = w_up[l].astype(BF16)
    w_down_b = w_down[l].astype(BF16)
    g_pre_mix, g_post_mix = norm_pre_mix[l][None], norm_post_mix[l][None]
    g_pre_mlp, g_post_mlp = norm_pre_mlp[l][None], norm_post_mlp[l][None]
    cw, cb = conv_w[l], conv_b[l][None]
    hn = hgrn_norm[l][None]
    sn = ssd_norm[l][None]

    n_c = n_seq + n_dec
    pad = (-n_c) % SUBLANE
    c_all = jnp.concatenate([c_prompt, c_sample, jnp.zeros((pad, D_MODEL), F32)], axis=0)
    mod = _ada(c_all, w_ada[l], b_ada[l][None])
    mod_p = mod[:n_seq].reshape(n_seq, 1, N_MOD * D_MODEL)
    mod_s = mod[n_seq:n_c]

    xp = x_prompt.reshape(n_seq * t, D_MODEL)
    tm_p = _pick_tile(t, 1024)
    proj_p, dt_p = _inproj(xp, mod_p, False, t, g_pre_mix, w_in_b, w_dt, tm_p, 1024)
    o_a, s_hgrn_p = _hgrn_prompt(proj_p, hgrn_lb_logits, hn, n_seq, t)

    n_chunks = t // CHUNK_B
    rep = SUBLANE // HPG
    dt_rows = dt_p[:, :H_B].reshape(n_seq, n_chunks, CHUNK_B, B_G, 1, HPG).transpose(3, 0, 1, 4, 5, 2)
    dt_rows = jnp.broadcast_to(dt_rows, (B_G, n_seq, n_chunks, rep, HPG, CHUNK_B))
    dt_rows = dt_rows.reshape(B_G, n_seq, n_chunks * SUBLANE, CHUNK_B)
    prow = lambda p: jnp.tile(p.reshape(B_G, 1, HPG), (1, n_chunks * rep, 1)).reshape(B_G, n_chunks * SUBLANE, 1)
    o_b, s_ssm_p = _ssd_prompt(proj_p, dt_rows, cw, cb, prow(dt_bias[l]), prow(a_log[l]),
                               d_skip[l].reshape(B_G, 1, HPG), sn, n_seq, t)
    x1 = _outproj(o_a, o_b, w_out_b, xp, mod_p, False, t, g_post_mix, _pick_tile(t, 256))
    y_p = _mlp(x1, mod_p, False, t, g_pre_mlp, g_post_mlp, w_up_b, w_down_b, _pick_tile(t, 512), 512)
    conv_p = proj_p.reshape(n_seq, t, D_MAIN)[:, t - (CONV_W - 1):, OFF_X:]

    xs = x_sample.reshape(n_dec, D_MODEL)
    proj_s, dt_s = _inproj(xs, mod_s, True, 1, g_pre_mix, w_in_b, w_dt, n_dec, 2048)
    bt = _pick_tile(n_dec, SUBLANE)
    to_t = lambda a, h, d: a.reshape(a.shape[:-1] + (h, d)).swapaxes(-1, -2)
    qt = to_t(proj_s[:, OFF_Q:OFF_Q + D_A], H_A, HA_D)
    ft = to_t(proj_s[:, OFF_F:OFF_F + D_A], H_A, HA_D)
    v_s = proj_s[:, OFF_I:OFF_I + D_A].reshape(n_dec, H_A, HA_D)
    g_s = proj_s[:, OFF_G:OFF_G + D_A].reshape(n_dec, H_A, HA_D)
    lbl_t = to_t(hgrn_lb_logits, H_A, HA_D)
    o_a_s, s_hgrn_s = _hgrn_step(qt, ft, v_s, g_s, lbl_t, hn.reshape(H_A, HA_D), state_hgrn[l], bt)

    xbc_s = proj_s[:, OFF_X:]
    cst = state_conv[l]
    xt = to_t(xbc_s[:, :D_B], H_B, B_P)
    xr = xbc_s[:, None, :D_B]
    bcr = xbc_s[:, D_B:].reshape(n_dec, 2 * B_G, B_N)
    cxt = to_t(cst[:, :, :D_B], H_B, B_P)
    cxr = cst[:, :, None, :D_B]
    cbc = cst[:, :, D_B:].reshape(n_dec, CONV_W - 1, 2 * B_G, B_N)
    wxt = to_t(cw[:, :D_B], H_B, B_P)
    wxr = cw[:, None, :D_B]
    wbc = cw[:, D_B:].reshape(CONV_W, 2 * B_G, B_N)
    bxt = to_t(cb[:, :D_B], H_B, B_P)[0]
    bxr = cb[:, :D_B]
    bbc = cb[0, D_B:].reshape(2 * B_G, B_N)
    o_b_s, s_ssm_s = _ssd_step(
        xt, xr, bcr, proj_s[:, OFF_Z:OFF_Z + D_B], dt_s[:, None, :H_B],
        cxt, cxr, cbc, wxt, wxr, wbc, bxt, bxr, bbc,
        dt_bias[l][None], a_log[l][None], jnp.repeat(d_skip[l], B_P)[None], sn,
        state_ssm[l], bt)
    x1_s = _outproj(o_a_s.reshape(n_dec, D_A), o_b_s, w_out_b, xs, mod_s, True, 1, g_post_mix, n_dec)
    y_s = _mlp(x1_s, mod_s, True, 1, g_pre_mlp, g_post_mlp, w_up_b, w_down_b, n_dec, 512)
    conv_s = jnp.concatenate([cst[:, 1:], xbc_s[:, None, :]], axis=1)

    return (y_p.reshape(n_seq, t, D_MODEL), y_s.reshape(n_dec, 1, D_MODEL),
            s_hgrn_p[None], s_ssm_p[None], conv_p[None],
            s_hgrn_s[None], s_ssm_s[None], conv_s[None])
```

```python
import functools

import jax
import jax.numpy as jnp
from jax import lax
from jax.experimental import pallas as pl
from jax.experimental.pallas import tpu as pltpu

F32 = jnp.float32
BF16 = jnp.bfloat16

D_MODEL = 2048
D_A = 2048
HA_D = 128
H_A = D_A // HA_D
D_B = 2048
B_P = 64
H_B = D_B // B_P
B_G = 8
HPG = H_B // B_G
GW = HPG * B_P
B_N = 128
CONV_W = 4
CONV_DIM = D_B + 2 * B_G * B_N
D_MAIN = 4 * D_A + D_B + CONV_DIM
D_FF = 4 * D_MODEL
N_MOD = 6
EPS = 1e-6

OFF_Q, OFF_F, OFF_I, OFF_G = 0, D_A, 2 * D_A, 3 * D_A
OFF_Z = 4 * D_A
OFF_X = OFF_Z + D_B
OFF_B = OFF_X + D_B
OFF_C = OFF_B + B_G * B_N

LANE = 128
SUBLANE = 8
VMEM_LIMIT = 56 * 1024 * 1024

CHUNK_A = 128
SUB_A = 16
CHUNK_B = 128


def _sigmoid(x):
    return 0.5 * jnp.tanh(0.5 * x) + 0.5


def _silu(x):
    hx = 0.5 * x
    return hx * jnp.tanh(hx) + hx


def _softplus(x):
    return jnp.maximum(x, 0.0) + jnp.log1p(jnp.exp(-jnp.abs(x)))


def _idiv(x, d):
    assert d & (d - 1) == 0
    return jnp.right_shift(x, d.bit_length() - 1)


def _rms(x):
    return x * lax.rsqrt(jnp.mean(x * x, axis=-1, keepdims=True) + EPS)


def _cumsum_rows(x, n, row):
    s = 1
    while s < n:
        x = x + jnp.where(row >= s, pltpu.roll(x, s, 0), 0.0)
        s *= 2
    return x


def _cumsum_lanes(x, n, lane):
    s = 1
    while s < n:
        x = x + jnp.where(lane >= s, pltpu.roll(x, s, 1), 0.0)
        s *= 2
    return x


def _dot(a, b):
    return jnp.dot(a.astype(BF16), b.astype(BF16), preferred_element_type=F32)


def _dot_nt(a, b):
    return lax.dot_general(a.astype(BF16), b.astype(BF16), (((1,), (1,)), ((), ())),
                           preferred_element_type=F32)


def _dot_tn(a, b):
    return lax.dot_general(a.astype(BF16), b.astype(BF16), (((0,), (0,)), ((), ())),
                           preferred_element_type=F32)


def _params(*sem):
    return pltpu.CompilerParams(dimension_semantics=sem, vmem_limit_bytes=VMEM_LIMIT)


def _ada_kernel(c_ref, w_ref, b_ref, o_ref):
    a = _silu(c_ref[...])
    o_ref[...] = _dot(a, w_ref[...]) + b_ref[...]


def _ada(c_all, w_ada, b_ada):
    m = c_all.shape[0]
    n = w_ada.shape[1]
    tn = 1024
    return pl.pallas_call(
        _ada_kernel,
        grid=(n // tn,),
        in_specs=[pl.BlockSpec((m, D_MODEL), lambda j: (0, 0)),
                  pl.BlockSpec((D_MODEL, tn), lambda j: (0, j)),
                  pl.BlockSpec((1, tn), lambda j: (0, j))],
        out_specs=pl.BlockSpec((m, tn), lambda j: (0, j)),
        out_shape=jax.ShapeDtypeStruct((m, n), F32),
        compiler_params=_params("parallel"),
        name="ada_mod",
    )(c_all, w_ada, b_ada)


def _mod_spec(per_row, rows_per_seq, tm, col):
    if per_row:
        return pl.BlockSpec((tm, D_MODEL), lambda i, j: (i, col))
    tiles_per_seq = rows_per_seq // tm
    return pl.BlockSpec((None, 1, D_MODEL), lambda i, j: (i // tiles_per_seq, 0, col))


def _inproj_kernel(x_ref, sh_ref, sc_ref, g_ref, w_ref, wdt_ref, o_ref, odt_ref, h_scr):
    @pl.when(pl.program_id(1) == 0)
    def _():
        h = _rms(x_ref[...]) * g_ref[...] * (1.0 + sc_ref[...]) + sh_ref[...]
        hb = h.astype(BF16)
        h_scr[...] = hb
        odt_ref[...] = jnp.dot(hb, wdt_ref[...].astype(BF16), preferred_element_type=F32)

    o_ref[...] = jnp.dot(h_scr[...], w_ref[...].astype(BF16), preferred_element_type=F32)


def _inproj(x, mod, per_row, rows_per_seq, g_pre, w_all, w_dt, tm, tn):
    m = x.shape[0]
    assert D_MAIN % tn == 0
    return pl.pallas_call(
        _inproj_kernel,
        grid=(m // tm, D_MAIN // tn),
        in_specs=[pl.BlockSpec((tm, D_MODEL), lambda i, j: (i, 0), pipeline_mode=pl.Buffered(1)),
                  _mod_spec(per_row, rows_per_seq, tm, 0),
                  _mod_spec(per_row, rows_per_seq, tm, 1),
                  pl.BlockSpec((1, D_MODEL), lambda i, j: (0, 0)),
                  pl.BlockSpec((D_MODEL, tn), lambda i, j: (0, j)),
                  pl.BlockSpec((D_MODEL, LANE), lambda i, j: (0, 0))],
        out_specs=[pl.BlockSpec((tm, tn), lambda i, j: (i, j)),
                   pl.BlockSpec((tm, LANE), lambda i, j: (i, 0))],
        out_shape=[jax.ShapeDtypeStruct((m, D_MAIN), F32),
                   jax.ShapeDtypeStruct((m, LANE), F32)],
        scratch_shapes=[pltpu.VMEM((tm, D_MODEL), BF16)],
        compiler_params=_params("parallel", "arbitrary"),
        name="in_proj",
    )(x, mod, mod, g_pre, w_all, w_dt)


def _lower_bound(logits):
    e = jnp.exp(logits - jnp.max(logits, axis=0, keepdims=True))
    return e[0:1] / jnp.sum(e, axis=0, keepdims=True)


def _hgrn_scores(q, fl, lb, row, a_row, a_col):
    c_len = CHUNK_A
    n_sub = c_len // SUB_A
    half = 0.5 * (1.0 - lb)
    f = (lb + half) + half * jnp.tanh(0.5 * fl)
    k = 1.0 - f
    qs = _silu(q)
    b = _cumsum_rows(jnp.log(f), c_len, row)
    k_raw = [k[j * SUB_A:(j + 1) * SUB_A, :].astype(BF16) for j in range(n_sub)]
    k_acc = []
    blocks, q_dec = [], []
    for i in range(n_sub):
        lo = i * SUB_A
        start = b[lo - 1:lo, :] if i > 0 else jnp.zeros((1, HA_D), F32)
        end = b[lo + SUB_A - 1:lo + SUB_A, :]
        b_i = b[lo:lo + SUB_A, :]
        k_i = k[lo:lo + SUB_A, :]
        qg = qs[lo:lo + SUB_A, :] * jnp.exp(b_i - start)
        k_diag = k_i * jnp.exp(start - b_i)
        rhs = jnp.concatenate([p.astype(BF16) for p in k_acc] + [k_diag.astype(BF16)] + k_raw[i + 1:],
                              axis=0)
        a_i = lax.dot_general(qg.astype(BF16), rhs, (((1,), (1,)), ((), ())),
                              preferred_element_type=F32)
        blocks.append(jnp.where(a_col <= a_row + lo, a_i, 0.0).astype(BF16))
        q_dec.append((qg * jnp.exp(start)).astype(BF16))
        sub_decay = jnp.exp(end - start)
        k_acc = [p * sub_decay for p in k_acc] + [k_i * jnp.exp(end - b_i)]
    scores = jnp.concatenate(blocks, axis=0)
    k_end = jnp.concatenate([p.astype(BF16) for p in k_acc], axis=0)
    chunk_decay = jnp.exp(b[c_len - 1:c_len, :])
    return scores, jnp.concatenate(q_dec, axis=0), k_end, chunk_decay


def _hgrn_apply(pend, v, g, nw, st):
    scores, q_dec, k_end, chunk_decay = pend
    vb = v.astype(BF16)
    o = (jnp.dot(scores, vb, preferred_element_type=F32)
         + lax.dot_general(q_dec, st.astype(BF16), (((1,), (1,)), ((), ())),
                           preferred_element_type=F32))
    st = chunk_decay * st + lax.dot_general(vb, k_end, (((0,), (0,)), ((), ())),
                                            preferred_element_type=F32)
    return (_rms(o) * nw * _silu(g)).astype(BF16), st


def _hgrn_prompt_kernel(q_ref, f_ref, i_ref, g_ref, lbl_ref, nw_ref, o_ref, s_ref, *, n_chunks, n_heads):
    c_len = CHUNK_A
    lb = _lower_bound(lbl_ref[...])
    nw = nw_ref[...]
    row = lax.broadcasted_iota(jnp.int32, (c_len, HA_D), 0)
    a_row = lax.broadcasted_iota(jnp.int32, (SUB_A, c_len), 0)
    a_col = lax.broadcasted_iota(jnp.int32, (SUB_A, c_len), 1)
    heads = [slice(h * HA_D, (h + 1) * HA_D) for h in range(n_heads)]

    def rows(c):
        return pl.ds(pl.multiple_of(c * c_len, c_len), c_len)

    def scores_of(c):
        sl = rows(c)
        return tuple(_hgrn_scores(q_ref[sl, hs], f_ref[sl, hs], lb[:, hs], row, a_row, a_col)
                     for hs in heads)

    def apply_to(c, pends, sts):
        sl = rows(c)
        new = []
        for hs, pend, st in zip(heads, pends, sts):
            out, st = _hgrn_apply(pend, i_ref[sl, hs], g_ref[sl, hs], nw[:, hs], st)
            o_ref[sl, hs] = out
            new.append(st)
        return tuple(new)

    def step(c, carry):
        pends, sts = carry
        sts = apply_to(c - 1, pends, sts)
        return scores_of(c), sts

    zeros = tuple(jnp.zeros((HA_D, HA_D), F32) for _ in range(n_heads))
    pends, sts = lax.fori_loop(1, n_chunks, step, (scores_of(0), zeros))
    sts = apply_to(n_chunks - 1, pends, sts)
    for h in range(n_heads):
        s_ref[h] = sts[h].T


HGRN_HEADS_PER_STEP = 4


def _hgrn_prompt(proj, lb_logits, hgrn_norm, n_seq, t):
    m = proj.shape[0]
    nh = HGRN_HEADS_PER_STEP
    w = nh * HA_D
    col = lambda off: pl.BlockSpec((t, w), lambda b, h: (b, off // w + h))
    return pl.pallas_call(
        functools.partial(_hgrn_prompt_kernel, n_chunks=t // CHUNK_A, n_heads=nh),
        grid=(n_seq, H_A // nh),
        in_specs=[col(OFF_Q), col(OFF_F), col(OFF_I), col(OFF_G),
                  pl.BlockSpec((lb_logits.shape[0], w), lambda b, h: (0, h)),
                  pl.BlockSpec((1, w), lambda b, h: (0, h))],
        out_specs=[pl.BlockSpec((t, w), lambda b, h: (b, h)),
                   pl.BlockSpec((None, nh, HA_D, HA_D), lambda b, h: (b, h, 0, 0))],
        out_shape=[jax.ShapeDtypeStruct((m, D_A), BF16),
                   jax.ShapeDtypeStruct((n_seq, H_A, HA_D, HA_D), F32)],
        compiler_params=_params("parallel", "parallel"),
        name="hgrn_prompt",
    )(proj, proj, proj, proj, lb_logits, hgrn_norm)


def _conv_chunk(tail, u, w, bias, n):
    full = jnp.concatenate([tail, u], axis=0)
    out = bias + w[CONV_W - 1:CONV_W, :] * u
    for i in range(CONV_W - 1):
        shifted = pltpu.roll(full, CONV_W - 1 - i, 0)[SUBLANE:SUBLANE + n, :]
        out = out + w[i:i + 1, :] * shifted
    return _silu(out)


def _ssd_prompt_kernel_old(z_ref, x_ref, b_ref, c_ref, dtr_ref,
                       wx_ref, wb_ref, wc_ref, bx_ref, bb_ref, bc_ref,
                       dtbr_ref, alr_ref, dsk_ref, nw_ref,
                       o_ref, h_ref, row_scr, col_scr, *, n_chunks):
    n = CHUNK_B
    wx, wb, wc = wx_ref[...], wb_ref[...], wc_ref[...]
    bx, bb, bc = bx_ref[...], bb_ref[...], bc_ref[...]
    nw = nw_ref[...]
    tri = (lax.broadcasted_iota(jnp.int32, (n, n), 0)
           >= lax.broadcasted_iota(jnp.int32, (n, n), 1))
    lane_head = _idiv(lax.broadcasted_iota(jnp.int32, (1, GW), 1), B_P)
    row_head = _idiv(lax.broadcasted_iota(jnp.int32, (GW, 1), 0), B_P)
    dsk = dsk_ref[...]
    dskip = jnp.zeros((1, GW), F32)
    for j in range(HPG):
        dskip = jnp.where(lane_head == j, dsk[:, j:j + 1], dskip)

    nr = n_chunks * SUBLANE
    dt_all = _softplus(dtr_ref[...] + dtbr_ref[...])
    cum_all = _cumsum_lanes(dt_all * -jnp.exp(alr_ref[...]), n,
                            lax.broadcasted_iota(jnp.int32, (nr, n), 1))
    is_dt = (lax.broadcasted_iota(jnp.int32, (nr, n), 0) & (SUBLANE - 1)) < HPG
    both = jnp.where(is_dt, dt_all, cum_all)
    for c in range(n_chunks):
        m8 = both[c * SUBLANE:(c + 1) * SUBLANE, :]
        row_scr[c] = m8
        col_scr[c] = jnp.concatenate([m8, jnp.zeros((n - SUBLANE, n), F32)], axis=0).T

    def rows(c):
        return pl.ds(pl.multiple_of(c * n, n), n)

    def scan_free(c):
        sl = rows(c)
        if isinstance(c, int) and c == 0:
            tail = lambda ref: jnp.zeros((SUBLANE, ref.shape[1]), F32)
        else:
            prev = pl.ds(pl.multiple_of(c * n - SUBLANE, SUBLANE), SUBLANE)
            tail = lambda ref: ref[prev, :]
        xc = _conv_chunk(tail(x_ref), x_ref[sl, :], wx, bx, n)
        bcv = _conv_chunk(tail(b_ref), b_ref[sl, :], wb, bb, n)
        ccv = _conv_chunk(tail(c_ref), c_ref[sl, :], wc, bc, n)
        r8 = row_scr[c]
        cols = col_scr[c]
        g = _dot_nt(ccv, bcv)
        y = dskip * xc
        e_cum = jnp.zeros((n, GW), F32)
        w_end = jnp.zeros((n, GW), F32)
        for j in range(HPG):
            cc = cols[:, HPG + j:HPG + j + 1]
            seg = cc - r8[HPG + j:HPG + j + 1, :]
            s = g * jnp.exp(jnp.where(tri, seg, -jnp.inf)) * r8[j:j + 1, :]
            y = y + _dot(s, jnp.where(lane_head == j, xc, 0.0))
            e_cum = jnp.where(lane_head == j, jnp.exp(cc), e_cum)
            c_end = r8[HPG + j:HPG + j + 1, n - 1:n]
            w_end = jnp.where(lane_head == j, jnp.exp(c_end - cc) * cols[:, j:j + 1], w_end)
        return y, e_cum, ccv.astype(BF16), (xc * w_end).astype(BF16), bcv.astype(BF16), r8

    def apply_state(c, pend, h):
        y, e_cum, ccb, xwb, bcb, r8 = pend
        sl = rows(c)
        ch = lax.dot_general(ccb, h.astype(BF16), (((1,), (1,)), ((), ())),
                             preferred_element_type=F32)
        yz = (y + ch * e_cum) * _silu(z_ref[sl, :])
        o_ref[sl, :] = (_rms(yz) * nw).astype(o_ref.dtype)
        h_dec = jnp.zeros((GW, 1), F32)
        for j in range(HPG):
            h_dec = jnp.where(row_head == j, jnp.exp(r8[HPG + j:HPG + j + 1, n - 1:n]), h_dec)
        return h_dec * h + lax.dot_general(xwb, bcb, (((0,), (0,)), ((), ())),
                                           preferred_element_type=F32)

    def step(c, carry):
        pend, h = carry
        h = apply_state(c - 1, pend, h)
        return scan_free(c), h

    pend, h = lax.fori_loop(1, n_chunks, step, (scan_free(0), jnp.zeros((GW, B_N), F32)))
    h = apply_state(n_chunks - 1, pend, h)
    h_ref[...] = h.reshape(HPG, B_P, B_N)


def _ssd_prompt_old(proj, dt_rows, conv_w, conv_b, dtb_rows, al_rows, dsk_col, ssd_norm, n_seq, t):
    m = proj.shape[0]
    n_chunks = t // CHUNK_B
    nr = n_chunks * SUBLANE
    wide = lambda off: pl.BlockSpec((t, GW), lambda b, g: (b, off // GW + g))
    narrow = lambda off: pl.BlockSpec((t, B_N), lambda b, g: (b, off // B_N + g))
    cw = lambda off, w: pl.BlockSpec((CONV_W, w), lambda b, g: (0, off // w + g))
    cbias = lambda off, w: pl.BlockSpec((1, w), lambda b, g: (0, off // w + g))
    pr = pl.BlockSpec((None, nr, 1), lambda b, g: (g, 0, 0))
    return pl.pallas_call(
        functools.partial(_ssd_prompt_kernel, n_chunks=n_chunks),
        grid=(n_seq, B_G),
        in_specs=[wide(OFF_Z), wide(OFF_X), narrow(OFF_B), narrow(OFF_C),
                  pl.BlockSpec((None, None, nr, CHUNK_B), lambda b, g: (g, b, 0, 0)),
                  cw(0, GW), cw(D_B, B_N), cw(D_B + B_G * B_N, B_N),
                  cbias(0, GW), cbias(D_B, B_N), cbias(D_B + B_G * B_N, B_N),
                  pr, pr,
                  pl.BlockSpec((None, 1, HPG), lambda b, g: (g, 0, 0)),
                  pl.BlockSpec((1, GW), lambda b, g: (0, g))],
        out_specs=[pl.BlockSpec((t, GW), lambda b, g: (b, g)),
                   pl.BlockSpec((None, HPG, B_P, B_N), lambda b, g: (b, g, 0, 0))],
        out_shape=[jax.ShapeDtypeStruct((m, D_B), BF16),
                   jax.ShapeDtypeStruct((n_seq, H_B, B_P, B_N), F32)],
        scratch_shapes=[pltpu.VMEM((n_chunks, SUBLANE, CHUNK_B), F32),
                        pltpu.VMEM((n_chunks, CHUNK_B, CHUNK_B), F32)],
        compiler_params=_params("parallel", "parallel"),
        name="ssd_prompt",
    )(proj, proj, proj, proj, dt_rows, conv_w, conv_w, conv_w, conv_b, conv_b, conv_b,
      dtb_rows, al_rows, dsk_col, ssd_norm)


SSD_GROUPS_PER_STEP = 2


def _ssd_prompt_kernel(z_ref, x_ref, b_ref, c_ref, dtr_ref,
                       wx_ref, wb_ref, wc_ref, bx_ref, bb_ref, bc_ref,
                       dtbr_ref, alr_ref, dsk_ref, nw_ref,
                       o_ref, h_ref, row_scr, *, n_chunks, n_groups):
    n = CHUNK_B
    assert n == LANE
    tri = (lax.broadcasted_iota(jnp.int32, (n, n), 0)
           >= lax.broadcasted_iota(jnp.int32, (n, n), 1))
    lane_head = _idiv(lax.broadcasted_iota(jnp.int32, (1, GW), 1), B_P)
    row_head = _idiv(lax.broadcasted_iota(jnp.int32, (GW, 1), 0), B_P)
    nr = n_chunks * SUBLANE
    lane_nr = lax.broadcasted_iota(jnp.int32, (nr, n), 1)
    is_dt = (lax.broadcasted_iota(jnp.int32, (nr, n), 0) & (SUBLANE - 1)) < HPG
    is_dt8 = lax.broadcasted_iota(jnp.int32, (SUBLANE, n), 0) < HPG
    n_src = 2 * SUBLANE
    sel_shape = (4 * n_src, HPG * n + 2 * GW)
    sel_k = lax.broadcasted_iota(jnp.int32, sel_shape, 0)
    sel_l = lax.broadcasted_iota(jnp.int32, sel_shape, 1)
    want = jnp.where(sel_l < HPG * n, HPG + _idiv(sel_l, n),
                     2 * HPG + _idiv(sel_l - HPG * n, B_P))
    bcast_sel = jnp.where(((sel_k & (n_src - 1)) == want) & (sel_k < 3 * n_src), 1.0, 0.0).astype(BF16)

    groups = []
    for gi in range(n_groups):
        xs = slice(gi * GW, (gi + 1) * GW)
        bs = slice(gi * B_N, (gi + 1) * B_N)
        dsk = dsk_ref[gi]
        dskip = jnp.zeros((1, GW), F32)
        for j in range(HPG):
            dskip = jnp.where(lane_head == j, dsk[:, j:j + 1], dskip)
        dt_all = _softplus(dtr_ref[gi] + dtbr_ref[gi])
        cum_all = _cumsum_lanes(dt_all * -jnp.exp(alr_ref[gi]), n, lane_nr)
        both = jnp.where(is_dt, dt_all, cum_all)
        row_scr[gi] = both.reshape(n_chunks, SUBLANE, n)
        groups.append((gi, xs, bs, dskip, wx_ref[:, xs], wb_ref[:, bs], wc_ref[:, bs],
                       bx_ref[:, xs], bb_ref[:, bs], bc_ref[:, bs], nw_ref[:, xs]))

    def rows(c):
        return pl.ds(pl.multiple_of(c * n, n), n)

    def scan_free(c, grp):
        gi, xs, bs, dskip, wx, wb, wc, bx, bb, bc, _ = grp
        sl = rows(c)
        if isinstance(c, int) and c == 0:
            tail = lambda ref, lanes: jnp.zeros((SUBLANE, lanes.stop - lanes.start), F32)
        else:
            prev = pl.ds(pl.multiple_of(c * n - SUBLANE, SUBLANE), SUBLANE)
            tail = lambda ref, lanes: ref[prev, lanes]
        xc = _conv_chunk(tail(x_ref, xs), x_ref[sl, xs], wx, bx, n)
        bcv = _conv_chunk(tail(b_ref, bs), b_ref[sl, bs], wb, bb, n)
        ccv = _conv_chunk(tail(c_ref, bs), c_ref[sl, bs], wc, bc, n)
        r8 = row_scr[gi, c]
        swapped = pltpu.roll(r8, HPG, 0)
        cum8 = jnp.where(is_dt8, swapped, r8)
        dt8 = jnp.where(is_dt8, r8, swapped)
        e8 = jnp.where(is_dt8, jnp.exp(cum8), jnp.exp(cum8[:, n - 1:n] - cum8) * dt8)
        both = jnp.concatenate([r8, e8], axis=0)
        hi = both.astype(BF16).astype(F32)
        mid = (both - hi).astype(BF16).astype(F32)
        lo = both - hi - mid
        terms = jnp.concatenate([hi, mid, lo, jnp.zeros_like(hi)], axis=0).astype(BF16)
        cols = lax.dot_general(terms, bcast_sel, (((0,), (0,)), ((), ())),
                               preferred_element_type=F32)
        e_cum = cols[:, HPG * n:HPG * n + GW]
        w_end = cols[:, HPG * n + GW:]
        g = _dot_nt(ccv, bcv)
        xb = xc.astype(BF16)
        s_all, x_all = [], []
        for j in range(HPG):
            cc = cols[:, j * n:(j + 1) * n]
            seg = cc - r8[HPG + j:HPG + j + 1, :]
            s = g * jnp.exp(jnp.where(tri, seg, -jnp.inf)) * r8[j:j + 1, :]
            s_all.append(s.astype(BF16))
            x_all.append(jnp.where(lane_head == j, xb, jnp.zeros_like(xb)))
        y = dskip * xc + jnp.dot(jnp.concatenate(s_all, axis=1), jnp.concatenate(x_all, axis=0),
                                 preferred_element_type=F32)
        return y, e_cum, ccv.astype(BF16), (xc * w_end).astype(BF16), bcv.astype(BF16), r8

    def apply_state(c, grp, pend, h):
        xs, nw = grp[1], grp[-1]
        y, e_cum, ccb, xwb, bcb, r8 = pend
        sl = rows(c)
        ch = lax.dot_general(ccb, h.astype(BF16), (((1,), (1,)), ((), ())),
                             preferred_element_type=F32)
        yz = (y + ch * e_cum) * _silu(z_ref[sl, xs])
        o_ref[sl, xs] = (_rms(yz) * nw).astype(o_ref.dtype)
        h_dec = jnp.zeros((GW, 1), F32)
        for j in range(HPG):
            h_dec = jnp.where(row_head == j, jnp.exp(r8[HPG + j:HPG + j + 1, n - 1:n]), h_dec)
        return h_dec * h + lax.dot_general(xwb, bcb, (((0,), (0,)), ((), ())),
                                           preferred_element_type=F32)

    def step(c, carry):
        pends, hs = carry
        hs = tuple(apply_state(c - 1, grp, pend, h) for grp, pend, h in zip(groups, pends, hs))
        return tuple(scan_free(c, grp) for grp in groups), hs

    init = (tuple(scan_free(0, grp) for grp in groups),
            tuple(jnp.zeros((GW, B_N), F32) for _ in groups))
    pends, hs = lax.fori_loop(1, n_chunks, step, init)
    for gi, (grp, pend, h) in enumerate(zip(groups, pends, hs)):
        h = apply_state(n_chunks - 1, grp, pend, h)
        h_ref[gi * HPG:(gi + 1) * HPG] = h.reshape(HPG, B_P, B_N)


def _ssd_prompt(proj, dt_rows, conv_w, conv_b, dtb_rows, al_rows, dsk_col, ssd_norm, n_seq, t):
    m = proj.shape[0]
    n_chunks = t // CHUNK_B
    nr = n_chunks * SUBLANE
    ng = SSD_GROUPS_PER_STEP
    xw, bw = ng * GW, ng * B_N
    wide = lambda off: pl.BlockSpec((t, xw), lambda b, g: (b, off // xw + g))
    narrow = lambda off: pl.BlockSpec((t, bw), lambda b, g: (b, off // bw + g))
    cw = lambda off, w: pl.BlockSpec((CONV_W, w), lambda b, g: (0, off // w + g))
    cbias = lambda off, w: pl.BlockSpec((1, w), lambda b, g: (0, off // w + g))
    pr = pl.BlockSpec((ng, nr, 1), lambda b, g: (g, 0, 0))
    return pl.pallas_call(
        functools.partial(_ssd_prompt_kernel, n_chunks=n_chunks, n_groups=ng),
        grid=(n_seq, B_G // ng),
        in_specs=[wide(OFF_Z), wide(OFF_X), narrow(OFF_B), narrow(OFF_C),
                  pl.BlockSpec((ng, None, nr, CHUNK_B), lambda b, g: (g, b, 0, 0)),
                  cw(0, xw), cw(D_B, bw), cw(D_B + B_G * B_N, bw),
                  cbias(0, xw), cbias(D_B, bw), cbias(D_B + B_G * B_N, bw),
                  pr, pr,
                  pl.BlockSpec((ng, 1, HPG), lambda b, g: (g, 0, 0)),
                  pl.BlockSpec((1, xw), lambda b, g: (0, g))],
        out_specs=[pl.BlockSpec((t, xw), lambda b, g: (b, g)),
                   pl.BlockSpec((None, ng * HPG, B_P, B_N), lambda b, g: (b, g, 0, 0))],
        out_shape=[jax.ShapeDtypeStruct((m, D_B), BF16),
                   jax.ShapeDtypeStruct((n_seq, H_B, B_P, B_N), F32)],
        scratch_shapes=[pltpu.VMEM((ng, n_chunks, SUBLANE, CHUNK_B), F32)],
        compiler_params=_params("parallel", "parallel"),
        name="ssd_prompt",
    )(proj, proj, proj, proj, dt_rows, conv_w, conv_w, conv_w, conv_b, conv_b, conv_b,
      dtb_rows, al_rows, dsk_col, ssd_norm)


def _hgrn_step_kernel(qt_ref, ft_ref, v_ref, g_ref, lblt_ref, nw_ref, s_ref, o_ref, so_ref, *, bt):
    lbt = _lower_bound_t(lblt_ref[...])
    nw = nw_ref[...]

    def seq(b, carry):
        ft = lbt + (1.0 - lbt) * _sigmoid(ft_ref[b])
        kt = 1.0 - ft
        qt = _silu(qt_ref[b])
        v = v_ref[b]
        rows = []
        for h in range(H_A):
            s_new = ft[:, h:h + 1] * s_ref[b, h] + kt[:, h:h + 1] * v[h:h + 1, :]
            so_ref[b, h] = s_new
            rows.append(jnp.sum(qt[:, h:h + 1] * s_new, axis=0, keepdims=True))
        o = jnp.concatenate(rows, axis=0)
        o_ref[b] = (_rms(o) * nw * _silu(g_ref[b])).astype(o_ref.dtype)
        return carry

    lax.fori_loop(0, bt, seq, 0)


def _lower_bound_t(logits):
    e = jnp.exp(logits - jnp.max(logits, axis=0, keepdims=True))
    return e[0] / jnp.sum(e, axis=0)


def _hgrn_step(qt, ft, v, g, lbl_t, nw, state, bt):
    nb = qt.shape[0]
    tr = pl.BlockSpec((bt, HA_D, H_A), lambda i: (i, 0, 0))
    rw = pl.BlockSpec((bt, H_A, HA_D), lambda i: (i, 0, 0))
    st = pl.BlockSpec((bt, H_A, HA_D, HA_D), lambda i: (i, 0, 0, 0))
    return pl.pallas_call(
        functools.partial(_hgrn_step_kernel, bt=bt),
        grid=(nb // bt,),
        in_specs=[tr, tr, rw, rw,
                  pl.BlockSpec(lbl_t.shape, lambda i: (0, 0, 0)),
                  pl.BlockSpec((H_A, HA_D), lambda i: (0, 0)),
                  st],
        out_specs=[rw, st],
        out_shape=[jax.ShapeDtypeStruct((nb, H_A, HA_D), BF16),
                   jax.ShapeDtypeStruct(state.shape, F32)],
        compiler_params=_params("parallel"),
        name="hgrn_step",
    )(qt, ft, v, g, lbl_t, nw, state)


def _conv_step(buf_ref, b, u, w, bias):
    out = bias + w[CONV_W - 1] * u
    for i in range(CONV_W - 1):
        out = out + w[i] * buf_ref[b, i]
    return _silu(out)


def _ssd_step_kernel(xt_ref, xr_ref, bcr_ref, z_ref, dt_ref,
                     cxt_ref, cxr_ref, cbc_ref,
                     wxt_ref, wxr_ref, wbc_ref, bxt_ref, bxr_ref, bbc_ref,
                     dtb_ref, al_ref, dskr_ref, nw_ref, h_ref,
                     o_ref, ho_ref, y_scr, xc_scr, *, bt):
    wxt, wxr, wbc = wxt_ref[...], wxr_ref[...], wbc_ref[...]
    bxt, bxr, bbc = bxt_ref[...], bxr_ref[...], bbc_ref[...]
    a = -jnp.exp(al_ref[...])
    dtb = dtb_ref[...]

    def seq(b, carry):
        xt = _conv_step(cxt_ref, b, xt_ref[b], wxt, bxt)
        xc_scr[pl.ds(b, 1), :] = _conv_step(cxr_ref, b, xr_ref[b], wxr, bxr)
        bcv = _conv_step(cbc_ref, b, bcr_ref[b], wbc, bbc)
        dt = _softplus(dt_ref[b] + dtb)
        d_a = jnp.exp(dt * a)
        xdt = xt * dt
        ys = []
        for g in range(B_G):
            b_row = bcv[g:g + 1, :]
            c_row = bcv[B_G + g:B_G + g + 1, :]
            hs = []
            for j in range(HPG):
                hh = g * HPG + j
                h_new = d_a[:, hh:hh + 1] * h_ref[b, hh] + xdt[:, hh:hh + 1] * b_row
                ho_ref[b, hh] = h_new
                hs.append(h_new)
            hg = jnp.concatenate(hs, axis=0)
            c8 = jnp.broadcast_to(c_row, (SUBLANE, B_N))
            ys.append(_dot_nt(c8, hg)[0:1, :])
        y_scr[pl.ds(b, 1), :] = jnp.concatenate(ys, axis=1)
        return carry

    lax.fori_loop(0, bt, seq, 0)
    xc = xc_scr[...]
    yz = (y_scr[...] + dskr_ref[...] * xc) * _silu(z_ref[...])
    nw = nw_ref[...]
    for g in range(B_G):
        sl = slice(g * GW, (g + 1) * GW)
        o_ref[:, sl] = (_rms(yz[:, sl]) * nw[:, sl]).astype(o_ref.dtype)


def _ssd_step(xt, xr, bcr, z, dt, cxt, cxr, cbc, wxt, wxr, wbc, bxt, bxr, bbc,
              dtb, al, dskr, nw, state, bt):
    nb = xt.shape[0]
    full = lambda a: pl.BlockSpec(a.shape, lambda i: (0,) * a.ndim)
    lead = lambda a: pl.BlockSpec((bt,) + a.shape[1:], lambda i: (i,) + (0,) * (a.ndim - 1))
    return pl.pallas_call(
        functools.partial(_ssd_step_kernel, bt=bt),
        grid=(nb // bt,),
        in_specs=[lead(xt), lead(xr), lead(bcr), lead(z), lead(dt),
                  lead(cxt), lead(cxr), lead(cbc),
                  full(wxt), full(wxr), full(wbc), full(bxt), full(bxr), full(bbc),
                  full(dtb), full(al), full(dskr), full(nw), lead(state)],
        out_specs=[pl.BlockSpec((bt, D_B), lambda i: (i, 0)), lead(state)],
        out_shape=[jax.ShapeDtypeStruct((nb, D_B), F32),
                   jax.ShapeDtypeStruct(state.shape, F32)],
        scratch_shapes=[pltpu.VMEM((bt, D_B), F32), pltpu.VMEM((bt, D_B), F32)],
        compiler_params=_params("parallel"),
        name="ssd_step",
    )(xt, xr, bcr, z, dt, cxt, cxr, cbc, wxt, wxr, wbc, bxt, bxr, bbc, dtb, al, dskr, nw, state)


def _outproj_kernel(oa_ref, ob_ref, wa_ref, wb_ref, x_ref, gt_ref, gp_ref, y_ref):
    mix = _dot(oa_ref[...], wa_ref[...]) + _dot(ob_ref[...], wb_ref[...])
    y_ref[...] = x_ref[...] + gt_ref[...] * (_rms(mix) * gp_ref[...])


def _outproj(o_a, o_b, w_out, x, mod, per_row, rows_per_seq, g_post, tm):
    m = x.shape[0]
    row = lambda w: pl.BlockSpec((tm, w), lambda i, j: (i, 0))
    return pl.pallas_call(
        _outproj_kernel,
        grid=(m // tm, 1),
        in_specs=[row(D_A), row(D_B),
                  pl.BlockSpec((D_A, D_MODEL), lambda i, j: (0, 0)),
                  pl.BlockSpec((D_B, D_MODEL), lambda i, j: (1, 0)),
                  row(D_MODEL),
                  _mod_spec(per_row, rows_per_seq, tm, 2),
                  pl.BlockSpec((1, D_MODEL), lambda i, j: (0, 0))],
        out_specs=row(D_MODEL),
        out_shape=jax.ShapeDtypeStruct((m, D_MODEL), F32),
        compiler_params=_params("parallel", "arbitrary"),
        name="out_proj",
    )(o_a, o_b, w_out, w_out, x, mod, g_post)


def _mlp_kernel(x_ref, sh_ref, sc_ref, gt_ref, gpre_ref, gpost_ref, wu_ref, wd_ref, y_ref, h_scr):
    f = pl.program_id(1)

    @pl.when(f == 0)
    def _():
        h = _rms(x_ref[...]) * gpre_ref[...] * (1.0 + sc_ref[...]) + sh_ref[...]
        h_scr[...] = h.astype(BF16)
        y_ref[...] = jnp.zeros_like(y_ref)

    u = jnp.maximum(jnp.dot(h_scr[...], wu_ref[...].astype(BF16), preferred_element_type=F32), 0.0)
    y_ref[...] += jnp.dot((u * u).astype(BF16), wd_ref[...].astype(BF16), preferred_element_type=F32)

    @pl.when(f == pl.num_programs(1) - 1)
    def _():
        y_ref[...] = x_ref[...] + gt_ref[...] * (_rms(y_ref[...]) * gpost_ref[...])


def _mlp(x, mod, per_row, rows_per_seq, g_pre, g_post, w_up, w_down, tm, tf):
    m = x.shape[0]
    row = pl.BlockSpec((tm, D_MODEL), lambda i, j: (i, 0))
    vec = pl.BlockSpec((1, D_MODEL), lambda i, j: (0, 0))
    return pl.pallas_call(
        _mlp_kernel,
        grid=(m // tm, D_FF // tf),
        in_specs=[pl.BlockSpec((tm, D_MODEL), lambda i, j: (i, 0), pipeline_mode=pl.Buffered(1)),
                  _mod_spec(per_row, rows_per_seq, tm, 3),
                  _mod_spec(per_row, rows_per_seq, tm, 4),
                  _mod_spec(per_row, rows_per_seq, tm, 5),
                  vec, vec,
                  pl.BlockSpec((D_MODEL, tf), lambda i, j: (0, j)),
                  pl.BlockSpec((tf, D_MODEL), lambda i, j: (j, 0))],
        out_specs=row,
        out_shape=jax.ShapeDtypeStruct((m, D_MODEL), F32),
        scratch_shapes=[pltpu.VMEM((tm, D_MODEL), BF16)],
        compiler_params=_params("parallel", "arbitrary"),
        name="mlp",
    )(x, mod, mod, mod, g_pre, g_post, w_up, w_down)


def _pick_tile(m, target):
    t = min(m, target)
    while m % t:
        t //= 2
    return t


def kernel(x_prompt, x_sample, c_prompt, c_sample, state_hgrn, state_ssm, state_conv, w_ada, b_ada, norm_pre_mix, norm_post_mix, norm_pre_mlp, norm_post_mlp, w_in, hgrn_lb_logits, hgrn_norm, conv_w, conv_b, dt_bias, a_log, d_skip, ssd_norm, w_out, w_up, w_down):
    n_seq, t, _ = x_prompt.shape
    n_dec = x_sample.shape[0]
    assert x_sample.shape[1] == 1 and t % CHUNK_A == 0 and t % CHUNK_B == 0
    assert w_ada.shape[0] == 1, "one layer"
    l = 0

    w_dt = jnp.pad(w_in[l][:, D_MAIN:], ((0, 0), (0, LANE - H_B)))
    w_out_b = w_out[l].astype(BF16)
    g_pre_mix, g_post_mix = norm_pre_mix[l][None], norm_post_mix[l][None]
    g_pre_mlp, g_post_mlp = norm_pre_mlp[l][None], norm_post_mlp[l][None]
    cw, cb = conv_w[l], conv_b[l][None]
    hn = hgrn_norm[l][None]
    sn = ssd_norm[l][None]

    n_c = n_seq + n_dec
    pad = (-n_c) % SUBLANE
    c_all = jnp.concatenate([c_prompt, c_sample, jnp.zeros((pad, D_MODEL), F32)], axis=0)
    mod = _ada(c_all, w_ada[l], b_ada[l][None])
    mod_p = mod[:n_seq].reshape(n_seq, 1, N_MOD * D_MODEL)
    mod_s = mod[n_seq:n_c]

    xp = x_prompt.reshape(n_seq * t, D_MODEL)
    proj_p, dt_p = _inproj(xp, mod_p, False, t, g_pre_mix, w_in[l], w_dt, _pick_tile(t, 2048), 512)
    o_a, s_hgrn_p = _hgrn_prompt(proj_p, hgrn_lb_logits, hn, n_seq, t)

    n_chunks = t // CHUNK_B
    rep = SUBLANE // HPG
    dt_rows = dt_p[:, :H_B].reshape(n_seq, n_chunks, CHUNK_B, B_G, 1, HPG).transpose(3, 0, 1, 4, 5, 2)
    dt_rows = jnp.broadcast_to(dt_rows, (B_G, n_seq, n_chunks, rep, HPG, CHUNK_B))
    dt_rows = dt_rows.reshape(B_G, n_seq, n_chunks * SUBLANE, CHUNK_B)
    prow = lambda p: jnp.tile(p.reshape(B_G, 1, HPG), (1, n_chunks * rep, 1)).reshape(B_G, n_chunks * SUBLANE, 1)
    o_b, s_ssm_p = _ssd_prompt(proj_p, dt_rows, cw, cb, prow(dt_bias[l]), prow(a_log[l]),
                               d_skip[l].reshape(B_G, 1, HPG), sn, n_seq, t)
    x1 = _outproj(o_a, o_b, w_out_b, xp, mod_p, False, t, g_post_mix, _pick_tile(t, 256))
    y_p = _mlp(x1, mod_p, False, t, g_pre_mlp, g_post_mlp, w_up[l], w_down[l], _pick_tile(t, 1024), 512)
    conv_p = proj_p.reshape(n_seq, t, D_MAIN)[:, t - (CONV_W - 1):, OFF_X:]

    xs = x_sample.reshape(n_dec, D_MODEL)
    proj_s, dt_s = _inproj(xs, mod_s, True, 1, g_pre_mix, w_in[l], w_dt, n_dec, 2048)
    bt = _pick_tile(n_dec, SUBLANE)
    to_t = lambda a, h, d: a.reshape(a.shape[:-1] + (h, d)).swapaxes(-1, -2)
    qt = to_t(proj_s[:, OFF_Q:OFF_Q + D_A], H_A, HA_D)
    ft = to_t(proj_s[:, OFF_F:OFF_F + D_A], H_A, HA_D)
    v_s = proj_s[:, OFF_I:OFF_I + D_A].reshape(n_dec, H_A, HA_D)
    g_s = proj_s[:, OFF_G:OFF_G + D_A].reshape(n_dec, H_A, HA_D)
    lbl_t = to_t(hgrn_lb_logits, H_A, HA_D)
    o_a_s, s_hgrn_s = _hgrn_step(qt, ft, v_s, g_s, lbl_t, hn.reshape(H_A, HA_D), state_hgrn[l], bt)

    xbc_s = proj_s[:, OFF_X:]
    cst = state_conv[l]
    xt = to_t(xbc_s[:, :D_B], H_B, B_P)
    xr = xbc_s[:, None, :D_B]
    bcr = xbc_s[:, D_B:].reshape(n_dec, 2 * B_G, B_N)
    cxt = to_t(cst[:, :, :D_B], H_B, B_P)
    cxr = cst[:, :, None, :D_B]
    cbc = cst[:, :, D_B:].reshape(n_dec, CONV_W - 1, 2 * B_G, B_N)
    wxt = to_t(cw[:, :D_B], H_B, B_P)
    wxr = cw[:, None, :D_B]
    wbc = cw[:, D_B:].reshape(CONV_W, 2 * B_G, B_N)
    bxt = to_t(cb[:, :D_B], H_B, B_P)[0]
    bxr = cb[:, :D_B]
    bbc = cb[0, D_B:].reshape(2 * B_G, B_N)
    o_b_s, s_ssm_s = _ssd_step(
        xt, xr, bcr, proj_s[:, OFF_Z:OFF_Z + D_B], dt_s[:, None, :H_B],
        cxt, cxr, cbc, wxt, wxr, wbc, bxt, bxr, bbc,
        dt_bias[l][None], a_log[l][None], jnp.repeat(d_skip[l], B_P)[None], sn,
        state_ssm[l], bt)
    x1_s = _outproj(o_a_s.reshape(n_dec, D_A), o_b_s, w_out_b, xs, mod_s, True, 1, g_post_mix, n_dec)
    y_s = _mlp(x1_s, mod_s, True, 1, g_pre_mlp, g_post_mlp, w_up[l], w_down[l], n_dec, 1024)
    conv_s = jnp.concatenate([cst[:, 1:], xbc_s[:, None, :]], axis=1)

    return (y_p.reshape(n_seq, t, D_MODEL), y_s.reshape(n_dec, 1, D_MODEL),
            s_hgrn_p[None], s_ssm_p[None], conv_p[None],
            s_hgrn_s[None], s_ssm_s[None], conv_s[None])
```

```python
import functools

import jax
import jax.numpy as jnp
from jax import lax
from jax.experimental import pallas as pl
from jax.experimental.pallas import tpu as pltpu

F32 = jnp.float32
BF16 = jnp.bfloat16

D_MODEL = 2048
D_A = 2048
HA_D = 128
H_A = D_A // HA_D
D_B = 2048
B_P = 64
H_B = D_B // B_P
B_G = 8
HPG = H_B // B_G
GW = HPG * B_P
B_N = 128
CONV_W = 4
CONV_DIM = D_B + 2 * B_G * B_N
D_MAIN = 4 * D_A + D_B + CONV_DIM
D_FF = 4 * D_MODEL
N_MOD = 6
EPS = 1e-6

OFF_Q, OFF_F, OFF_I, OFF_G = 0, D_A, 2 * D_A, 3 * D_A
OFF_Z = 4 * D_A
OFF_X = OFF_Z + D_B
OFF_B = OFF_X + D_B
OFF_C = OFF_B + B_G * B_N

LANE = 128
SUBLANE = 8
VMEM_LIMIT = 56 * 1024 * 1024

CHUNK_A = 128
SUB_A = 16
CHUNK_B = 128


def _sigmoid(x):
    return 0.5 * jnp.tanh(0.5 * x) + 0.5


def _silu(x):
    hx = 0.5 * x
    return hx * jnp.tanh(hx) + hx


def _softplus(x):
    return jnp.maximum(x, 0.0) + jnp.log1p(jnp.exp(-jnp.abs(x)))


def _idiv(x, d):
    assert d & (d - 1) == 0
    return jnp.right_shift(x, d.bit_length() - 1)


def _rms(x):
    return x * lax.rsqrt(jnp.mean(x * x, axis=-1, keepdims=True) + EPS)


def _cumsum_rows(x, n, row):
    s = 1
    while s < n:
        x = x + jnp.where(row >= s, pltpu.roll(x, s, 0), 0.0)
        s *= 2
    return x


def _cumsum_lanes(x, n, lane):
    s = 1
    while s < n:
        x = x + jnp.where(lane >= s, pltpu.roll(x, s, 1), 0.0)
        s *= 2
    return x


def _dot(a, b):
    return jnp.dot(a.astype(BF16), b.astype(BF16), preferred_element_type=F32)


def _dot_nt(a, b):
    return lax.dot_general(a.astype(BF16), b.astype(BF16), (((1,), (1,)), ((), ())),
                           preferred_element_type=F32)


def _dot_tn(a, b):
    return lax.dot_general(a.astype(BF16), b.astype(BF16), (((0,), (0,)), ((), ())),
                           preferred_element_type=F32)


def _params(*sem):
    return pltpu.CompilerParams(dimension_semantics=sem, vmem_limit_bytes=VMEM_LIMIT)


def _ada_kernel(c_ref, w_ref, b_ref, o_ref):
    a = _silu(c_ref[...])
    o_ref[...] = _dot(a, w_ref[...]) + b_ref[...]


def _ada(c_all, w_ada, b_ada):
    m = c_all.shape[0]
    n = w_ada.shape[1]
    tn = 1024
    return pl.pallas_call(
        _ada_kernel,
        grid=(n // tn,),
        in_specs=[pl.BlockSpec((m, D_MODEL), lambda j: (0, 0)),
                  pl.BlockSpec((D_MODEL, tn), lambda j: (0, j)),
                  pl.BlockSpec((1, tn), lambda j: (0, j))],
        out_specs=pl.BlockSpec((m, tn), lambda j: (0, j)),
        out_shape=jax.ShapeDtypeStruct((m, n), F32),
        compiler_params=_params("parallel"),
        name="ada_mod",
    )(c_all, w_ada, b_ada)


def _mod_spec(per_row, rows_per_seq, tm, col):
    if per_row:
        return pl.BlockSpec((tm, D_MODEL), lambda i, j: (i, col))
    tiles_per_seq = rows_per_seq // tm
    return pl.BlockSpec((None, 1, D_MODEL), lambda i, j: (i // tiles_per_seq, 0, col))


def _inproj_kernel(x_ref, sh_ref, sc_ref, g_ref, w_ref, wdt_ref, o_ref, odt_ref, h_scr):
    @pl.when(pl.program_id(1) == 0)
    def _():
        h = _rms(x_ref[...]) * g_ref[...] * (1.0 + sc_ref[...]) + sh_ref[...]
        hb = h.astype(BF16)
        h_scr[...] = hb
        odt_ref[...] = _dot_nt(hb, wdt_ref[...])

    o_ref[...] = _dot_nt(h_scr[...], w_ref[...])


def _inproj(x, mod, per_row, rows_per_seq, g_pre, w_t, tm, tn):
    m = x.shape[0]
    assert D_MAIN % tn == 0 and D_MAIN % H_B == 0
    return pl.pallas_call(
        _inproj_kernel,
        grid=(m // tm, D_MAIN // tn),
        in_specs=[pl.BlockSpec((tm, D_MODEL), lambda i, j: (i, 0), pipeline_mode=pl.Buffered(1)),
                  _mod_spec(per_row, rows_per_seq, tm, 0),
                  _mod_spec(per_row, rows_per_seq, tm, 1),
                  pl.BlockSpec((1, D_MODEL), lambda i, j: (0, 0)),
                  pl.BlockSpec((tn, D_MODEL), lambda i, j: (j, 0)),
                  pl.BlockSpec((H_B, D_MODEL), lambda i, j: (D_MAIN // H_B, 0))],
        out_specs=[pl.BlockSpec((tm, tn), lambda i, j: (i, j)),
                   pl.BlockSpec((tm, H_B), lambda i, j: (i, 0))],
        out_shape=[jax.ShapeDtypeStruct((m, D_MAIN), F32),
                   jax.ShapeDtypeStruct((m, H_B), F32)],
        scratch_shapes=[pltpu.VMEM((tm, D_MODEL), BF16)],
        compiler_params=_params("parallel", "arbitrary"),
        name="in_proj",
    )(x, mod, mod, g_pre, w_t, w_t)


def _lower_bound(logits):
    e = jnp.exp(logits - jnp.max(logits, axis=0, keepdims=True))
    return e[0:1] / jnp.sum(e, axis=0, keepdims=True)


def _hgrn_scores(q, fl, lb, row, a_row, a_col):
    c_len = CHUNK_A
    n_sub = c_len // SUB_A
    half = 0.5 * (1.0 - lb)
    f = (lb + half) + half * jnp.tanh(0.5 * fl)
    k = 1.0 - f
    qs = _silu(q)
    b = _cumsum_rows(jnp.log(f), c_len, row)
    k_raw = [k[j * SUB_A:(j + 1) * SUB_A, :].astype(BF16) for j in range(n_sub)]
    k_acc = []
    blocks, q_dec = [], []
    for i in range(n_sub):
        lo = i * SUB_A
        start = b[lo - 1:lo, :] if i > 0 else jnp.zeros((1, HA_D), F32)
        end = b[lo + SUB_A - 1:lo + SUB_A, :]
        b_i = b[lo:lo + SUB_A, :]
        k_i = k[lo:lo + SUB_A, :]
        qg = qs[lo:lo + SUB_A, :] * jnp.exp(b_i - start)
        k_diag = k_i * jnp.exp(start - b_i)
        hi = lo + SUB_A
        rhs = jnp.concatenate([p.astype(BF16) for p in k_acc] + [k_diag.astype(BF16)], axis=0)
        a_i = lax.dot_general(qg.astype(BF16), rhs, (((1,), (1,)), ((), ())),
                              preferred_element_type=F32)
        causal = (lax.broadcasted_iota(jnp.int32, (SUB_A, hi), 1)
                  <= lax.broadcasted_iota(jnp.int32, (SUB_A, hi), 0) + lo)
        a_i = jnp.where(causal, a_i, 0.0)
        blocks.append(jnp.pad(a_i, ((0, 0), (0, c_len - hi))).astype(BF16))
        q_dec.append((qg * jnp.exp(start)).astype(BF16))
        sub_decay = jnp.exp(end - start)
        k_acc = [p * sub_decay for p in k_acc] + [k_i * jnp.exp(end - b_i)]
    scores = jnp.concatenate(blocks, axis=0)
    k_end = jnp.concatenate([p.astype(BF16) for p in k_acc], axis=0)
    chunk_decay = jnp.exp(b[c_len - 1:c_len, :])
    return scores, jnp.concatenate(q_dec, axis=0), k_end, chunk_decay


def _hgrn_apply(pend, v, g, nw, st):
    scores, q_dec, k_end, chunk_decay = pend
    vb = v.astype(BF16)
    o = (jnp.dot(scores, vb, preferred_element_type=F32)
         + lax.dot_general(q_dec, st.astype(BF16), (((1,), (1,)), ((), ())),
                           preferred_element_type=F32))
    st = chunk_decay * st + lax.dot_general(vb, k_end, (((0,), (0,)), ((), ())),
                                            preferred_element_type=F32)
    return (_rms(o) * nw * _silu(g)).astype(BF16), st


def _hgrn_prompt_kernel(q_ref, f_ref, i_ref, g_ref, lbl_ref, nw_ref, o_ref, s_ref, *, n_chunks, n_heads):
    c_len = CHUNK_A
    lb = _lower_bound(lbl_ref[...])
    nw = nw_ref[...]
    row = lax.broadcasted_iota(jnp.int32, (c_len, HA_D), 0)
    a_row = lax.broadcasted_iota(jnp.int32, (SUB_A, c_len), 0)
    a_col = lax.broadcasted_iota(jnp.int32, (SUB_A, c_len), 1)
    heads = [slice(h * HA_D, (h + 1) * HA_D) for h in range(n_heads)]

    def rows(c):
        return pl.ds(pl.multiple_of(c * c_len, c_len), c_len)

    def scores_of(c):
        sl = rows(c)
        return tuple(_hgrn_scores(q_ref[sl, hs], f_ref[sl, hs], lb[:, hs], row, a_row, a_col)
                     for hs in heads)

    def apply_to(c, pends, sts):
        sl = rows(c)
        new = []
        for hs, pend, st in zip(heads, pends, sts):
            out, st = _hgrn_apply(pend, i_ref[sl, hs], g_ref[sl, hs], nw[:, hs], st)
            o_ref[sl, hs] = out
            new.append(st)
        return tuple(new)

    def step(c, carry):
        pends, sts = carry
        sts = apply_to(c - 1, pends, sts)
        return scores_of(c), sts

    zeros = tuple(jnp.zeros((HA_D, HA_D), F32) for _ in range(n_heads))
    pends, sts = lax.fori_loop(1, n_chunks, step, (scores_of(0), zeros))
    sts = apply_to(n_chunks - 1, pends, sts)
    for h in range(n_heads):
        s_ref[h] = sts[h].T


HGRN_HEADS_PER_STEP = 4


def _hgrn_prompt(proj, lb_logits, hgrn_norm, n_seq, t):
    m = proj.shape[0]
    nh = HGRN_HEADS_PER_STEP
    w = nh * HA_D
    col = lambda off: pl.BlockSpec((t, w), lambda b, h: (b, off // w + h))
    return pl.pallas_call(
        functools.partial(_hgrn_prompt_kernel, n_chunks=t // CHUNK_A, n_heads=nh),
        grid=(n_seq, H_A // nh),
        in_specs=[col(OFF_Q), col(OFF_F), col(OFF_I), col(OFF_G),
                  pl.BlockSpec((lb_logits.shape[0], w), lambda b, h: (0, h)),
                  pl.BlockSpec((1, w), lambda b, h: (0, h))],
        out_specs=[pl.BlockSpec((t, w), lambda b, h: (b, h)),
                   pl.BlockSpec((None, nh, HA_D, HA_D), lambda b, h: (b, h, 0, 0))],
        out_shape=[jax.ShapeDtypeStruct((m, D_A), BF16),
                   jax.ShapeDtypeStruct((n_seq, H_A, HA_D, HA_D), F32)],
        compiler_params=_params("parallel", "parallel"),
        name="hgrn_prompt",
    )(proj, proj, proj, proj, lb_logits, hgrn_norm)


def _conv_chunk(tail, u, w, bias, n):
    full = jnp.concatenate([tail, u], axis=0)
    out = bias + w[CONV_W - 1:CONV_W, :] * u
    for i in range(CONV_W - 1):
        shifted = pltpu.roll(full, CONV_W - 1 - i, 0)[SUBLANE:SUBLANE + n, :]
        out = out + w[i:i + 1, :] * shifted
    return _silu(out)


def _ssd_prompt_kernel_old(z_ref, x_ref, b_ref, c_ref, dtr_ref,
                       wx_ref, wb_ref, wc_ref, bx_ref, bb_ref, bc_ref,
                       dtbr_ref, alr_ref, dsk_ref, nw_ref,
                       o_ref, h_ref, row_scr, col_scr, *, n_chunks):
    n = CHUNK_B
    wx, wb, wc = wx_ref[...], wb_ref[...], wc_ref[...]
    bx, bb, bc = bx_ref[...], bb_ref[...], bc_ref[...]
    nw = nw_ref[...]
    tri = (lax.broadcasted_iota(jnp.int32, (n, n), 0)
           >= lax.broadcasted_iota(jnp.int32, (n, n), 1))
    lane_head = _idiv(lax.broadcasted_iota(jnp.int32, (1, GW), 1), B_P)
    row_head = _idiv(lax.broadcasted_iota(jnp.int32, (GW, 1), 0), B_P)
    dsk = dsk_ref[...]
    dskip = jnp.zeros((1, GW), F32)
    for j in range(HPG):
        dskip = jnp.where(lane_head == j, dsk[:, j:j + 1], dskip)

    nr = n_chunks * SUBLANE
    dt_all = _softplus(dtr_ref[...] + dtbr_ref[...])
    cum_all = _cumsum_lanes(dt_all * -jnp.exp(alr_ref[...]), n,
                            lax.broadcasted_iota(jnp.int32, (nr, n), 1))
    is_dt = (lax.broadcasted_iota(jnp.int32, (nr, n), 0) & (SUBLANE - 1)) < HPG
    both = jnp.where(is_dt, dt_all, cum_all)
    for c in range(n_chunks):
        m8 = both[c * SUBLANE:(c + 1) * SUBLANE, :]
        row_scr[c] = m8
        col_scr[c] = jnp.concatenate([m8, jnp.zeros((n - SUBLANE, n), F32)], axis=0).T

    def rows(c):
        return pl.ds(pl.multiple_of(c * n, n), n)

    def scan_free(c):
        sl = rows(c)
        if isinstance(c, int) and c == 0:
            tail = lambda ref: jnp.zeros((SUBLANE, ref.shape[1]), F32)
        else:
            prev = pl.ds(pl.multiple_of(c * n - SUBLANE, SUBLANE), SUBLANE)
            tail = lambda ref: ref[prev, :]
        xc = _conv_chunk(tail(x_ref), x_ref[sl, :], wx, bx, n)
        bcv = _conv_chunk(tail(b_ref), b_ref[sl, :], wb, bb, n)
        ccv = _conv_chunk(tail(c_ref), c_ref[sl, :], wc, bc, n)
        r8 = row_scr[c]
        cols = col_scr[c]
        g = _dot_nt(ccv, bcv)
        y = dskip * xc
        e_cum = jnp.zeros((n, GW), F32)
        w_end = jnp.zeros((n, GW), F32)
        for j in range(HPG):
            cc = cols[:, HPG + j:HPG + j + 1]
            seg = cc - r8[HPG + j:HPG + j + 1, :]
            s = g * jnp.exp(jnp.where(tri, seg, -jnp.inf)) * r8[j:j + 1, :]
            y = y + _dot(s, jnp.where(lane_head == j, xc, 0.0))
            e_cum = jnp.where(lane_head == j, jnp.exp(cc), e_cum)
            c_end = r8[HPG + j:HPG + j + 1, n - 1:n]
            w_end = jnp.where(lane_head == j, jnp.exp(c_end - cc) * cols[:, j:j + 1], w_end)
        return y, e_cum, ccv.astype(BF16), (xc * w_end).astype(BF16), bcv.astype(BF16), r8

    def apply_state(c, pend, h):
        y, e_cum, ccb, xwb, bcb, r8 = pend
        sl = rows(c)
        ch = lax.dot_general(ccb, h.astype(BF16), (((1,), (1,)), ((), ())),
                             preferred_element_type=F32)
        yz = (y + ch * e_cum) * _silu(z_ref[sl, :])
        o_ref[sl, :] = (_rms(yz) * nw).astype(o_ref.dtype)
        h_dec = jnp.zeros((GW, 1), F32)
        for j in range(HPG):
            h_dec = jnp.where(row_head == j, jnp.exp(r8[HPG + j:HPG + j + 1, n - 1:n]), h_dec)
        return h_dec * h + lax.dot_general(xwb, bcb, (((0,), (0,)), ((), ())),
                                           preferred_element_type=F32)

    def step(c, carry):
        pend, h = carry
        h = apply_state(c - 1, pend, h)
        return scan_free(c), h

    pend, h = lax.fori_loop(1, n_chunks, step, (scan_free(0), jnp.zeros((GW, B_N), F32)))
    h = apply_state(n_chunks - 1, pend, h)
    h_ref[...] = h.reshape(HPG, B_P, B_N)


def _ssd_prompt_old(proj, dt_rows, conv_w, conv_b, dtb_rows, al_rows, dsk_col, ssd_norm, n_seq, t):
    m = proj.shape[0]
    n_chunks = t // CHUNK_B
    nr = n_chunks * SUBLANE
    wide = lambda off: pl.BlockSpec((t, GW), lambda b, g: (b, off // GW + g))
    narrow = lambda off: pl.BlockSpec((t, B_N), lambda b, g: (b, off // B_N + g))
    cw = lambda off, w: pl.BlockSpec((CONV_W, w), lambda b, g: (0, off // w + g))
    cbias = lambda off, w: pl.BlockSpec((1, w), lambda b, g: (0, off // w + g))
    pr = pl.BlockSpec((None, nr, 1), lambda b, g: (g, 0, 0))
    return pl.pallas_call(
        functools.partial(_ssd_prompt_kernel, n_chunks=n_chunks),
        grid=(n_seq, B_G),
        in_specs=[wide(OFF_Z), wide(OFF_X), narrow(OFF_B), narrow(OFF_C),
                  pl.BlockSpec((None, None, nr, CHUNK_B), lambda b, g: (g, b, 0, 0)),
                  cw(0, GW), cw(D_B, B_N), cw(D_B + B_G * B_N, B_N),
                  cbias(0, GW), cbias(D_B, B_N), cbias(D_B + B_G * B_N, B_N),
                  pr, pr,
                  pl.BlockSpec((None, 1, HPG), lambda b, g: (g, 0, 0)),
                  pl.BlockSpec((1, GW), lambda b, g: (0, g))],
        out_specs=[pl.BlockSpec((t, GW), lambda b, g: (b, g)),
                   pl.BlockSpec((None, HPG, B_P, B_N), lambda b, g: (b, g, 0, 0))],
        out_shape=[jax.ShapeDtypeStruct((m, D_B), BF16),
                   jax.ShapeDtypeStruct((n_seq, H_B, B_P, B_N), F32)],
        scratch_shapes=[pltpu.VMEM((n_chunks, SUBLANE, CHUNK_B), F32),
                        pltpu.VMEM((n_chunks, CHUNK_B, CHUNK_B), F32)],
        compiler_params=_params("parallel", "parallel"),
        name="ssd_prompt",
    )(proj, proj, proj, proj, dt_rows, conv_w, conv_w, conv_w, conv_b, conv_b, conv_b,
      dtb_rows, al_rows, dsk_col, ssd_norm)


SSD_GROUPS_PER_STEP = 2


def _ssd_prompt_kernel(z_ref, x_ref, b_ref, c_ref, dtr_ref,
                       wx_ref, wb_ref, wc_ref, bx_ref, bb_ref, bc_ref,
                       dtbr_ref, alr_ref, dsk_ref, nw_ref,
                       o_ref, h_ref, row_scr, *, n_chunks, n_groups):
    n = CHUNK_B
    assert n == LANE
    tri = (lax.broadcasted_iota(jnp.int32, (n, n), 0)
           >= lax.broadcasted_iota(jnp.int32, (n, n), 1))
    lane_head = _idiv(lax.broadcasted_iota(jnp.int32, (1, GW), 1), B_P)
    row_head = _idiv(lax.broadcasted_iota(jnp.int32, (GW, 1), 0), B_P)
    nr = n_chunks * SUBLANE
    lane_nr = lax.broadcasted_iota(jnp.int32, (nr, n), 1)
    is_dt = (lax.broadcasted_iota(jnp.int32, (nr, n), 0) & (SUBLANE - 1)) < HPG
    is_dt8 = lax.broadcasted_iota(jnp.int32, (SUBLANE, n), 0) < HPG
    n_src = 2 * SUBLANE
    sel_shape = (4 * n_src, HPG * n + 2 * GW)
    sel_k = lax.broadcasted_iota(jnp.int32, sel_shape, 0)
    sel_l = lax.broadcasted_iota(jnp.int32, sel_shape, 1)
    want = jnp.where(sel_l < HPG * n, HPG + _idiv(sel_l, n),
                     2 * HPG + _idiv(sel_l - HPG * n, B_P))
    bcast_sel = jnp.where(((sel_k & (n_src - 1)) == want) & (sel_k < 3 * n_src), 1.0, 0.0).astype(BF16)

    groups = []
    for gi in range(n_groups):
        xs = slice(gi * GW, (gi + 1) * GW)
        bs = slice(gi * B_N, (gi + 1) * B_N)
        dsk = dsk_ref[gi]
        dskip = jnp.zeros((1, GW), F32)
        for j in range(HPG):
            dskip = jnp.where(lane_head == j, dsk[:, j:j + 1], dskip)
        dt_all = _softplus(dtr_ref[gi] + dtbr_ref[gi])
        cum_all = _cumsum_lanes(dt_all * -jnp.exp(alr_ref[gi]), n, lane_nr)
        both = jnp.where(is_dt, dt_all, cum_all)
        row_scr[gi] = both.reshape(n_chunks, SUBLANE, n)
        groups.append((gi, xs, bs, dskip, wx_ref[:, xs], wb_ref[:, bs], wc_ref[:, bs],
                       bx_ref[:, xs], bb_ref[:, bs], bc_ref[:, bs], nw_ref[:, xs]))

    def rows(c):
        return pl.ds(pl.multiple_of(c * n, n), n)

    def scan_free(c, grp):
        gi, xs, bs, dskip, wx, wb, wc, bx, bb, bc, _ = grp
        sl = rows(c)
        if isinstance(c, int) and c == 0:
            tail = lambda ref, lanes: jnp.zeros((SUBLANE, lanes.stop - lanes.start), F32)
        else:
            prev = pl.ds(pl.multiple_of(c * n - SUBLANE, SUBLANE), SUBLANE)
            tail = lambda ref, lanes: ref[prev, lanes]
        xc = _conv_chunk(tail(x_ref, xs), x_ref[sl, xs], wx, bx, n)
        bcv = _conv_chunk(tail(b_ref, bs), b_ref[sl, bs], wb, bb, n)
        ccv = _conv_chunk(tail(c_ref, bs), c_ref[sl, bs], wc, bc, n)
        r8 = row_scr[gi, c]
        swapped = pltpu.roll(r8, HPG, 0)
        cum8 = jnp.where(is_dt8, swapped, r8)
        dt8 = jnp.where(is_dt8, r8, swapped)
        e8 = jnp.where(is_dt8, jnp.exp(cum8), jnp.exp(cum8[:, n - 1:n] - cum8) * dt8)
        both = jnp.concatenate([r8, e8], axis=0)
        hi = both.astype(BF16).astype(F32)
        mid = (both - hi).astype(BF16).astype(F32)
        lo = both - hi - mid
        terms = jnp.concatenate([hi, mid, lo, jnp.zeros_like(hi)], axis=0).astype(BF16)
        cols = lax.dot_general(terms, bcast_sel, (((0,), (0,)), ((), ())),
                               preferred_element_type=F32)
        e_cum = cols[:, HPG * n:HPG * n + GW]
        w_end = cols[:, HPG * n + GW:]
        g = _dot_nt(ccv, bcv)
        xb = xc.astype(BF16)
        s_all, x_all = [], []
        for j in range(HPG):
            cc = cols[:, j * n:(j + 1) * n]
            seg = cc - r8[HPG + j:HPG + j + 1, :]
            s = g * jnp.exp(jnp.where(tri, seg, -jnp.inf)) * r8[j:j + 1, :]
            s_all.append(s.astype(BF16))
            x_all.append(jnp.where(lane_head == j, xb, jnp.zeros_like(xb)))
        y = dskip * xc + jnp.dot(jnp.concatenate(s_all, axis=1), jnp.concatenate(x_all, axis=0),
                                 preferred_element_type=F32)
        return y, e_cum, ccv.astype(BF16), (xc * w_end).astype(BF16), bcv.astype(BF16), r8

    def apply_state(c, grp, pend, h):
        xs, nw = grp[1], grp[-1]
        y, e_cum, ccb, xwb, bcb, r8 = pend
        sl = rows(c)
        ch = lax.dot_general(ccb, h.astype(BF16), (((1,), (1,)), ((), ())),
                             preferred_element_type=F32)
        yz = (y + ch * e_cum) * _silu(z_ref[sl, xs])
        o_ref[sl, xs] = (_rms(yz) * nw).astype(o_ref.dtype)
        h_dec = jnp.zeros((GW, 1), F32)
        for j in range(HPG):
            h_dec = jnp.where(row_head == j, jnp.exp(r8[HPG + j:HPG + j + 1, n - 1:n]), h_dec)
        return h_dec * h + lax.dot_general(xwb, bcb, (((0,), (0,)), ((), ())),
                                           preferred_element_type=F32)

    def step(c, carry):
        pends, hs = carry
        hs = tuple(apply_state(c - 1, grp, pend, h) for grp, pend, h in zip(groups, pends, hs))
        return tuple(scan_free(c, grp) for grp in groups), hs

    init = (tuple(scan_free(0, grp) for grp in groups),
            tuple(jnp.zeros((GW, B_N), F32) for _ in groups))
    pends, hs = lax.fori_loop(1, n_chunks, step, init)
    for gi, (grp, pend, h) in enumerate(zip(groups, pends, hs)):
        h = apply_state(n_chunks - 1, grp, pend, h)
        h_ref[gi * HPG:(gi + 1) * HPG] = h.reshape(HPG, B_P, B_N)


def _ssd_prompt(proj, dt_rows, conv_w, conv_b, dtb_rows, al_rows, dsk_col, ssd_norm, n_seq, t):
    m = proj.shape[0]
    n_chunks = t // CHUNK_B
    nr = n_chunks * SUBLANE
    ng = SSD_GROUPS_PER_STEP
    xw, bw = ng * GW, ng * B_N
    wide = lambda off: pl.BlockSpec((t, xw), lambda b, g: (b, off // xw + g))
    narrow = lambda off: pl.BlockSpec((t, bw), lambda b, g: (b, off // bw + g))
    cw = lambda off, w: pl.BlockSpec((CONV_W, w), lambda b, g: (0, off // w + g))
    cbias = lambda off, w: pl.BlockSpec((1, w), lambda b, g: (0, off // w + g))
    pr = pl.BlockSpec((ng, nr, 1), lambda b, g: (g, 0, 0))
    return pl.pallas_call(
        functools.partial(_ssd_prompt_kernel, n_chunks=n_chunks, n_groups=ng),
        grid=(n_seq, B_G // ng),
        in_specs=[wide(OFF_Z), wide(OFF_X), narrow(OFF_B), narrow(OFF_C),
                  pl.BlockSpec((ng, None, nr, CHUNK_B), lambda b, g: (g, b, 0, 0)),
                  cw(0, xw), cw(D_B, bw), cw(D_B + B_G * B_N, bw),
                  cbias(0, xw), cbias(D_B, bw), cbias(D_B + B_G * B_N, bw),
                  pr, pr,
                  pl.BlockSpec((ng, 1, HPG), lambda b, g: (g, 0, 0)),
                  pl.BlockSpec((1, xw), lambda b, g: (0, g))],
        out_specs=[pl.BlockSpec((t, xw), lambda b, g: (b, g)),
                   pl.BlockSpec((None, ng * HPG, B_P, B_N), lambda b, g: (b, g, 0, 0))],
        out_shape=[jax.ShapeDtypeStruct((m, D_B), BF16),
                   jax.ShapeDtypeStruct((n_seq, H_B, B_P, B_N), F32)],
        scratch_shapes=[pltpu.VMEM((ng, n_chunks, SUBLANE, CHUNK_B), F32)],
        compiler_params=_params("parallel", "parallel"),
        name="ssd_prompt",
    )(proj, proj, proj, proj, dt_rows, conv_w, conv_w, conv_w, conv_b, conv_b, conv_b,
      dtb_rows, al_rows, dsk_col, ssd_norm)


def _hgrn_step_kernel(qt_ref, ft_ref, v_ref, g_ref, lblt_ref, nw_ref, s_ref, o_ref, so_ref, *, bt):
    lbt = _lower_bound_t(lblt_ref[...])
    nw = nw_ref[...]

    def seq(b, carry):
        ft = lbt + (1.0 - lbt) * _sigmoid(ft_ref[b])
        kt = 1.0 - ft
        qt = _silu(qt_ref[b])
        v = v_ref[b]
        rows = []
        for h in range(H_A):
            s_new = ft[:, h:h + 1] * s_ref[b, h] + kt[:, h:h + 1] * v[h:h + 1, :]
            so_ref[b, h] = s_new
            rows.append(jnp.sum(qt[:, h:h + 1] * s_new, axis=0, keepdims=True))
        o = jnp.concatenate(rows, axis=0)
        o_ref[b] = (_rms(o) * nw * _silu(g_ref[b])).astype(o_ref.dtype)
        return carry

    lax.fori_loop(0, bt, seq, 0)


def _lower_bound_t(logits):
    e = jnp.exp(logits - jnp.max(logits, axis=0, keepdims=True))
    return e[0] / jnp.sum(e, axis=0)


def _hgrn_step(qt, ft, v, g, lbl_t, nw, state, bt):
    nb = qt.shape[0]
    tr = pl.BlockSpec((bt, HA_D, H_A), lambda i: (i, 0, 0))
    rw = pl.BlockSpec((bt, H_A, HA_D), lambda i: (i, 0, 0))
    st = pl.BlockSpec((bt, H_A, HA_D, HA_D), lambda i: (i, 0, 0, 0))
    return pl.pallas_call(
        functools.partial(_hgrn_step_kernel, bt=bt),
        grid=(nb // bt,),
        in_specs=[tr, tr, rw, rw,
                  pl.BlockSpec(lbl_t.shape, lambda i: (0, 0, 0)),
                  pl.BlockSpec((H_A, HA_D), lambda i: (0, 0)),
                  st],
        out_specs=[rw, st],
        out_shape=[jax.ShapeDtypeStruct((nb, H_A, HA_D), BF16),
                   jax.ShapeDtypeStruct(state.shape, F32)],
        compiler_params=_params("parallel"),
        name="hgrn_step",
    )(qt, ft, v, g, lbl_t, nw, state)


def _conv_step(buf_ref, b, u, w, bias):
    out = bias + w[CONV_W - 1] * u
    for i in range(CONV_W - 1):
        out = out + w[i] * buf_ref[b, i]
    return _silu(out)


def _ssd_step_kernel(xt_ref, xr_ref, bcr_ref, z_ref, dt_ref,
                     cxt_ref, cxr_ref, cbc_ref,
                     wxt_ref, wxr_ref, wbc_ref, bxt_ref, bxr_ref, bbc_ref,
                     dtb_ref, al_ref, dskr_ref, nw_ref, h_ref,
                     o_ref, ho_ref, y_scr, xc_scr, *, bt):
    wxt, wxr, wbc = wxt_ref[...], wxr_ref[...], wbc_ref[...]
    bxt, bxr, bbc = bxt_ref[...], bxr_ref[...], bbc_ref[...]
    a = -jnp.exp(al_ref[...])
    dtb = dtb_ref[...]

    def seq(b, carry):
        xt = _conv_step(cxt_ref, b, xt_ref[b], wxt, bxt)
        xc_scr[pl.ds(b, 1), :] = _conv_step(cxr_ref, b, xr_ref[b], wxr, bxr)
        bcv = _conv_step(cbc_ref, b, bcr_ref[b], wbc, bbc)
        dt = _softplus(dt_ref[b] + dtb)
        d_a = jnp.exp(dt * a)
        xdt = xt * dt
        ys = []
        for g in range(B_G):
            b_row = bcv[g:g + 1, :]
            c_row = bcv[B_G + g:B_G + g + 1, :]
            hs = []
            for j in range(HPG):
                hh = g * HPG + j
                h_new = d_a[:, hh:hh + 1] * h_ref[b, hh] + xdt[:, hh:hh + 1] * b_row
                ho_ref[b, hh] = h_new
                hs.append(h_new)
            hg = jnp.concatenate(hs, axis=0)
            c8 = jnp.broadcast_to(c_row, (SUBLANE, B_N))
            ys.append(_dot_nt(c8, hg)[0:1, :])
        y_scr[pl.ds(b, 1), :] = jnp.concatenate(ys, axis=1)
        return carry

    lax.fori_loop(0, bt, seq, 0)
    xc = xc_scr[...]
    yz = (y_scr[...] + dskr_ref[...] * xc) * _silu(z_ref[...])
    nw = nw_ref[...]
    for g in range(B_G):
        sl = slice(g * GW, (g + 1) * GW)
        o_ref[:, sl] = (_rms(yz[:, sl]) * nw[:, sl]).astype(o_ref.dtype)


def _ssd_step(xt, xr, bcr, z, dt, cxt, cxr, cbc, wxt, wxr, wbc, bxt, bxr, bbc,
              dtb, al, dskr, nw, state, bt):
    nb = xt.shape[0]
    full = lambda a: pl.BlockSpec(a.shape, lambda i: (0,) * a.ndim)
    lead = lambda a: pl.BlockSpec((bt,) + a.shape[1:], lambda i: (i,) + (0,) * (a.ndim - 1))
    return pl.pallas_call(
        functools.partial(_ssd_step_kernel, bt=bt),
        grid=(nb // bt,),
        in_specs=[lead(xt), lead(xr), lead(bcr), lead(z), lead(dt),
                  lead(cxt), lead(cxr), lead(cbc),
                  full(wxt), full(wxr), full(wbc), full(bxt), full(bxr), full(bbc),
                  full(dtb), full(al), full(dskr), full(nw), lead(state)],
        out_specs=[pl.BlockSpec((bt, D_B), lambda i: (i, 0)), lead(state)],
        out_shape=[jax.ShapeDtypeStruct((nb, D_B), F32),
                   jax.ShapeDtypeStruct(state.shape, F32)],
        scratch_shapes=[pltpu.VMEM((bt, D_B), F32), pltpu.VMEM((bt, D_B), F32)],
        compiler_params=_params("parallel"),
        name="ssd_step",
    )(xt, xr, bcr, z, dt, cxt, cxr, cbc, wxt, wxr, wbc, bxt, bxr, bbc, dtb, al, dskr, nw, state)


def _outproj_kernel(oa_ref, ob_ref, wa_ref, wb_ref, x_ref, gt_ref, gp_ref, y_ref):
    mix = _dot(oa_ref[...], wa_ref[...]) + _dot(ob_ref[...], wb_ref[...])
    y_ref[...] = x_ref[...] + gt_ref[...] * (_rms(mix) * gp_ref[...])


def _outproj(o_a, o_b, w_out, x, mod, per_row, rows_per_seq, g_post, tm):
    m = x.shape[0]
    row = lambda w: pl.BlockSpec((tm, w), lambda i, j: (i, 0))
    return pl.pallas_call(
        _outproj_kernel,
        grid=(m // tm, 1),
        in_specs=[row(D_A), row(D_B),
                  pl.BlockSpec((D_A, D_MODEL), lambda i, j: (0, 0)),
                  pl.BlockSpec((D_B, D_MODEL), lambda i, j: (1, 0)),
                  row(D_MODEL),
                  _mod_spec(per_row, rows_per_seq, tm, 2),
                  pl.BlockSpec((1, D_MODEL), lambda i, j: (0, 0))],
        out_specs=row(D_MODEL),
        out_shape=jax.ShapeDtypeStruct((m, D_MODEL), F32),
        compiler_params=_params("parallel", "arbitrary"),
        name="out_proj",
    )(o_a, o_b, w_out, w_out, x, mod, g_post)


def _mlp_kernel(x_ref, sh_ref, sc_ref, gt_ref, gpre_ref, gpost_ref, wu_ref, wd_ref, y_ref, h_scr):
    f = pl.program_id(1)

    @pl.when(f == 0)
    def _():
        h = _rms(x_ref[...]) * gpre_ref[...] * (1.0 + sc_ref[...]) + sh_ref[...]
        h_scr[...] = h.astype(BF16)
        y_ref[...] = jnp.zeros_like(y_ref)

    u = jnp.maximum(jnp.dot(h_scr[...], wu_ref[...].astype(BF16), preferred_element_type=F32), 0.0)
    y_ref[...] += jnp.dot((u * u).astype(BF16), wd_ref[...].astype(BF16), preferred_element_type=F32)

    @pl.when(f == pl.num_programs(1) - 1)
    def _():
        y_ref[...] = x_ref[...] + gt_ref[...] * (_rms(y_ref[...]) * gpost_ref[...])


def _mlp(x, mod, per_row, rows_per_seq, g_pre, g_post, w_up, w_down, tm, tf):
    m = x.shape[0]
    row = pl.BlockSpec((tm, D_MODEL), lambda i, j: (i, 0))
    vec = pl.BlockSpec((1, D_MODEL), lambda i, j: (0, 0))
    return pl.pallas_call(
        _mlp_kernel,
        grid=(m // tm, D_FF // tf),
        in_specs=[pl.BlockSpec((tm, D_MODEL), lambda i, j: (i, 0), pipeline_mode=pl.Buffered(1)),
                  _mod_spec(per_row, rows_per_seq, tm, 3),
                  _mod_spec(per_row, rows_per_seq, tm, 4),
                  _mod_spec(per_row, rows_per_seq, tm, 5),
                  vec, vec,
                  pl.BlockSpec((D_MODEL, tf), lambda i, j: (0, j)),
                  pl.BlockSpec((tf, D_MODEL), lambda i, j: (j, 0))],
        out_specs=row,
        out_shape=jax.ShapeDtypeStruct((m, D_MODEL), F32),
        scratch_shapes=[pltpu.VMEM((tm, D_MODEL), BF16)],
        compiler_params=_params("parallel", "arbitrary"),
        name="mlp",
    )(x, mod, mod, mod, g_pre, g_post, w_up, w_down)


def _pick_tile(m, target):
    t = min(m, target)
    while m % t:
        t //= 2
    return t


def kernel(x_prompt, x_sample, c_prompt, c_sample, state_hgrn, state_ssm, state_conv, w_ada, b_ada, norm_pre_mix, norm_post_mix, norm_pre_mlp, norm_post_mlp, w_in, hgrn_lb_logits, hgrn_norm, conv_w, conv_b, dt_bias, a_log, d_skip, ssd_norm, w_out, w_up, w_down):
    n_seq, t, _ = x_prompt.shape
    n_dec = x_sample.shape[0]
    assert x_sample.shape[1] == 1 and t % CHUNK_A == 0 and t % CHUNK_B == 0
    assert w_ada.shape[0] == 1, "one layer"
    l = 0

    w_in_t = w_in[l].T
    w_out_b = w_out[l].astype(BF16)
    g_pre_mix, g_post_mix = norm_pre_mix[l][None], norm_post_mix[l][None]
    g_pre_mlp, g_post_mlp = norm_pre_mlp[l][None], norm_post_mlp[l][None]
    cw, cb = conv_w[l], conv_b[l][None]
    hn = hgrn_norm[l][None]
    sn = ssd_norm[l][None]

    n_c = n_seq + n_dec
    pad = (-n_c) % SUBLANE
    c_all = jnp.concatenate([c_prompt, c_sample, jnp.zeros((pad, D_MODEL), F32)], axis=0)
    mod = _ada(c_all, w_ada[l], b_ada[l][None])
    mod_p = mod[:n_seq].reshape(n_seq, 1, N_MOD * D_MODEL)
    mod_s = mod[n_seq:n_c]

    xp = x_prompt.reshape(n_seq * t, D_MODEL)
    proj_p, dt_p = _inproj(xp, mod_p, False, t, g_pre_mix, w_in_t, _pick_tile(t, 2048), 512)
    o_a, s_hgrn_p = _hgrn_prompt(proj_p, hgrn_lb_logits, hn, n_seq, t)

    n_chunks = t // CHUNK_B
    rep = SUBLANE // HPG
    dt_rows = dt_p.reshape(n_seq, n_chunks, CHUNK_B, B_G, 1, HPG).transpose(3, 0, 1, 4, 5, 2)
    dt_rows = jnp.broadcast_to(dt_rows, (B_G, n_seq, n_chunks, rep, HPG, CHUNK_B))
    dt_rows = dt_rows.reshape(B_G, n_seq, n_chunks * SUBLANE, CHUNK_B)
    prow = lambda p: jnp.tile(p.reshape(B_G, 1, HPG), (1, n_chunks * rep, 1)).reshape(B_G, n_chunks * SUBLANE, 1)
    o_b, s_ssm_p = _ssd_prompt(proj_p, dt_rows, cw, cb, prow(dt_bias[l]), prow(a_log[l]),
                               d_skip[l].reshape(B_G, 1, HPG), sn, n_seq, t)
    x1 = _outproj(o_a, o_b, w_out_b, xp, mod_p, False, t, g_post_mix, _pick_tile(t, 256))
    y_p = _mlp(x1, mod_p, False, t, g_pre_mlp, g_post_mlp, w_up[l], w_down[l], _pick_tile(t, 1024), 512)
    conv_p = proj_p.reshape(n_seq, t, D_MAIN)[:, t - (CONV_W - 1):, OFF_X:]

    xs = x_sample.reshape(n_dec, D_MODEL)
    proj_s, dt_s = _inproj(xs, mod_s, True, 1, g_pre_mix, w_in_t, n_dec, 2048)
    bt = _pick_tile(n_dec, SUBLANE)
    to_t = lambda a, h, d: a.reshape(a.shape[:-1] + (h, d)).swapaxes(-1, -2)
    qt = to_t(proj_s[:, OFF_Q:OFF_Q + D_A], H_A, HA_D)
    ft = to_t(proj_s[:, OFF_F:OFF_F + D_A], H_A, HA_D)
    v_s = proj_s[:, OFF_I:OFF_I + D_A].reshape(n_dec, H_A, HA_D)
    g_s = proj_s[:, OFF_G:OFF_G + D_A].reshape(n_dec, H_A, HA_D)
    lbl_t = to_t(hgrn_lb_logits, H_A, HA_D)
    o_a_s, s_hgrn_s = _hgrn_step(qt, ft, v_s, g_s, lbl_t, hn.reshape(H_A, HA_D), state_hgrn[l], bt)

    xbc_s = proj_s[:, OFF_X:]
    cst = state_conv[l]
    xt = to_t(xbc_s[:, :D_B], H_B, B_P)
    xr = xbc_s[:, None, :D_B]
    bcr = xbc_s[:, D_B:].reshape(n_dec, 2 * B_G, B_N)
    cxt = to_t(cst[:, :, :D_B], H_B, B_P)
    cxr = cst[:, :, None, :D_B]
    cbc = cst[:, :, D_B:].reshape(n_dec, CONV_W - 1, 2 * B_G, B_N)
    wxt = to_t(cw[:, :D_B], H_B, B_P)
    wxr = cw[:, None, :D_B]
    wbc = cw[:, D_B:].reshape(CONV_W, 2 * B_G, B_N)
    bxt = to_t(cb[:, :D_B], H_B, B_P)[0]
    bxr = cb[:, :D_B]
    bbc = cb[0, D_B:].reshape(2 * B_G, B_N)
    o_b_s, s_ssm_s = _ssd_step(
        xt, xr, bcr, proj_s[:, OFF_Z:OFF_Z + D_B], dt_s[:, None, :],
        cxt, cxr, cbc, wxt, wxr, wbc, bxt, bxr, bbc,
        dt_bias[l][None], a_log[l][None], jnp.repeat(d_skip[l], B_P)[None], sn,
        state_ssm[l], bt)
    x1_s = _outproj(o_a_s.reshape(n_dec, D_A), o_b_s, w_out_b, xs, mod_s, True, 1, g_post_mix, n_dec)
    y_s = _mlp(x1_s, mod_s, True, 1, g_pre_mlp, g_post_mlp, w_up[l], w_down[l], n_dec, 1024)
    conv_s = jnp.concatenate([cst[:, 1:], xbc_s[:, None, :]], axis=1)

    return (y_p.reshape(n_seq, t, D_MODEL), y_s.reshape(n_dec, 1, D_MODEL),
            s_hgrn_p[None], s_ssm_p[None], conv_p[None],
            s_hgrn_s[None], s_ssm_s[None], conv_s[None])
```

```python
import functools

import jax
import jax.numpy as jnp
from jax import lax
from jax.experimental import pallas as pl
from jax.experimental.pallas import tpu as pltpu

F32 = jnp.float32
BF16 = jnp.bfloat16

D_MODEL = 2048
D_A = 2048
HA_D = 128
H_A = D_A // HA_D
D_B = 2048
B_P = 64
H_B = D_B // B_P
B_G = 8
HPG = H_B // B_G
GW = HPG * B_P
B_N = 128
CONV_W = 4
CONV_DIM = D_B + 2 * B_G * B_N
D_MAIN = 4 * D_A + D_B + CONV_DIM
D_FF = 4 * D_MODEL
N_MOD = 6
EPS = 1e-6

OFF_Q, OFF_F, OFF_I, OFF_G = 0, D_A, 2 * D_A, 3 * D_A
OFF_Z = 4 * D_A
OFF_X = OFF_Z + D_B
OFF_B = OFF_X + D_B
OFF_C = OFF_B + B_G * B_N

LANE = 128
SUBLANE = 8
VMEM_LIMIT = 56 * 1024 * 1024

CHUNK_A = 128
SUB_A = 16
CHUNK_B = 128


def _sigmoid(x):
    return 0.5 * jnp.tanh(0.5 * x) + 0.5


def _silu(x):
    hx = 0.5 * x
    return hx * jnp.tanh(hx) + hx


def _softplus(x):
    return jnp.maximum(x, 0.0) + jnp.log1p(jnp.exp(-jnp.abs(x)))


def _idiv(x, d):
    assert d & (d - 1) == 0
    return jnp.right_shift(x, d.bit_length() - 1)


def _rms(x):
    return x * lax.rsqrt(jnp.mean(x * x, axis=-1, keepdims=True) + EPS)


def _cumsum_rows(x, n, row):
    s = 1
    while s < n:
        x = x + jnp.where(row >= s, pltpu.roll(x, s, 0), 0.0)
        s *= 2
    return x


def _cumsum_lanes(x, n, lane):
    s = 1
    while s < n:
        x = x + jnp.where(lane >= s, pltpu.roll(x, s, 1), 0.0)
        s *= 2
    return x


def _dot(a, b):
    return jnp.dot(a.astype(BF16), b.astype(BF16), preferred_element_type=F32)


def _dot_nt(a, b):
    return lax.dot_general(a.astype(BF16), b.astype(BF16), (((1,), (1,)), ((), ())),
                           preferred_element_type=F32)


def _dot_tn(a, b):
    return lax.dot_general(a.astype(BF16), b.astype(BF16), (((0,), (0,)), ((), ())),
                           preferred_element_type=F32)


def _params(*sem):
    return pltpu.CompilerParams(dimension_semantics=sem, vmem_limit_bytes=VMEM_LIMIT)


def _ada_kernel(c_ref, w_ref, b_ref, o_ref):
    a = _silu(c_ref[...])
    o_ref[...] = _dot(a, w_ref[...]) + b_ref[...]


def _ada(c_all, w_ada, b_ada):
    m = c_all.shape[0]
    n = w_ada.shape[1]
    tn = 1024
    return pl.pallas_call(
        _ada_kernel,
        grid=(n // tn,),
        in_specs=[pl.BlockSpec((m, D_MODEL), lambda j: (0, 0)),
                  pl.BlockSpec((D_MODEL, tn), lambda j: (0, j)),
                  pl.BlockSpec((1, tn), lambda j: (0, j))],
        out_specs=pl.BlockSpec((m, tn), lambda j: (0, j)),
        out_shape=jax.ShapeDtypeStruct((m, n), F32),
        compiler_params=_params("parallel"),
        name="ada_mod",
    )(c_all, w_ada, b_ada)


def _mod_spec(per_row, rows_per_seq, tm, col):
    if per_row:
        return pl.BlockSpec((tm, D_MODEL), lambda i, j: (i, col))
    tiles_per_seq = rows_per_seq // tm
    return pl.BlockSpec((None, 1, D_MODEL), lambda i, j: (i // tiles_per_seq, 0, col))


def _inproj_kernel(x_ref, sh_ref, sc_ref, g_ref, w_ref, wdt_ref, o_ref, odt_ref, h_scr):
    @pl.when(pl.program_id(1) == 0)
    def _():
        h = _rms(x_ref[...]) * g_ref[...] * (1.0 + sc_ref[...]) + sh_ref[...]
        hb = h.astype(BF16)
        h_scr[...] = hb
        odt_ref[...] = _dot_nt(hb, wdt_ref[...])

    o_ref[...] = _dot_nt(h_scr[...], w_ref[...])


def _inproj(x, mod, per_row, rows_per_seq, g_pre, w_t, tm, tn):
    m = x.shape[0]
    assert D_MAIN % tn == 0 and D_MAIN % H_B == 0
    return pl.pallas_call(
        _inproj_kernel,
        grid=(m // tm, D_MAIN // tn),
        in_specs=[pl.BlockSpec((tm, D_MODEL), lambda i, j: (i, 0), pipeline_mode=pl.Buffered(1)),
                  _mod_spec(per_row, rows_per_seq, tm, 0),
                  _mod_spec(per_row, rows_per_seq, tm, 1),
                  pl.BlockSpec((1, D_MODEL), lambda i, j: (0, 0)),
                  pl.BlockSpec((tn, D_MODEL), lambda i, j: (j, 0)),
                  pl.BlockSpec((H_B, D_MODEL), lambda i, j: (D_MAIN // H_B, 0))],
        out_specs=[pl.BlockSpec((tm, tn), lambda i, j: (i, j)),
                   pl.BlockSpec((tm, H_B), lambda i, j: (i, 0))],
        out_shape=[jax.ShapeDtypeStruct((m, D_MAIN), F32),
                   jax.ShapeDtypeStruct((m, H_B), F32)],
        scratch_shapes=[pltpu.VMEM((tm, D_MODEL), BF16)],
        compiler_params=_params("parallel", "arbitrary"),
        name="in_proj",
    )(x, mod, mod, g_pre, w_t, w_t)


def _lower_bound(logits):
    e = jnp.exp(logits - jnp.max(logits, axis=0, keepdims=True))
    return e[0:1] / jnp.sum(e, axis=0, keepdims=True)


def _hgrn_scores(q, fl, lb, row, a_row, a_col):
    c_len = CHUNK_A
    n_sub = c_len // SUB_A
    half = 0.5 * (1.0 - lb)
    f = (lb + half) + half * jnp.tanh(0.5 * fl)
    k = 1.0 - f
    qs = _silu(q)
    b = _cumsum_rows(jnp.log(f), SUB_A, row & (SUB_A - 1))
    k_acc = []
    blocks, q_dec = [], []
    start = jnp.zeros((1, HA_D), F32)
    for i in range(n_sub):
        lo = i * SUB_A
        hi = lo + SUB_A
        b_i = b[lo:hi, :]
        b_sub = b[hi - 1:hi, :]
        k_i = k[lo:hi, :]
        qg = qs[lo:hi, :] * jnp.exp(b_i)
        k_diag = k_i * jnp.exp(-b_i)
        rhs = jnp.concatenate([p.astype(BF16) for p in k_acc] + [k_diag.astype(BF16)], axis=0)
        a_i = lax.dot_general(qg.astype(BF16), rhs, (((1,), (1,)), ((), ())),
                              preferred_element_type=F32)
        causal = (lax.broadcasted_iota(jnp.int32, (SUB_A, hi), 1)
                  <= lax.broadcasted_iota(jnp.int32, (SUB_A, hi), 0) + lo)
        a_i = jnp.where(causal, a_i, 0.0)
        blocks.append(jnp.pad(a_i, ((0, 0), (0, c_len - hi))).astype(BF16))
        q_dec.append((qg * jnp.exp(start)).astype(BF16))
        sub_decay = jnp.exp(b_sub)
        k_acc = [p * sub_decay for p in k_acc] + [k_i * jnp.exp(b_sub - b_i)]
        start = start + b_sub
    scores = jnp.concatenate(blocks, axis=0)
    k_end = jnp.concatenate([p.astype(BF16) for p in k_acc], axis=0)
    chunk_decay = jnp.exp(start)
    return scores, jnp.concatenate(q_dec, axis=0), k_end, chunk_decay


def _hgrn_apply(pend, v, g, nw, st):
    scores, q_dec, k_end, chunk_decay = pend
    vb = v.astype(BF16)
    o = (jnp.dot(scores, vb, preferred_element_type=F32)
         + lax.dot_general(q_dec, st.astype(BF16), (((1,), (1,)), ((), ())),
                           preferred_element_type=F32))
    st = chunk_decay * st + lax.dot_general(vb, k_end, (((0,), (0,)), ((), ())),
                                            preferred_element_type=F32)
    return (_rms(o) * nw * _silu(g)).astype(BF16), st


def _hgrn_prompt_kernel(q_ref, f_ref, i_ref, g_ref, lbl_ref, nw_ref, o_ref, s_ref, *, n_chunks, n_heads):
    c_len = CHUNK_A
    lb = _lower_bound(lbl_ref[...])
    nw = nw_ref[...]
    row = lax.broadcasted_iota(jnp.int32, (c_len, HA_D), 0)
    a_row = lax.broadcasted_iota(jnp.int32, (SUB_A, c_len), 0)
    a_col = lax.broadcasted_iota(jnp.int32, (SUB_A, c_len), 1)
    heads = [slice(h * HA_D, (h + 1) * HA_D) for h in range(n_heads)]

    def rows(c):
        return pl.ds(pl.multiple_of(c * c_len, c_len), c_len)

    def scores_of(c):
        sl = rows(c)
        return tuple(_hgrn_scores(q_ref[sl, hs], f_ref[sl, hs], lb[:, hs], row, a_row, a_col)
                     for hs in heads)

    def apply_to(c, pends, sts):
        sl = rows(c)
        new = []
        for hs, pend, st in zip(heads, pends, sts):
            out, st = _hgrn_apply(pend, i_ref[sl, hs], g_ref[sl, hs], nw[:, hs], st)
            o_ref[sl, hs] = out
            new.append(st)
        return tuple(new)

    def step(c, carry):
        pends, sts = carry
        sts = apply_to(c - 1, pends, sts)
        return scores_of(c), sts

    zeros = tuple(jnp.zeros((HA_D, HA_D), F32) for _ in range(n_heads))
    pends, sts = lax.fori_loop(1, n_chunks, step, (scores_of(0), zeros))
    sts = apply_to(n_chunks - 1, pends, sts)
    for h in range(n_heads):
        s_ref[h] = sts[h].T


HGRN_HEADS_PER_STEP = 4


def _hgrn_prompt(proj, lb_logits, hgrn_norm, n_seq, t):
    m = proj.shape[0]
    nh = HGRN_HEADS_PER_STEP
    w = nh * HA_D
    col = lambda off: pl.BlockSpec((t, w), lambda b, h: (b, off // w + h))
    return pl.pallas_call(
        functools.partial(_hgrn_prompt_kernel, n_chunks=t // CHUNK_A, n_heads=nh),
        grid=(n_seq, H_A // nh),
        in_specs=[col(OFF_Q), col(OFF_F), col(OFF_I), col(OFF_G),
                  pl.BlockSpec((lb_logits.shape[0], w), lambda b, h: (0, h)),
                  pl.BlockSpec((1, w), lambda b, h: (0, h))],
        out_specs=[pl.BlockSpec((t, w), lambda b, h: (b, h)),
                   pl.BlockSpec((None, nh, HA_D, HA_D), lambda b, h: (b, h, 0, 0))],
        out_shape=[jax.ShapeDtypeStruct((m, D_A), BF16),
                   jax.ShapeDtypeStruct((n_seq, H_A, HA_D, HA_D), F32)],
        compiler_params=_params("parallel", "parallel"),
        name="hgrn_prompt",
    )(proj, proj, proj, proj, lb_logits, hgrn_norm)


def _conv_chunk(tail, u, w, bias, n):
    full = jnp.concatenate([tail, u], axis=0)
    out = bias + w[CONV_W - 1:CONV_W, :] * u
    for i in range(CONV_W - 1):
        shifted = pltpu.roll(full, CONV_W - 1 - i, 0)[SUBLANE:SUBLANE + n, :]
        out = out + w[i:i + 1, :] * shifted
    return _silu(out)


def _ssd_prompt_kernel_old(z_ref, x_ref, b_ref, c_ref, dtr_ref,
                       wx_ref, wb_ref, wc_ref, bx_ref, bb_ref, bc_ref,
                       dtbr_ref, alr_ref, dsk_ref, nw_ref,
                       o_ref, h_ref, row_scr, col_scr, *, n_chunks):
    n = CHUNK_B
    wx, wb, wc = wx_ref[...], wb_ref[...], wc_ref[...]
    bx, bb, bc = bx_ref[...], bb_ref[...], bc_ref[...]
    nw = nw_ref[...]
    tri = (lax.broadcasted_iota(jnp.int32, (n, n), 0)
           >= lax.broadcasted_iota(jnp.int32, (n, n), 1))
    lane_head = _idiv(lax.broadcasted_iota(jnp.int32, (1, GW), 1), B_P)
    row_head = _idiv(lax.broadcasted_iota(jnp.int32, (GW, 1), 0), B_P)
    dsk = dsk_ref[...]
    dskip = jnp.zeros((1, GW), F32)
    for j in range(HPG):
        dskip = jnp.where(lane_head == j, dsk[:, j:j + 1], dskip)

    nr = n_chunks * SUBLANE
    dt_all = _softplus(dtr_ref[...] + dtbr_ref[...])
    cum_all = _cumsum_lanes(dt_all * -jnp.exp(alr_ref[...]), n,
                            lax.broadcasted_iota(jnp.int32, (nr, n), 1))
    is_dt = (lax.broadcasted_iota(jnp.int32, (nr, n), 0) & (SUBLANE - 1)) < HPG
    both = jnp.where(is_dt, dt_all, cum_all)
    for c in range(n_chunks):
        m8 = both[c * SUBLANE:(c + 1) * SUBLANE, :]
        row_scr[c] = m8
        col_scr[c] = jnp.concatenate([m8, jnp.zeros((n - SUBLANE, n), F32)], axis=0).T

    def rows(c):
        return pl.ds(pl.multiple_of(c * n, n), n)

    def scan_free(c):
        sl = rows(c)
        if isinstance(c, int) and c == 0:
            tail = lambda ref: jnp.zeros((SUBLANE, ref.shape[1]), F32)
        else:
            prev = pl.ds(pl.multiple_of(c * n - SUBLANE, SUBLANE), SUBLANE)
            tail = lambda ref: ref[prev, :]
        xc = _conv_chunk(tail(x_ref), x_ref[sl, :], wx, bx, n)
        bcv = _conv_chunk(tail(b_ref), b_ref[sl, :], wb, bb, n)
        ccv = _conv_chunk(tail(c_ref), c_ref[sl, :], wc, bc, n)
        r8 = row_scr[c]
        cols = col_scr[c]
        g = _dot_nt(ccv, bcv)
        y = dskip * xc
        e_cum = jnp.zeros((n, GW), F32)
        w_end = jnp.zeros((n, GW), F32)
        for j in range(HPG):
            cc = cols[:, HPG + j:HPG + j + 1]
            seg = cc - r8[HPG + j:HPG + j + 1, :]
            s = g * jnp.exp(jnp.where(tri, seg, -jnp.inf)) * r8[j:j + 1, :]
            y = y + _dot(s, jnp.where(lane_head == j, xc, 0.0))
            e_cum = jnp.where(lane_head == j, jnp.exp(cc), e_cum)
            c_end = r8[HPG + j:HPG + j + 1, n - 1:n]
            w_end = jnp.where(lane_head == j, jnp.exp(c_end - cc) * cols[:, j:j + 1], w_end)
        return y, e_cum, ccv.astype(BF16), (xc * w_end).astype(BF16), bcv.astype(BF16), r8

    def apply_state(c, pend, h):
        y, e_cum, ccb, xwb, bcb, r8 = pend
        sl = rows(c)
        ch = lax.dot_general(ccb, h.astype(BF16), (((1,), (1,)), ((), ())),
                             preferred_element_type=F32)
        yz = (y + ch * e_cum) * _silu(z_ref[sl, :])
        o_ref[sl, :] = (_rms(yz) * nw).astype(o_ref.dtype)
        h_dec = jnp.zeros((GW, 1), F32)
        for j in range(HPG):
            h_dec = jnp.where(row_head == j, jnp.exp(r8[HPG + j:HPG + j + 1, n - 1:n]), h_dec)
        return h_dec * h + lax.dot_general(xwb, bcb, (((0,), (0,)), ((), ())),
                                           preferred_element_type=F32)

    def step(c, carry):
        pend, h = carry
        h = apply_state(c - 1, pend, h)
        return scan_free(c), h

    pend, h = lax.fori_loop(1, n_chunks, step, (scan_free(0), jnp.zeros((GW, B_N), F32)))
    h = apply_state(n_chunks - 1, pend, h)
    h_ref[...] = h.reshape(HPG, B_P, B_N)


def _ssd_prompt_old(proj, dt_rows, conv_w, conv_b, dtb_rows, al_rows, dsk_col, ssd_norm, n_seq, t):
    m = proj.shape[0]
    n_chunks = t // CHUNK_B
    nr = n_chunks * SUBLANE
    wide = lambda off: pl.BlockSpec((t, GW), lambda b, g: (b, off // GW + g))
    narrow = lambda off: pl.BlockSpec((t, B_N), lambda b, g: (b, off // B_N + g))
    cw = lambda off, w: pl.BlockSpec((CONV_W, w), lambda b, g: (0, off // w + g))
    cbias = lambda off, w: pl.BlockSpec((1, w), lambda b, g: (0, off // w + g))
    pr = pl.BlockSpec((None, nr, 1), lambda b, g: (g, 0, 0))
    return pl.pallas_call(
        functools.partial(_ssd_prompt_kernel, n_chunks=n_chunks),
        grid=(n_seq, B_G),
        in_specs=[wide(OFF_Z), wide(OFF_X), narrow(OFF_B), narrow(OFF_C),
                  pl.BlockSpec((None, None, nr, CHUNK_B), lambda b, g: (g, b, 0, 0)),
                  cw(0, GW), cw(D_B, B_N), cw(D_B + B_G * B_N, B_N),
                  cbias(0, GW), cbias(D_B, B_N), cbias(D_B + B_G * B_N, B_N),
                  pr, pr,
                  pl.BlockSpec((None, 1, HPG), lambda b, g: (g, 0, 0)),
                  pl.BlockSpec((1, GW), lambda b, g: (0, g))],
        out_specs=[pl.BlockSpec((t, GW), lambda b, g: (b, g)),
                   pl.BlockSpec((None, HPG, B_P, B_N), lambda b, g: (b, g, 0, 0))],
        out_shape=[jax.ShapeDtypeStruct((m, D_B), BF16),
                   jax.ShapeDtypeStruct((n_seq, H_B, B_P, B_N), F32)],
        scratch_shapes=[pltpu.VMEM((n_chunks, SUBLANE, CHUNK_B), F32),
                        pltpu.VMEM((n_chunks, CHUNK_B, CHUNK_B), F32)],
        compiler_params=_params("parallel", "parallel"),
        name="ssd_prompt",
    )(proj, proj, proj, proj, dt_rows, conv_w, conv_w, conv_w, conv_b, conv_b, conv_b,
      dtb_rows, al_rows, dsk_col, ssd_norm)


SSD_GROUPS_PER_STEP = 2


def _ssd_prompt_kernel(z_ref, x_ref, b_ref, c_ref, dtr_ref,
                       wx_ref, wb_ref, wc_ref, bx_ref, bb_ref, bc_ref,
                       dtbr_ref, alr_ref, dsk_ref, nw_ref,
                       o_ref, h_ref, row_scr, *, n_chunks, n_groups):
    n = CHUNK_B
    assert n == LANE
    tri = (lax.broadcasted_iota(jnp.int32, (n, n), 0)
           >= lax.broadcasted_iota(jnp.int32, (n, n), 1))
    lane_head = _idiv(lax.broadcasted_iota(jnp.int32, (1, GW), 1), B_P)
    row_head = _idiv(lax.broadcasted_iota(jnp.int32, (GW, 1), 0), B_P)
    nr = n_chunks * SUBLANE
    lane_nr = lax.broadcasted_iota(jnp.int32, (nr, n), 1)
    is_dt = (lax.broadcasted_iota(jnp.int32, (nr, n), 0) & (SUBLANE - 1)) < HPG
    is_dt8 = lax.broadcasted_iota(jnp.int32, (SUBLANE, n), 0) < HPG
    n_src = 2 * SUBLANE
    sel_shape = (4 * n_src, HPG * n + 2 * GW)
    sel_k = lax.broadcasted_iota(jnp.int32, sel_shape, 0)
    sel_l = lax.broadcasted_iota(jnp.int32, sel_shape, 1)
    want = jnp.where(sel_l < HPG * n, HPG + _idiv(sel_l, n),
                     2 * HPG + _idiv(sel_l - HPG * n, B_P))
    bcast_sel = jnp.where(((sel_k & (n_src - 1)) == want) & (sel_k < 3 * n_src), 1.0, 0.0).astype(BF16)

    groups = []
    for gi in range(n_groups):
        xs = slice(gi * GW, (gi + 1) * GW)
        bs = slice(gi * B_N, (gi + 1) * B_N)
        dsk = dsk_ref[gi]
        dskip = jnp.zeros((1, GW), F32)
        for j in range(HPG):
            dskip = jnp.where(lane_head == j, dsk[:, j:j + 1], dskip)
        dt_all = _softplus(dtr_ref[gi] + dtbr_ref[gi])
        cum_all = _cumsum_lanes(dt_all * -jnp.exp(alr_ref[gi]), n, lane_nr)
        both = jnp.where(is_dt, dt_all, cum_all)
        row_scr[gi] = both.reshape(n_chunks, SUBLANE, n)
        groups.append((gi, xs, bs, dskip, wx_ref[:, xs], wb_ref[:, bs], wc_ref[:, bs],
                       bx_ref[:, xs], bb_ref[:, bs], bc_ref[:, bs], nw_ref[:, xs]))

    def rows(c):
        return pl.ds(pl.multiple_of(c * n, n), n)

    def scan_free(c, grp):
        gi, xs, bs, dskip, wx, wb, wc, bx, bb, bc, _ = grp
        sl = rows(c)
        if isinstance(c, int) and c == 0:
            tail = lambda ref, lanes: jnp.zeros((SUBLANE, lanes.stop - lanes.start), F32)
        else:
            prev = pl.ds(pl.multiple_of(c * n - SUBLANE, SUBLANE), SUBLANE)
            tail = lambda ref, lanes: ref[prev, lanes]
        xc = _conv_chunk(tail(x_ref, xs), x_ref[sl, xs], wx, bx, n)
        bcv = _conv_chunk(tail(b_ref, bs), b_ref[sl, bs], wb, bb, n)
        ccv = _conv_chunk(tail(c_ref, bs), c_ref[sl, bs], wc, bc, n)
        r8 = row_scr[gi, c]
        swapped = pltpu.roll(r8, HPG, 0)
        cum8 = jnp.where(is_dt8, swapped, r8)
        dt8 = jnp.where(is_dt8, r8, swapped)
        e8 = jnp.where(is_dt8, jnp.exp(cum8), jnp.exp(cum8[:, n - 1:n] - cum8) * dt8)
        both = jnp.concatenate([r8, e8], axis=0)
        hi = both.astype(BF16).astype(F32)
        mid = (both - hi).astype(BF16).astype(F32)
        lo = both - hi - mid
        terms = jnp.concatenate([hi, mid, lo, jnp.zeros_like(hi)], axis=0).astype(BF16)
        cols = lax.dot_general(terms, bcast_sel, (((0,), (0,)), ((), ())),
                               preferred_element_type=F32)
        e_cum = cols[:, HPG * n:HPG * n + GW]
        w_end = cols[:, HPG * n + GW:]
        g = _dot_nt(ccv, bcv)
        xb = xc.astype(BF16)
        s_all, x_all = [], []
        for j in range(HPG):
            cc = cols[:, j * n:(j + 1) * n]
            seg = cc - r8[HPG + j:HPG + j + 1, :]
            s = g * jnp.exp(jnp.where(tri, seg, -jnp.inf)) * r8[j:j + 1, :]
            s_all.append(s.astype(BF16))
            x_all.append(jnp.where(lane_head == j, xb, jnp.zeros_like(xb)))
        y = dskip * xc + jnp.dot(jnp.concatenate(s_all, axis=1), jnp.concatenate(x_all, axis=0),
                                 preferred_element_type=F32)
        return y, e_cum, ccv.astype(BF16), (xc * w_end).astype(BF16), bcv.astype(BF16), r8

    def apply_state(c, grp, pend, h):
        xs, nw = grp[1], grp[-1]
        y, e_cum, ccb, xwb, bcb, r8 = pend
        sl = rows(c)
        ch = lax.dot_general(ccb, h.astype(BF16), (((1,), (1,)), ((), ())),
                             preferred_element_type=F32)
        yz = (y + ch * e_cum) * _silu(z_ref[sl, xs])
        o_ref[sl, xs] = (_rms(yz) * nw).astype(o_ref.dtype)
        h_dec = jnp.zeros((GW, 1), F32)
        for j in range(HPG):
            h_dec = jnp.where(row_head == j, jnp.exp(r8[HPG + j:HPG + j + 1, n - 1:n]), h_dec)
        return h_dec * h + lax.dot_general(xwb, bcb, (((0,), (0,)), ((), ())),
                                           preferred_element_type=F32)

    def step(c, carry):
        pends, hs = carry
        hs = tuple(apply_state(c - 1, grp, pend, h) for grp, pend, h in zip(groups, pends, hs))
        return tuple(scan_free(c, grp) for grp in groups), hs

    init = (tuple(scan_free(0, grp) for grp in groups),
            tuple(jnp.zeros((GW, B_N), F32) for _ in groups))
    pends, hs = lax.fori_loop(1, n_chunks, step, init)
    for gi, (grp, pend, h) in enumerate(zip(groups, pends, hs)):
        h = apply_state(n_chunks - 1, grp, pend, h)
        h_ref[gi * HPG:(gi + 1) * HPG] = h.reshape(HPG, B_P, B_N)


def _ssd_prompt(proj, dt_rows, conv_w, conv_b, dtb_rows, al_rows, dsk_col, ssd_norm, n_seq, t):
    m = proj.shape[0]
    n_chunks = t // CHUNK_B
    nr = n_chunks * SUBLANE
    ng = SSD_GROUPS_PER_STEP
    xw, bw = ng * GW, ng * B_N
    wide = lambda off: pl.BlockSpec((t, xw), lambda b, g: (b, off // xw + g))
    narrow = lambda off: pl.BlockSpec((t, bw), lambda b, g: (b, off // bw + g))
    cw = lambda off, w: pl.BlockSpec((CONV_W, w), lambda b, g: (0, off // w + g))
    cbias = lambda off, w: pl.BlockSpec((1, w), lambda b, g: (0, off // w + g))
    pr = pl.BlockSpec((ng, nr, 1), lambda b, g: (g, 0, 0))
    return pl.pallas_call(
        functools.partial(_ssd_prompt_kernel, n_chunks=n_chunks, n_groups=ng),
        grid=(n_seq, B_G // ng),
        in_specs=[wide(OFF_Z), wide(OFF_X), narrow(OFF_B), narrow(OFF_C),
                  pl.BlockSpec((ng, None, nr, CHUNK_B), lambda b, g: (g, b, 0, 0)),
                  cw(0, xw), cw(D_B, bw), cw(D_B + B_G * B_N, bw),
                  cbias(0, xw), cbias(D_B, bw), cbias(D_B + B_G * B_N, bw),
                  pr, pr,
                  pl.BlockSpec((ng, 1, HPG), lambda b, g: (g, 0, 0)),
                  pl.BlockSpec((1, xw), lambda b, g: (0, g))],
        out_specs=[pl.BlockSpec((t, xw), lambda b, g: (b, g)),
                   pl.BlockSpec((None, ng * HPG, B_P, B_N), lambda b, g: (b, g, 0, 0))],
        out_shape=[jax.ShapeDtypeStruct((m, D_B), BF16),
                   jax.ShapeDtypeStruct((n_seq, H_B, B_P, B_N), F32)],
        scratch_shapes=[pltpu.VMEM((ng, n_chunks, SUBLANE, CHUNK_B), F32)],
        compiler_params=_params("parallel", "parallel"),
        name="ssd_prompt",
    )(proj, proj, proj, proj, dt_rows, conv_w, conv_w, conv_w, conv_b, conv_b, conv_b,
      dtb_rows, al_rows, dsk_col, ssd_norm)


def _hgrn_step_kernel(q_ref, f_ref, v_ref, g_ref, lbl_ref, nw_ref, s_ref, o_ref, so_ref, *, bt):
    logits = lbl_ref[...]
    e = jnp.exp(logits - jnp.max(logits, axis=0, keepdims=True))
    lb = e[0] / jnp.sum(e, axis=0)
    half = 0.5 * (1.0 - lb)
    nw = nw_ref[...]
    sel_k = lax.broadcasted_iota(jnp.int32, (4 * H_A, H_A * HA_D), 0)
    sel_h = _idiv(lax.broadcasted_iota(jnp.int32, (4 * H_A, H_A * HA_D), 1), HA_D)
    bcast_sel = jnp.where(((sel_k & (H_A - 1)) == sel_h) & (sel_k < 3 * H_A), 1.0, 0.0).astype(BF16)

    def seq(b, carry):
        f = (lb + half) + half * jnp.tanh(0.5 * f_ref[b])
        hi = f.astype(BF16).astype(F32)
        mid = (f - hi).astype(BF16).astype(F32)
        lo = f - hi - mid
        terms = jnp.concatenate([hi, mid, lo, jnp.zeros_like(hi)], axis=0).astype(BF16)
        f_cols = lax.dot_general(terms, bcast_sel, (((0,), (0,)), ((), ())),
                                 preferred_element_type=F32)
        qs = _silu(q_ref[b])
        v = v_ref[b]
        rows = []
        for h in range(H_A):
            fc = f_cols[:, h * HA_D:(h + 1) * HA_D]
            s_new = fc * s_ref[b, h] + (1.0 - fc) * v[h:h + 1, :]
            so_ref[b, h] = s_new
            q8 = jnp.broadcast_to(qs[h:h + 1, :], (SUBLANE, HA_D))
            rows.append(_dot(q8, s_new)[0:1, :])
        o = jnp.concatenate(rows, axis=0)
        o_ref[b] = (_rms(o) * nw * _silu(g_ref[b])).astype(o_ref.dtype)
        return carry

    lax.fori_loop(0, bt, seq, 0)


def _hgrn_step(q, f, v, g, lbl, nw, state, bt):
    nb = q.shape[0]
    rw = pl.BlockSpec((bt, H_A, HA_D), lambda i: (i, 0, 0))
    st = pl.BlockSpec((bt, H_A, HA_D, HA_D), lambda i: (i, 0, 0, 0))
    return pl.pallas_call(
        functools.partial(_hgrn_step_kernel, bt=bt),
        grid=(nb // bt,),
        in_specs=[rw, rw, rw, rw,
                  pl.BlockSpec(lbl.shape, lambda i: (0, 0, 0)),
                  pl.BlockSpec((H_A, HA_D), lambda i: (0, 0)),
                  st],
        out_specs=[rw, st],
        out_shape=[jax.ShapeDtypeStruct((nb, H_A, HA_D), BF16),
                   jax.ShapeDtypeStruct(state.shape, F32)],
        compiler_params=_params("parallel"),
        name="hgrn_step",
    )(q, f, v, g, lbl, nw, state)


def _conv_step(buf_ref, b, u, w, bias):
    out = bias + w[CONV_W - 1] * u
    for i in range(CONV_W - 1):
        out = out + w[i] * buf_ref[b, i]
    return _silu(out)


def _ssd_step_kernel(xt_ref, xr_ref, bcr_ref, z_ref, dt_ref,
                     cxt_ref, cxr_ref, cbc_ref,
                     wxt_ref, wxr_ref, wbc_ref, bxt_ref, bxr_ref, bbc_ref,
                     dtb_ref, al_ref, dskr_ref, nw_ref, h_ref,
                     o_ref, ho_ref, y_scr, xc_scr, *, bt):
    wxt, wxr, wbc = wxt_ref[...], wxr_ref[...], wbc_ref[...]
    bxt, bxr, bbc = bxt_ref[...], bxr_ref[...], bbc_ref[...]
    a = -jnp.exp(al_ref[...])
    dtb = dtb_ref[...]

    def seq(b, carry):
        xt = _conv_step(cxt_ref, b, xt_ref[b], wxt, bxt)
        xc_scr[pl.ds(b, 1), :] = _conv_step(cxr_ref, b, xr_ref[b], wxr, bxr)
        bcv = _conv_step(cbc_ref, b, bcr_ref[b], wbc, bbc)
        dt = _softplus(dt_ref[b] + dtb)
        d_a = jnp.exp(dt * a)
        xdt = xt * dt
        ys = []
        for g in range(B_G):
            b_row = bcv[g:g + 1, :]
            c_row = bcv[B_G + g:B_G + g + 1, :]
            hs = []
            for j in range(HPG):
                hh = g * HPG + j
                h_new = d_a[:, hh:hh + 1] * h_ref[b, hh] + xdt[:, hh:hh + 1] * b_row
                ho_ref[b, hh] = h_new
                hs.append(h_new)
            hg = jnp.concatenate(hs, axis=0)
            c8 = jnp.broadcast_to(c_row, (SUBLANE, B_N))
            ys.append(_dot_nt(c8, hg)[0:1, :])
        y_scr[pl.ds(b, 1), :] = jnp.concatenate(ys, axis=1)
        return carry

    lax.fori_loop(0, bt, seq, 0)
    xc = xc_scr[...]
    yz = (y_scr[...] + dskr_ref[...] * xc) * _silu(z_ref[...])
    nw = nw_ref[...]
    for g in range(B_G):
        sl = slice(g * GW, (g + 1) * GW)
        o_ref[:, sl] = (_rms(yz[:, sl]) * nw[:, sl]).astype(o_ref.dtype)


def _ssd_step(xt, xr, bcr, z, dt, cxt, cxr, cbc, wxt, wxr, wbc, bxt, bxr, bbc,
              dtb, al, dskr, nw, state, bt):
    nb = xt.shape[0]
    full = lambda a: pl.BlockSpec(a.shape, lambda i: (0,) * a.ndim)
    lead = lambda a: pl.BlockSpec((bt,) + a.shape[1:], lambda i: (i,) + (0,) * (a.ndim - 1))
    return pl.pallas_call(
        functools.partial(_ssd_step_kernel, bt=bt),
        grid=(nb // bt,),
        in_specs=[lead(xt), lead(xr), lead(bcr), lead(z), lead(dt),
                  lead(cxt), lead(cxr), lead(cbc),
                  full(wxt), full(wxr), full(wbc), full(bxt), full(bxr), full(bbc),
                  full(dtb), full(al), full(dskr), full(nw), lead(state)],
        out_specs=[pl.BlockSpec((bt, D_B), lambda i: (i, 0)), lead(state)],
        out_shape=[jax.ShapeDtypeStruct((nb, D_B), F32),
                   jax.ShapeDtypeStruct(state.shape, F32)],
        scratch_shapes=[pltpu.VMEM((bt, D_B), F32), pltpu.VMEM((bt, D_B), F32)],
        compiler_params=_params("parallel"),
        name="ssd_step",
    )(xt, xr, bcr, z, dt, cxt, cxr, cbc, wxt, wxr, wbc, bxt, bxr, bbc, dtb, al, dskr, nw, state)


def _outproj_kernel(oa_ref, ob_ref, wa_ref, wb_ref, x_ref, gt_ref, gp_ref, y_ref):
    mix = _dot(oa_ref[...], wa_ref[...]) + _dot(ob_ref[...], wb_ref[...])
    y_ref[...] = x_ref[...] + gt_ref[...] * (_rms(mix) * gp_ref[...])


def _outproj(o_a, o_b, w_out, x, mod, per_row, rows_per_seq, g_post, tm):
    m = x.shape[0]
    row = lambda w: pl.BlockSpec((tm, w), lambda i, j: (i, 0))
    return pl.pallas_call(
        _outproj_kernel,
        grid=(m // tm, 1),
        in_specs=[row(D_A), row(D_B),
                  pl.BlockSpec((D_A, D_MODEL), lambda i, j: (0, 0)),
                  pl.BlockSpec((D_B, D_MODEL), lambda i, j: (1, 0)),
                  row(D_MODEL),
                  _mod_spec(per_row, rows_per_seq, tm, 2),
                  pl.BlockSpec((1, D_MODEL), lambda i, j: (0, 0))],
        out_specs=row(D_MODEL),
        out_shape=jax.ShapeDtypeStruct((m, D_MODEL), F32),
        compiler_params=_params("parallel", "arbitrary"),
        name="out_proj",
    )(o_a, o_b, w_out, w_out, x, mod, g_post)


def _mlp_kernel(x_ref, sh_ref, sc_ref, gt_ref, gpre_ref, gpost_ref, wu_ref, wd_ref, y_ref, h_scr):
    f = pl.program_id(1)

    @pl.when(f == 0)
    def _():
        h = _rms(x_ref[...]) * gpre_ref[...] * (1.0 + sc_ref[...]) + sh_ref[...]
        h_scr[...] = h.astype(BF16)
        y_ref[...] = jnp.zeros_like(y_ref)

    u = jnp.maximum(jnp.dot(h_scr[...], wu_ref[...].astype(BF16), preferred_element_type=F32), 0.0)
    y_ref[...] += jnp.dot((u * u).astype(BF16), wd_ref[...].astype(BF16), preferred_element_type=F32)

    @pl.when(f == pl.num_programs(1) - 1)
    def _():
        y_ref[...] = x_ref[...] + gt_ref[...] * (_rms(y_ref[...]) * gpost_ref[...])


def _mlp(x, mod, per_row, rows_per_seq, g_pre, g_post, w_up, w_down, tm, tf):
    m = x.shape[0]
    row = pl.BlockSpec((tm, D_MODEL), lambda i, j: (i, 0))
    vec = pl.BlockSpec((1, D_MODEL), lambda i, j: (0, 0))
    return pl.pallas_call(
        _mlp_kernel,
        grid=(m // tm, D_FF // tf),
        in_specs=[pl.BlockSpec((tm, D_MODEL), lambda i, j: (i, 0), pipeline_mode=pl.Buffered(1)),
                  _mod_spec(per_row, rows_per_seq, tm, 3),
                  _mod_spec(per_row, rows_per_seq, tm, 4),
                  _mod_spec(per_row, rows_per_seq, tm, 5),
                  vec, vec,
                  pl.BlockSpec((D_MODEL, tf), lambda i, j: (0, j)),
                  pl.BlockSpec((tf, D_MODEL), lambda i, j: (j, 0))],
        out_specs=row,
        out_shape=jax.ShapeDtypeStruct((m, D_MODEL), F32),
        scratch_shapes=[pltpu.VMEM((tm, D_MODEL), BF16)],
        compiler_params=_params("parallel", "arbitrary"),
        name="mlp",
    )(x, mod, mod, mod, g_pre, g_post, w_up, w_down)


def _pick_tile(m, target):
    t = min(m, target)
    while m % t:
        t //= 2
    return t


def kernel(x_prompt, x_sample, c_prompt, c_sample, state_hgrn, state_ssm, state_conv, w_ada, b_ada, norm_pre_mix, norm_post_mix, norm_pre_mlp, norm_post_mlp, w_in, hgrn_lb_logits, hgrn_norm, conv_w, conv_b, dt_bias, a_log, d_skip, ssd_norm, w_out, w_up, w_down):
    n_seq, t, _ = x_prompt.shape
    n_dec = x_sample.shape[0]
    assert x_sample.shape[1] == 1 and t % CHUNK_A == 0 and t % CHUNK_B == 0
    assert w_ada.shape[0] == 1, "one layer"
    l = 0

    w_in_t = w_in[l].T
    w_out_b = w_out[l].astype(BF16)
    g_pre_mix, g_post_mix = norm_pre_mix[l][None], norm_post_mix[l][None]
    g_pre_mlp, g_post_mlp = norm_pre_mlp[l][None], norm_post_mlp[l][None]
    cw, cb = conv_w[l], conv_b[l][None]
    hn = hgrn_norm[l][None]
    sn = ssd_norm[l][None]

    n_c = n_seq + n_dec
    pad = (-n_c) % SUBLANE
    c_all = jnp.concatenate([c_prompt, c_sample, jnp.zeros((pad, D_MODEL), F32)], axis=0)
    mod = _ada(c_all, w_ada[l], b_ada[l][None])
    mod_p = mod[:n_seq].reshape(n_seq, 1, N_MOD * D_MODEL)
    mod_s = mod[n_seq:n_c]

    xp = x_prompt.reshape(n_seq * t, D_MODEL)
    proj_p, dt_p = _inproj(xp, mod_p, False, t, g_pre_mix, w_in_t, _pick_tile(t, 2048), 512)
    o_a, s_hgrn_p = _hgrn_prompt(proj_p, hgrn_lb_logits, hn, n_seq, t)

    n_chunks = t // CHUNK_B
    rep = SUBLANE // HPG
    dt_rows = dt_p.reshape(n_seq, n_chunks, CHUNK_B, B_G, 1, HPG).transpose(3, 0, 1, 4, 5, 2)
    dt_rows = jnp.broadcast_to(dt_rows, (B_G, n_seq, n_chunks, rep, HPG, CHUNK_B))
    dt_rows = dt_rows.reshape(B_G, n_seq, n_chunks * SUBLANE, CHUNK_B)
    prow = lambda p: jnp.tile(p.reshape(B_G, 1, HPG), (1, n_chunks * rep, 1)).reshape(B_G, n_chunks * SUBLANE, 1)
    o_b, s_ssm_p = _ssd_prompt(proj_p, dt_rows, cw, cb, prow(dt_bias[l]), prow(a_log[l]),
                               d_skip[l].reshape(B_G, 1, HPG), sn, n_seq, t)
    x1 = _outproj(o_a, o_b, w_out_b, xp, mod_p, False, t, g_post_mix, _pick_tile(t, 256))
    y_p = _mlp(x1, mod_p, False, t, g_pre_mlp, g_post_mlp, w_up[l], w_down[l], _pick_tile(t, 1024), 512)
    conv_p = proj_p.reshape(n_seq, t, D_MAIN)[:, t - (CONV_W - 1):, OFF_X:]

    xs = x_sample.reshape(n_dec, D_MODEL)
    proj_s, dt_s = _inproj(xs, mod_s, True, 1, g_pre_mix, w_in_t, n_dec, 2048)
    bt = _pick_tile(n_dec, SUBLANE)
    to_t = lambda a, h, d: a.reshape(a.shape[:-1] + (h, d)).swapaxes(-1, -2)
    heads = lambda off: proj_s[:, off:off + D_A].reshape(n_dec, H_A, HA_D)
    o_a_s, s_hgrn_s = _hgrn_step(heads(OFF_Q), heads(OFF_F), heads(OFF_I), heads(OFF_G),
                                 hgrn_lb_logits.reshape(-1, H_A, HA_D), hn.reshape(H_A, HA_D),
                                 state_hgrn[l], bt)

    xbc_s = proj_s[:, OFF_X:]
    cst = state_conv[l]
    xt = to_t(xbc_s[:, :D_B], H_B, B_P)
    xr = xbc_s[:, None, :D_B]
    bcr = xbc_s[:, D_B:].reshape(n_dec, 2 * B_G, B_N)
    cxt = to_t(cst[:, :, :D_B], H_B, B_P)
    cxr = cst[:, :, None, :D_B]
    cbc = cst[:, :, D_B:].reshape(n_dec, CONV_W - 1, 2 * B_G, B_N)
    wxt = to_t(cw[:, :D_B], H_B, B_P)
    wxr = cw[:, None, :D_B]
    wbc = cw[:, D_B:].reshape(CONV_W, 2 * B_G, B_N)
    bxt = to_t(cb[:, :D_B], H_B, B_P)[0]
    bxr = cb[:, :D_B]
    bbc = cb[0, D_B:].reshape(2 * B_G, B_N)
    o_b_s, s_ssm_s = _ssd_step(
        xt, xr, bcr, proj_s[:, OFF_Z:OFF_Z + D_B], dt_s[:, None, :],
        cxt, cxr, cbc, wxt, wxr, wbc, bxt, bxr, bbc,
        dt_bias[l][None], a_log[l][None], jnp.repeat(d_skip[l], B_P)[None], sn,
        state_ssm[l], bt)
    x1_s = _outproj(o_a_s.reshape(n_dec, D_A), o_b_s, w_out_b, xs, mod_s, True, 1, g_post_mix, n_dec)
    y_s = _mlp(x1_s, mod_s, True, 1, g_pre_mlp, g_post_mlp, w_up[l], w_down[l], n_dec, 1024)
    conv_s = jnp.concatenate([cst[:, 1:], xbc_s[:, None, :]], axis=1)

    return (y_p.reshape(n_seq, t, D_MODEL), y_s.reshape(n_dec, 1, D_MODEL),
            s_hgrn_p[None], s_ssm_p[None], conv_p[None],
            s_hgrn_s[None], s_ssm_s[None], conv_s[None])
```

```python
import functools

import jax
import jax.numpy as jnp
from jax import lax
from jax.experimental import pallas as pl
from jax.experimental.pallas import tpu as pltpu

F32 = jnp.float32
BF16 = jnp.bfloat16

D_MODEL = 2048
D_A = 2048
HA_D = 128
H_A = D_A // HA_D
D_B = 2048
B_P = 64
H_B = D_B // B_P
B_G = 8
HPG = H_B // B_G
GW = HPG * B_P
B_N = 128
CONV_W = 4
CONV_DIM = D_B + 2 * B_G * B_N
D_MAIN = 4 * D_A + D_B + CONV_DIM
D_FF = 4 * D_MODEL
N_MOD = 6
EPS = 1e-6

OFF_Q, OFF_F, OFF_I, OFF_G = 0, D_A, 2 * D_A, 3 * D_A
OFF_Z = 4 * D_A
OFF_X = OFF_Z + D_B
OFF_B = OFF_X + D_B
OFF_C = OFF_B + B_G * B_N

LANE = 128
SUBLANE = 8
VMEM_LIMIT = 56 * 1024 * 1024

CHUNK_A = 128
SUB_A = 16
CHUNK_B = 128


def _sigmoid(x):
    return 0.5 * jnp.tanh(0.5 * x) + 0.5


def _silu(x):
    hx = 0.5 * x
    return hx * jnp.tanh(hx) + hx


def _softplus(x):
    return jnp.maximum(x, 0.0) + jnp.log1p(jnp.exp(-jnp.abs(x)))


def _idiv(x, d):
    assert d & (d - 1) == 0
    return jnp.right_shift(x, d.bit_length() - 1)


def _rms(x):
    return x * lax.rsqrt(jnp.mean(x * x, axis=-1, keepdims=True) + EPS)


def _cumsum_rows(x, n, row):
    s = 1
    while s < n:
        x = x + jnp.where(row >= s, pltpu.roll(x, s, 0), 0.0)
        s *= 2
    return x


def _cumsum_lanes(x, n, lane):
    s = 1
    while s < n:
        x = x + jnp.where(lane >= s, pltpu.roll(x, s, 1), 0.0)
        s *= 2
    return x


def _dot(a, b):
    return jnp.dot(a.astype(BF16), b.astype(BF16), preferred_element_type=F32)


def _dot_nt(a, b):
    return lax.dot_general(a.astype(BF16), b.astype(BF16), (((1,), (1,)), ((), ())),
                           preferred_element_type=F32)


def _dot_tn(a, b):
    return lax.dot_general(a.astype(BF16), b.astype(BF16), (((0,), (0,)), ((), ())),
                           preferred_element_type=F32)


def _params(*sem):
    return pltpu.CompilerParams(dimension_semantics=sem, vmem_limit_bytes=VMEM_LIMIT)


def _ada_kernel(c_ref, w_ref, b_ref, o_ref):
    a = _silu(c_ref[...])
    o_ref[...] = _dot(a, w_ref[...]) + b_ref[...]


def _ada(c_all, w_ada, b_ada):
    m = c_all.shape[0]
    n = w_ada.shape[1]
    tn = 1024
    return pl.pallas_call(
        _ada_kernel,
        grid=(n // tn,),
        in_specs=[pl.BlockSpec((m, D_MODEL), lambda j: (0, 0)),
                  pl.BlockSpec((D_MODEL, tn), lambda j: (0, j)),
                  pl.BlockSpec((1, tn), lambda j: (0, j))],
        out_specs=pl.BlockSpec((m, tn), lambda j: (0, j)),
        out_shape=jax.ShapeDtypeStruct((m, n), F32),
        compiler_params=_params("parallel"),
        name="ada_mod",
    )(c_all, w_ada, b_ada)


def _mod_spec(per_row, rows_per_seq, tm, col):
    if per_row:
        return pl.BlockSpec((tm, D_MODEL), lambda i, j: (i, col))
    tiles_per_seq = rows_per_seq // tm
    return pl.BlockSpec((None, 1, D_MODEL), lambda i, j: (i // tiles_per_seq, 0, col))


def _inproj_kernel(x_ref, sh_ref, sc_ref, g_ref, w_ref, wdt_ref, o_ref, odt_ref, h_scr):
    @pl.when(pl.program_id(1) == 0)
    def _():
        h = _rms(x_ref[...]) * g_ref[...] * (1.0 + sc_ref[...]) + sh_ref[...]
        hb = h.astype(BF16)
        h_scr[...] = hb
        odt_ref[...] = _dot_nt(hb, wdt_ref[...])

    o_ref[...] = _dot_nt(h_scr[...], w_ref[...])


def _inproj(x, mod, per_row, rows_per_seq, g_pre, w_t, tm, tn, out_tile=lambda j: j):
    m = x.shape[0]
    assert D_MAIN % tn == 0 and D_MAIN % H_B == 0
    return pl.pallas_call(
        _inproj_kernel,
        grid=(m // tm, D_MAIN // tn),
        in_specs=[pl.BlockSpec((tm, D_MODEL), lambda i, j: (i, 0), pipeline_mode=pl.Buffered(1)),
                  _mod_spec(per_row, rows_per_seq, tm, 0),
                  _mod_spec(per_row, rows_per_seq, tm, 1),
                  pl.BlockSpec((1, D_MODEL), lambda i, j: (0, 0)),
                  pl.BlockSpec((tn, D_MODEL), lambda i, j: (j, 0)),
                  pl.BlockSpec((H_B, D_MODEL), lambda i, j: (D_MAIN // H_B, 0))],
        out_specs=[pl.BlockSpec((tm, tn), lambda i, j: (i, out_tile(j))),
                   pl.BlockSpec((tm, H_B), lambda i, j: (i, 0))],
        out_shape=[jax.ShapeDtypeStruct((m, D_MAIN), F32),
                   jax.ShapeDtypeStruct((m, H_B), F32)],
        scratch_shapes=[pltpu.VMEM((tm, D_MODEL), BF16)],
        compiler_params=_params("parallel", "arbitrary"),
        name="in_proj",
    )(x, mod, mod, g_pre, w_t, w_t)


def _lower_bound(logits):
    e = jnp.exp(logits - jnp.max(logits, axis=0, keepdims=True))
    return e[0:1] / jnp.sum(e, axis=0, keepdims=True)


def _hgrn_scores(q, fl, lb, row, a_row, a_col):
    c_len = CHUNK_A
    n_sub = c_len // SUB_A
    half = 0.5 * (1.0 - lb)
    f = (lb + half) + half * jnp.tanh(0.5 * fl)
    k = 1.0 - f
    qs = _silu(q)
    b = _cumsum_rows(jnp.log(f), SUB_A, row & (SUB_A - 1))
    k_acc = []
    blocks, q_dec = [], []
    start = jnp.zeros((1, HA_D), F32)
    for i in range(n_sub):
        lo = i * SUB_A
        hi = lo + SUB_A
        b_i = b[lo:hi, :]
        b_sub = b[hi - 1:hi, :]
        k_i = k[lo:hi, :]
        qg = qs[lo:hi, :] * jnp.exp(b_i)
        k_diag = k_i * jnp.exp(-b_i)
        rhs = jnp.concatenate([p.astype(BF16) for p in k_acc] + [k_diag.astype(BF16)], axis=0)
        a_i = lax.dot_general(qg.astype(BF16), rhs, (((1,), (1,)), ((), ())),
                              preferred_element_type=F32)
        causal = (lax.broadcasted_iota(jnp.int32, (SUB_A, hi), 1)
                  <= lax.broadcasted_iota(jnp.int32, (SUB_A, hi), 0) + lo)
        a_i = jnp.where(causal, a_i, 0.0)
        blocks.append(jnp.pad(a_i, ((0, 0), (0, c_len - hi))).astype(BF16))
        q_dec.append((qg * jnp.exp(start)).astype(BF16))
        sub_decay = jnp.exp(b_sub)
        k_acc = [p * sub_decay for p in k_acc] + [k_i * jnp.exp(b_sub - b_i)]
        start = start + b_sub
    scores = jnp.concatenate(blocks, axis=0)
    k_end = jnp.concatenate([p.astype(BF16) for p in k_acc], axis=0)
    chunk_decay = jnp.exp(start)
    return scores, jnp.concatenate(q_dec, axis=0), k_end, chunk_decay


def _hgrn_apply(pend, v, g, nw, st):
    scores, q_dec, k_end, chunk_decay = pend
    vb = v.astype(BF16)
    o = (jnp.dot(scores, vb, preferred_element_type=F32)
         + lax.dot_general(q_dec, st.astype(BF16), (((1,), (1,)), ((), ())),
                           preferred_element_type=F32))
    st = chunk_decay * st + lax.dot_general(vb, k_end, (((0,), (0,)), ((), ())),
                                            preferred_element_type=F32)
    return (_rms(o) * nw * _silu(g)).astype(BF16), st


def _hgrn_prompt_kernel(p_ref, hpar_ref, o_ref, s_ref, *, n_chunks, n_heads, n_slots):
    c_len = CHUNK_A
    w = n_heads * HA_D

    class _Part:
        def __init__(self, k):
            self.k = k

        def __getitem__(self, idx):
            sl, hs = idx
            return p_ref[sl, self.k * w + hs.start:self.k * w + hs.stop]

    q_ref, f_ref, i_ref, g_ref = _Part(0), _Part(1), _Part(2), _Part(3)
    lb = _lower_bound(hpar_ref[0:n_slots, :])
    nw = hpar_ref[n_slots:n_slots + 1, :]
    row = lax.broadcasted_iota(jnp.int32, (c_len, HA_D), 0)
    a_row = lax.broadcasted_iota(jnp.int32, (SUB_A, c_len), 0)
    a_col = lax.broadcasted_iota(jnp.int32, (SUB_A, c_len), 1)
    heads = [slice(h * HA_D, (h + 1) * HA_D) for h in range(n_heads)]

    def rows(c):
        return pl.ds(pl.multiple_of(c * c_len, c_len), c_len)

    def scores_of(c):
        sl = rows(c)
        return tuple(_hgrn_scores(q_ref[sl, hs], f_ref[sl, hs], lb[:, hs], row, a_row, a_col)
                     for hs in heads)

    def apply_to(c, pends, sts):
        sl = rows(c)
        new = []
        for hs, pend, st in zip(heads, pends, sts):
            out, st = _hgrn_apply(pend, i_ref[sl, hs], g_ref[sl, hs], nw[:, hs], st)
            o_ref[sl, hs] = out
            new.append(st)
        return tuple(new)

    def step(c, carry):
        pends, sts = carry
        sts = apply_to(c - 1, pends, sts)
        return scores_of(c), sts

    zeros = tuple(jnp.zeros((HA_D, HA_D), F32) for _ in range(n_heads))
    pends, sts = lax.fori_loop(1, n_chunks, step, (scores_of(0), zeros))
    sts = apply_to(n_chunks - 1, pends, sts)
    for h in range(n_heads):
        s_ref[h] = sts[h].T


HGRN_HEADS_PER_STEP = 4


def _hgrn_prompt(proj, lb_logits, hgrn_norm, n_seq, t):
    m = proj.shape[0]
    nh = HGRN_HEADS_PER_STEP
    w = nh * HA_D
    n_slots = lb_logits.shape[0]
    hpar = jnp.concatenate([lb_logits, hgrn_norm, jnp.zeros(((-n_slots - 1) % SUBLANE, D_A), F32)], axis=0)
    return pl.pallas_call(
        functools.partial(_hgrn_prompt_kernel, n_chunks=t // CHUNK_A, n_heads=nh, n_slots=n_slots),
        grid=(n_seq, H_A // nh),
        in_specs=[pl.BlockSpec((t, 4 * w), lambda b, h: (b, h)),
                  pl.BlockSpec((hpar.shape[0], w), lambda b, h: (0, h))],
        out_specs=[pl.BlockSpec((t, w), lambda b, h: (b, h)),
                   pl.BlockSpec((None, nh, HA_D, HA_D), lambda b, h: (b, h, 0, 0))],
        out_shape=[jax.ShapeDtypeStruct((m, D_A), BF16),
                   jax.ShapeDtypeStruct((n_seq, H_A, HA_D, HA_D), F32)],
        compiler_params=_params("parallel", "parallel"),
        name="hgrn_prompt",
    )(proj, hpar)


def _hgrn_interleave(j, tn):
    n_hg = D_A // tn
    return jnp.where(j < 4 * n_hg, 4 * (j % n_hg) + j // n_hg, j)


def _conv_chunk(tail, u, w, bias, n):
    full = jnp.concatenate([tail, u], axis=0)
    out = bias + w[CONV_W - 1:CONV_W, :] * u
    for i in range(CONV_W - 1):
        shifted = pltpu.roll(full, CONV_W - 1 - i, 0)[SUBLANE:SUBLANE + n, :]
        out = out + w[i:i + 1, :] * shifted
    return _silu(out)


def _ssd_prompt_kernel_old(z_ref, x_ref, b_ref, c_ref, dtr_ref,
                       wx_ref, wb_ref, wc_ref, bx_ref, bb_ref, bc_ref,
                       dtbr_ref, alr_ref, dsk_ref, nw_ref,
                       o_ref, h_ref, row_scr, col_scr, *, n_chunks):
    n = CHUNK_B
    wx, wb, wc = wx_ref[...], wb_ref[...], wc_ref[...]
    bx, bb, bc = bx_ref[...], bb_ref[...], bc_ref[...]
    nw = nw_ref[...]
    tri = (lax.broadcasted_iota(jnp.int32, (n, n), 0)
           >= lax.broadcasted_iota(jnp.int32, (n, n), 1))
    lane_head = _idiv(lax.broadcasted_iota(jnp.int32, (1, GW), 1), B_P)
    row_head = _idiv(lax.broadcasted_iota(jnp.int32, (GW, 1), 0), B_P)
    dsk = dsk_ref[...]
    dskip = jnp.zeros((1, GW), F32)
    for j in range(HPG):
        dskip = jnp.where(lane_head == j, dsk[:, j:j + 1], dskip)

    nr = n_chunks * SUBLANE
    dt_all = _softplus(dtr_ref[...] + dtbr_ref[...])
    cum_all = _cumsum_lanes(dt_all * -jnp.exp(alr_ref[...]), n,
                            lax.broadcasted_iota(jnp.int32, (nr, n), 1))
    is_dt = (lax.broadcasted_iota(jnp.int32, (nr, n), 0) & (SUBLANE - 1)) < HPG
    both = jnp.where(is_dt, dt_all, cum_all)
    for c in range(n_chunks):
        m8 = both[c * SUBLANE:(c + 1) * SUBLANE, :]
        row_scr[c] = m8
        col_scr[c] = jnp.concatenate([m8, jnp.zeros((n - SUBLANE, n), F32)], axis=0).T

    def rows(c):
        return pl.ds(pl.multiple_of(c * n, n), n)

    def scan_free(c):
        sl = rows(c)
        if isinstance(c, int) and c == 0:
            tail = lambda ref: jnp.zeros((SUBLANE, ref.shape[1]), F32)
        else:
            prev = pl.ds(pl.multiple_of(c * n - SUBLANE, SUBLANE), SUBLANE)
            tail = lambda ref: ref[prev, :]
        xc = _conv_chunk(tail(x_ref), x_ref[sl, :], wx, bx, n)
        bcv = _conv_chunk(tail(b_ref), b_ref[sl, :], wb, bb, n)
        ccv = _conv_chunk(tail(c_ref), c_ref[sl, :], wc, bc, n)
        r8 = row_scr[c]
        cols = col_scr[c]
        g = _dot_nt(ccv, bcv)
        y = dskip * xc
        e_cum = jnp.zeros((n, GW), F32)
        w_end = jnp.zeros((n, GW), F32)
        for j in range(HPG):
            cc = cols[:, HPG + j:HPG + j + 1]
            seg = cc - r8[HPG + j:HPG + j + 1, :]
            s = g * jnp.exp(jnp.where(tri, seg, -jnp.inf)) * r8[j:j + 1, :]
            y = y + _dot(s, jnp.where(lane_head == j, xc, 0.0))
            e_cum = jnp.where(lane_head == j, jnp.exp(cc), e_cum)
            c_end = r8[HPG + j:HPG + j + 1, n - 1:n]
            w_end = jnp.where(lane_head == j, jnp.exp(c_end - cc) * cols[:, j:j + 1], w_end)
        return y, e_cum, ccv.astype(BF16), (xc * w_end).astype(BF16), bcv.astype(BF16), r8

    def apply_state(c, pend, h):
        y, e_cum, ccb, xwb, bcb, r8 = pend
        sl = rows(c)
        ch = lax.dot_general(ccb, h.astype(BF16), (((1,), (1,)), ((), ())),
                             preferred_element_type=F32)
        yz = (y + ch * e_cum) * _silu(z_ref[sl, :])
        o_ref[sl, :] = (_rms(yz) * nw).astype(o_ref.dtype)
        h_dec = jnp.zeros((GW, 1), F32)
        for j in range(HPG):
            h_dec = jnp.where(row_head == j, jnp.exp(r8[HPG + j:HPG + j + 1, n - 1:n]), h_dec)
        return h_dec * h + lax.dot_general(xwb, bcb, (((0,), (0,)), ((), ())),
                                           preferred_element_type=F32)

    def step(c, carry):
        pend, h = carry
        h = apply_state(c - 1, pend, h)
        return scan_free(c), h

    pend, h = lax.fori_loop(1, n_chunks, step, (scan_free(0), jnp.zeros((GW, B_N), F32)))
    h = apply_state(n_chunks - 1, pend, h)
    h_ref[...] = h.reshape(HPG, B_P, B_N)


def _ssd_prompt_old(proj, dt_rows, conv_w, conv_b, dtb_rows, al_rows, dsk_col, ssd_norm, n_seq, t):
    m = proj.shape[0]
    n_chunks = t // CHUNK_B
    nr = n_chunks * SUBLANE
    wide = lambda off: pl.BlockSpec((t, GW), lambda b, g: (b, off // GW + g))
    narrow = lambda off: pl.BlockSpec((t, B_N), lambda b, g: (b, off // B_N + g))
    cw = lambda off, w: pl.BlockSpec((CONV_W, w), lambda b, g: (0, off // w + g))
    cbias = lambda off, w: pl.BlockSpec((1, w), lambda b, g: (0, off // w + g))
    pr = pl.BlockSpec((None, nr, 1), lambda b, g: (g, 0, 0))
    return pl.pallas_call(
        functools.partial(_ssd_prompt_kernel, n_chunks=n_chunks),
        grid=(n_seq, B_G),
        in_specs=[wide(OFF_Z), wide(OFF_X), narrow(OFF_B), narrow(OFF_C),
                  pl.BlockSpec((None, None, nr, CHUNK_B), lambda b, g: (g, b, 0, 0)),
                  cw(0, GW), cw(D_B, B_N), cw(D_B + B_G * B_N, B_N),
                  cbias(0, GW), cbias(D_B, B_N), cbias(D_B + B_G * B_N, B_N),
                  pr, pr,
                  pl.BlockSpec((None, 1, HPG), lambda b, g: (g, 0, 0)),
                  pl.BlockSpec((1, GW), lambda b, g: (0, g))],
        out_specs=[pl.BlockSpec((t, GW), lambda b, g: (b, g)),
                   pl.BlockSpec((None, HPG, B_P, B_N), lambda b, g: (b, g, 0, 0))],
        out_shape=[jax.ShapeDtypeStruct((m, D_B), BF16),
                   jax.ShapeDtypeStruct((n_seq, H_B, B_P, B_N), F32)],
        scratch_shapes=[pltpu.VMEM((n_chunks, SUBLANE, CHUNK_B), F32),
                        pltpu.VMEM((n_chunks, CHUNK_B, CHUNK_B), F32)],
        compiler_params=_params("parallel", "parallel"),
        name="ssd_prompt",
    )(proj, proj, proj, proj, dt_rows, cpar, dpar)


SSD_GROUPS_PER_STEP = 2


def _ssd_prompt_kernel(z_ref, x_ref, b_ref, c_ref, dtr_ref, cpar_ref, dpar_ref,
                       o_ref, h_ref, row_scr, *, n_chunks, n_groups):
    n = CHUNK_B
    assert n == LANE
    tri = (lax.broadcasted_iota(jnp.int32, (n, n), 0)
           >= lax.broadcasted_iota(jnp.int32, (n, n), 1))
    lane_head = _idiv(lax.broadcasted_iota(jnp.int32, (1, GW), 1), B_P)
    row_head = _idiv(lax.broadcasted_iota(jnp.int32, (GW, 1), 0), B_P)
    nr = n_chunks * SUBLANE
    lane_nr = lax.broadcasted_iota(jnp.int32, (nr, n), 1)
    is_dt = (lax.broadcasted_iota(jnp.int32, (nr, n), 0) & (SUBLANE - 1)) < HPG
    is_dt8 = lax.broadcasted_iota(jnp.int32, (SUBLANE, n), 0) < HPG
    n_src = 2 * SUBLANE
    sel_shape = (4 * n_src, HPG * n + 2 * GW)
    sel_k = lax.broadcasted_iota(jnp.int32, sel_shape, 0)
    sel_l = lax.broadcasted_iota(jnp.int32, sel_shape, 1)
    want = jnp.where(sel_l < HPG * n, HPG + _idiv(sel_l, n),
                     2 * HPG + _idiv(sel_l - HPG * n, B_P))
    bcast_sel = jnp.where(((sel_k & (n_src - 1)) == want) & (sel_k < 3 * n_src), 1.0, 0.0).astype(BF16)

    groups = []
    for gi in range(n_groups):
        xs = slice(gi * GW, (gi + 1) * GW)
        bs = slice(gi * B_N, (gi + 1) * B_N)
        dpar = dpar_ref[gi]
        dt_all = _softplus(dtr_ref[gi] + dpar[:, 0:1])
        cum_all = _cumsum_lanes(dt_all * -jnp.exp(dpar[:, 1:2]), n, lane_nr)
        both = jnp.where(is_dt, dt_all, cum_all)
        row_scr[gi] = both.reshape(n_chunks, SUBLANE, n)
        bl = slice(n_groups * GW + gi * B_N, n_groups * GW + (gi + 1) * B_N)
        cl = slice(n_groups * (GW + B_N) + gi * B_N, n_groups * (GW + B_N) + (gi + 1) * B_N)
        taps = slice(0, CONV_W)
        groups.append((gi, xs, bs, cpar_ref[6:7, xs], cpar_ref[taps, xs], cpar_ref[taps, bl],
                       cpar_ref[taps, cl], cpar_ref[CONV_W:CONV_W + 1, xs], cpar_ref[CONV_W:CONV_W + 1, bl],
                       cpar_ref[CONV_W:CONV_W + 1, cl], cpar_ref[5:6, xs]))

    def rows(c):
        return pl.ds(pl.multiple_of(c * n, n), n)

    def scan_free(c, grp):
        gi, xs, bs, dskip, wx, wb, wc, bx, bb, bc, _ = grp
        sl = rows(c)
        if isinstance(c, int) and c == 0:
            tail = lambda ref, lanes: jnp.zeros((SUBLANE, lanes.stop - lanes.start), F32)
        else:
            prev = pl.ds(pl.multiple_of(c * n - SUBLANE, SUBLANE), SUBLANE)
            tail = lambda ref, lanes: ref[prev, lanes]
        xc = _conv_chunk(tail(x_ref, xs), x_ref[sl, xs], wx, bx, n)
        bcv = _conv_chunk(tail(b_ref, bs), b_ref[sl, bs], wb, bb, n)
        ccv = _conv_chunk(tail(c_ref, bs), c_ref[sl, bs], wc, bc, n)
        r8 = row_scr[gi, c]
        swapped = pltpu.roll(r8, HPG, 0)
        cum8 = jnp.where(is_dt8, swapped, r8)
        dt8 = jnp.where(is_dt8, r8, swapped)
        e8 = jnp.where(is_dt8, jnp.exp(cum8), jnp.exp(cum8[:, n - 1:n] - cum8) * dt8)
        both = jnp.concatenate([r8, e8], axis=0)
        hi = both.astype(BF16).astype(F32)
        mid = (both - hi).astype(BF16).astype(F32)
        lo = both - hi - mid
        terms = jnp.concatenate([hi, mid, lo, jnp.zeros_like(hi)], axis=0).astype(BF16)
        cols = lax.dot_general(terms, bcast_sel, (((0,), (0,)), ((), ())),
                               preferred_element_type=F32)
        e_cum = cols[:, HPG * n:HPG * n + GW]
        w_end = cols[:, HPG * n + GW:]
        g = _dot_nt(ccv, bcv)
        xb = xc.astype(BF16)
        s_all, x_all = [], []
        for j in range(HPG):
            cc = cols[:, j * n:(j + 1) * n]
            seg = cc - r8[HPG + j:HPG + j + 1, :]
            s = g * jnp.exp(jnp.where(tri, seg, -jnp.inf)) * r8[j:j + 1, :]
            s_all.append(s.astype(BF16))
            x_all.append(jnp.where(lane_head == j, xb, jnp.zeros_like(xb)))
        y = dskip * xc + jnp.dot(jnp.concatenate(s_all, axis=1), jnp.concatenate(x_all, axis=0),
                                 preferred_element_type=F32)
        return y, e_cum, ccv.astype(BF16), (xc * w_end).astype(BF16), bcv.astype(BF16), r8

    def apply_state(c, grp, pend, h):
        xs, nw = grp[1], grp[-1]
        y, e_cum, ccb, xwb, bcb, r8 = pend
        sl = rows(c)
        ch = lax.dot_general(ccb, h.astype(BF16), (((1,), (1,)), ((), ())),
                             preferred_element_type=F32)
        yz = (y + ch * e_cum) * _silu(z_ref[sl, xs])
        o_ref[sl, xs] = (_rms(yz) * nw).astype(o_ref.dtype)
        h_dec = jnp.zeros((GW, 1), F32)
        for j in range(HPG):
            h_dec = jnp.where(row_head == j, jnp.exp(r8[HPG + j:HPG + j + 1, n - 1:n]), h_dec)
        return h_dec * h + lax.dot_general(xwb, bcb, (((0,), (0,)), ((), ())),
                                           preferred_element_type=F32)

    def step(c, carry):
        pends, hs = carry
        hs = tuple(apply_state(c - 1, grp, pend, h) for grp, pend, h in zip(groups, pends, hs))
        return tuple(scan_free(c, grp) for grp in groups), hs

    init = (tuple(scan_free(0, grp) for grp in groups),
            tuple(jnp.zeros((GW, B_N), F32) for _ in groups))
    pends, hs = lax.fori_loop(1, n_chunks, step, init)
    for gi, (grp, pend, h) in enumerate(zip(groups, pends, hs)):
        h = apply_state(n_chunks - 1, grp, pend, h)
        h_ref[gi * HPG:(gi + 1) * HPG] = h.reshape(HPG, B_P, B_N)


def _ssd_channel_params(conv_w, conv_b, ssd_norm, d_skip, ng):
    zeros = jnp.zeros((1, 2 * B_G * B_N), F32)
    rows = jnp.concatenate([
        conv_w, conv_b,
        jnp.concatenate([ssd_norm, zeros], axis=1),
        jnp.concatenate([jnp.repeat(d_skip, B_P)[None], zeros], axis=1),
        jnp.zeros((SUBLANE - CONV_W - 3, CONV_DIM), F32)], axis=0)
    n_steps = B_G // ng
    part = lambda lo, width: rows[:, lo:lo + n_steps * width].reshape(SUBLANE, n_steps, width)
    slab = jnp.concatenate([part(0, ng * GW), part(D_B, ng * B_N), part(D_B + B_G * B_N, ng * B_N)], axis=2)
    return slab.transpose(1, 0, 2)


def _ssd_prompt(proj, dt_rows, cpar, dpar, n_seq, t):
    m = proj.shape[0]
    n_chunks = t // CHUNK_B
    nr = n_chunks * SUBLANE
    ng = SSD_GROUPS_PER_STEP
    xw, bw = ng * GW, ng * B_N
    wide = lambda off: pl.BlockSpec((t, xw), lambda b, g: (b, off // xw + g))
    narrow = lambda off: pl.BlockSpec((t, bw), lambda b, g: (b, off // bw + g))
    return pl.pallas_call(
        functools.partial(_ssd_prompt_kernel, n_chunks=n_chunks, n_groups=ng),
        grid=(n_seq, B_G // ng),
        in_specs=[wide(OFF_Z), wide(OFF_X), narrow(OFF_B), narrow(OFF_C),
                  pl.BlockSpec((ng, None, nr, CHUNK_B), lambda b, g: (g, b, 0, 0)),
                  pl.BlockSpec((None, SUBLANE, xw + 2 * bw), lambda b, g: (g, 0, 0)),
                  pl.BlockSpec((ng, nr, 2), lambda b, g: (g, 0, 0))],
        out_specs=[pl.BlockSpec((t, xw), lambda b, g: (b, g)),
                   pl.BlockSpec((None, ng * HPG, B_P, B_N), lambda b, g: (b, g, 0, 0))],
        out_shape=[jax.ShapeDtypeStruct((m, D_B), BF16),
                   jax.ShapeDtypeStruct((n_seq, H_B, B_P, B_N), F32)],
        scratch_shapes=[pltpu.VMEM((ng, n_chunks, SUBLANE, CHUNK_B), F32)],
        compiler_params=_params("parallel", "parallel"),
        name="ssd_prompt",
    )(proj, proj, proj, proj, dt_rows, cpar, dpar)


def _hgrn_step_kernel(q_ref, f_ref, v_ref, g_ref, lbl_ref, nw_ref, s_ref, o_ref, so_ref, *, bt):
    logits = lbl_ref[...]
    e = jnp.exp(logits - jnp.max(logits, axis=0, keepdims=True))
    lb = e[0] / jnp.sum(e, axis=0)
    half = 0.5 * (1.0 - lb)
    nw = nw_ref[...]
    sel_k = lax.broadcasted_iota(jnp.int32, (4 * H_A, H_A * HA_D), 0)
    sel_h = _idiv(lax.broadcasted_iota(jnp.int32, (4 * H_A, H_A * HA_D), 1), HA_D)
    bcast_sel = jnp.where(((sel_k & (H_A - 1)) == sel_h) & (sel_k < 3 * H_A), 1.0, 0.0).astype(BF16)

    def seq(b, carry):
        f = (lb + half) + half * jnp.tanh(0.5 * f_ref[b])
        hi = f.astype(BF16).astype(F32)
        mid = (f - hi).astype(BF16).astype(F32)
        lo = f - hi - mid
        terms = jnp.concatenate([hi, mid, lo, jnp.zeros_like(hi)], axis=0).astype(BF16)
        f_cols = lax.dot_general(terms, bcast_sel, (((0,), (0,)), ((), ())),
                                 preferred_element_type=F32)
        qs = _silu(q_ref[b])
        v = v_ref[b]
        rows = []
        for h in range(H_A):
            fc = f_cols[:, h * HA_D:(h + 1) * HA_D]
            s_new = fc * s_ref[b, h] + (1.0 - fc) * v[h:h + 1, :]
            so_ref[b, h] = s_new
            q8 = jnp.broadcast_to(qs[h:h + 1, :], (SUBLANE, HA_D))
            rows.append(_dot(q8, s_new)[0:1, :])
        o = jnp.concatenate(rows, axis=0)
        o_ref[b] = (_rms(o) * nw * _silu(g_ref[b])).astype(o_ref.dtype)
        return carry

    lax.fori_loop(0, bt, seq, 0)


def _hgrn_step(q, f, v, g, lbl, nw, state, bt):
    nb = q.shape[0]
    rw = pl.BlockSpec((bt, H_A, HA_D), lambda i: (i, 0, 0))
    st = pl.BlockSpec((bt, H_A, HA_D, HA_D), lambda i: (i, 0, 0, 0))
    return pl.pallas_call(
        functools.partial(_hgrn_step_kernel, bt=bt),
        grid=(nb // bt,),
        in_specs=[rw, rw, rw, rw,
                  pl.BlockSpec(lbl.shape, lambda i: (0, 0, 0)),
                  pl.BlockSpec((H_A, HA_D), lambda i: (0, 0)),
                  st],
        out_specs=[rw, st],
        out_shape=[jax.ShapeDtypeStruct((nb, H_A, HA_D), BF16),
                   jax.ShapeDtypeStruct(state.shape, F32)],
        compiler_params=_params("parallel"),
        name="hgrn_step",
    )(q, f, v, g, lbl, nw, state)


def _conv_step(buf_ref, b, u, w, bias):
    out = bias + w[CONV_W - 1] * u
    for i in range(CONV_W - 1):
        out = out + w[i] * buf_ref[b, i]
    return _silu(out)


def _ssd_step_kernel(xt_ref, xr_ref, bcr_ref, z_ref, dt_ref,
                     cxt_ref, cxr_ref, cbc_ref,
                     wxt_ref, wxr_ref, wbc_ref, bxt_ref, bxr_ref, bbc_ref,
                     dtb_ref, al_ref, dskr_ref, nw_ref, h_ref,
                     o_ref, ho_ref, y_scr, xc_scr, *, bt):
    wxt, wxr, wbc = wxt_ref[...], wxr_ref[...], wbc_ref[...]
    bxt, bxr, bbc = bxt_ref[...], bxr_ref[...], bbc_ref[...]
    a = -jnp.exp(al_ref[...])
    dtb = dtb_ref[...]

    def seq(b, carry):
        xt = _conv_step(cxt_ref, b, xt_ref[b], wxt, bxt)
        xc_scr[pl.ds(b, 1), :] = _conv_step(cxr_ref, b, xr_ref[b], wxr, bxr)
        bcv = _conv_step(cbc_ref, b, bcr_ref[b], wbc, bbc)
        dt = _softplus(dt_ref[b] + dtb)
        d_a = jnp.exp(dt * a)
        xdt = xt * dt
        ys = []
        for g in range(B_G):
            b_row = bcv[g:g + 1, :]
            c_row = bcv[B_G + g:B_G + g + 1, :]
            hs = []
            for j in range(HPG):
                hh = g * HPG + j
                h_new = d_a[:, hh:hh + 1] * h_ref[b, hh] + xdt[:, hh:hh + 1] * b_row
                ho_ref[b, hh] = h_new
                hs.append(h_new)
            hg = jnp.concatenate(hs, axis=0)
            c8 = jnp.broadcast_to(c_row, (SUBLANE, B_N))
            ys.append(_dot_nt(c8, hg)[0:1, :])
        y_scr[pl.ds(b, 1), :] = jnp.concatenate(ys, axis=1)
        return carry

    lax.fori_loop(0, bt, seq, 0)
    xc = xc_scr[...]
    yz = (y_scr[...] + dskr_ref[...] * xc) * _silu(z_ref[...])
    nw = nw_ref[...]
    for g in range(B_G):
        sl = slice(g * GW, (g + 1) * GW)
        o_ref[:, sl] = (_rms(yz[:, sl]) * nw[:, sl]).astype(o_ref.dtype)


def _ssd_step(xt, xr, bcr, z, dt, cxt, cxr, cbc, wxt, wxr, wbc, bxt, bxr, bbc,
              dtb, al, dskr, nw, state, bt):
    nb = xt.shape[0]
    full = lambda a: pl.BlockSpec(a.shape, lambda i: (0,) * a.ndim)
    lead = lambda a: pl.BlockSpec((bt,) + a.shape[1:], lambda i: (i,) + (0,) * (a.ndim - 1))
    return pl.pallas_call(
        functools.partial(_ssd_step_kernel, bt=bt),
        grid=(nb // bt,),
        in_specs=[lead(xt), lead(xr), lead(bcr), lead(z), lead(dt),
                  lead(cxt), lead(cxr), lead(cbc),
                  full(wxt), full(wxr), full(wbc), full(bxt), full(bxr), full(bbc),
                  full(dtb), full(al), full(dskr), full(nw), lead(state)],
        out_specs=[pl.BlockSpec((bt, D_B), lambda i: (i, 0)), lead(state)],
        out_shape=[jax.ShapeDtypeStruct((nb, D_B), F32),
                   jax.ShapeDtypeStruct(state.shape, F32)],
        scratch_shapes=[pltpu.VMEM((bt, D_B), F32), pltpu.VMEM((bt, D_B), F32)],
        compiler_params=_params("parallel"),
        name="ssd_step",
    )(xt, xr, bcr, z, dt, cxt, cxr, cbc, wxt, wxr, wbc, bxt, bxr, bbc, dtb, al, dskr, nw, state)


def _outproj_kernel(oa_ref, ob_ref, wa_ref, wb_ref, x_ref, gt_ref, gp_ref, y_ref):
    mix = _dot(oa_ref[...], wa_ref[...]) + _dot(ob_ref[...], wb_ref[...])
    y_ref[...] = x_ref[...] + gt_ref[...] * (_rms(mix) * gp_ref[...])


def _outproj(o_a, o_b, w_out, x, mod, per_row, rows_per_seq, g_post, tm):
    m = x.shape[0]
    row = lambda w: pl.BlockSpec((tm, w), lambda i, j: (i, 0))
    return pl.pallas_call(
        _outproj_kernel,
        grid=(m // tm, 1),
        in_specs=[row(D_A), row(D_B),
                  pl.BlockSpec((D_A, D_MODEL), lambda i, j: (0, 0)),
                  pl.BlockSpec((D_B, D_MODEL), lambda i, j: (1, 0)),
                  row(D_MODEL),
                  _mod_spec(per_row, rows_per_seq, tm, 2),
                  pl.BlockSpec((1, D_MODEL), lambda i, j: (0, 0))],
        out_specs=row(D_MODEL),
        out_shape=jax.ShapeDtypeStruct((m, D_MODEL), F32),
        compiler_params=_params("parallel", "arbitrary"),
        name="out_proj",
    )(o_a, o_b, w_out, w_out, x, mod, g_post)


def _mlp_kernel(x_ref, sh_ref, sc_ref, gt_ref, gpre_ref, gpost_ref, wu_ref, wd_ref, y_ref, h_scr):
    f = pl.program_id(1)

    @pl.when(f == 0)
    def _():
        h = _rms(x_ref[...]) * gpre_ref[...] * (1.0 + sc_ref[...]) + sh_ref[...]
        h_scr[...] = h.astype(BF16)
        y_ref[...] = jnp.zeros_like(y_ref)

    u = jnp.maximum(jnp.dot(h_scr[...], wu_ref[...].astype(BF16), preferred_element_type=F32), 0.0)
    y_ref[...] += jnp.dot((u * u).astype(BF16), wd_ref[...].astype(BF16), preferred_element_type=F32)

    @pl.when(f == pl.num_programs(1) - 1)
    def _():
        y_ref[...] = x_ref[...] + gt_ref[...] * (_rms(y_ref[...]) * gpost_ref[...])


def _mlp(x, mod, per_row, rows_per_seq, g_pre, g_post, w_up, w_down, tm, tf):
    m = x.shape[0]
    row = pl.BlockSpec((tm, D_MODEL), lambda i, j: (i, 0))
    vec = pl.BlockSpec((1, D_MODEL), lambda i, j: (0, 0))
    return pl.pallas_call(
        _mlp_kernel,
        grid=(m // tm, D_FF // tf),
        in_specs=[pl.BlockSpec((tm, D_MODEL), lambda i, j: (i, 0), pipeline_mode=pl.Buffered(1)),
                  _mod_spec(per_row, rows_per_seq, tm, 3),
                  _mod_spec(per_row, rows_per_seq, tm, 4),
                  _mod_spec(per_row, rows_per_seq, tm, 5),
                  vec, vec,
                  pl.BlockSpec((D_MODEL, tf), lambda i, j: (0, j)),
                  pl.BlockSpec((tf, D_MODEL), lambda i, j: (j, 0))],
        out_specs=row,
        out_shape=jax.ShapeDtypeStruct((m, D_MODEL), F32),
        scratch_shapes=[pltpu.VMEM((tm, D_MODEL), BF16)],
        compiler_params=_params("parallel", "arbitrary"),
        name="mlp",
    )(x, mod, mod, mod, g_pre, g_post, w_up, w_down)


def _pick_tile(m, target):
    t = min(m, target)
    while m % t:
        t //= 2
    return t


def kernel(x_prompt, x_sample, c_prompt, c_sample, state_hgrn, state_ssm, state_conv, w_ada, b_ada, norm_pre_mix, norm_post_mix, norm_pre_mlp, norm_post_mlp, w_in, hgrn_lb_logits, hgrn_norm, conv_w, conv_b, dt_bias, a_log, d_skip, ssd_norm, w_out, w_up, w_down):
    n_seq, t, _ = x_prompt.shape
    n_dec = x_sample.shape[0]
    assert x_sample.shape[1] == 1 and t % CHUNK_A == 0 and t % CHUNK_B == 0
    assert w_ada.shape[0] == 1, "one layer"
    l = 0

    w_in_t = w_in[l].T
    w_out_b = w_out[l].astype(BF16)
    g_pre_mix, g_post_mix = norm_pre_mix[l][None], norm_post_mix[l][None]
    g_pre_mlp, g_post_mlp = norm_pre_mlp[l][None], norm_post_mlp[l][None]
    cw, cb = conv_w[l], conv_b[l][None]
    hn = hgrn_norm[l][None]
    sn = ssd_norm[l][None]

    n_c = n_seq + n_dec
    pad = (-n_c) % SUBLANE
    c_all = jnp.concatenate([c_prompt, c_sample, jnp.zeros((pad, D_MODEL), F32)], axis=0)
    mod = _ada(c_all, w_ada[l], b_ada[l][None])
    mod_p = mod[:n_seq].reshape(n_seq, 1, N_MOD * D_MODEL)
    mod_s = mod[n_seq:n_c]

    xp = x_prompt.reshape(n_seq * t, D_MODEL)
    tn_p = HGRN_HEADS_PER_STEP * HA_D
    proj_p, dt_p = _inproj(xp, mod_p, False, t, g_pre_mix, w_in_t, _pick_tile(t, 2048), tn_p,
                           functools.partial(_hgrn_interleave, tn=tn_p))
    o_a, s_hgrn_p = _hgrn_prompt(proj_p, hgrn_lb_logits, hn, n_seq, t)

    n_chunks = t // CHUNK_B
    rep = SUBLANE // HPG
    dt_rows = dt_p.reshape(n_seq, n_chunks, CHUNK_B, B_G, 1, HPG).transpose(3, 0, 1, 4, 5, 2)
    dt_rows = jnp.broadcast_to(dt_rows, (B_G, n_seq, n_chunks, rep, HPG, CHUNK_B))
    dt_rows = dt_rows.reshape(B_G, n_seq, n_chunks * SUBLANE, CHUNK_B)
    prow = lambda p: jnp.tile(p.reshape(B_G, 1, HPG), (1, n_chunks * rep, 1)).reshape(B_G, n_chunks * SUBLANE, 1)
    cpar = _ssd_channel_params(cw, cb, sn, d_skip[l], SSD_GROUPS_PER_STEP)
    dpar = jnp.concatenate([prow(dt_bias[l]), prow(a_log[l])], axis=2)
    o_b, s_ssm_p = _ssd_prompt(proj_p, dt_rows, cpar, dpar, n_seq, t)
    x1 = _outproj(o_a, o_b, w_out_b, xp, mod_p, False, t, g_post_mix, _pick_tile(t, 256))
    y_p = _mlp(x1, mod_p, False, t, g_pre_mlp, g_post_mlp, w_up[l], w_down[l], _pick_tile(t, 1024), 512)
    conv_p = proj_p.reshape(n_seq, t, D_MAIN)[:, t - (CONV_W - 1):, OFF_X:]

    xs = x_sample.reshape(n_dec, D_MODEL)
    proj_s, dt_s = _inproj(xs, mod_s, True, 1, g_pre_mix, w_in_t, n_dec, 2048)
    bt = _pick_tile(n_dec, SUBLANE)
    to_t = lambda a, h, d: a.reshape(a.shape[:-1] + (h, d)).swapaxes(-1, -2)
    heads = lambda off: proj_s[:, off:off + D_A].reshape(n_dec, H_A, HA_D)
    o_a_s, s_hgrn_s = _hgrn_step(heads(OFF_Q), heads(OFF_F), heads(OFF_I), heads(OFF_G),
                                 hgrn_lb_logits.reshape(-1, H_A, HA_D), hn.reshape(H_A, HA_D),
                                 state_hgrn[l], bt)

    xbc_s = proj_s[:, OFF_X:]
    cst = state_conv[l]
    xt = to_t(xbc_s[:, :D_B], H_B, B_P)
    xr = xbc_s[:, None, :D_B]
    bcr = xbc_s[:, D_B:].reshape(n_dec, 2 * B_G, B_N)
    cxt = to_t(cst[:, :, :D_B], H_B, B_P)
    cxr = cst[:, :, None, :D_B]
    cbc = cst[:, :, D_B:].reshape(n_dec, CONV_W - 1, 2 * B_G, B_N)
    wxt = to_t(cw[:, :D_B], H_B, B_P)
    wxr = cw[:, None, :D_B]
    wbc = cw[:, D_B:].reshape(CONV_W, 2 * B_G, B_N)
    bxt = to_t(cb[:, :D_B], H_B, B_P)[0]
    bxr = cb[:, :D_B]
    bbc = cb[0, D_B:].reshape(2 * B_G, B_N)
    o_b_s, s_ssm_s = _ssd_step(
        xt, xr, bcr, proj_s[:, OFF_Z:OFF_Z + D_B], dt_s[:, None, :],
        cxt, cxr, cbc, wxt, wxr, wbc, bxt, bxr, bbc,
        dt_bias[l][None], a_log[l][None], jnp.repeat(d_skip[l], B_P)[None], sn,
        state_ssm[l], bt)
    x1_s = _outproj(o_a_s.reshape(n_dec, D_A), o_b_s, w_out_b, xs, mod_s, True, 1, g_post_mix, n_dec)
    y_s = _mlp(x1_s, mod_s, True, 1, g_pre_mlp, g_post_mlp, w_up[l], w_down[l], n_dec, 1024)
    conv_s = jnp.concatenate([cst[:, 1:], xbc_s[:, None, :]], axis=1)

    return (y_p.reshape(n_seq, t, D_MODEL), y_s.reshape(n_dec, 1, D_MODEL),
            s_hgrn_p[None], s_ssm_p[None], conv_p[None],
            s_hgrn_s[None], s_ssm_s[None], conv_s[None])
```

```python
import functools

import jax
import jax.numpy as jnp
from jax import lax
from jax.experimental import pallas as pl
from jax.experimental.pallas import tpu as pltpu

F32 = jnp.float32
BF16 = jnp.bfloat16

D_MODEL = 2048
D_A = 2048
HA_D = 128
H_A = D_A // HA_D
D_B = 2048
B_P = 64
H_B = D_B // B_P
B_G = 8
HPG = H_B // B_G
GW = HPG * B_P
B_N = 128
CONV_W = 4
CONV_DIM = D_B + 2 * B_G * B_N
D_MAIN = 4 * D_A + D_B + CONV_DIM
D_FF = 4 * D_MODEL
N_MOD = 6
EPS = 1e-6

OFF_Q, OFF_F, OFF_I, OFF_G = 0, D_A, 2 * D_A, 3 * D_A
OFF_Z = 4 * D_A
OFF_X = OFF_Z + D_B
OFF_B = OFF_X + D_B
OFF_C = OFF_B + B_G * B_N
REGION_SILU = (0, 3, 4)
REGION_FORGET = 1

LANE = 128
SUBLANE = 8
VMEM_LIMIT = 56 * 1024 * 1024

CHUNK_A = 128
SUB_A = 16
CHUNK_B = 128
HGRN_HEADS_PER_STEP = 4
SSD_GROUPS_PER_STEP = 2


def _tiles(t, n_dec):
    return dict(
        inproj_p=(_pick_tile(t, 2048), HGRN_HEADS_PER_STEP * HA_D), inproj_s=(n_dec, 2048),
        outproj_p=_pick_tile(t, 256), outproj_s=n_dec,
        mlp_p=(_pick_tile(t, 1024), 512), mlp_s=(n_dec, 1024),
        ada_n=1024, step_b=_pick_tile(n_dec, SUBLANE))


def _pick_tile(m, target):
    t = min(m, target)
    while m % t:
        t //= 2
    return t


def _silu(x):
    hx = 0.5 * x
    return hx * jnp.tanh(hx) + hx


def _softplus(x):
    return jnp.maximum(x, 0.0) + jnp.log1p(jnp.exp(-jnp.abs(x)))


def _idiv(x, d):
    assert d & (d - 1) == 0
    return jnp.right_shift(x, d.bit_length() - 1)


def _rms(x):
    return x * lax.rsqrt(jnp.mean(x * x, axis=-1, keepdims=True) + EPS)


def _cumsum_rows(x, n, row):
    s = 1
    while s < n:
        x = x + jnp.where(row >= s, pltpu.roll(x, s, 0), 0.0)
        s *= 2
    return x


def _cumsum_lanes(x, n, lane):
    s = 1
    while s < n:
        x = x + jnp.where(lane >= s, pltpu.roll(x, s, 1), 0.0)
        s *= 2
    return x


def _dot(a, b):
    return jnp.dot(a.astype(BF16), b.astype(BF16), preferred_element_type=F32)


def _dot_nt(a, b):
    return lax.dot_general(a.astype(BF16), b.astype(BF16), (((1,), (1,)), ((), ())),
                           preferred_element_type=F32)


def _exact_terms(x):
    hi = x.astype(BF16).astype(F32)
    mid = (x - hi).astype(BF16).astype(F32)
    lo = x - hi - mid
    return jnp.concatenate([hi, mid, lo, jnp.zeros_like(hi)], axis=0).astype(BF16)


def _transpose_bcast(x, sel):
    return lax.dot_general(_exact_terms(x), sel, (((0,), (0,)), ((), ())),
                           preferred_element_type=F32)


def _params(*sem):
    return pltpu.CompilerParams(dimension_semantics=sem, vmem_limit_bytes=VMEM_LIMIT)


def _ada_kernel(c_ref, w_ref, b_ref, o_ref):
    o_ref[...] = _dot(_silu(c_ref[...]), w_ref[...]) + b_ref[...]


def _ada(c_all, w_ada, b_ada, tn):
    m = c_all.shape[0]
    n = w_ada.shape[1]
    return pl.pallas_call(
        _ada_kernel,
        grid=(n // tn,),
        in_specs=[pl.BlockSpec((m, D_MODEL), lambda j: (0, 0)),
                  pl.BlockSpec((D_MODEL, tn), lambda j: (0, j)),
                  pl.BlockSpec((1, tn), lambda j: (0, j))],
        out_specs=pl.BlockSpec((m, tn), lambda j: (0, j)),
        out_shape=jax.ShapeDtypeStruct((m, n), F32),
        compiler_params=_params("parallel"),
        name="ada_mod",
    )(c_all, w_ada, b_ada)


def _mod_spec(per_row, rows_per_seq, tm, col):
    if per_row:
        return pl.BlockSpec((tm, D_MODEL), lambda i, j: (i, col))
    tiles_per_seq = rows_per_seq // tm
    return pl.BlockSpec((None, 1, D_MODEL), lambda i, j: (i // tiles_per_seq, 0, col))


def _lower_bound(logits):
    e = jnp.exp(logits - jnp.max(logits, axis=0, keepdims=True))
    return e[0:1] / jnp.sum(e, axis=0, keepdims=True)


def _inproj_kernel(x_ref, sh_ref, sc_ref, g_ref, w_ref, wdt_ref, lbl_ref, o_ref, odt_ref, h_scr, *, tn):
    j = pl.program_id(1)

    @pl.when(j == 0)
    def _():
        h = _rms(x_ref[...]) * g_ref[...] * (1.0 + sc_ref[...]) + sh_ref[...]
        hb = h.astype(BF16)
        h_scr[...] = hb
        odt_ref[...] = _dot_nt(hb, wdt_ref[...])

    region = (j * tn) // D_A
    is_silu = functools.reduce(jnp.logical_or, [region == r for r in REGION_SILU])

    @pl.when(is_silu)
    def _():
        o_ref[...] = _silu(_dot_nt(h_scr[...], w_ref[...]))

    @pl.when(region == REGION_FORGET)
    def _():
        lb = _lower_bound(lbl_ref[...])
        half = 0.5 * (1.0 - lb)
        o_ref[...] = jnp.log((lb + half) + half * jnp.tanh(0.5 * _dot_nt(h_scr[...], w_ref[...])))

    @pl.when(jnp.logical_not(jnp.logical_or(is_silu, region == REGION_FORGET)))
    def _():
        o_ref[...] = _dot_nt(h_scr[...], w_ref[...])


def _inproj(x, mod, per_row, rows_per_seq, g_pre, w_t, lb_logits, tm, tn, out_tile=lambda j: j):
    m = x.shape[0]
    assert D_MAIN % tn == 0 and D_MAIN % H_B == 0 and D_A % tn == 0
    f_tiles = D_A // tn
    return pl.pallas_call(
        functools.partial(_inproj_kernel, tn=tn),
        grid=(m // tm, D_MAIN // tn),
        in_specs=[pl.BlockSpec((tm, D_MODEL), lambda i, j: (i, 0), pipeline_mode=pl.Buffered(1)),
                  _mod_spec(per_row, rows_per_seq, tm, 0),
                  _mod_spec(per_row, rows_per_seq, tm, 1),
                  pl.BlockSpec((1, D_MODEL), lambda i, j: (0, 0)),
                  pl.BlockSpec((tn, D_MODEL), lambda i, j: (j, 0)),
                  pl.BlockSpec((H_B, D_MODEL), lambda i, j: (D_MAIN // H_B, 0)),
                  pl.BlockSpec((lb_logits.shape[0], tn),
                               lambda i, j: (0, jnp.clip(j - REGION_FORGET * f_tiles, 0, f_tiles - 1)))],
        out_specs=[pl.BlockSpec((tm, tn), lambda i, j: (i, out_tile(j))),
                   pl.BlockSpec((tm, H_B), lambda i, j: (i, 0))],
        out_shape=[jax.ShapeDtypeStruct((m, D_MAIN), F32),
                   jax.ShapeDtypeStruct((m, H_B), F32)],
        scratch_shapes=[pltpu.VMEM((tm, D_MODEL), BF16)],
        compiler_params=_params("parallel", "arbitrary"),
        name="in_proj",
    )(x, mod, mod, g_pre, w_t, w_t, lb_logits)


def _hgrn_interleave(j, tn):
    n_hg = D_A // tn
    return jnp.where(j < 4 * n_hg, 4 * (j % n_hg) + j // n_hg, j)


def _hgrn_scores(qs, lf, row):
    c_len = CHUNK_A
    n_sub = c_len // SUB_A
    k = 1.0 - jnp.exp(lf)
    b = _cumsum_rows(lf, SUB_A, row & (SUB_A - 1))
    k_acc = []
    blocks, q_dec = [], []
    start = jnp.zeros((1, HA_D), F32)
    for i in range(n_sub):
        lo = i * SUB_A
        hi = lo + SUB_A
        b_i = b[lo:hi, :]
        b_sub = b[hi - 1:hi, :]
        k_i = k[lo:hi, :]
        qg = qs[lo:hi, :] * jnp.exp(b_i)
        k_diag = k_i * jnp.exp(-b_i)
        rhs = jnp.concatenate([p.astype(BF16) for p in k_acc] + [k_diag.astype(BF16)], axis=0)
        a_i = lax.dot_general(qg.astype(BF16), rhs, (((1,), (1,)), ((), ())),
                              preferred_element_type=F32)
        causal = (lax.broadcasted_iota(jnp.int32, (SUB_A, hi), 1)
                  <= lax.broadcasted_iota(jnp.int32, (SUB_A, hi), 0) + lo)
        a_i = jnp.where(causal, a_i, 0.0)
        blocks.append(jnp.pad(a_i, ((0, 0), (0, c_len - hi))).astype(BF16))
        q_dec.append((qg * jnp.exp(start)).astype(BF16))
        sub_decay = jnp.exp(b_sub)
        k_acc = [p * sub_decay for p in k_acc] + [k_i * jnp.exp(b_sub - b_i)]
        start = start + b_sub
    scores = jnp.concatenate(blocks, axis=0)
    k_end = jnp.concatenate([p.astype(BF16) for p in k_acc], axis=0)
    chunk_decay = jnp.exp(start)
    return scores, jnp.concatenate(q_dec, axis=0), k_end, chunk_decay


def _hgrn_apply(pend, v, gs, nw, st):
    scores, q_dec, k_end, chunk_decay = pend
    vb = v.astype(BF16)
    o = (jnp.dot(scores, vb, preferred_element_type=F32)
         + lax.dot_general(q_dec, st.astype(BF16), (((1,), (1,)), ((), ())),
                           preferred_element_type=F32))
    st = chunk_decay * st + lax.dot_general(vb, k_end, (((0,), (0,)), ((), ())),
                                            preferred_element_type=F32)
    return (_rms(o) * nw * gs).astype(BF16), st


def _hgrn_prompt_kernel(p_ref, nw_ref, o_ref, s_ref, *, n_chunks, n_heads):
    c_len = CHUNK_A
    w = n_heads * HA_D
    nw = nw_ref[...]
    row = lax.broadcasted_iota(jnp.int32, (c_len, HA_D), 0)
    heads = [slice(h * HA_D, (h + 1) * HA_D) for h in range(n_heads)]

    def part(k, sl, hs):
        return p_ref[sl, k * w + hs.start:k * w + hs.stop]

    def rows(c):
        return pl.ds(pl.multiple_of(c * c_len, c_len), c_len)

    def scores_of(c):
        sl = rows(c)
        return tuple(_hgrn_scores(part(0, sl, hs), part(1, sl, hs), row) for hs in heads)

    def apply_to(c, pends, sts):
        sl = rows(c)
        new = []
        for hs, pend, st in zip(heads, pends, sts):
            out, st = _hgrn_apply(pend, part(2, sl, hs), part(3, sl, hs), nw[:, hs], st)
            o_ref[sl, hs] = out
            new.append(st)
        return tuple(new)

    def step(c, carry):
        pends, sts = carry
        sts = apply_to(c - 1, pends, sts)
        return scores_of(c), sts

    zeros = tuple(jnp.zeros((HA_D, HA_D), F32) for _ in range(n_heads))
    pends, sts = lax.fori_loop(1, n_chunks, step, (scores_of(0), zeros))
    sts = apply_to(n_chunks - 1, pends, sts)
    for h in range(n_heads):
        s_ref[h] = sts[h].T


def _hgrn_prompt(proj, hgrn_norm, n_seq, t):
    m = proj.shape[0]
    nh = HGRN_HEADS_PER_STEP
    w = nh * HA_D
    return pl.pallas_call(
        functools.partial(_hgrn_prompt_kernel, n_chunks=t // CHUNK_A, n_heads=nh),
        grid=(n_seq, H_A // nh),
        in_specs=[pl.BlockSpec((t, 4 * w), lambda b, h: (b, h)),
                  pl.BlockSpec((1, w), lambda b, h: (0, h))],
        out_specs=[pl.BlockSpec((t, w), lambda b, h: (b, h)),
                   pl.BlockSpec((None, nh, HA_D, HA_D), lambda b, h: (b, h, 0, 0))],
        out_shape=[jax.ShapeDtypeStruct((m, D_A), BF16),
                   jax.ShapeDtypeStruct((n_seq, H_A, HA_D, HA_D), F32)],
        compiler_params=_params("parallel", "parallel"),
        name="hgrn_prompt",
    )(proj, hgrn_norm)


def _conv_chunk(tail, u, w, bias, n):
    full = jnp.concatenate([tail, u], axis=0)
    out = bias + w[CONV_W - 1:CONV_W, :] * u
    for i in range(CONV_W - 1):
        shifted = pltpu.roll(full, CONV_W - 1 - i, 0)[SUBLANE:SUBLANE + n, :]
        out = out + w[i:i + 1, :] * shifted
    return _silu(out)


def _ssd_prompt_kernel(z_ref, x_ref, b_ref, c_ref, dtr_ref, cpar_ref, dpar_ref,
                       o_ref, h_ref, row_scr, *, n_chunks, n_groups):
    n = CHUNK_B
    assert n == LANE
    tri = (lax.broadcasted_iota(jnp.int32, (n, n), 0)
           >= lax.broadcasted_iota(jnp.int32, (n, n), 1))
    lane_head = _idiv(lax.broadcasted_iota(jnp.int32, (1, GW), 1), B_P)
    row_head = _idiv(lax.broadcasted_iota(jnp.int32, (GW, 1), 0), B_P)
    nr = n_chunks * SUBLANE
    lane_nr = lax.broadcasted_iota(jnp.int32, (nr, n), 1)
    is_dt = (lax.broadcasted_iota(jnp.int32, (nr, n), 0) & (SUBLANE - 1)) < HPG
    is_dt8 = lax.broadcasted_iota(jnp.int32, (SUBLANE, n), 0) < HPG
    n_src = 2 * SUBLANE
    sel_shape = (4 * n_src, HPG * n + 2 * GW)
    sel_k = lax.broadcasted_iota(jnp.int32, sel_shape, 0)
    sel_l = lax.broadcasted_iota(jnp.int32, sel_shape, 1)
    want = jnp.where(sel_l < HPG * n, HPG + _idiv(sel_l, n),
                     2 * HPG + _idiv(sel_l - HPG * n, B_P))
    bcast_sel = jnp.where(((sel_k & (n_src - 1)) == want) & (sel_k < 3 * n_src), 1.0, 0.0).astype(BF16)

    groups = []
    for gi in range(n_groups):
        xs = slice(gi * GW, (gi + 1) * GW)
        bs = slice(gi * B_N, (gi + 1) * B_N)
        dpar = dpar_ref[gi]
        dt_all = _softplus(dtr_ref[gi] + dpar[:, 0:1])
        cum_all = _cumsum_lanes(dt_all * -jnp.exp(dpar[:, 1:2]), n, lane_nr)
        row_scr[gi] = jnp.where(is_dt, dt_all, cum_all).reshape(n_chunks, SUBLANE, n)
        bl = slice(n_groups * GW + gi * B_N, n_groups * GW + (gi + 1) * B_N)
        cl = slice(n_groups * (GW + B_N) + gi * B_N, n_groups * (GW + B_N) + (gi + 1) * B_N)
        taps, bias = slice(0, CONV_W), slice(CONV_W, CONV_W + 1)
        groups.append(dict(gi=gi, xs=xs, bs=bs, dskip=cpar_ref[6:7, xs], nw=cpar_ref[5:6, xs],
                           wx=cpar_ref[taps, xs], wb=cpar_ref[taps, bl], wc=cpar_ref[taps, cl],
                           bx=cpar_ref[bias, xs], bb=cpar_ref[bias, bl], bc=cpar_ref[bias, cl]))

    def rows(c):
        return pl.ds(pl.multiple_of(c * n, n), n)

    def scan_free(c, grp):
        xs, bs = grp["xs"], grp["bs"]
        sl = rows(c)
        if isinstance(c, int) and c == 0:
            tail = lambda ref, lanes: jnp.zeros((SUBLANE, lanes.stop - lanes.start), F32)
        else:
            prev = pl.ds(pl.multiple_of(c * n - SUBLANE, SUBLANE), SUBLANE)
            tail = lambda ref, lanes: ref[prev, lanes]
        xc = _conv_chunk(tail(x_ref, xs), x_ref[sl, xs], grp["wx"], grp["bx"], n)
        bcv = _conv_chunk(tail(b_ref, bs), b_ref[sl, bs], grp["wb"], grp["bb"], n)
        ccv = _conv_chunk(tail(c_ref, bs), c_ref[sl, bs], grp["wc"], grp["bc"], n)
        r8 = row_scr[grp["gi"], c]
        swapped = pltpu.roll(r8, HPG, 0)
        cum8 = jnp.where(is_dt8, swapped, r8)
        dt8 = jnp.where(is_dt8, r8, swapped)
        e8 = jnp.where(is_dt8, jnp.exp(cum8), jnp.exp(cum8[:, n - 1:n] - cum8) * dt8)
        cols = _transpose_bcast(jnp.concatenate([r8, e8], axis=0), bcast_sel)
        e_cum = cols[:, HPG * n:HPG * n + GW]
        w_end = cols[:, HPG * n + GW:]
        g = _dot_nt(ccv, bcv)
        xb = xc.astype(BF16)
        s_all, x_all = [], []
        for j in range(HPG):
            cc = cols[:, j * n:(j + 1) * n]
            seg = cc - r8[HPG + j:HPG + j + 1, :]
            s = g * jnp.exp(jnp.where(tri, seg, -jnp.inf)) * r8[j:j + 1, :]
            s_all.append(s.astype(BF16))
            x_all.append(jnp.where(lane_head == j, xb, jnp.zeros_like(xb)))
        y = grp["dskip"] * xc + jnp.dot(jnp.concatenate(s_all, axis=1), jnp.concatenate(x_all, axis=0),
                                        preferred_element_type=F32)
        return y, e_cum, ccv.astype(BF16), (xc * w_end).astype(BF16), bcv.astype(BF16), r8

    def apply_state(c, grp, pend, h):
        xs = grp["xs"]
        y, e_cum, ccb, xwb, bcb, r8 = pend
        sl = rows(c)
        ch = lax.dot_general(ccb, h.astype(BF16), (((1,), (1,)), ((), ())),
                             preferred_element_type=F32)
        yz = (y + ch * e_cum) * z_ref[sl, xs]
        o_ref[sl, xs] = (_rms(yz) * grp["nw"]).astype(o_ref.dtype)
        h_dec = jnp.zeros((GW, 1), F32)
        for j in range(HPG):
            h_dec = jnp.where(row_head == j, jnp.exp(r8[HPG + j:HPG + j + 1, n - 1:n]), h_dec)
        return h_dec * h + lax.dot_general(xwb, bcb, (((0,), (0,)), ((), ())),
                                           preferred_element_type=F32)

    def step(c, carry):
        pends, hs = carry
        hs = tuple(apply_state(c - 1, grp, pend, h) for grp, pend, h in zip(groups, pends, hs))
        return tuple(scan_free(c, grp) for grp in groups), hs

    init = (tuple(scan_free(0, grp) for grp in groups),
            tuple(jnp.zeros((GW, B_N), F32) for _ in groups))
    pends, hs = lax.fori_loop(1, n_chunks, step, init)
    for gi, (grp, pend, h) in enumerate(zip(groups, pends, hs)):
        h = apply_state(n_chunks - 1, grp, pend, h)
        h_ref[gi * HPG:(gi + 1) * HPG] = h.reshape(HPG, B_P, B_N)


def _ssd_channel_params(conv_w, conv_b, ssd_norm, d_skip, ng):
    zeros = jnp.zeros((1, 2 * B_G * B_N), F32)
    rows = jnp.concatenate([
        conv_w, conv_b,
        jnp.concatenate([ssd_norm, zeros], axis=1),
        jnp.concatenate([jnp.repeat(d_skip, B_P)[None], zeros], axis=1),
        jnp.zeros((SUBLANE - CONV_W - 3, CONV_DIM), F32)], axis=0)
    n_steps = B_G // ng
    part = lambda lo, width: rows[:, lo:lo + n_steps * width].reshape(SUBLANE, n_steps, width)
    slab = jnp.concatenate([part(0, ng * GW), part(D_B, ng * B_N), part(D_B + B_G * B_N, ng * B_N)], axis=2)
    return slab.transpose(1, 0, 2)


def _ssd_prompt(proj, dt_rows, cpar, dpar, n_seq, t):
    m = proj.shape[0]
    n_chunks = t // CHUNK_B
    nr = n_chunks * SUBLANE
    ng = SSD_GROUPS_PER_STEP
    xw, bw = ng * GW, ng * B_N
    wide = lambda off: pl.BlockSpec((t, xw), lambda b, g: (b, off // xw + g))
    narrow = lambda off: pl.BlockSpec((t, bw), lambda b, g: (b, off // bw + g))
    return pl.pallas_call(
        functools.partial(_ssd_prompt_kernel, n_chunks=n_chunks, n_groups=ng),
        grid=(n_seq, B_G // ng),
        in_specs=[wide(OFF_Z), wide(OFF_X), narrow(OFF_B), narrow(OFF_C),
                  pl.BlockSpec((ng, None, nr, CHUNK_B), lambda b, g: (g, b, 0, 0)),
                  pl.BlockSpec((None, SUBLANE, xw + 2 * bw), lambda b, g: (g, 0, 0)),
                  pl.BlockSpec((ng, nr, 2), lambda b, g: (g, 0, 0))],
        out_specs=[pl.BlockSpec((t, xw), lambda b, g: (b, g)),
                   pl.BlockSpec((None, ng * HPG, B_P, B_N), lambda b, g: (b, g, 0, 0))],
        out_shape=[jax.ShapeDtypeStruct((m, D_B), BF16),
                   jax.ShapeDtypeStruct((n_seq, H_B, B_P, B_N), F32)],
        scratch_shapes=[pltpu.VMEM((ng, n_chunks, SUBLANE, CHUNK_B), F32)],
        compiler_params=_params("parallel", "parallel"),
        name="ssd_prompt",
    )(proj, proj, proj, proj, dt_rows, cpar, dpar)


def _hgrn_step_kernel(q_ref, f_ref, v_ref, g_ref, nw_ref, s_ref, o_ref, so_ref, *, bt):
    nw = nw_ref[...]
    sel_k = lax.broadcasted_iota(jnp.int32, (4 * H_A, H_A * HA_D), 0)
    sel_h = _idiv(lax.broadcasted_iota(jnp.int32, (4 * H_A, H_A * HA_D), 1), HA_D)
    bcast_sel = jnp.where(((sel_k & (H_A - 1)) == sel_h) & (sel_k < 3 * H_A), 1.0, 0.0).astype(BF16)

    def seq(b, carry):
        f_cols = _transpose_bcast(jnp.exp(f_ref[b]), bcast_sel)
        qs = q_ref[b]
        v = v_ref[b]
        rows = []
        for h in range(H_A):
            fc = f_cols[:, h * HA_D:(h + 1) * HA_D]
            s_new = fc * s_ref[b, h] + (1.0 - fc) * v[h:h + 1, :]
            so_ref[b, h] = s_new
            q8 = jnp.broadcast_to(qs[h:h + 1, :], (SUBLANE, HA_D))
            rows.append(_dot(q8, s_new)[0:1, :])
        o = jnp.concatenate(rows, axis=0)
        o_ref[b] = (_rms(o) * nw * g_ref[b]).astype(o_ref.dtype)
        return carry

    lax.fori_loop(0, bt, seq, 0)


def _hgrn_step(q, f, v, g, nw, state, bt):
    nb = q.shape[0]
    rw = pl.BlockSpec((bt, H_A, HA_D), lambda i: (i, 0, 0))
    st = pl.BlockSpec((bt, H_A, HA_D, HA_D), lambda i: (i, 0, 0, 0))
    return pl.pallas_call(
        functools.partial(_hgrn_step_kernel, bt=bt),
        grid=(nb // bt,),
        in_specs=[rw, rw, rw, rw, pl.BlockSpec((H_A, HA_D), lambda i: (0, 0)), st],
        out_specs=[rw, st],
        out_shape=[jax.ShapeDtypeStruct((nb, H_A, HA_D), BF16),
                   jax.ShapeDtypeStruct(state.shape, F32)],
        compiler_params=_params("parallel"),
        name="hgrn_step",
    )(q, f, v, g, nw, state)


def _conv_step(buf_ref, b, u, w, bias):
    out = bias + w[CONV_W - 1] * u
    for i in range(CONV_W - 1):
        out = out + w[i] * buf_ref[b, i]
    return _silu(out)


def _ssd_step_kernel(xt_ref, xr_ref, bcr_ref, z_ref, dt_ref,
                     cxt_ref, cxr_ref, cbc_ref,
                     wxt_ref, wxr_ref, wbc_ref, bxt_ref, bxr_ref, bbc_ref,
                     dtb_ref, al_ref, dskr_ref, nw_ref, h_ref,
                     o_ref, ho_ref, y_scr, xc_scr, *, bt):
    wxt, wxr, wbc = wxt_ref[...], wxr_ref[...], wbc_ref[...]
    bxt, bxr, bbc = bxt_ref[...], bxr_ref[...], bbc_ref[...]
    a = -jnp.exp(al_ref[...])
    dtb = dtb_ref[...]

    def seq(b, carry):
        xt = _conv_step(cxt_ref, b, xt_ref[b], wxt, bxt)
        xc_scr[pl.ds(b, 1), :] = _conv_step(cxr_ref, b, xr_ref[b], wxr, bxr)
        bcv = _conv_step(cbc_ref, b, bcr_ref[b], wbc, bbc)
        dt = _softplus(dt_ref[b] + dtb)
        d_a = jnp.exp(dt * a)
        xdt = xt * dt
        ys = []
        for g in range(B_G):
            b_row = bcv[g:g + 1, :]
            c_row = bcv[B_G + g:B_G + g + 1, :]
            hs = []
            for j in range(HPG):
                hh = g * HPG + j
                h_new = d_a[:, hh:hh + 1] * h_ref[b, hh] + xdt[:, hh:hh + 1] * b_row
                ho_ref[b, hh] = h_new
                hs.append(h_new)
            hg = jnp.concatenate(hs, axis=0)
            c8 = jnp.broadcast_to(c_row, (SUBLANE, B_N))
            ys.append(_dot_nt(c8, hg)[0:1, :])
        y_scr[pl.ds(b, 1), :] = jnp.concatenate(ys, axis=1)
        return carry

    lax.fori_loop(0, bt, seq, 0)
    yz = (y_scr[...] + dskr_ref[...] * xc_scr[...]) * z_ref[...]
    nw = nw_ref[...]
    for g in range(B_G):
        sl = slice(g * GW, (g + 1) * GW)
        o_ref[:, sl] = (_rms(yz[:, sl]) * nw[:, sl]).astype(o_ref.dtype)


def _ssd_step(xt, xr, bcr, z, dt, cxt, cxr, cbc, wxt, wxr, wbc, bxt, bxr, bbc,
              dtb, al, dskr, nw, state, bt):
    nb = xt.shape[0]
    full = lambda a: pl.BlockSpec(a.shape, lambda i: (0,) * a.ndim)
    lead = lambda a: pl.BlockSpec((bt,) + a.shape[1:], lambda i: (i,) + (0,) * (a.ndim - 1))
    return pl.pallas_call(
        functools.partial(_ssd_step_kernel, bt=bt),
        grid=(nb // bt,),
        in_specs=[lead(xt), lead(xr), lead(bcr), lead(z), lead(dt),
                  lead(cxt), lead(cxr), lead(cbc),
                  full(wxt), full(wxr), full(wbc), full(bxt), full(bxr), full(bbc),
                  full(dtb), full(al), full(dskr), full(nw), lead(state)],
        out_specs=[pl.BlockSpec((bt, D_B), lambda i: (i, 0)), lead(state)],
        out_shape=[jax.ShapeDtypeStruct((nb, D_B), F32),
                   jax.ShapeDtypeStruct(state.shape, F32)],
        scratch_shapes=[pltpu.VMEM((bt, D_B), F32), pltpu.VMEM((bt, D_B), F32)],
        compiler_params=_params("parallel"),
        name="ssd_step",
    )(xt, xr, bcr, z, dt, cxt, cxr, cbc, wxt, wxr, wbc, bxt, bxr, bbc, dtb, al, dskr, nw, state)


def _outproj_kernel(oa_ref, ob_ref, wa_ref, wb_ref, x_ref, gt_ref, gp_ref, y_ref):
    mix = _dot(oa_ref[...], wa_ref[...]) + _dot(ob_ref[...], wb_ref[...])
    y_ref[...] = x_ref[...] + gt_ref[...] * (_rms(mix) * gp_ref[...])


def _outproj(o_a, o_b, w_out, x, mod, per_row, rows_per_seq, g_post, tm):
    m = x.shape[0]
    row = lambda w: pl.BlockSpec((tm, w), lambda i, j: (i, 0))
    return pl.pallas_call(
        _outproj_kernel,
        grid=(m // tm, 1),
        in_specs=[row(D_A), row(D_B),
                  pl.BlockSpec((D_A, D_MODEL), lambda i, j: (0, 0)),
                  pl.BlockSpec((D_B, D_MODEL), lambda i, j: (1, 0)),
                  row(D_MODEL),
                  _mod_spec(per_row, rows_per_seq, tm, 2),
                  pl.BlockSpec((1, D_MODEL), lambda i, j: (0, 0))],
        out_specs=row(D_MODEL),
        out_shape=jax.ShapeDtypeStruct((m, D_MODEL), F32),
        compiler_params=_params("parallel", "arbitrary"),
        name="out_proj",
    )(o_a, o_b, w_out, w_out, x, mod, g_post)


def _mlp_kernel(x_ref, sh_ref, sc_ref, gt_ref, gpre_ref, gpost_ref, wu_ref, wd_ref, y_ref, h_scr):
    f = pl.program_id(1)

    @pl.when(f == 0)
    def _():
        h = _rms(x_ref[...]) * gpre_ref[...] * (1.0 + sc_ref[...]) + sh_ref[...]
        h_scr[...] = h.astype(BF16)
        y_ref[...] = jnp.zeros_like(y_ref)

    u = jnp.maximum(jnp.dot(h_scr[...], wu_ref[...].astype(BF16), preferred_element_type=F32), 0.0)
    y_ref[...] += jnp.dot((u * u).astype(BF16), wd_ref[...].astype(BF16), preferred_element_type=F32)

    @pl.when(f == pl.num_programs(1) - 1)
    def _():
        y_ref[...] = x_ref[...] + gt_ref[...] * (_rms(y_ref[...]) * gpost_ref[...])


def _mlp(x, mod, per_row, rows_per_seq, g_pre, g_post, w_up, w_down, tm, tf):
    m = x.shape[0]
    row = pl.BlockSpec((tm, D_MODEL), lambda i, j: (i, 0))
    vec = pl.BlockSpec((1, D_MODEL), lambda i, j: (0, 0))
    return pl.pallas_call(
        _mlp_kernel,
        grid=(m // tm, D_FF // tf),
        in_specs=[pl.BlockSpec((tm, D_MODEL), lambda i, j: (i, 0), pipeline_mode=pl.Buffered(1)),
                  _mod_spec(per_row, rows_per_seq, tm, 3),
                  _mod_spec(per_row, rows_per_seq, tm, 4),
                  _mod_spec(per_row, rows_per_seq, tm, 5),
                  vec, vec,
                  pl.BlockSpec((D_MODEL, tf), lambda i, j: (0, j)),
                  pl.BlockSpec((tf, D_MODEL), lambda i, j: (j, 0))],
        out_specs=row,
        out_shape=jax.ShapeDtypeStruct((m, D_MODEL), F32),
        scratch_shapes=[pltpu.VMEM((tm, D_MODEL), BF16)],
        compiler_params=_params("parallel", "arbitrary"),
        name="mlp",
    )(x, mod, mod, mod, g_pre, g_post, w_up, w_down)


def kernel(x_prompt, x_sample, c_prompt, c_sample, state_hgrn, state_ssm, state_conv, w_ada, b_ada, norm_pre_mix, norm_post_mix, norm_pre_mlp, norm_post_mlp, w_in, hgrn_lb_logits, hgrn_norm, conv_w, conv_b, dt_bias, a_log, d_skip, ssd_norm, w_out, w_up, w_down):
    n_seq, t, _ = x_prompt.shape
    n_dec = x_sample.shape[0]
    assert x_sample.shape[1] == 1 and t % CHUNK_A == 0 and t % CHUNK_B == 0
    assert w_ada.shape[0] == 1, "one layer"
    l = 0
    tiles = _tiles(t, n_dec)

    w_in_t = w_in[l].T
    w_out_b = w_out[l].astype(BF16)
    g_pre_mix, g_post_mix = norm_pre_mix[l][None], norm_post_mix[l][None]
    g_pre_mlp, g_post_mlp = norm_pre_mlp[l][None], norm_post_mlp[l][None]
    cw, cb = conv_w[l], conv_b[l][None]
    hn = hgrn_norm[l][None]
    sn = ssd_norm[l][None]

    n_c = n_seq + n_dec
    pad = (-n_c) % SUBLANE
    c_all = jnp.concatenate([c_prompt, c_sample, jnp.zeros((pad, D_MODEL), F32)], axis=0)
    mod = _ada(c_all, w_ada[l], b_ada[l][None], tiles["ada_n"])
    mod_p = mod[:n_seq].reshape(n_seq, 1, N_MOD * D_MODEL)
    mod_s = mod[n_seq:n_c]

    xp = x_prompt.reshape(n_seq * t, D_MODEL)
    tm, tn = tiles["inproj_p"]
    proj_p, dt_p = _inproj(xp, mod_p, False, t, g_pre_mix, w_in_t, hgrn_lb_logits, tm, tn,
                           functools.partial(_hgrn_interleave, tn=tn))
    o_a, s_hgrn_p = _hgrn_prompt(proj_p, hn, n_seq, t)

    n_chunks = t // CHUNK_B
    rep = SUBLANE // HPG
    dt_rows = dt_p.reshape(n_seq, n_chunks, CHUNK_B, B_G, 1, HPG).transpose(3, 0, 1, 4, 5, 2)
    dt_rows = jnp.broadcast_to(dt_rows, (B_G, n_seq, n_chunks, rep, HPG, CHUNK_B))
    dt_rows = dt_rows.reshape(B_G, n_seq, n_chunks * SUBLANE, CHUNK_B)
    prow = lambda p: jnp.tile(p.reshape(B_G, 1, HPG), (1, n_chunks * rep, 1)).reshape(B_G, n_chunks * SUBLANE, 1)
    cpar = _ssd_channel_params(cw, cb, sn, d_skip[l], SSD_GROUPS_PER_STEP)
    dpar = jnp.concatenate([prow(dt_bias[l]), prow(a_log[l])], axis=2)
    o_b, s_ssm_p = _ssd_prompt(proj_p, dt_rows, cpar, dpar, n_seq, t)
    x1 = _outproj(o_a, o_b, w_out_b, xp, mod_p, False, t, g_post_mix, tiles["outproj_p"])
    y_p = _mlp(x1, mod_p, False, t, g_pre_mlp, g_post_mlp, w_up[l], w_down[l], *tiles["mlp_p"])
    conv_p = proj_p.reshape(n_seq, t, D_MAIN)[:, t - (CONV_W - 1):, OFF_X:]

    xs = x_sample.reshape(n_dec, D_MODEL)
    proj_s, dt_s = _inproj(xs, mod_s, True, 1, g_pre_mix, w_in_t, hgrn_lb_logits, *tiles["inproj_s"])
    bt = tiles["step_b"]
    heads = lambda off: proj_s[:, off:off + D_A].reshape(n_dec, H_A, HA_D)
    o_a_s, s_hgrn_s = _hgrn_step(heads(OFF_Q), heads(OFF_F), heads(OFF_I), heads(OFF_G),
                                 hn.reshape(H_A, HA_D), state_hgrn[l], bt)

    to_t = lambda a, h, d: a.reshape(a.shape[:-1] + (h, d)).swapaxes(-1, -2)
    xbc_s = proj_s[:, OFF_X:]
    cst = state_conv[l]
    xt = to_t(xbc_s[:, :D_B], H_B, B_P)
    xr = xbc_s[:, None, :D_B]
    bcr = xbc_s[:, D_B:].reshape(n_dec, 2 * B_G, B_N)
    cxt = to_t(cst[:, :, :D_B], H_B, B_P)
    cxr = cst[:, :, None, :D_B]
    cbc = cst[:, :, D_B:].reshape(n_dec, CONV_W - 1, 2 * B_G, B_N)
    wxt = to_t(cw[:, :D_B], H_B, B_P)
    wxr = cw[:, None, :D_B]
    wbc = cw[:, D_B:].reshape(CONV_W, 2 * B_G, B_N)
    bxt = to_t(cb[:, :D_B], H_B, B_P)[0]
    bxr = cb[:, :D_B]
    bbc = cb[0, D_B:].reshape(2 * B_G, B_N)
    o_b_s, s_ssm_s = _ssd_step(
        xt, xr, bcr, proj_s[:, OFF_Z:OFF_Z + D_B], dt_s[:, None, :],
        cxt, cxr, cbc, wxt, wxr, wbc, bxt, bxr, bbc,
        dt_bias[l][None], a_log[l][None], jnp.repeat(d_skip[l], B_P)[None], sn,
        state_ssm[l], bt)
    x1_s = _outproj(o_a_s.reshape(n_dec, D_A), o_b_s, w_out_b, xs, mod_s, True, 1, g_post_mix,
                    tiles["outproj_s"])
    y_s = _mlp(x1_s, mod_s, True, 1, g_pre_mlp, g_post_mlp, w_up[l], w_down[l], *tiles["mlp_s"])
    conv_s = jnp.concatenate([cst[:, 1:], xbc_s[:, None, :]], axis=1)

    return (y_p.reshape(n_seq, t, D_MODEL), y_s.reshape(n_dec, 1, D_MODEL),
            s_hgrn_p[None], s_ssm_p[None], conv_p[None],
            s_hgrn_s[None], s_ssm_s[None], conv_s[None])
```

```python
import functools

import jax
import jax.numpy as jnp
from jax import lax
from jax.experimental import pallas as pl
from jax.experimental.pallas import tpu as pltpu

F32 = jnp.float32
BF16 = jnp.bfloat16

D_MODEL = 2048
D_A = 2048
HA_D = 128
H_A = D_A // HA_D
D_B = 2048
B_P = 64
H_B = D_B // B_P
B_G = 8
HPG = H_B // B_G
GW = HPG * B_P
B_N = 128
CONV_W = 4
CONV_DIM = D_B + 2 * B_G * B_N
D_MAIN = 4 * D_A + D_B + CONV_DIM
D_FF = 4 * D_MODEL
N_MOD = 6
EPS = 1e-6

OFF_Q, OFF_F, OFF_I, OFF_G = 0, D_A, 2 * D_A, 3 * D_A
OFF_Z = 4 * D_A
OFF_X = OFF_Z + D_B
OFF_B = OFF_X + D_B
OFF_C = OFF_B + B_G * B_N
REGION_SILU = (0, 3, 4)
REGION_FORGET = 1

LANE = 128
SUBLANE = 8
VMEM_LIMIT = 56 * 1024 * 1024

CHUNK_A = 128
SUB_A = 16
CHUNK_B = 128
HGRN_HEADS_PER_STEP = 4
SSD_GROUPS_PER_STEP = 2


def _tiles(t, n_dec):
    return dict(
        inproj_p=(_pick_tile(t, 2048), HGRN_HEADS_PER_STEP * HA_D), inproj_s=(n_dec, 2048),
        outproj_p=_pick_tile(t, 256), outproj_s=n_dec,
        mlp_p=(_pick_tile(t, 1024), 512), mlp_s=(n_dec, 1024),
        ada_n=1024, step_b=_pick_tile(n_dec, SUBLANE))


def _pick_tile(m, target):
    t = min(m, target)
    while m % t:
        t //= 2
    return t


def _silu(x):
    hx = 0.5 * x
    return hx * jnp.tanh(hx) + hx


def _softplus(x):
    return jnp.maximum(x, 0.0) + jnp.log1p(jnp.exp(-jnp.abs(x)))


def _idiv(x, d):
    assert d & (d - 1) == 0
    return jnp.right_shift(x, d.bit_length() - 1)


def _rms(x):
    return x * lax.rsqrt(jnp.mean(x * x, axis=-1, keepdims=True) + EPS)


def _cumsum_rows(x, n, row):
    s = 1
    while s < n:
        x = x + jnp.where(row >= s, pltpu.roll(x, s, 0), 0.0)
        s *= 2
    return x


def _cumsum_lanes(x, n, lane):
    s = 1
    while s < n:
        x = x + jnp.where(lane >= s, pltpu.roll(x, s, 1), 0.0)
        s *= 2
    return x


def _dot(a, b):
    return jnp.dot(a.astype(BF16), b.astype(BF16), preferred_element_type=F32)


def _dot_nt(a, b):
    return lax.dot_general(a.astype(BF16), b.astype(BF16), (((1,), (1,)), ((), ())),
                           preferred_element_type=F32)


def _exact_terms(x):
    hi = x.astype(BF16).astype(F32)
    mid = (x - hi).astype(BF16).astype(F32)
    lo = x - hi - mid
    return jnp.concatenate([hi, mid, lo, jnp.zeros_like(hi)], axis=0).astype(BF16)


def _transpose_bcast(x, sel):
    return lax.dot_general(_exact_terms(x), sel, (((0,), (0,)), ((), ())),
                           preferred_element_type=F32)


def _params(*sem):
    return pltpu.CompilerParams(dimension_semantics=sem, vmem_limit_bytes=VMEM_LIMIT)


def _ada_kernel(c_ref, w_ref, b_ref, o_ref):
    o_ref[...] = _dot(_silu(c_ref[...]), w_ref[...]) + b_ref[...]


def _ada(c_all, w_ada, b_ada, tn):
    m = c_all.shape[0]
    n = w_ada.shape[1]
    return pl.pallas_call(
        _ada_kernel,
        grid=(n // tn,),
        in_specs=[pl.BlockSpec((m, D_MODEL), lambda j: (0, 0)),
                  pl.BlockSpec((D_MODEL, tn), lambda j: (0, j)),
                  pl.BlockSpec((1, tn), lambda j: (0, j))],
        out_specs=pl.BlockSpec((m, tn), lambda j: (0, j)),
        out_shape=jax.ShapeDtypeStruct((m, n), F32),
        compiler_params=_params("parallel"),
        name="ada_mod",
    )(c_all, w_ada, b_ada)


def _mod_spec(per_row, rows_per_seq, tm, col):
    if per_row:
        return pl.BlockSpec((tm, D_MODEL), lambda i, j: (i, col))
    tiles_per_seq = rows_per_seq // tm
    return pl.BlockSpec((None, 1, D_MODEL), lambda i, j: (i // tiles_per_seq, 0, col))


def _lower_bound(logits):
    e = jnp.exp(logits - jnp.max(logits, axis=0, keepdims=True))
    return e[0:1] / jnp.sum(e, axis=0, keepdims=True)


def _inproj_kernel(x_ref, sh_ref, sc_ref, g_ref, w_ref, wdt_ref, lbl_ref, o_ref, odt_ref, h_scr, *, tn):
    j = pl.program_id(1)

    @pl.when(j == 0)
    def _():
        h = _rms(x_ref[...]) * g_ref[...] * (1.0 + sc_ref[...]) + sh_ref[...]
        hb = h.astype(BF16)
        h_scr[...] = hb
        odt_ref[...] = _dot_nt(hb, wdt_ref[...])

    region = (j * tn) // D_A
    is_silu = functools.reduce(jnp.logical_or, [region == r for r in REGION_SILU])

    @pl.when(is_silu)
    def _():
        o_ref[...] = _silu(_dot_nt(h_scr[...], w_ref[...]))

    @pl.when(region == REGION_FORGET)
    def _():
        lb = _lower_bound(lbl_ref[...])
        half = 0.5 * (1.0 - lb)
        o_ref[...] = jnp.log((lb + half) + half * jnp.tanh(0.5 * _dot_nt(h_scr[...], w_ref[...])))

    @pl.when(jnp.logical_not(jnp.logical_or(is_silu, region == REGION_FORGET)))
    def _():
        o_ref[...] = _dot_nt(h_scr[...], w_ref[...])


def _inproj(x, mod, per_row, rows_per_seq, g_pre, w_t, lb_logits, tm, tn, out_tile=lambda j: j):
    m = x.shape[0]
    assert D_MAIN % tn == 0 and D_MAIN % H_B == 0 and D_A % tn == 0
    f_tiles = D_A // tn
    return pl.pallas_call(
        functools.partial(_inproj_kernel, tn=tn),
        grid=(m // tm, D_MAIN // tn),
        in_specs=[pl.BlockSpec((tm, D_MODEL), lambda i, j: (i, 0), pipeline_mode=pl.Buffered(1)),
                  _mod_spec(per_row, rows_per_seq, tm, 0),
                  _mod_spec(per_row, rows_per_seq, tm, 1),
                  pl.BlockSpec((1, D_MODEL), lambda i, j: (0, 0)),
                  pl.BlockSpec((tn, D_MODEL), lambda i, j: (j, 0)),
                  pl.BlockSpec((H_B, D_MODEL), lambda i, j: (D_MAIN // H_B, 0)),
                  pl.BlockSpec((lb_logits.shape[0], tn),
                               lambda i, j: (0, jnp.clip(j - REGION_FORGET * f_tiles, 0, f_tiles - 1)))],
        out_specs=[pl.BlockSpec((tm, tn), lambda i, j: (i, out_tile(j))),
                   pl.BlockSpec((tm, H_B), lambda i, j: (i, 0))],
        out_shape=[jax.ShapeDtypeStruct((m, D_MAIN), F32),
                   jax.ShapeDtypeStruct((m, H_B), F32)],
        scratch_shapes=[pltpu.VMEM((tm, D_MODEL), BF16)],
        compiler_params=_params("parallel", "arbitrary"),
        name="in_proj",
    )(x, mod, mod, g_pre, w_t, w_t, lb_logits)


def _hgrn_interleave(j, tn):
    n_hg = D_A // tn
    return jnp.where(j < 4 * n_hg, 4 * (j % n_hg) + j // n_hg, j)


def _hgrn_scores(qs, lf, row):
    c_len = CHUNK_A
    n_sub = c_len // SUB_A
    k = 1.0 - jnp.exp(lf)
    b = _cumsum_rows(lf, SUB_A, row & (SUB_A - 1))
    k_acc = []
    blocks, q_dec = [], []
    start = jnp.zeros((1, HA_D), F32)
    for i in range(n_sub):
        lo = i * SUB_A
        hi = lo + SUB_A
        b_i = b[lo:hi, :]
        b_sub = b[hi - 1:hi, :]
        k_i = k[lo:hi, :]
        qg = qs[lo:hi, :] * jnp.exp(b_i)
        k_diag = k_i * jnp.exp(-b_i)
        rhs = jnp.concatenate([p.astype(BF16) for p in k_acc] + [k_diag.astype(BF16)], axis=0)
        a_i = lax.dot_general(qg.astype(BF16), rhs, (((1,), (1,)), ((), ())),
                              preferred_element_type=F32)
        causal = (lax.broadcasted_iota(jnp.int32, (SUB_A, hi), 1)
                  <= lax.broadcasted_iota(jnp.int32, (SUB_A, hi), 0) + lo)
        a_i = jnp.where(causal, a_i, 0.0)
        blocks.append(jnp.pad(a_i, ((0, 0), (0, c_len - hi))).astype(BF16))
        q_dec.append((qg * jnp.exp(start)).astype(BF16))
        sub_decay = jnp.exp(b_sub)
        k_acc = [p * sub_decay for p in k_acc] + [k_i * jnp.exp(b_sub - b_i)]
        start = start + b_sub
    scores = jnp.concatenate(blocks, axis=0)
    k_end = jnp.concatenate([p.astype(BF16) for p in k_acc], axis=0)
    chunk_decay = jnp.exp(start)
    return scores, jnp.concatenate(q_dec, axis=0), k_end, chunk_decay


def _hgrn_apply(pend, v, gs, nw, st):
    scores, q_dec, k_end, chunk_decay = pend
    vb = v.astype(BF16)
    o = (jnp.dot(scores, vb, preferred_element_type=F32)
         + lax.dot_general(q_dec, st.astype(BF16), (((1,), (1,)), ((), ())),
                           preferred_element_type=F32))
    st = chunk_decay * st + lax.dot_general(vb, k_end, (((0,), (0,)), ((), ())),
                                            preferred_element_type=F32)
    return (_rms(o) * nw * gs).astype(BF16), st


def _hgrn_prompt_kernel(p_ref, nw_ref, o_ref, s_ref, *, n_chunks, n_heads):
    c_len = CHUNK_A
    w = n_heads * HA_D
    nw = nw_ref[...]
    row = lax.broadcasted_iota(jnp.int32, (c_len, HA_D), 0)
    heads = [slice(h * HA_D, (h + 1) * HA_D) for h in range(n_heads)]

    def part(k, sl, hs):
        return p_ref[sl, k * w + hs.start:k * w + hs.stop]

    def rows(c):
        return pl.ds(pl.multiple_of(c * c_len, c_len), c_len)

    def scores_of(c):
        sl = rows(c)
        return tuple(_hgrn_scores(part(0, sl, hs), part(1, sl, hs), row) for hs in heads)

    def apply_to(c, pends, sts):
        sl = rows(c)
        new = []
        for hs, pend, st in zip(heads, pends, sts):
            out, st = _hgrn_apply(pend, part(2, sl, hs), part(3, sl, hs), nw[:, hs], st)
            o_ref[sl, hs] = out
            new.append(st)
        return tuple(new)

    def step(c, carry):
        pends, sts = carry
        sts = apply_to(c - 1, pends, sts)
        return scores_of(c), sts

    zeros = tuple(jnp.zeros((HA_D, HA_D), F32) for _ in range(n_heads))
    pends, sts = lax.fori_loop(1, n_chunks, step, (scores_of(0), zeros))
    sts = apply_to(n_chunks - 1, pends, sts)
    for h in range(n_heads):
        s_ref[h] = sts[h].T


def _hgrn_prompt(proj, hgrn_norm, n_seq, t):
    m = proj.shape[0]
    nh = HGRN_HEADS_PER_STEP
    w = nh * HA_D
    return pl.pallas_call(
        functools.partial(_hgrn_prompt_kernel, n_chunks=t // CHUNK_A, n_heads=nh),
        grid=(n_seq, H_A // nh),
        in_specs=[pl.BlockSpec((t, 4 * w), lambda b, h: (b, h)),
                  pl.BlockSpec((1, w), lambda b, h: (0, h))],
        out_specs=[pl.BlockSpec((t, w), lambda b, h: (b, h)),
                   pl.BlockSpec((None, nh, HA_D, HA_D), lambda b, h: (b, h, 0, 0))],
        out_shape=[jax.ShapeDtypeStruct((m, D_A), BF16),
                   jax.ShapeDtypeStruct((n_seq, H_A, HA_D, HA_D), F32)],
        compiler_params=_params("parallel", "parallel"),
        name="hgrn_prompt",
    )(proj, hgrn_norm)


def _conv_chunk(tail, u, w, bias, n):
    full = jnp.concatenate([tail, u], axis=0)
    out = bias + w[CONV_W - 1:CONV_W, :] * u
    for i in range(CONV_W - 1):
        shifted = pltpu.roll(full, CONV_W - 1 - i, 0)[SUBLANE:SUBLANE + n, :]
        out = out + w[i:i + 1, :] * shifted
    return _silu(out)


def _ssd_prompt_kernel(z_ref, x_ref, b_ref, c_ref, dtr_ref, cpar_ref, dpar_ref,
                       o_ref, h_ref, row_scr, *, n_chunks, n_groups):
    n = CHUNK_B
    assert n == LANE
    tri = (lax.broadcasted_iota(jnp.int32, (n, n), 0)
           >= lax.broadcasted_iota(jnp.int32, (n, n), 1))
    lane_head = _idiv(lax.broadcasted_iota(jnp.int32, (1, GW), 1), B_P)
    row_head = _idiv(lax.broadcasted_iota(jnp.int32, (GW, 1), 0), B_P)
    nr = n_chunks * SUBLANE
    lane_nr = lax.broadcasted_iota(jnp.int32, (nr, n), 1)
    is_dt = (lax.broadcasted_iota(jnp.int32, (nr, n), 0) & (SUBLANE - 1)) < HPG
    is_dt8 = lax.broadcasted_iota(jnp.int32, (SUBLANE, n), 0) < HPG
    n_src = 2 * SUBLANE
    sel_shape = (4 * n_src, HPG * n + 2 * GW)
    sel_k = lax.broadcasted_iota(jnp.int32, sel_shape, 0)
    sel_l = lax.broadcasted_iota(jnp.int32, sel_shape, 1)
    want = jnp.where(sel_l < HPG * n, HPG + _idiv(sel_l, n),
                     2 * HPG + _idiv(sel_l - HPG * n, B_P))
    bcast_sel = jnp.where(((sel_k & (n_src - 1)) == want) & (sel_k < 3 * n_src), 1.0, 0.0).astype(BF16)

    groups = []
    for gi in range(n_groups):
        xs = slice(gi * GW, (gi + 1) * GW)
        bs = slice(gi * B_N, (gi + 1) * B_N)
        dpar = dpar_ref[gi]
        dt_all = _softplus(dtr_ref[gi] + dpar[:, 0:1])
        cum_all = _cumsum_lanes(dt_all * -jnp.exp(dpar[:, 1:2]), n, lane_nr)
        row_scr[gi] = jnp.where(is_dt, dt_all, cum_all).reshape(n_chunks, SUBLANE, n)
        bl = slice(n_groups * GW + gi * B_N, n_groups * GW + (gi + 1) * B_N)
        cl = slice(n_groups * (GW + B_N) + gi * B_N, n_groups * (GW + B_N) + (gi + 1) * B_N)
        taps, bias = slice(0, CONV_W), slice(CONV_W, CONV_W + 1)
        groups.append(dict(gi=gi, xs=xs, bs=bs, dskip=cpar_ref[6:7, xs], nw=cpar_ref[5:6, xs],
                           wx=cpar_ref[taps, xs], wb=cpar_ref[taps, bl], wc=cpar_ref[taps, cl],
                           bx=cpar_ref[bias, xs], bb=cpar_ref[bias, bl], bc=cpar_ref[bias, cl]))

    def rows(c):
        return pl.ds(pl.multiple_of(c * n, n), n)

    def scan_free(c, grp):
        xs, bs = grp["xs"], grp["bs"]
        sl = rows(c)
        if isinstance(c, int) and c == 0:
            tail = lambda ref, lanes: jnp.zeros((SUBLANE, lanes.stop - lanes.start), F32)
        else:
            prev = pl.ds(pl.multiple_of(c * n - SUBLANE, SUBLANE), SUBLANE)
            tail = lambda ref, lanes: ref[prev, lanes]
        xc = _conv_chunk(tail(x_ref, xs), x_ref[sl, xs], grp["wx"], grp["bx"], n)
        bcv = _conv_chunk(tail(b_ref, bs), b_ref[sl, bs], grp["wb"], grp["bb"], n)
        ccv = _conv_chunk(tail(c_ref, bs), c_ref[sl, bs], grp["wc"], grp["bc"], n)
        r8 = row_scr[grp["gi"], c]
        swapped = pltpu.roll(r8, HPG, 0)
        cum8 = jnp.where(is_dt8, swapped, r8)
        dt8 = jnp.where(is_dt8, r8, swapped)
        e8 = jnp.where(is_dt8, jnp.exp(cum8), jnp.exp(cum8[:, n - 1:n] - cum8) * dt8)
        cols = _transpose_bcast(jnp.concatenate([r8, e8], axis=0), bcast_sel)
        e_cum = cols[:, HPG * n:HPG * n + GW]
        w_end = cols[:, HPG * n + GW:]
        g = _dot_nt(ccv, bcv)
        xb = xc.astype(BF16)
        s_all, x_all = [], []
        for j in range(HPG):
            cc = cols[:, j * n:(j + 1) * n]
            seg = cc - r8[HPG + j:HPG + j + 1, :]
            s = g * jnp.exp(jnp.where(tri, seg, -jnp.inf)) * r8[j:j + 1, :]
            s_all.append(s.astype(BF16))
            x_all.append(jnp.where(lane_head == j, xb, jnp.zeros_like(xb)))
        y = grp["dskip"] * xc + jnp.dot(jnp.concatenate(s_all, axis=1), jnp.concatenate(x_all, axis=0),
                                        preferred_element_type=F32)
        return y, e_cum, ccv.astype(BF16), (xc * w_end).astype(BF16), bcv.astype(BF16), r8

    def apply_state(c, grp, pend, h):
        xs = grp["xs"]
        y, e_cum, ccb, xwb, bcb, r8 = pend
        sl = rows(c)
        ch = lax.dot_general(ccb, h.astype(BF16), (((1,), (1,)), ((), ())),
                             preferred_element_type=F32)
        yz = (y + ch * e_cum) * z_ref[sl, xs]
        o_ref[sl, xs] = (_rms(yz) * grp["nw"]).astype(o_ref.dtype)
        h_dec = jnp.zeros((GW, 1), F32)
        for j in range(HPG):
            h_dec = jnp.where(row_head == j, jnp.exp(r8[HPG + j:HPG + j + 1, n - 1:n]), h_dec)
        return h_dec * h + lax.dot_general(xwb, bcb, (((0,), (0,)), ((), ())),
                                           preferred_element_type=F32)

    def step(c, carry):
        pends, hs = carry
        hs = tuple(apply_state(c - 1, grp, pend, h) for grp, pend, h in zip(groups, pends, hs))
        return tuple(scan_free(c, grp) for grp in groups), hs

    init = (tuple(scan_free(0, grp) for grp in groups),
            tuple(jnp.zeros((GW, B_N), F32) for _ in groups))
    pends, hs = lax.fori_loop(1, n_chunks, step, init)
    for gi, (grp, pend, h) in enumerate(zip(groups, pends, hs)):
        h = apply_state(n_chunks - 1, grp, pend, h)
        h_ref[gi * HPG:(gi + 1) * HPG] = h.reshape(HPG, B_P, B_N)


def _ssd_channel_params(conv_w, conv_b, ssd_norm, d_skip, ng):
    zeros = jnp.zeros((1, 2 * B_G * B_N), F32)
    rows = jnp.concatenate([
        conv_w, conv_b,
        jnp.concatenate([ssd_norm, zeros], axis=1),
        jnp.concatenate([jnp.repeat(d_skip, B_P)[None], zeros], axis=1),
        jnp.zeros((SUBLANE - CONV_W - 3, CONV_DIM), F32)], axis=0)
    n_steps = B_G // ng
    part = lambda lo, width: rows[:, lo:lo + n_steps * width].reshape(SUBLANE, n_steps, width)
    slab = jnp.concatenate([part(0, ng * GW), part(D_B, ng * B_N), part(D_B + B_G * B_N, ng * B_N)], axis=2)
    return slab.transpose(1, 0, 2)


def _ssd_prompt(proj, dt_rows, cpar, dpar, n_seq, t):
    m = proj.shape[0]
    n_chunks = t // CHUNK_B
    nr = n_chunks * SUBLANE
    ng = SSD_GROUPS_PER_STEP
    xw, bw = ng * GW, ng * B_N
    wide = lambda off: pl.BlockSpec((t, xw), lambda b, g: (b, off // xw + g))
    narrow = lambda off: pl.BlockSpec((t, bw), lambda b, g: (b, off // bw + g))
    return pl.pallas_call(
        functools.partial(_ssd_prompt_kernel, n_chunks=n_chunks, n_groups=ng),
        grid=(n_seq, B_G // ng),
        in_specs=[wide(OFF_Z), wide(OFF_X), narrow(OFF_B), narrow(OFF_C),
                  pl.BlockSpec((ng, None, nr, CHUNK_B), lambda b, g: (g, b, 0, 0)),
                  pl.BlockSpec((None, SUBLANE, xw + 2 * bw), lambda b, g: (g, 0, 0)),
                  pl.BlockSpec((ng, nr, 2), lambda b, g: (g, 0, 0))],
        out_specs=[pl.BlockSpec((t, xw), lambda b, g: (b, g)),
                   pl.BlockSpec((None, ng * HPG, B_P, B_N), lambda b, g: (b, g, 0, 0))],
        out_shape=[jax.ShapeDtypeStruct((m, D_B), BF16),
                   jax.ShapeDtypeStruct((n_seq, H_B, B_P, B_N), F32)],
        scratch_shapes=[pltpu.VMEM((ng, n_chunks, SUBLANE, CHUNK_B), F32)],
        compiler_params=_params("parallel", "parallel"),
        name="ssd_prompt",
    )(proj, proj, proj, proj, dt_rows, cpar, dpar)


def _hgrn_step_rows(q_ref, f_ref, v_ref, g_ref, nw_ref, s_ref, o_ref, so_ref):
    nw = nw_ref[...]
    pad = jnp.zeros((HA_D - H_A, HA_D), F32)
    for b in range(q_ref.shape[0]):
        ft = jnp.concatenate([jnp.exp(f_ref[b]), pad], axis=0).T
        qt = jnp.concatenate([q_ref[b], pad], axis=0).T
        v = v_ref[b]
        rows = []
        for h in range(H_A):
            fc = ft[:, h:h + 1]
            s_new = fc * s_ref[b, h] + (1.0 - fc) * v[h:h + 1, :]
            so_ref[b, h] = s_new
            rows.append(jnp.sum(qt[:, h:h + 1] * s_new, axis=0, keepdims=True))
        o = jnp.concatenate(rows, axis=0)
        o_ref[b] = (_rms(o) * nw * g_ref[b]).astype(o_ref.dtype)


def _conv_step(buf_ref, b, u, w, bias):
    out = bias + w[CONV_W - 1] * u
    for i in range(CONV_W - 1):
        out = out + w[i] * buf_ref[b, i]
    return _silu(out)


def _ssd_step_kernel(xt_ref, xr_ref, bcr_ref, z_ref, dt_ref,
                     cxt_ref, cxr_ref, cbc_ref,
                     wxt_ref, wxr_ref, wbc_ref, bxt_ref, bxr_ref, bbc_ref,
                     dtb_ref, al_ref, dskr_ref, nw_ref, h_ref,
                     o_ref, ho_ref, y_scr, xc_scr, *, bt):
    wxt, wxr, wbc = wxt_ref[...], wxr_ref[...], wbc_ref[...]
    bxt, bxr, bbc = bxt_ref[...], bxr_ref[...], bbc_ref[...]
    a = -jnp.exp(al_ref[...])
    dtb = dtb_ref[...]

    def seq(b, carry):
        xt = _conv_step(cxt_ref, b, xt_ref[b], wxt, bxt)
        xc_scr[pl.ds(b, 1), :] = _conv_step(cxr_ref, b, xr_ref[b], wxr, bxr)
        bcv = _conv_step(cbc_ref, b, bcr_ref[b], wbc, bbc)
        dt = _softplus(dt_ref[b] + dtb)
        d_a = jnp.exp(dt * a)
        xdt = xt * dt
        ys = []
        for g in range(B_G):
            b_row = bcv[g:g + 1, :]
            c_row = bcv[B_G + g:B_G + g + 1, :]
            hs = []
            for j in range(HPG):
                hh = g * HPG + j
                h_new = d_a[:, hh:hh + 1] * h_ref[b, hh] + xdt[:, hh:hh + 1] * b_row
                ho_ref[b, hh] = h_new
                hs.append(h_new)
            hg = jnp.concatenate(hs, axis=0)
            c8 = jnp.broadcast_to(c_row, (SUBLANE, B_N))
            ys.append(_dot_nt(c8, hg)[0:1, :])
        y_scr[pl.ds(b, 1), :] = jnp.concatenate(ys, axis=1)
        return carry

    lax.fori_loop(0, bt, seq, 0)
    yz = (y_scr[...] + dskr_ref[...] * xc_scr[...]) * z_ref[...]
    nw = nw_ref[...]
    for g in range(B_G):
        sl = slice(g * GW, (g + 1) * GW)
        o_ref[:, sl] = (_rms(yz[:, sl]) * nw[:, sl]).astype(o_ref.dtype)


def _ssd_step(xt, xr, bcr, z, dt, cxt, cxr, cbc, wxt, wxr, wbc, bxt, bxr, bbc,
              dtb, al, dskr, nw, state, bt):
    nb = xt.shape[0]
    full = lambda a: pl.BlockSpec(a.shape, lambda i: (0,) * a.ndim)
    lead = lambda a: pl.BlockSpec((bt,) + a.shape[1:], lambda i: (i,) + (0,) * (a.ndim - 1))
    return pl.pallas_call(
        functools.partial(_ssd_step_kernel, bt=bt),
        grid=(nb // bt,),
        in_specs=[lead(xt), lead(xr), lead(bcr), lead(z), lead(dt),
                  lead(cxt), lead(cxr), lead(cbc),
                  full(wxt), full(wxr), full(wbc), full(bxt), full(bxr), full(bbc),
                  full(dtb), full(al), full(dskr), full(nw), lead(state)],
        out_specs=[pl.BlockSpec((bt, D_B), lambda i: (i, 0)), lead(state)],
        out_shape=[jax.ShapeDtypeStruct((nb, D_B), F32),
                   jax.ShapeDtypeStruct(state.shape, F32)],
        scratch_shapes=[pltpu.VMEM((bt, D_B), F32), pltpu.VMEM((bt, D_B), F32)],
        compiler_params=_params("parallel"),
        name="ssd_step",
    )(xt, xr, bcr, z, dt, cxt, cxr, cbc, wxt, wxr, wbc, bxt, bxr, bbc, dtb, al, dskr, nw, state)


def _outproj_kernel(oa_ref, ob_ref, wa_ref, wb_ref, x_ref, gt_ref, gp_ref, *rest):
    y_ref = rest[-3] if len(rest) > 1 else rest[0]
    mix = _dot(oa_ref[...], wa_ref[...]) + _dot(ob_ref[...], wb_ref[...])
    y_ref[...] = x_ref[...] + gt_ref[...] * (_rms(mix) * gp_ref[...])
    if len(rest) > 1:
        q_ref, f_ref, v_ref, g_ref, nw_ref, s_ref, _, o_ref, so_ref = rest
        _hgrn_step_rows(q_ref, f_ref, v_ref, g_ref, nw_ref, s_ref, o_ref, so_ref)


def _outproj(o_a, o_b, w_out, x, mod, per_row, rows_per_seq, g_post, tm, hgrn_step=None):
    m = x.shape[0]
    n_steps = m // tm
    row = lambda w: pl.BlockSpec((tm, w), lambda i, j: (i, 0))
    const = lambda shape, idx: pl.BlockSpec(shape, lambda i, j: idx, pipeline_mode=pl.Buffered(1))
    in_specs = [row(D_A), row(D_B),
                const((D_A, D_MODEL), (0, 0)), const((D_B, D_MODEL), (1, 0)),
                row(D_MODEL),
                _mod_spec(per_row, rows_per_seq, tm, 2),
                pl.BlockSpec((1, D_MODEL), lambda i, j: (0, 0))]
    out_specs = [row(D_MODEL)]
    out_shape = [jax.ShapeDtypeStruct((m, D_MODEL), F32)]
    args = [o_a, o_b, w_out, w_out, x, mod, g_post]
    if hgrn_step is not None:
        q, f, v, g, nw, state = hgrn_step
        nb = q.shape[0]
        assert nb % n_steps == 0
        bt = nb // n_steps
        rw = pl.BlockSpec((bt, H_A, HA_D), lambda i, j: (i, 0, 0))
        st = pl.BlockSpec((bt, H_A, HA_D, HA_D), lambda i, j: (i, 0, 0, 0))
        in_specs += [rw, rw, rw, rw, pl.BlockSpec((H_A, HA_D), lambda i, j: (0, 0)), st]
        out_specs += [rw, st]
        out_shape += [jax.ShapeDtypeStruct((nb, H_A, HA_D), BF16), jax.ShapeDtypeStruct(state.shape, F32)]
        args += [q, f, v, g, nw, state]
    out = pl.pallas_call(
        _outproj_kernel,
        grid=(n_steps, 1),
        in_specs=in_specs,
        out_specs=out_specs,
        out_shape=out_shape,
        compiler_params=_params("parallel", "arbitrary"),
        name="out_proj",
    )(*args)
    return out if hgrn_step is not None else out[0]


def _mlp_kernel(x_ref, sh_ref, sc_ref, gt_ref, gpre_ref, gpost_ref, wu_ref, wd_ref, y_ref, h_scr):
    f = pl.program_id(1)

    @pl.when(f == 0)
    def _():
        h = _rms(x_ref[...]) * gpre_ref[...] * (1.0 + sc_ref[...]) + sh_ref[...]
        h_scr[...] = h.astype(BF16)
        y_ref[...] = jnp.zeros_like(y_ref)

    u = jnp.maximum(jnp.dot(h_scr[...], wu_ref[...].astype(BF16), preferred_element_type=F32), 0.0)
    y_ref[...] += jnp.dot((u * u).astype(BF16), wd_ref[...].astype(BF16), preferred_element_type=F32)

    @pl.when(f == pl.num_programs(1) - 1)
    def _():
        y_ref[...] = x_ref[...] + gt_ref[...] * (_rms(y_ref[...]) * gpost_ref[...])


def _mlp(x, mod, per_row, rows_per_seq, g_pre, g_post, w_up, w_down, tm, tf):
    m = x.shape[0]
    row = pl.BlockSpec((tm, D_MODEL), lambda i, j: (i, 0))
    vec = pl.BlockSpec((1, D_MODEL), lambda i, j: (0, 0))
    return pl.pallas_call(
        _mlp_kernel,
        grid=(m // tm, D_FF // tf),
        in_specs=[pl.BlockSpec((tm, D_MODEL), lambda i, j: (i, 0), pipeline_mode=pl.Buffered(1)),
                  _mod_spec(per_row, rows_per_seq, tm, 3),
                  _mod_spec(per_row, rows_per_seq, tm, 4),
                  _mod_spec(per_row, rows_per_seq, tm, 5),
                  vec, vec,
                  pl.BlockSpec((D_MODEL, tf), lambda i, j: (0, j)),
                  pl.BlockSpec((tf, D_MODEL), lambda i, j: (j, 0))],
        out_specs=row,
        out_shape=jax.ShapeDtypeStruct((m, D_MODEL), F32),
        scratch_shapes=[pltpu.VMEM((tm, D_MODEL), BF16)],
        compiler_params=_params("parallel", "arbitrary"),
        name="mlp",
    )(x, mod, mod, mod, g_pre, g_post, w_up, w_down)


def kernel(x_prompt, x_sample, c_prompt, c_sample, state_hgrn, state_ssm, state_conv, w_ada, b_ada, norm_pre_mix, norm_post_mix, norm_pre_mlp, norm_post_mlp, w_in, hgrn_lb_logits, hgrn_norm, conv_w, conv_b, dt_bias, a_log, d_skip, ssd_norm, w_out, w_up, w_down):
    n_seq, t, _ = x_prompt.shape
    n_dec = x_sample.shape[0]
    assert x_sample.shape[1] == 1 and t % CHUNK_A == 0 and t % CHUNK_B == 0
    assert w_ada.shape[0] == 1, "one layer"
    l = 0
    tiles = _tiles(t, n_dec)

    w_in_t = w_in[l].T
    w_out_b = w_out[l].astype(BF16)
    g_pre_mix, g_post_mix = norm_pre_mix[l][None], norm_post_mix[l][None]
    g_pre_mlp, g_post_mlp = norm_pre_mlp[l][None], norm_post_mlp[l][None]
    cw, cb = conv_w[l], conv_b[l][None]
    hn = hgrn_norm[l][None]
    sn = ssd_norm[l][None]

    n_c = n_seq + n_dec
    pad = (-n_c) % SUBLANE
    c_all = jnp.concatenate([c_prompt, c_sample, jnp.zeros((pad, D_MODEL), F32)], axis=0)
    mod = _ada(c_all, w_ada[l], b_ada[l][None], tiles["ada_n"])
    mod_p = mod[:n_seq].reshape(n_seq, 1, N_MOD * D_MODEL)
    mod_s = mod[n_seq:n_c]

    xp = x_prompt.reshape(n_seq * t, D_MODEL)
    tm, tn = tiles["inproj_p"]
    proj_p, dt_p = _inproj(xp, mod_p, False, t, g_pre_mix, w_in_t, hgrn_lb_logits, tm, tn,
                           functools.partial(_hgrn_interleave, tn=tn))
    o_a, s_hgrn_p = _hgrn_prompt(proj_p, hn, n_seq, t)

    n_chunks = t // CHUNK_B
    rep = SUBLANE // HPG
    dt_rows = dt_p.reshape(n_seq, n_chunks, CHUNK_B, B_G, 1, HPG).transpose(3, 0, 1, 4, 5, 2)
    dt_rows = jnp.broadcast_to(dt_rows, (B_G, n_seq, n_chunks, rep, HPG, CHUNK_B))
    dt_rows = dt_rows.reshape(B_G, n_seq, n_chunks * SUBLANE, CHUNK_B)
    prow = lambda p: jnp.tile(p.reshape(B_G, 1, HPG), (1, n_chunks * rep, 1)).reshape(B_G, n_chunks * SUBLANE, 1)
    cpar = _ssd_channel_params(cw, cb, sn, d_skip[l], SSD_GROUPS_PER_STEP)
    dpar = jnp.concatenate([prow(dt_bias[l]), prow(a_log[l])], axis=2)
    o_b, s_ssm_p = _ssd_prompt(proj_p, dt_rows, cpar, dpar, n_seq, t)
    xs = x_sample.reshape(n_dec, D_MODEL)
    proj_s, dt_s = _inproj(xs, mod_s, True, 1, g_pre_mix, w_in_t, hgrn_lb_logits, *tiles["inproj_s"])
    heads = lambda off: proj_s[:, off:off + D_A].reshape(n_dec, H_A, HA_D)
    x1, o_a_s, s_hgrn_s = _outproj(
        o_a, o_b, w_out_b, xp, mod_p, False, t, g_post_mix, tiles["outproj_p"],
        hgrn_step=(heads(OFF_Q), heads(OFF_F), heads(OFF_I), heads(OFF_G), hn.reshape(H_A, HA_D),
                   state_hgrn[l]))
    y_p = _mlp(x1, mod_p, False, t, g_pre_mlp, g_post_mlp, w_up[l], w_down[l], *tiles["mlp_p"])
    conv_p = proj_p.reshape(n_seq, t, D_MAIN)[:, t - (CONV_W - 1):, OFF_X:]

    bt = tiles["step_b"]

    to_t = lambda a, h, d: a.reshape(a.shape[:-1] + (h, d)).swapaxes(-1, -2)
    xbc_s = proj_s[:, OFF_X:]
    cst = state_conv[l]
    xt = to_t(xbc_s[:, :D_B], H_B, B_P)
    xr = xbc_s[:, None, :D_B]
    bcr = xbc_s[:, D_B:].reshape(n_dec, 2 * B_G, B_N)
    cxt = to_t(cst[:, :, :D_B], H_B, B_P)
    cxr = cst[:, :, None, :D_B]
    cbc = cst[:, :, D_B:].reshape(n_dec, CONV_W - 1, 2 * B_G, B_N)
    wxt = to_t(cw[:, :D_B], H_B, B_P)
    wxr = cw[:, None, :D_B]
    wbc = cw[:, D_B:].reshape(CONV_W, 2 * B_G, B_N)
    bxt = to_t(cb[:, :D_B], H_B, B_P)[0]
    bxr = cb[:, :D_B]
    bbc = cb[0, D_B:].reshape(2 * B_G, B_N)
    o_b_s, s_ssm_s = _ssd_step(
        xt, xr, bcr, proj_s[:, OFF_Z:OFF_Z + D_B], dt_s[:, None, :],
        cxt, cxr, cbc, wxt, wxr, wbc, bxt, bxr, bbc,
        dt_bias[l][None], a_log[l][None], jnp.repeat(d_skip[l], B_P)[None], sn,
        state_ssm[l], bt)
    x1_s = _outproj(o_a_s.reshape(n_dec, D_A), o_b_s, w_out_b, xs, mod_s, True, 1, g_post_mix,
                    tiles["outproj_s"])
    y_s = _mlp(x1_s, mod_s, True, 1, g_pre_mlp, g_post_mlp, w_up[l], w_down[l], *tiles["mlp_s"])
    conv_s = jnp.concatenate([cst[:, 1:], xbc_s[:, None, :]], axis=1)

    return (y_p.reshape(n_seq, t, D_MODEL), y_s.reshape(n_dec, 1, D_MODEL),
            s_hgrn_p[None], s_ssm_p[None], conv_p[None],
            s_hgrn_s[None], s_ssm_s[None], conv_s[None])
```

```python
import functools

import jax
import jax.numpy as jnp
from jax import lax
from jax.experimental import pallas as pl
from jax.experimental.pallas import tpu as pltpu

F32 = jnp.float32
BF16 = jnp.bfloat16

D_MODEL = 2048
D_A = 2048
HA_D = 128
H_A = D_A // HA_D
D_B = 2048
B_P = 64
H_B = D_B // B_P
B_G = 8
HPG = H_B // B_G
GW = HPG * B_P
B_N = 128
CONV_W = 4
CONV_DIM = D_B + 2 * B_G * B_N
D_MAIN = 4 * D_A + D_B + CONV_DIM
D_FF = 4 * D_MODEL
N_MOD = 6
EPS = 1e-6

OFF_Q, OFF_F, OFF_I, OFF_G = 0, D_A, 2 * D_A, 3 * D_A
OFF_Z = 4 * D_A
OFF_X = OFF_Z + D_B
OFF_B = OFF_X + D_B
OFF_C = OFF_B + B_G * B_N
REGION_SILU = (0, 3, 4)
REGION_FORGET = 1

LANE = 128
SUBLANE = 8
VMEM_LIMIT = 56 * 1024 * 1024

CHUNK_A = 128
SUB_A = 16
CHUNK_B = 128
HGRN_HEADS_PER_STEP = 4
SSD_GROUPS_PER_STEP = 2


def _tiles(t, n_dec):
    return dict(
        inproj_p=(_pick_tile(t, 2048), HGRN_HEADS_PER_STEP * HA_D), inproj_s=(n_dec, 2048),
        outproj_p=_pick_tile(t, 256), outproj_s=n_dec,
        mlp_p=(_pick_tile(t, 1024), 512), mlp_s=(n_dec, 1024),
        ada_n=1024)


def _pick_tile(m, target):
    t = min(m, target)
    while m % t:
        t //= 2
    return t


def _silu(x):
    hx = 0.5 * x
    return hx * jnp.tanh(hx) + hx


def _softplus(x):
    return jnp.maximum(x, 0.0) + jnp.log1p(jnp.exp(-jnp.abs(x)))


def _idiv(x, d):
    assert d & (d - 1) == 0
    return jnp.right_shift(x, d.bit_length() - 1)


def _rms(x):
    return x * lax.rsqrt(jnp.mean(x * x, axis=-1, keepdims=True) + EPS)


def _cumsum_rows(x, n, row):
    s = 1
    while s < n:
        x = x + jnp.where(row >= s, pltpu.roll(x, s, 0), 0.0)
        s *= 2
    return x


def _cumsum_lanes(x, n, lane):
    s = 1
    while s < n:
        x = x + jnp.where(lane >= s, pltpu.roll(x, s, 1), 0.0)
        s *= 2
    return x


def _dot(a, b):
    return jnp.dot(a.astype(BF16), b.astype(BF16), preferred_element_type=F32)


def _dot_nt(a, b):
    return lax.dot_general(a.astype(BF16), b.astype(BF16), (((1,), (1,)), ((), ())),
                           preferred_element_type=F32)


def _exact_terms(x):
    hi = x.astype(BF16).astype(F32)
    mid = (x - hi).astype(BF16).astype(F32)
    lo = x - hi - mid
    return jnp.concatenate([hi, mid, lo, jnp.zeros_like(hi)], axis=0).astype(BF16)


def _transpose_bcast(x, sel):
    return lax.dot_general(_exact_terms(x), sel, (((0,), (0,)), ((), ())),
                           preferred_element_type=F32)


def _params(*sem):
    return pltpu.CompilerParams(dimension_semantics=sem, vmem_limit_bytes=VMEM_LIMIT)


def _ada_kernel(c_ref, w_ref, b_ref, o_ref):
    o_ref[...] = _dot(_silu(c_ref[...]), w_ref[...]) + b_ref[...]


def _ada(c_all, w_ada, b_ada, tn):
    m = c_all.shape[0]
    n = w_ada.shape[1]
    return pl.pallas_call(
        _ada_kernel,
        grid=(n // tn,),
        in_specs=[pl.BlockSpec((m, D_MODEL), lambda j: (0, 0)),
                  pl.BlockSpec((D_MODEL, tn), lambda j: (0, j)),
                  pl.BlockSpec((1, tn), lambda j: (0, j))],
        out_specs=pl.BlockSpec((m, tn), lambda j: (0, j)),
        out_shape=jax.ShapeDtypeStruct((m, n), F32),
        compiler_params=_params("parallel"),
        name="ada_mod",
    )(c_all, w_ada, b_ada)


def _mod_spec(per_row, rows_per_seq, tm, col):
    if per_row:
        return pl.BlockSpec((tm, D_MODEL), lambda i, j: (i, col))
    tiles_per_seq = rows_per_seq // tm
    return pl.BlockSpec((None, 1, D_MODEL), lambda i, j: (i // tiles_per_seq, 0, col))


def _lower_bound(logits):
    e = jnp.exp(logits - jnp.max(logits, axis=0, keepdims=True))
    return e[0:1] / jnp.sum(e, axis=0, keepdims=True)


def _inproj_kernel(x_ref, sh_ref, sc_ref, g_ref, w_ref, wdt_ref, lbl_ref, o_ref, odt_ref, h_scr, *, tn):
    j = pl.program_id(1)

    @pl.when(j == 0)
    def _():
        h = _rms(x_ref[...]) * g_ref[...] * (1.0 + sc_ref[...]) + sh_ref[...]
        hb = h.astype(BF16)
        h_scr[...] = hb
        odt_ref[...] = _dot_nt(hb, wdt_ref[...])

    region = (j * tn) // D_A
    is_silu = functools.reduce(jnp.logical_or, [region == r for r in REGION_SILU])

    @pl.when(is_silu)
    def _():
        o_ref[...] = _silu(_dot_nt(h_scr[...], w_ref[...]))

    @pl.when(region == REGION_FORGET)
    def _():
        lb = _lower_bound(lbl_ref[...])
        half = 0.5 * (1.0 - lb)
        o_ref[...] = jnp.log((lb + half) + half * jnp.tanh(0.5 * _dot_nt(h_scr[...], w_ref[...])))

    @pl.when(jnp.logical_not(jnp.logical_or(is_silu, region == REGION_FORGET)))
    def _():
        o_ref[...] = _dot_nt(h_scr[...], w_ref[...])


def _inproj(x, mod, per_row, rows_per_seq, g_pre, w_t, lb_logits, tm, tn, out_tile=lambda j: j):
    m = x.shape[0]
    assert D_MAIN % tn == 0 and D_MAIN % H_B == 0 and D_A % tn == 0
    f_tiles = D_A // tn
    return pl.pallas_call(
        functools.partial(_inproj_kernel, tn=tn),
        grid=(m // tm, D_MAIN // tn),
        in_specs=[pl.BlockSpec((tm, D_MODEL), lambda i, j: (i, 0), pipeline_mode=pl.Buffered(1)),
                  _mod_spec(per_row, rows_per_seq, tm, 0),
                  _mod_spec(per_row, rows_per_seq, tm, 1),
                  pl.BlockSpec((1, D_MODEL), lambda i, j: (0, 0)),
                  pl.BlockSpec((tn, D_MODEL), lambda i, j: (j, 0)),
                  pl.BlockSpec((H_B, D_MODEL), lambda i, j: (D_MAIN // H_B, 0)),
                  pl.BlockSpec((lb_logits.shape[0], tn),
                               lambda i, j: (0, jnp.clip(j - REGION_FORGET * f_tiles, 0, f_tiles - 1)))],
        out_specs=[pl.BlockSpec((tm, tn), lambda i, j: (i, out_tile(j))),
                   pl.BlockSpec((tm, H_B), lambda i, j: (i, 0))],
        out_shape=[jax.ShapeDtypeStruct((m, D_MAIN), F32),
                   jax.ShapeDtypeStruct((m, H_B), F32)],
        scratch_shapes=[pltpu.VMEM((tm, D_MODEL), BF16)],
        compiler_params=_params("parallel", "arbitrary"),
        name="in_proj",
    )(x, mod, mod, g_pre, w_t, w_t, lb_logits)


def _hgrn_interleave(j, tn):
    n_hg = D_A // tn
    return jnp.where(j < 4 * n_hg, 4 * (j % n_hg) + j // n_hg, j)


def _hgrn_scores(qs, lf, row):
    c_len = CHUNK_A
    n_sub = c_len // SUB_A
    k = 1.0 - jnp.exp(lf)
    b = _cumsum_rows(lf, SUB_A, row & (SUB_A - 1))
    k_acc = []
    blocks, q_dec = [], []
    start = jnp.zeros((1, HA_D), F32)
    for i in range(n_sub):
        lo = i * SUB_A
        hi = lo + SUB_A
        b_i = b[lo:hi, :]
        b_sub = b[hi - 1:hi, :]
        k_i = k[lo:hi, :]
        qg = qs[lo:hi, :] * jnp.exp(b_i)
        k_diag = k_i * jnp.exp(-b_i)
        rhs = jnp.concatenate([p.astype(BF16) for p in k_acc] + [k_diag.astype(BF16)], axis=0)
        a_i = lax.dot_general(qg.astype(BF16), rhs, (((1,), (1,)), ((), ())),
                              preferred_element_type=F32)
        causal = (lax.broadcasted_iota(jnp.int32, (SUB_A, hi), 1)
                  <= lax.broadcasted_iota(jnp.int32, (SUB_A, hi), 0) + lo)
        a_i = jnp.where(causal, a_i, 0.0)
        blocks.append(jnp.pad(a_i, ((0, 0), (0, c_len - hi))).astype(BF16))
        q_dec.append((qg * jnp.exp(start)).astype(BF16))
        sub_decay = jnp.exp(b_sub)
        k_acc = [p * sub_decay for p in k_acc] + [k_i * jnp.exp(b_sub - b_i)]
        start = start + b_sub
    scores = jnp.concatenate(blocks, axis=0)
    k_end = jnp.concatenate([p.astype(BF16) for p in k_acc], axis=0)
    chunk_decay = jnp.exp(start)
    return scores, jnp.concatenate(q_dec, axis=0), k_end, chunk_decay


def _hgrn_apply(pend, v, gs, nw, st):
    scores, q_dec, k_end, chunk_decay = pend
    vb = v.astype(BF16)
    o = (jnp.dot(scores, vb, preferred_element_type=F32)
         + lax.dot_general(q_dec, st.astype(BF16), (((1,), (1,)), ((), ())),
                           preferred_element_type=F32))
    st = chunk_decay * st + lax.dot_general(vb, k_end, (((0,), (0,)), ((), ())),
                                            preferred_element_type=F32)
    return (_rms(o) * nw * gs).astype(BF16), st


def _hgrn_prompt_kernel(p_ref, nw_ref, o_ref, s_ref, *, n_chunks, n_heads):
    c_len = CHUNK_A
    w = n_heads * HA_D
    nw = nw_ref[...]
    row = lax.broadcasted_iota(jnp.int32, (c_len, HA_D), 0)
    heads = [slice(h * HA_D, (h + 1) * HA_D) for h in range(n_heads)]

    def part(k, sl, hs):
        return p_ref[sl, k * w + hs.start:k * w + hs.stop]

    def rows(c):
        return pl.ds(pl.multiple_of(c * c_len, c_len), c_len)

    def scores_of(c):
        sl = rows(c)
        return tuple(_hgrn_scores(part(0, sl, hs), part(1, sl, hs), row) for hs in heads)

    def apply_to(c, pends, sts):
        sl = rows(c)
        new = []
        for hs, pend, st in zip(heads, pends, sts):
            out, st = _hgrn_apply(pend, part(2, sl, hs), part(3, sl, hs), nw[:, hs], st)
            o_ref[sl, hs] = out
            new.append(st)
        return tuple(new)

    def step(c, carry):
        pends, sts = carry
        sts = apply_to(c - 1, pends, sts)
        return scores_of(c), sts

    zeros = tuple(jnp.zeros((HA_D, HA_D), F32) for _ in range(n_heads))
    pends, sts = lax.fori_loop(1, n_chunks, step, (scores_of(0), zeros))
    sts = apply_to(n_chunks - 1, pends, sts)
    for h in range(n_heads):
        s_ref[h] = sts[h].T


def _hgrn_prompt(proj, hgrn_norm, n_seq, t):
    m = proj.shape[0]
    nh = HGRN_HEADS_PER_STEP
    w = nh * HA_D
    return pl.pallas_call(
        functools.partial(_hgrn_prompt_kernel, n_chunks=t // CHUNK_A, n_heads=nh),
        grid=(n_seq, H_A // nh),
        in_specs=[pl.BlockSpec((t, 4 * w), lambda b, h: (b, h)),
                  pl.BlockSpec((1, w), lambda b, h: (0, h))],
        out_specs=[pl.BlockSpec((t, w), lambda b, h: (b, h)),
                   pl.BlockSpec((None, nh, HA_D, HA_D), lambda b, h: (b, h, 0, 0))],
        out_shape=[jax.ShapeDtypeStruct((m, D_A), BF16),
                   jax.ShapeDtypeStruct((n_seq, H_A, HA_D, HA_D), F32)],
        compiler_params=_params("parallel", "parallel"),
        name="hgrn_prompt",
    )(proj, hgrn_norm)


def _conv_chunk(tail, u, w, bias, n):
    full = jnp.concatenate([tail, u], axis=0)
    out = bias + w[CONV_W - 1:CONV_W, :] * u
    for i in range(CONV_W - 1):
        shifted = pltpu.roll(full, CONV_W - 1 - i, 0)[SUBLANE:SUBLANE + n, :]
        out = out + w[i:i + 1, :] * shifted
    return _silu(out)


def _ssd_prompt_kernel(z_ref, x_ref, b_ref, c_ref, dtr_ref, cpar_ref, dpar_ref,
                       o_ref, h_ref, row_scr, *, n_chunks, n_groups):
    n = CHUNK_B
    assert n == LANE
    tri = (lax.broadcasted_iota(jnp.int32, (n, n), 0)
           >= lax.broadcasted_iota(jnp.int32, (n, n), 1))
    lane_head = _idiv(lax.broadcasted_iota(jnp.int32, (1, GW), 1), B_P)
    row_head = _idiv(lax.broadcasted_iota(jnp.int32, (GW, 1), 0), B_P)
    nr = n_chunks * SUBLANE
    lane_nr = lax.broadcasted_iota(jnp.int32, (nr, n), 1)
    is_dt = (lax.broadcasted_iota(jnp.int32, (nr, n), 0) & (SUBLANE - 1)) < HPG
    is_dt8 = lax.broadcasted_iota(jnp.int32, (SUBLANE, n), 0) < HPG
    n_src = 2 * SUBLANE
    sel_shape = (4 * n_src, HPG * n + 2 * GW)
    sel_k = lax.broadcasted_iota(jnp.int32, sel_shape, 0)
    sel_l = lax.broadcasted_iota(jnp.int32, sel_shape, 1)
    want = jnp.where(sel_l < HPG * n, HPG + _idiv(sel_l, n),
                     2 * HPG + _idiv(sel_l - HPG * n, B_P))
    bcast_sel = jnp.where(((sel_k & (n_src - 1)) == want) & (sel_k < 3 * n_src), 1.0, 0.0).astype(BF16)

    groups = []
    for gi in range(n_groups):
        xs = slice(gi * GW, (gi + 1) * GW)
        bs = slice(gi * B_N, (gi + 1) * B_N)
        dpar = dpar_ref[gi]
        dt_all = _softplus(dtr_ref[gi] + dpar[:, 0:1])
        cum_all = _cumsum_lanes(dt_all * -jnp.exp(dpar[:, 1:2]), n, lane_nr)
        row_scr[gi] = jnp.where(is_dt, dt_all, cum_all).reshape(n_chunks, SUBLANE, n)
        bl = slice(n_groups * GW + gi * B_N, n_groups * GW + (gi + 1) * B_N)
        cl = slice(n_groups * (GW + B_N) + gi * B_N, n_groups * (GW + B_N) + (gi + 1) * B_N)
        taps, bias = slice(0, CONV_W), slice(CONV_W, CONV_W + 1)
        groups.append(dict(gi=gi, xs=xs, bs=bs, dskip=cpar_ref[6:7, xs], nw=cpar_ref[5:6, xs],
                           wx=cpar_ref[taps, xs], wb=cpar_ref[taps, bl], wc=cpar_ref[taps, cl],
                           bx=cpar_ref[bias, xs], bb=cpar_ref[bias, bl], bc=cpar_ref[bias, cl]))

    def rows(c):
        return pl.ds(pl.multiple_of(c * n, n), n)

    def scan_free(c, grp):
        xs, bs = grp["xs"], grp["bs"]
        sl = rows(c)
        if isinstance(c, int) and c == 0:
            tail = lambda ref, lanes: jnp.zeros((SUBLANE, lanes.stop - lanes.start), F32)
        else:
            prev = pl.ds(pl.multiple_of(c * n - SUBLANE, SUBLANE), SUBLANE)
            tail = lambda ref, lanes: ref[prev, lanes]
        xc = _conv_chunk(tail(x_ref, xs), x_ref[sl, xs], grp["wx"], grp["bx"], n)
        bcv = _conv_chunk(tail(b_ref, bs), b_ref[sl, bs], grp["wb"], grp["bb"], n)
        ccv = _conv_chunk(tail(c_ref, bs), c_ref[sl, bs], grp["wc"], grp["bc"], n)
        r8 = row_scr[grp["gi"], c]
        swapped = pltpu.roll(r8, HPG, 0)
        cum8 = jnp.where(is_dt8, swapped, r8)
        dt8 = jnp.where(is_dt8, r8, swapped)
        e8 = jnp.where(is_dt8, jnp.exp(cum8), jnp.exp(cum8[:, n - 1:n] - cum8) * dt8)
        cols = _transpose_bcast(jnp.concatenate([r8, e8], axis=0), bcast_sel)
        e_cum = cols[:, HPG * n:HPG * n + GW]
        w_end = cols[:, HPG * n + GW:]
        g = _dot_nt(ccv, bcv)
        xb = xc.astype(BF16)
        s_all, x_all = [], []
        for j in range(HPG):
            cc = cols[:, j * n:(j + 1) * n]
            seg = cc - r8[HPG + j:HPG + j + 1, :]
            s = g * jnp.exp(jnp.where(tri, seg, -jnp.inf)) * r8[j:j + 1, :]
            s_all.append(s.astype(BF16))
            x_all.append(jnp.where(lane_head == j, xb, jnp.zeros_like(xb)))
        y = grp["dskip"] * xc + jnp.dot(jnp.concatenate(s_all, axis=1), jnp.concatenate(x_all, axis=0),
                                        preferred_element_type=F32)
        return y, e_cum, ccv.astype(BF16), (xc * w_end).astype(BF16), bcv.astype(BF16), r8

    def apply_state(c, grp, pend, h):
        xs = grp["xs"]
        y, e_cum, ccb, xwb, bcb, r8 = pend
        sl = rows(c)
        ch = lax.dot_general(ccb, h.astype(BF16), (((1,), (1,)), ((), ())),
                             preferred_element_type=F32)
        yz = (y + ch * e_cum) * z_ref[sl, xs]
        o_ref[sl, xs] = (_rms(yz) * grp["nw"]).astype(o_ref.dtype)
        h_dec = jnp.zeros((GW, 1), F32)
        for j in range(HPG):
            h_dec = jnp.where(row_head == j, jnp.exp(r8[HPG + j:HPG + j + 1, n - 1:n]), h_dec)
        return h_dec * h + lax.dot_general(xwb, bcb, (((0,), (0,)), ((), ())),
                                           preferred_element_type=F32)

    def step(c, carry):
        pends, hs = carry
        hs = tuple(apply_state(c - 1, grp, pend, h) for grp, pend, h in zip(groups, pends, hs))
        return tuple(scan_free(c, grp) for grp in groups), hs

    init = (tuple(scan_free(0, grp) for grp in groups),
            tuple(jnp.zeros((GW, B_N), F32) for _ in groups))
    pends, hs = lax.fori_loop(1, n_chunks, step, init)
    for gi, (grp, pend, h) in enumerate(zip(groups, pends, hs)):
        h = apply_state(n_chunks - 1, grp, pend, h)
        h_ref[gi * HPG:(gi + 1) * HPG] = h.reshape(HPG, B_P, B_N)


def _ssd_channel_params(conv_w, conv_b, ssd_norm, d_skip, ng):
    zeros = jnp.zeros((1, 2 * B_G * B_N), F32)
    rows = jnp.concatenate([
        conv_w, conv_b,
        jnp.concatenate([ssd_norm, zeros], axis=1),
        jnp.concatenate([jnp.repeat(d_skip, B_P)[None], zeros], axis=1),
        jnp.zeros((SUBLANE - CONV_W - 3, CONV_DIM), F32)], axis=0)
    n_steps = B_G // ng
    part = lambda lo, width: rows[:, lo:lo + n_steps * width].reshape(SUBLANE, n_steps, width)
    slab = jnp.concatenate([part(0, ng * GW), part(D_B, ng * B_N), part(D_B + B_G * B_N, ng * B_N)], axis=2)
    return slab.transpose(1, 0, 2)


def _ssd_prompt(proj, dt_rows, cpar, dpar, n_seq, t):
    m = proj.shape[0]
    n_chunks = t // CHUNK_B
    nr = n_chunks * SUBLANE
    ng = SSD_GROUPS_PER_STEP
    xw, bw = ng * GW, ng * B_N
    wide = lambda off: pl.BlockSpec((t, xw), lambda b, g: (b, off // xw + g))
    narrow = lambda off: pl.BlockSpec((t, bw), lambda b, g: (b, off // bw + g))
    return pl.pallas_call(
        functools.partial(_ssd_prompt_kernel, n_chunks=n_chunks, n_groups=ng),
        grid=(n_seq, B_G // ng),
        in_specs=[wide(OFF_Z), wide(OFF_X), narrow(OFF_B), narrow(OFF_C),
                  pl.BlockSpec((ng, None, nr, CHUNK_B), lambda b, g: (g, b, 0, 0)),
                  pl.BlockSpec((None, SUBLANE, xw + 2 * bw), lambda b, g: (g, 0, 0)),
                  pl.BlockSpec((ng, nr, 2), lambda b, g: (g, 0, 0))],
        out_specs=[pl.BlockSpec((t, xw), lambda b, g: (b, g)),
                   pl.BlockSpec((None, ng * HPG, B_P, B_N), lambda b, g: (b, g, 0, 0))],
        out_shape=[jax.ShapeDtypeStruct((m, D_B), BF16),
                   jax.ShapeDtypeStruct((n_seq, H_B, B_P, B_N), F32)],
        scratch_shapes=[pltpu.VMEM((ng, n_chunks, SUBLANE, CHUNK_B), F32)],
        compiler_params=_params("parallel", "parallel"),
        name="ssd_prompt",
    )(proj, proj, proj, proj, dt_rows, cpar, dpar)


def _hgrn_step_rows(q_ref, f_ref, v_ref, g_ref, nw_ref, s_ref, o_ref, so_ref):
    nw = nw_ref[...]
    pad = jnp.zeros((HA_D - H_A, HA_D), F32)
    for b in range(q_ref.shape[0]):
        ft = jnp.concatenate([jnp.exp(f_ref[b]), pad], axis=0).T
        qt = jnp.concatenate([q_ref[b], pad], axis=0).T
        v = v_ref[b]
        rows = []
        for h in range(H_A):
            fc = ft[:, h:h + 1]
            s_new = fc * s_ref[b, h] + (1.0 - fc) * v[h:h + 1, :]
            so_ref[b, h] = s_new
            rows.append(jnp.sum(qt[:, h:h + 1] * s_new, axis=0, keepdims=True))
        o = jnp.concatenate(rows, axis=0)
        o_ref[b] = (_rms(o) * nw * g_ref[b]).astype(o_ref.dtype)


def _conv_step(buf_ref, b, u, w, bias):
    out = bias + w[CONV_W - 1] * u
    for i in range(CONV_W - 1):
        out = out + w[i] * buf_ref[b, i]
    return _silu(out)


N_SSD_STEP_LEAD = 6
N_SSD_STEP_IN = 15


def _ssd_step_rows(xt_ref, bcr_ref, zt_ref, dt_ref, cxt_ref, cbc_ref,
                   wxt_ref, wbc_ref, bxt_ref, bbc_ref, dtb_ref, al_ref, dsk_ref, nwt_ref, h_ref,
                   o_ref, ho_ref):
    wxt, wbc, bxt, bbc = wxt_ref[...], wbc_ref[...], bxt_ref[...], bbc_ref[...]
    a = -jnp.exp(al_ref[...])
    dtb, dsk, nwt = dtb_ref[...], dsk_ref[...], nwt_ref[...]
    lane = lax.broadcasted_iota(jnp.int32, (1, H_B), 1)
    lane_group = _idiv(lane, HPG)
    for b in range(xt_ref.shape[0]):
        xt = _conv_step(cxt_ref, b, xt_ref[b], wxt, bxt)
        bcv = _conv_step(cbc_ref, b, bcr_ref[b], wbc, bbc)
        dt = _softplus(dt_ref[b] + dtb)
        d_a = jnp.exp(dt * a)
        xdt = xt * dt
        yt = jnp.zeros((B_P, H_B), F32)
        for hh in range(H_B):
            g = hh // HPG
            h_new = d_a[:, hh:hh + 1] * h_ref[b, hh] + xdt[:, hh:hh + 1] * bcv[g:g + 1, :]
            ho_ref[b, hh] = h_new
            y_col = jnp.sum(h_new * bcv[B_G + g:B_G + g + 1, :], axis=1, keepdims=True)
            yt = jnp.where(lane == hh, y_col, yt)
        yz = (yt + dsk * xt) * zt_ref[b]
        col_sq = jnp.sum(yz * yz, axis=0, keepdims=True)
        ms = jnp.zeros((1, H_B), F32)
        for g in range(B_G):
            total = jnp.sum(col_sq[:, g * HPG:(g + 1) * HPG], axis=1, keepdims=True)
            ms = jnp.where(lane_group == g, total * (1.0 / GW), ms)
        o_ref[b] = (yz * lax.rsqrt(ms + EPS) * nwt).astype(o_ref.dtype)


def _ssd_step_specs(operands, state, bt, step_index):
    lead_ops, const_ops = operands[:N_SSD_STEP_LEAD], operands[N_SSD_STEP_LEAD:]
    lead = lambda a: pl.BlockSpec((bt,) + a.shape[1:], lambda *g: (step_index(*g),) + (0,) * (a.ndim - 1))
    full = lambda a: pl.BlockSpec(a.shape, lambda *g: (0,) * a.ndim)
    nb = state.shape[0]
    in_specs = [lead(a) for a in lead_ops] + [full(a) for a in const_ops] + [lead(state)]
    out_specs = [pl.BlockSpec((bt, B_P, H_B), lambda *g: (step_index(*g), 0, 0)), lead(state)]
    out_shape = [jax.ShapeDtypeStruct((nb, B_P, H_B), F32), jax.ShapeDtypeStruct(state.shape, F32)]
    return in_specs, out_specs, out_shape


def _outproj_kernel(oa_ref, ob_ref, wa_ref, wb_ref, x_ref, gt_ref, gp_ref, *rest):
    y_ref = rest[-3] if len(rest) > 1 else rest[0]
    mix = _dot(oa_ref[...], wa_ref[...]) + _dot(ob_ref[...], wb_ref[...])
    y_ref[...] = x_ref[...] + gt_ref[...] * (_rms(mix) * gp_ref[...])
    if len(rest) > 1:
        q_ref, f_ref, v_ref, g_ref, nw_ref, s_ref, _, o_ref, so_ref = rest
        _hgrn_step_rows(q_ref, f_ref, v_ref, g_ref, nw_ref, s_ref, o_ref, so_ref)


def _outproj(o_a, o_b, w_out, x, mod, per_row, rows_per_seq, g_post, tm, hgrn_step=None):
    m = x.shape[0]
    n_steps = m // tm
    row = lambda w: pl.BlockSpec((tm, w), lambda i, j: (i, 0))
    const = lambda shape, idx: pl.BlockSpec(shape, lambda i, j: idx, pipeline_mode=pl.Buffered(1))
    in_specs = [row(D_A), row(D_B),
                const((D_A, D_MODEL), (0, 0)), const((D_B, D_MODEL), (1, 0)),
                row(D_MODEL),
                _mod_spec(per_row, rows_per_seq, tm, 2),
                pl.BlockSpec((1, D_MODEL), lambda i, j: (0, 0))]
    out_specs = [row(D_MODEL)]
    out_shape = [jax.ShapeDtypeStruct((m, D_MODEL), F32)]
    args = [o_a, o_b, w_out, w_out, x, mod, g_post]
    if hgrn_step is not None:
        q, f, v, g, nw, state = hgrn_step
        nb = q.shape[0]
        assert nb % n_steps == 0
        bt = nb // n_steps
        rw = pl.BlockSpec((bt, H_A, HA_D), lambda i, j: (i, 0, 0))
        st = pl.BlockSpec((bt, H_A, HA_D, HA_D), lambda i, j: (i, 0, 0, 0))
        in_specs += [rw, rw, rw, rw, pl.BlockSpec((H_A, HA_D), lambda i, j: (0, 0)), st]
        out_specs += [rw, st]
        out_shape += [jax.ShapeDtypeStruct((nb, H_A, HA_D), BF16), jax.ShapeDtypeStruct(state.shape, F32)]
        args += [q, f, v, g, nw, state]
    out = pl.pallas_call(
        _outproj_kernel,
        grid=(n_steps, 1),
        in_specs=in_specs,
        out_specs=out_specs,
        out_shape=out_shape,
        compiler_params=_params("parallel", "arbitrary"),
        name="out_proj",
    )(*args)
    return out if hgrn_step is not None else out[0]


def _mlp_kernel(x_ref, sh_ref, sc_ref, gt_ref, gpre_ref, gpost_ref, wu_ref, wd_ref, *rest):
    ssd = len(rest) > 2
    if ssd:
        ssd_in, (y_ref, so_ref, sh_out_ref, h_scr) = rest[:N_SSD_STEP_IN], rest[N_SSD_STEP_IN:]
    else:
        y_ref, h_scr = rest
    f = pl.program_id(1)

    @pl.when(f == 0)
    def _():
        h = _rms(x_ref[...]) * gpre_ref[...] * (1.0 + sc_ref[...]) + sh_ref[...]
        h_scr[...] = h.astype(BF16)
        y_ref[...] = jnp.zeros_like(y_ref)

    if ssd:
        _ssd_step_rows(*ssd_in, so_ref, sh_out_ref)
    u = jnp.maximum(jnp.dot(h_scr[...], wu_ref[...].astype(BF16), preferred_element_type=F32), 0.0)
    y_ref[...] += jnp.dot((u * u).astype(BF16), wd_ref[...].astype(BF16), preferred_element_type=F32)

    @pl.when(f == pl.num_programs(1) - 1)
    def _():
        y_ref[...] = x_ref[...] + gt_ref[...] * (_rms(y_ref[...]) * gpost_ref[...])


def _mlp(x, mod, per_row, rows_per_seq, g_pre, g_post, w_up, w_down, tm, tf, ssd_step=None):
    m = x.shape[0]
    n_f = D_FF // tf
    n_steps = (m // tm) * n_f
    row = pl.BlockSpec((tm, D_MODEL), lambda i, j: (i, 0))
    vec = pl.BlockSpec((1, D_MODEL), lambda i, j: (0, 0))
    in_specs = [pl.BlockSpec((tm, D_MODEL), lambda i, j: (i, 0), pipeline_mode=pl.Buffered(1)),
                _mod_spec(per_row, rows_per_seq, tm, 3),
                _mod_spec(per_row, rows_per_seq, tm, 4),
                _mod_spec(per_row, rows_per_seq, tm, 5),
                vec, vec,
                pl.BlockSpec((D_MODEL, tf), lambda i, j: (0, j)),
                pl.BlockSpec((tf, D_MODEL), lambda i, j: (j, 0))]
    out_specs = [row]
    out_shape = [jax.ShapeDtypeStruct((m, D_MODEL), F32)]
    args = [x, mod, mod, mod, g_pre, g_post, w_up, w_down]
    if ssd_step is not None:
        out_specs = [pl.BlockSpec((tm, D_MODEL), lambda i, j: (i, 0), pipeline_mode=pl.Buffered(1))]
        operands, state = ssd_step
        nb = state.shape[0]
        assert nb % n_steps == 0
        s_in, s_out, s_shape = _ssd_step_specs(operands, state, nb // n_steps, lambda i, j: i * n_f + j)
        in_specs += s_in
        out_specs += s_out
        out_shape += s_shape
        args += list(operands) + [state]
    out = pl.pallas_call(
        _mlp_kernel,
        grid=(m // tm, n_f),
        in_specs=in_specs,
        out_specs=out_specs,
        out_shape=out_shape,
        scratch_shapes=[pltpu.VMEM((tm, D_MODEL), BF16)],
        compiler_params=_params("parallel", "arbitrary"),
        name="mlp",
    )(*args)
    return out if ssd_step is not None else out[0]


def kernel(x_prompt, x_sample, c_prompt, c_sample, state_hgrn, state_ssm, state_conv, w_ada, b_ada, norm_pre_mix, norm_post_mix, norm_pre_mlp, norm_post_mlp, w_in, hgrn_lb_logits, hgrn_norm, conv_w, conv_b, dt_bias, a_log, d_skip, ssd_norm, w_out, w_up, w_down):
    n_seq, t, _ = x_prompt.shape
    n_dec = x_sample.shape[0]
    assert x_sample.shape[1] == 1 and t % CHUNK_A == 0 and t % CHUNK_B == 0
    assert w_ada.shape[0] == 1, "one layer"
    l = 0
    tiles = _tiles(t, n_dec)

    w_in_t = w_in[l].T
    w_out_b = w_out[l].astype(BF16)
    g_pre_mix, g_post_mix = norm_pre_mix[l][None], norm_post_mix[l][None]
    g_pre_mlp, g_post_mlp = norm_pre_mlp[l][None], norm_post_mlp[l][None]
    cw, cb = conv_w[l], conv_b[l][None]
    hn = hgrn_norm[l][None]
    sn = ssd_norm[l][None]

    n_c = n_seq + n_dec
    pad = (-n_c) % SUBLANE
    c_all = jnp.concatenate([c_prompt, c_sample, jnp.zeros((pad, D_MODEL), F32)], axis=0)
    mod = _ada(c_all, w_ada[l], b_ada[l][None], tiles["ada_n"])
    mod_p = mod[:n_seq].reshape(n_seq, 1, N_MOD * D_MODEL)
    mod_s = mod[n_seq:n_c]

    xp = x_prompt.reshape(n_seq * t, D_MODEL)
    tm, tn = tiles["inproj_p"]
    proj_p, dt_p = _inproj(xp, mod_p, False, t, g_pre_mix, w_in_t, hgrn_lb_logits, tm, tn,
                           functools.partial(_hgrn_interleave, tn=tn))
    o_a, s_hgrn_p = _hgrn_prompt(proj_p, hn, n_seq, t)

    n_chunks = t // CHUNK_B
    rep = SUBLANE // HPG
    dt_rows = dt_p.reshape(n_seq, n_chunks, CHUNK_B, B_G, 1, HPG).transpose(3, 0, 1, 4, 5, 2)
    dt_rows = jnp.broadcast_to(dt_rows, (B_G, n_seq, n_chunks, rep, HPG, CHUNK_B))
    dt_rows = dt_rows.reshape(B_G, n_seq, n_chunks * SUBLANE, CHUNK_B)
    prow = lambda p: jnp.tile(p.reshape(B_G, 1, HPG), (1, n_chunks * rep, 1)).reshape(B_G, n_chunks * SUBLANE, 1)
    cpar = _ssd_channel_params(cw, cb, sn, d_skip[l], SSD_GROUPS_PER_STEP)
    dpar = jnp.concatenate([prow(dt_bias[l]), prow(a_log[l])], axis=2)
    o_b, s_ssm_p = _ssd_prompt(proj_p, dt_rows, cpar, dpar, n_seq, t)
    xs = x_sample.reshape(n_dec, D_MODEL)
    proj_s, dt_s = _inproj(xs, mod_s, True, 1, g_pre_mix, w_in_t, hgrn_lb_logits, *tiles["inproj_s"])
    heads = lambda off: proj_s[:, off:off + D_A].reshape(n_dec, H_A, HA_D)
    x1, o_a_s, s_hgrn_s = _outproj(
        o_a, o_b, w_out_b, xp, mod_p, False, t, g_post_mix, tiles["outproj_p"],
        hgrn_step=(heads(OFF_Q), heads(OFF_F), heads(OFF_I), heads(OFF_G), hn.reshape(H_A, HA_D),
                   state_hgrn[l]))
    conv_p = proj_p.reshape(n_seq, t, D_MAIN)[:, t - (CONV_W - 1):, OFF_X:]

    to_t = lambda a, h, d: a.reshape(a.shape[:-1] + (h, d)).swapaxes(-1, -2)
    xbc_s = proj_s[:, OFF_X:]
    cst = state_conv[l]
    xt = to_t(xbc_s[:, :D_B], H_B, B_P)
    zt = to_t(proj_s[:, OFF_Z:OFF_Z + D_B], H_B, B_P)
    bcr = xbc_s[:, D_B:].reshape(n_dec, 2 * B_G, B_N)
    cxt = to_t(cst[:, :, :D_B], H_B, B_P)
    cbc = cst[:, :, D_B:].reshape(n_dec, CONV_W - 1, 2 * B_G, B_N)
    wxt = to_t(cw[:, :D_B], H_B, B_P)
    wbc = cw[:, D_B:].reshape(CONV_W, 2 * B_G, B_N)
    bxt = to_t(cb[:, :D_B], H_B, B_P)[0]
    bbc = cb[0, D_B:].reshape(2 * B_G, B_N)
    ssd_operands = (xt, bcr, zt, dt_s[:, None, :], cxt, cbc,
                    wxt, wbc, bxt, bbc, dt_bias[l][None], a_log[l][None], d_skip[l][None],
                    to_t(sn, H_B, B_P)[0])
    assert len(ssd_operands) + 1 == N_SSD_STEP_IN
    y_p, o_b_st, s_ssm_s = _mlp(x1, mod_p, False, t, g_pre_mlp, g_post_mlp, w_up[l], w_down[l],
                                *tiles["mlp_p"], ssd_step=(ssd_operands, state_ssm[l]))
    o_b_s = o_b_st.swapaxes(1, 2).reshape(n_dec, D_B)

    x1_s = _outproj(o_a_s.reshape(n_dec, D_A), o_b_s, w_out_b, xs, mod_s, True, 1,
                    g_post_mix, tiles["outproj_s"])
    y_s = _mlp(x1_s, mod_s, True, 1, g_pre_mlp, g_post_mlp, w_up[l], w_down[l], *tiles["mlp_s"])
    conv_s = jnp.concatenate([cst[:, 1:], xbc_s[:, None, :]], axis=1)

    return (y_p.reshape(n_seq, t, D_MODEL), y_s.reshape(n_dec, 1, D_MODEL),
            s_hgrn_p[None], s_ssm_p[None], conv_p[None],
            s_hgrn_s[None], s_ssm_s[None], conv_s[None])
```

```python
import functools

import jax
import jax.numpy as jnp
from jax import lax
from jax.experimental import pallas as pl
from jax.experimental.pallas import tpu as pltpu

F32 = jnp.float32
BF16 = jnp.bfloat16

D_MODEL = 2048
D_A = 2048
HA_D = 128
H_A = D_A // HA_D
D_B = 2048
B_P = 64
H_B = D_B // B_P
B_G = 8
HPG = H_B // B_G
GW = HPG * B_P
B_N = 128
CONV_W = 4
CONV_DIM = D_B + 2 * B_G * B_N
D_MAIN = 4 * D_A + D_B + CONV_DIM
D_FF = 4 * D_MODEL
N_MOD = 6
EPS = 1e-6

OFF_Q, OFF_F, OFF_I, OFF_G = 0, D_A, 2 * D_A, 3 * D_A
OFF_Z = 4 * D_A
OFF_X = OFF_Z + D_B
OFF_B = OFF_X + D_B
OFF_C = OFF_B + B_G * B_N
REGION_SILU = (0, 3, 4)
REGION_FORGET = 1

LANE = 128
SUBLANE = 8
VMEM_LIMIT = 56 * 1024 * 1024

CHUNK_A = 128
SUB_A = 16
CHUNK_B = 128
HGRN_HEADS_PER_STEP = 4
SSD_GROUPS_PER_STEP = 2


def _tiles(t, n_dec):
    return dict(
        inproj=(_pick_tile(t, 2048), HGRN_HEADS_PER_STEP * HA_D),
        outproj_p=_pick_tile(t, 256), outproj_s=n_dec,
        mlp_p=(_pick_tile(t, 1024), 512), mlp_s=(n_dec, 1024),
        ada_n=1024)


def _pick_tile(m, target):
    t = min(m, target)
    while m % t:
        t //= 2
    return t


def _silu(x):
    hx = 0.5 * x
    return hx * jnp.tanh(hx) + hx


def _softplus(x):
    return jnp.maximum(x, 0.0) + jnp.log1p(jnp.exp(-jnp.abs(x)))


def _idiv(x, d):
    assert d & (d - 1) == 0
    return jnp.right_shift(x, d.bit_length() - 1)


def _rms(x):
    return x * lax.rsqrt(jnp.mean(x * x, axis=-1, keepdims=True) + EPS)


def _cumsum_rows(x, n, row):
    s = 1
    while s < n:
        x = x + jnp.where(row >= s, pltpu.roll(x, s, 0), 0.0)
        s *= 2
    return x


def _cumsum_lanes(x, n, lane):
    s = 1
    while s < n:
        x = x + jnp.where(lane >= s, pltpu.roll(x, s, 1), 0.0)
        s *= 2
    return x


def _dot(a, b):
    return jnp.dot(a.astype(BF16), b.astype(BF16), preferred_element_type=F32)


def _dot_nt(a, b):
    return lax.dot_general(a.astype(BF16), b.astype(BF16), (((1,), (1,)), ((), ())),
                           preferred_element_type=F32)


def _exact_terms(x):
    hi = x.astype(BF16).astype(F32)
    mid = (x - hi).astype(BF16).astype(F32)
    lo = x - hi - mid
    return jnp.concatenate([hi, mid, lo, jnp.zeros_like(hi)], axis=0).astype(BF16)


def _transpose_bcast(x, sel):
    return lax.dot_general(_exact_terms(x), sel, (((0,), (0,)), ((), ())),
                           preferred_element_type=F32)


def _params(*sem):
    return pltpu.CompilerParams(dimension_semantics=sem, vmem_limit_bytes=VMEM_LIMIT)


def _ada_kernel(c_ref, w_ref, b_ref, o_ref):
    o_ref[...] = _dot(_silu(c_ref[...]), w_ref[...]) + b_ref[...]


def _ada(c_all, w_ada, b_ada, tn):
    m = c_all.shape[0]
    n = w_ada.shape[1]
    return pl.pallas_call(
        _ada_kernel,
        grid=(n // tn,),
        in_specs=[pl.BlockSpec((m, D_MODEL), lambda j: (0, 0)),
                  pl.BlockSpec((D_MODEL, tn), lambda j: (0, j)),
                  pl.BlockSpec((1, tn), lambda j: (0, j))],
        out_specs=pl.BlockSpec((m, tn), lambda j: (0, j)),
        out_shape=jax.ShapeDtypeStruct((m, n), F32),
        compiler_params=_params("parallel"),
        name="ada_mod",
    )(c_all, w_ada, b_ada)


def _mod_spec(per_row, rows_per_seq, tm, col):
    if per_row:
        return pl.BlockSpec((tm, D_MODEL), lambda i, j: (i, col))
    tiles_per_seq = rows_per_seq // tm
    return pl.BlockSpec((None, 1, D_MODEL), lambda i, j: (i // tiles_per_seq, 0, col))


def _lower_bound(logits):
    e = jnp.exp(logits - jnp.max(logits, axis=0, keepdims=True))
    return e[0:1] / jnp.sum(e, axis=0, keepdims=True)


def _inproj_kernel(x_ref, sh_ref, sc_ref, g_ref, w_ref, wdt_ref, lbl_ref, xs_ref, shs_ref, scs_ref,
                   o_ref, odt_ref, os_ref, odts_ref, h_scr, hs_scr, *, tn):
    i, j = pl.program_id(0), pl.program_id(1)
    first_rows = i == 0

    def normed(x, sh, sc):
        return (_rms(x) * g_ref[...] * (1.0 + sc) + sh).astype(BF16)

    @pl.when(j == 0)
    def _():
        hb = normed(x_ref[...], sh_ref[...], sc_ref[...])
        h_scr[...] = hb
        odt_ref[...] = _dot_nt(hb, wdt_ref[...])

    @pl.when(jnp.logical_and(first_rows, j == 0))
    def _():
        hb = normed(xs_ref[...], shs_ref[...], scs_ref[...])
        hs_scr[...] = hb
        odts_ref[...] = _dot_nt(hb, wdt_ref[...])

    region = (j * tn) // D_A
    is_silu = functools.reduce(jnp.logical_or, [region == r for r in REGION_SILU])

    def both_groups(act):
        o_ref[...] = act(_dot_nt(h_scr[...], w_ref[...]))

        @pl.when(first_rows)
        def _():
            os_ref[...] = act(_dot_nt(hs_scr[...], w_ref[...]))

    def log_forget(p):
        lb = _lower_bound(lbl_ref[...])
        half = 0.5 * (1.0 - lb)
        return jnp.log((lb + half) + half * jnp.tanh(0.5 * p))

    pl.when(is_silu)(lambda: both_groups(_silu))
    pl.when(region == REGION_FORGET)(lambda: both_groups(log_forget))
    pl.when(jnp.logical_not(jnp.logical_or(is_silu, region == REGION_FORGET)))(lambda: both_groups(lambda p: p))


def _inproj(x, mod, rows_per_seq, xs, mod_s, g_pre, w_t, lb_logits, tm, tn, out_tile):
    m, ns = x.shape[0], xs.shape[0]
    assert D_MAIN % tn == 0 and D_MAIN % H_B == 0 and D_A % tn == 0
    f_tiles = D_A // tn
    n_j = D_MAIN // tn
    once = lambda shape, idx: pl.BlockSpec(shape, lambda i, j: idx, pipeline_mode=pl.Buffered(1))
    return pl.pallas_call(
        functools.partial(_inproj_kernel, tn=tn),
        grid=(m // tm, n_j),
        in_specs=[pl.BlockSpec((tm, D_MODEL), lambda i, j: (i, 0), pipeline_mode=pl.Buffered(1)),
                  _mod_spec(False, rows_per_seq, tm, 0),
                  _mod_spec(False, rows_per_seq, tm, 1),
                  pl.BlockSpec((1, D_MODEL), lambda i, j: (0, 0)),
                  pl.BlockSpec((tn, D_MODEL), lambda i, j: (j, 0)),
                  pl.BlockSpec((H_B, D_MODEL), lambda i, j: (D_MAIN // H_B, 0)),
                  pl.BlockSpec((lb_logits.shape[0], tn),
                               lambda i, j: (0, jnp.clip(j - REGION_FORGET * f_tiles, 0, f_tiles - 1))),
                  once((ns, D_MODEL), (0, 0)), once((ns, D_MODEL), (0, 0)), once((ns, D_MODEL), (0, 1))],
        out_specs=[pl.BlockSpec((tm, tn), lambda i, j: (i, out_tile(j))),
                   pl.BlockSpec((tm, H_B), lambda i, j: (i, 0)),
                   pl.BlockSpec((ns, tn), lambda i, j: (0, jnp.where(i == 0, j, n_j - 1))),
                   pl.BlockSpec((ns, H_B), lambda i, j: (0, 0))],
        out_shape=[jax.ShapeDtypeStruct((m, D_MAIN), F32),
                   jax.ShapeDtypeStruct((m, H_B), F32),
                   jax.ShapeDtypeStruct((ns, D_MAIN), F32),
                   jax.ShapeDtypeStruct((ns, H_B), F32)],
        scratch_shapes=[pltpu.VMEM((tm, D_MODEL), BF16), pltpu.VMEM((ns, D_MODEL), BF16)],
        compiler_params=_params("arbitrary", "arbitrary"),
        name="in_proj",
    )(x, mod, mod, g_pre, w_t, w_t, lb_logits, xs, mod_s, mod_s)


def _hgrn_interleave(j, tn):
    n_hg = D_A // tn
    return jnp.where(j < 4 * n_hg, 4 * (j % n_hg) + j // n_hg, j)


def _hgrn_scores(qs, lf, row):
    c_len = CHUNK_A
    n_sub = c_len // SUB_A
    k = 1.0 - jnp.exp(lf)
    b = _cumsum_rows(lf, SUB_A, row & (SUB_A - 1))
    k_acc = []
    blocks, q_dec = [], []
    start = jnp.zeros((1, HA_D), F32)
    for i in range(n_sub):
        lo = i * SUB_A
        hi = lo + SUB_A
        b_i = b[lo:hi, :]
        b_sub = b[hi - 1:hi, :]
        k_i = k[lo:hi, :]
        qg = qs[lo:hi, :] * jnp.exp(b_i)
        k_diag = k_i * jnp.exp(-b_i)
        rhs = jnp.concatenate([p.astype(BF16) for p in k_acc] + [k_diag.astype(BF16)], axis=0)
        a_i = lax.dot_general(qg.astype(BF16), rhs, (((1,), (1,)), ((), ())),
                              preferred_element_type=F32)
        causal = (lax.broadcasted_iota(jnp.int32, (SUB_A, hi), 1)
                  <= lax.broadcasted_iota(jnp.int32, (SUB_A, hi), 0) + lo)
        a_i = jnp.where(causal, a_i, 0.0)
        blocks.append(jnp.pad(a_i, ((0, 0), (0, c_len - hi))).astype(BF16))
        q_dec.append((qg * jnp.exp(start)).astype(BF16))
        sub_decay = jnp.exp(b_sub)
        k_acc = [p * sub_decay for p in k_acc] + [k_i * jnp.exp(b_sub - b_i)]
        start = start + b_sub
    scores = jnp.concatenate(blocks, axis=0)
    k_end = jnp.concatenate([p.astype(BF16) for p in k_acc], axis=0)
    chunk_decay = jnp.exp(start)
    return scores, jnp.concatenate(q_dec, axis=0), k_end, chunk_decay


def _hgrn_apply(pend, v, gs, nw, st):
    scores, q_dec, k_end, chunk_decay = pend
    vb = v.astype(BF16)
    o = (jnp.dot(scores, vb, preferred_element_type=F32)
         + lax.dot_general(q_dec, st.astype(BF16), (((1,), (1,)), ((), ())),
                           preferred_element_type=F32))
    st = chunk_decay * st + lax.dot_general(vb, k_end, (((0,), (0,)), ((), ())),
                                            preferred_element_type=F32)
    return (_rms(o) * nw * gs).astype(BF16), st


def _hgrn_prompt_kernel(p_ref, nw_ref, o_ref, s_ref, *, n_chunks, n_heads):
    c_len = CHUNK_A
    w = n_heads * HA_D
    nw = nw_ref[...]
    row = lax.broadcasted_iota(jnp.int32, (c_len, HA_D), 0)
    heads = [slice(h * HA_D, (h + 1) * HA_D) for h in range(n_heads)]

    def part(k, sl, hs):
        return p_ref[sl, k * w + hs.start:k * w + hs.stop]

    def rows(c):
        return pl.ds(pl.multiple_of(c * c_len, c_len), c_len)

    def scores_of(c):
        sl = rows(c)
        return tuple(_hgrn_scores(part(0, sl, hs), part(1, sl, hs), row) for hs in heads)

    def apply_to(c, pends, sts):
        sl = rows(c)
        new = []
        for hs, pend, st in zip(heads, pends, sts):
            out, st = _hgrn_apply(pend, part(2, sl, hs), part(3, sl, hs), nw[:, hs], st)
            o_ref[sl, hs] = out
            new.append(st)
        return tuple(new)

    def step(c, carry):
        pends, sts = carry
        sts = apply_to(c - 1, pends, sts)
        return scores_of(c), sts

    zeros = tuple(jnp.zeros((HA_D, HA_D), F32) for _ in range(n_heads))
    pends, sts = lax.fori_loop(1, n_chunks, step, (scores_of(0), zeros))
    sts = apply_to(n_chunks - 1, pends, sts)
    for h in range(n_heads):
        s_ref[h] = sts[h].T


def _hgrn_prompt(proj, hgrn_norm, n_seq, t):
    m = proj.shape[0]
    nh = HGRN_HEADS_PER_STEP
    w = nh * HA_D
    return pl.pallas_call(
        functools.partial(_hgrn_prompt_kernel, n_chunks=t // CHUNK_A, n_heads=nh),
        grid=(n_seq, H_A // nh),
        in_specs=[pl.BlockSpec((t, 4 * w), lambda b, h: (b, h)),
                  pl.BlockSpec((1, w), lambda b, h: (0, h))],
        out_specs=[pl.BlockSpec((t, w), lambda b, h: (b, h)),
                   pl.BlockSpec((None, nh, HA_D, HA_D), lambda b, h: (b, h, 0, 0))],
        out_shape=[jax.ShapeDtypeStruct((m, D_A), BF16),
                   jax.ShapeDtypeStruct((n_seq, H_A, HA_D, HA_D), F32)],
        compiler_params=_params("parallel", "parallel"),
        name="hgrn_prompt",
    )(proj, hgrn_norm)


def _conv_chunk(tail, u, w, bias, n):
    full = jnp.concatenate([tail, u], axis=0)
    out = bias + w[CONV_W - 1:CONV_W, :] * u
    for i in range(CONV_W - 1):
        shifted = pltpu.roll(full, CONV_W - 1 - i, 0)[SUBLANE:SUBLANE + n, :]
        out = out + w[i:i + 1, :] * shifted
    return _silu(out)


def _ssd_prompt_kernel(z_ref, x_ref, b_ref, c_ref, dtr_ref, cpar_ref, dpar_ref,
                       o_ref, h_ref, row_scr, *, n_chunks, n_groups):
    n = CHUNK_B
    assert n == LANE
    tri = (lax.broadcasted_iota(jnp.int32, (n, n), 0)
           >= lax.broadcasted_iota(jnp.int32, (n, n), 1))
    lane_head = _idiv(lax.broadcasted_iota(jnp.int32, (1, GW), 1), B_P)
    row_head = _idiv(lax.broadcasted_iota(jnp.int32, (GW, 1), 0), B_P)
    nr = n_chunks * SUBLANE
    lane_nr = lax.broadcasted_iota(jnp.int32, (nr, n), 1)
    is_dt = (lax.broadcasted_iota(jnp.int32, (nr, n), 0) & (SUBLANE - 1)) < HPG
    is_dt8 = lax.broadcasted_iota(jnp.int32, (SUBLANE, n), 0) < HPG
    n_src = 2 * SUBLANE
    sel_shape = (4 * n_src, HPG * n + 2 * GW)
    sel_k = lax.broadcasted_iota(jnp.int32, sel_shape, 0)
    sel_l = lax.broadcasted_iota(jnp.int32, sel_shape, 1)
    want = jnp.where(sel_l < HPG * n, HPG + _idiv(sel_l, n),
                     2 * HPG + _idiv(sel_l - HPG * n, B_P))
    bcast_sel = jnp.where(((sel_k & (n_src - 1)) == want) & (sel_k < 3 * n_src), 1.0, 0.0).astype(BF16)

    groups = []
    for gi in range(n_groups):
        xs = slice(gi * GW, (gi + 1) * GW)
        bs = slice(gi * B_N, (gi + 1) * B_N)
        dpar = dpar_ref[gi]
        dt_all = _softplus(dtr_ref[gi] + dpar[:, 0:1])
        cum_all = _cumsum_lanes(dt_all * -jnp.exp(dpar[:, 1:2]), n, lane_nr)
        row_scr[gi] = jnp.where(is_dt, dt_all, cum_all).reshape(n_chunks, SUBLANE, n)
        bl = slice(n_groups * GW + gi * B_N, n_groups * GW + (gi + 1) * B_N)
        cl = slice(n_groups * (GW + B_N) + gi * B_N, n_groups * (GW + B_N) + (gi + 1) * B_N)
        taps, bias = slice(0, CONV_W), slice(CONV_W, CONV_W + 1)
        groups.append(dict(gi=gi, xs=xs, bs=bs, dskip=cpar_ref[6:7, xs], nw=cpar_ref[5:6, xs],
                           wx=cpar_ref[taps, xs], wb=cpar_ref[taps, bl], wc=cpar_ref[taps, cl],
                           bx=cpar_ref[bias, xs], bb=cpar_ref[bias, bl], bc=cpar_ref[bias, cl]))

    def rows(c):
        return pl.ds(pl.multiple_of(c * n, n), n)

    def scan_free(c, grp):
        xs, bs = grp["xs"], grp["bs"]
        sl = rows(c)
        if isinstance(c, int) and c == 0:
            tail = lambda ref, lanes: jnp.zeros((SUBLANE, lanes.stop - lanes.start), F32)
        else:
            prev = pl.ds(pl.multiple_of(c * n - SUBLANE, SUBLANE), SUBLANE)
            tail = lambda ref, lanes: ref[prev, lanes]
        xc = _conv_chunk(tail(x_ref, xs), x_ref[sl, xs], grp["wx"], grp["bx"], n)
        bcv = _conv_chunk(tail(b_ref, bs), b_ref[sl, bs], grp["wb"], grp["bb"], n)
        ccv = _conv_chunk(tail(c_ref, bs), c_ref[sl, bs], grp["wc"], grp["bc"], n)
        r8 = row_scr[grp["gi"], c]
        swapped = pltpu.roll(r8, HPG, 0)
        cum8 = jnp.where(is_dt8, swapped, r8)
        dt8 = jnp.where(is_dt8, r8, swapped)
        e8 = jnp.where(is_dt8, jnp.exp(cum8), jnp.exp(cum8[:, n - 1:n] - cum8) * dt8)
        cols = _transpose_bcast(jnp.concatenate([r8, e8], axis=0), bcast_sel)
        e_cum = cols[:, HPG * n:HPG * n + GW]
        w_end = cols[:, HPG * n + GW:]
        g = _dot_nt(ccv, bcv)
        xb = xc.astype(BF16)
        s_all, x_all = [], []
        for j in range(HPG):
            cc = cols[:, j * n:(j + 1) * n]
            seg = cc - r8[HPG + j:HPG + j + 1, :]
            s = g * jnp.exp(jnp.where(tri, seg, -jnp.inf)) * r8[j:j + 1, :]
            s_all.append(s.astype(BF16))
            x_all.append(jnp.where(lane_head == j, xb, jnp.zeros_like(xb)))
        y = grp["dskip"] * xc + jnp.dot(jnp.concatenate(s_all, axis=1), jnp.concatenate(x_all, axis=0),
                                        preferred_element_type=F32)
        return y, e_cum, ccv.astype(BF16), (xc * w_end).astype(BF16), bcv.astype(BF16), r8

    def apply_state(c, grp, pend, h):
        xs = grp["xs"]
        y, e_cum, ccb, xwb, bcb, r8 = pend
        sl = rows(c)
        ch = lax.dot_general(ccb, h.astype(BF16), (((1,), (1,)), ((), ())),
                             preferred_element_type=F32)
        yz = (y + ch * e_cum) * z_ref[sl, xs]
        o_ref[sl, xs] = (_rms(yz) * grp["nw"]).astype(o_ref.dtype)
        h_dec = jnp.zeros((GW, 1), F32)
        for j in range(HPG):
            h_dec = jnp.where(row_head == j, jnp.exp(r8[HPG + j:HPG + j + 1, n - 1:n]), h_dec)
        return h_dec * h + lax.dot_general(xwb, bcb, (((0,), (0,)), ((), ())),
                                           preferred_element_type=F32)

    def step(c, carry):
        pends, hs = carry
        hs = tuple(apply_state(c - 1, grp, pend, h) for grp, pend, h in zip(groups, pends, hs))
        return tuple(scan_free(c, grp) for grp in groups), hs

    init = (tuple(scan_free(0, grp) for grp in groups),
            tuple(jnp.zeros((GW, B_N), F32) for _ in groups))
    pends, hs = lax.fori_loop(1, n_chunks, step, init)
    for gi, (grp, pend, h) in enumerate(zip(groups, pends, hs)):
        h = apply_state(n_chunks - 1, grp, pend, h)
        h_ref[gi * HPG:(gi + 1) * HPG] = h.reshape(HPG, B_P, B_N)


def _ssd_channel_params(conv_w, conv_b, ssd_norm, d_skip, ng):
    zeros = jnp.zeros((1, 2 * B_G * B_N), F32)
    rows = jnp.concatenate([
        conv_w, conv_b,
        jnp.concatenate([ssd_norm, zeros], axis=1),
        jnp.concatenate([jnp.repeat(d_skip, B_P)[None], zeros], axis=1),
        jnp.zeros((SUBLANE - CONV_W - 3, CONV_DIM), F32)], axis=0)
    n_steps = B_G // ng
    part = lambda lo, width: rows[:, lo:lo + n_steps * width].reshape(SUBLANE, n_steps, width)
    slab = jnp.concatenate([part(0, ng * GW), part(D_B, ng * B_N), part(D_B + B_G * B_N, ng * B_N)], axis=2)
    return slab.transpose(1, 0, 2)


def _ssd_prompt(proj, dt_rows, cpar, dpar, n_seq, t):
    m = proj.shape[0]
    n_chunks = t // CHUNK_B
    nr = n_chunks * SUBLANE
    ng = SSD_GROUPS_PER_STEP
    xw, bw = ng * GW, ng * B_N
    wide = lambda off: pl.BlockSpec((t, xw), lambda b, g: (b, off // xw + g))
    narrow = lambda off: pl.BlockSpec((t, bw), lambda b, g: (b, off // bw + g))
    return pl.pallas_call(
        functools.partial(_ssd_prompt_kernel, n_chunks=n_chunks, n_groups=ng),
        grid=(n_seq, B_G // ng),
        in_specs=[wide(OFF_Z), wide(OFF_X), narrow(OFF_B), narrow(OFF_C),
                  pl.BlockSpec((ng, None, nr, CHUNK_B), lambda b, g: (g, b, 0, 0)),
                  pl.BlockSpec((None, SUBLANE, xw + 2 * bw), lambda b, g: (g, 0, 0)),
                  pl.BlockSpec((ng, nr, 2), lambda b, g: (g, 0, 0))],
        out_specs=[pl.BlockSpec((t, xw), lambda b, g: (b, g)),
                   pl.BlockSpec((None, ng * HPG, B_P, B_N), lambda b, g: (b, g, 0, 0))],
        out_shape=[jax.ShapeDtypeStruct((m, D_B), BF16),
                   jax.ShapeDtypeStruct((n_seq, H_B, B_P, B_N), F32)],
        scratch_shapes=[pltpu.VMEM((ng, n_chunks, SUBLANE, CHUNK_B), F32)],
        compiler_params=_params("parallel", "parallel"),
        name="ssd_prompt",
    )(proj, proj, proj, proj, dt_rows, cpar, dpar)


def _hgrn_step_rows(q_ref, f_ref, v_ref, g_ref, nw_ref, s_ref, o_ref, so_ref):
    nw = nw_ref[...]
    pad = jnp.zeros((HA_D - H_A, HA_D), F32)
    for b in range(q_ref.shape[0]):
        ft = jnp.concatenate([jnp.exp(f_ref[b]), pad], axis=0).T
        qt = jnp.concatenate([q_ref[b], pad], axis=0).T
        v = v_ref[b]
        rows = []
        for h in range(H_A):
            fc = ft[:, h:h + 1]
            s_new = fc * s_ref[b, h] + (1.0 - fc) * v[h:h + 1, :]
            so_ref[b, h] = s_new
            rows.append(jnp.sum(qt[:, h:h + 1] * s_new, axis=0, keepdims=True))
        o = jnp.concatenate(rows, axis=0)
        o_ref[b] = (_rms(o) * nw * g_ref[b]).astype(o_ref.dtype)


def _conv_step(buf_ref, b, u, w, bias):
    out = bias + w[CONV_W - 1] * u
    for i in range(CONV_W - 1):
        out = out + w[i] * buf_ref[b, i]
    return _silu(out)


N_SSD_STEP_LEAD = 6
N_SSD_STEP_IN = 15


def _ssd_step_rows(xt_ref, bcr_ref, zt_ref, dt_ref, cxt_ref, cbc_ref,
                   wxt_ref, wbc_ref, bxt_ref, bbc_ref, dtb_ref, al_ref, dsk_ref, nwt_ref, h_ref,
                   o_ref, ho_ref):
    wxt, wbc, bxt, bbc = wxt_ref[...], wbc_ref[...], bxt_ref[...], bbc_ref[...]
    a = -jnp.exp(al_ref[...])
    dtb, dsk, nwt = dtb_ref[...], dsk_ref[...], nwt_ref[...]
    lane = lax.broadcasted_iota(jnp.int32, (1, H_B), 1)
    lane_group = _idiv(lane, HPG)
    for b in range(xt_ref.shape[0]):
        xt = _conv_step(cxt_ref, b, xt_ref[b], wxt, bxt)
        bcv = _conv_step(cbc_ref, b, bcr_ref[b], wbc, bbc)
        dt = _softplus(dt_ref[b] + dtb)
        d_a = jnp.exp(dt * a)
        xdt = xt * dt
        yt = jnp.zeros((B_P, H_B), F32)
        for hh in range(H_B):
            g = hh // HPG
            h_new = d_a[:, hh:hh + 1] * h_ref[b, hh] + xdt[:, hh:hh + 1] * bcv[g:g + 1, :]
            ho_ref[b, hh] = h_new
            y_col = jnp.sum(h_new * bcv[B_G + g:B_G + g + 1, :], axis=1, keepdims=True)
            yt = jnp.where(lane == hh, y_col, yt)
        yz = (yt + dsk * xt) * zt_ref[b]
        col_sq = jnp.sum(yz * yz, axis=0, keepdims=True)
        ms = jnp.zeros((1, H_B), F32)
        for g in range(B_G):
            total = jnp.sum(col_sq[:, g * HPG:(g + 1) * HPG], axis=1, keepdims=True)
            ms = jnp.where(lane_group == g, total * (1.0 / GW), ms)
        o_ref[b] = (yz * lax.rsqrt(ms + EPS) * nwt).astype(o_ref.dtype)


def _ssd_step_specs(operands, state, bt, step_index):
    lead_ops, const_ops = operands[:N_SSD_STEP_LEAD], operands[N_SSD_STEP_LEAD:]
    lead = lambda a: pl.BlockSpec((bt,) + a.shape[1:], lambda *g: (step_index(*g),) + (0,) * (a.ndim - 1))
    full = lambda a: pl.BlockSpec(a.shape, lambda *g: (0,) * a.ndim)
    nb = state.shape[0]
    in_specs = [lead(a) for a in lead_ops] + [full(a) for a in const_ops] + [lead(state)]
    out_specs = [pl.BlockSpec((bt, B_P, H_B), lambda *g: (step_index(*g), 0, 0)), lead(state)]
    out_shape = [jax.ShapeDtypeStruct((nb, B_P, H_B), F32), jax.ShapeDtypeStruct(state.shape, F32)]
    return in_specs, out_specs, out_shape


def _outproj_kernel(oa_ref, ob_ref, wa_ref, wb_ref, x_ref, gt_ref, gp_ref, *rest):
    y_ref = rest[-3] if len(rest) > 1 else rest[0]
    mix = _dot(oa_ref[...], wa_ref[...]) + _dot(ob_ref[...], wb_ref[...])
    y_ref[...] = x_ref[...] + gt_ref[...] * (_rms(mix) * gp_ref[...])
    if len(rest) > 1:
        q_ref, f_ref, v_ref, g_ref, nw_ref, s_ref, _, o_ref, so_ref = rest
        _hgrn_step_rows(q_ref, f_ref, v_ref, g_ref, nw_ref, s_ref, o_ref, so_ref)


def _outproj(o_a, o_b, w_out, x, mod, per_row, rows_per_seq, g_post, tm, hgrn_step=None):
    m = x.shape[0]
    n_steps = m // tm
    row = lambda w: pl.BlockSpec((tm, w), lambda i, j: (i, 0))
    const = lambda shape, idx: pl.BlockSpec(shape, lambda i, j: idx, pipeline_mode=pl.Buffered(1))
    in_specs = [row(D_A), row(D_B),
                const((D_A, D_MODEL), (0, 0)), const((D_B, D_MODEL), (1, 0)),
                row(D_MODEL),
                _mod_spec(per_row, rows_per_seq, tm, 2),
                pl.BlockSpec((1, D_MODEL), lambda i, j: (0, 0))]
    out_specs = [row(D_MODEL)]
    out_shape = [jax.ShapeDtypeStruct((m, D_MODEL), F32)]
    args = [o_a, o_b, w_out, w_out, x, mod, g_post]
    if hgrn_step is not None:
        q, f, v, g, nw, state = hgrn_step
        nb = q.shape[0]
        assert nb % n_steps == 0
        bt = nb // n_steps
        rw = pl.BlockSpec((bt, H_A, HA_D), lambda i, j: (i, 0, 0))
        st = pl.BlockSpec((bt, H_A, HA_D, HA_D), lambda i, j: (i, 0, 0, 0))
        in_specs += [rw, rw, rw, rw, pl.BlockSpec((H_A, HA_D), lambda i, j: (0, 0)), st]
        out_specs += [rw, st]
        out_shape += [jax.ShapeDtypeStruct((nb, H_A, HA_D), BF16), jax.ShapeDtypeStruct(state.shape, F32)]
        args += [q, f, v, g, nw, state]
    out = pl.pallas_call(
        _outproj_kernel,
        grid=(n_steps, 1),
        in_specs=in_specs,
        out_specs=out_specs,
        out_shape=out_shape,
        compiler_params=_params("parallel", "arbitrary"),
        name="out_proj",
    )(*args)
    return out if hgrn_step is not None else out[0]


def _mlp_kernel(x_ref, sh_ref, sc_ref, gt_ref, gpre_ref, gpost_ref, wu_ref, wd_ref, *rest):
    ssd = len(rest) > 2
    if ssd:
        ssd_in, (y_ref, so_ref, sh_out_ref, h_scr) = rest[:N_SSD_STEP_IN], rest[N_SSD_STEP_IN:]
    else:
        y_ref, h_scr = rest
    f = pl.program_id(1)

    @pl.when(f == 0)
    def _():
        h = _rms(x_ref[...]) * gpre_ref[...] * (1.0 + sc_ref[...]) + sh_ref[...]
        h_scr[...] = h.astype(BF16)
        y_ref[...] = jnp.zeros_like(y_ref)

    if ssd:
        _ssd_step_rows(*ssd_in, so_ref, sh_out_ref)
    u = jnp.maximum(jnp.dot(h_scr[...], wu_ref[...].astype(BF16), preferred_element_type=F32), 0.0)
    y_ref[...] += jnp.dot((u * u).astype(BF16), wd_ref[...].astype(BF16), preferred_element_type=F32)

    @pl.when(f == pl.num_programs(1) - 1)
    def _():
        y_ref[...] = x_ref[...] + gt_ref[...] * (_rms(y_ref[...]) * gpost_ref[...])


def _mlp(x, mod, per_row, rows_per_seq, g_pre, g_post, w_up, w_down, tm, tf, ssd_step=None):
    m = x.shape[0]
    n_f = D_FF // tf
    n_steps = (m // tm) * n_f
    row = pl.BlockSpec((tm, D_MODEL), lambda i, j: (i, 0))
    vec = pl.BlockSpec((1, D_MODEL), lambda i, j: (0, 0))
    in_specs = [pl.BlockSpec((tm, D_MODEL), lambda i, j: (i, 0), pipeline_mode=pl.Buffered(1)),
                _mod_spec(per_row, rows_per_seq, tm, 3),
                _mod_spec(per_row, rows_per_seq, tm, 4),
                _mod_spec(per_row, rows_per_seq, tm, 5),
                vec, vec,
                pl.BlockSpec((D_MODEL, tf), lambda i, j: (0, j)),
                pl.BlockSpec((tf, D_MODEL), lambda i, j: (j, 0))]
    out_specs = [row]
    out_shape = [jax.ShapeDtypeStruct((m, D_MODEL), F32)]
    args = [x, mod, mod, mod, g_pre, g_post, w_up, w_down]
    if ssd_step is not None:
        out_specs = [pl.BlockSpec((tm, D_MODEL), lambda i, j: (i, 0), pipeline_mode=pl.Buffered(1))]
        operands, state = ssd_step
        nb = state.shape[0]
        assert nb % n_steps == 0
        s_in, s_out, s_shape = _ssd_step_specs(operands, state, nb // n_steps, lambda i, j: i * n_f + j)
        in_specs += s_in
        out_specs += s_out
        out_shape += s_shape
        args += list(operands) + [state]
    out = pl.pallas_call(
        _mlp_kernel,
        grid=(m // tm, n_f),
        in_specs=in_specs,
        out_specs=out_specs,
        out_shape=out_shape,
        scratch_shapes=[pltpu.VMEM((tm, D_MODEL), BF16)],
        compiler_params=_params("parallel", "arbitrary"),
        name="mlp",
    )(*args)
    return out if ssd_step is not None else out[0]


def kernel(x_prompt, x_sample, c_prompt, c_sample, state_hgrn, state_ssm, state_conv, w_ada, b_ada, norm_pre_mix, norm_post_mix, norm_pre_mlp, norm_post_mlp, w_in, hgrn_lb_logits, hgrn_norm, conv_w, conv_b, dt_bias, a_log, d_skip, ssd_norm, w_out, w_up, w_down):
    n_seq, t, _ = x_prompt.shape
    n_dec = x_sample.shape[0]
    assert x_sample.shape[1] == 1 and t % CHUNK_A == 0 and t % CHUNK_B == 0
    assert w_ada.shape[0] == 1, "one layer"
    l = 0
    tiles = _tiles(t, n_dec)

    w_in_t = w_in[l].T
    w_out_b = w_out[l].astype(BF16)
    g_pre_mix, g_post_mix = norm_pre_mix[l][None], norm_post_mix[l][None]
    g_pre_mlp, g_post_mlp = norm_pre_mlp[l][None], norm_post_mlp[l][None]
    cw, cb = conv_w[l], conv_b[l][None]
    hn = hgrn_norm[l][None]
    sn = ssd_norm[l][None]

    n_c = n_seq + n_dec
    pad = (-n_c) % SUBLANE
    c_all = jnp.concatenate([c_prompt, c_sample, jnp.zeros((pad, D_MODEL), F32)], axis=0)
    mod = _ada(c_all, w_ada[l], b_ada[l][None], tiles["ada_n"])
    mod_p = mod[:n_seq].reshape(n_seq, 1, N_MOD * D_MODEL)
    mod_s = mod[n_seq:n_c]

    xp = x_prompt.reshape(n_seq * t, D_MODEL)
    xs = x_sample.reshape(n_dec, D_MODEL)
    tm, tn = tiles["inproj"]
    proj_p, dt_p, proj_s, dt_s = _inproj(xp, mod_p, t, xs, mod_s, g_pre_mix, w_in_t, hgrn_lb_logits, tm, tn,
                                         functools.partial(_hgrn_interleave, tn=tn))
    o_a, s_hgrn_p = _hgrn_prompt(proj_p, hn, n_seq, t)

    n_chunks = t // CHUNK_B
    rep = SUBLANE // HPG
    dt_rows = dt_p.reshape(n_seq, n_chunks, CHUNK_B, B_G, 1, HPG).transpose(3, 0, 1, 4, 5, 2)
    dt_rows = jnp.broadcast_to(dt_rows, (B_G, n_seq, n_chunks, rep, HPG, CHUNK_B))
    dt_rows = dt_rows.reshape(B_G, n_seq, n_chunks * SUBLANE, CHUNK_B)
    prow = lambda p: jnp.tile(p.reshape(B_G, 1, HPG), (1, n_chunks * rep, 1)).reshape(B_G, n_chunks * SUBLANE, 1)
    cpar = _ssd_channel_params(cw, cb, sn, d_skip[l], SSD_GROUPS_PER_STEP)
    dpar = jnp.concatenate([prow(dt_bias[l]), prow(a_log[l])], axis=2)
    o_b, s_ssm_p = _ssd_prompt(proj_p, dt_rows, cpar, dpar, n_seq, t)
    heads = lambda off: proj_s[:, off:off + D_A].reshape(n_dec, H_A, HA_D)
    x1, o_a_s, s_hgrn_s = _outproj(
        o_a, o_b, w_out_b, xp, mod_p, False, t, g_post_mix, tiles["outproj_p"],
        hgrn_step=(heads(OFF_Q), heads(OFF_F), heads(OFF_I), heads(OFF_G), hn.reshape(H_A, HA_D),
                   state_hgrn[l]))
    conv_p = proj_p.reshape(n_seq, t, D_MAIN)[:, t - (CONV_W - 1):, OFF_X:]

    to_t = lambda a, h, d: a.reshape(a.shape[:-1] + (h, d)).swapaxes(-1, -2)
    xbc_s = proj_s[:, OFF_X:]
    cst = state_conv[l]
    xt = to_t(xbc_s[:, :D_B], H_B, B_P)
    zt = to_t(proj_s[:, OFF_Z:OFF_Z + D_B], H_B, B_P)
    bcr = xbc_s[:, D_B:].reshape(n_dec, 2 * B_G, B_N)
    cxt = to_t(cst[:, :, :D_B], H_B, B_P)
    cbc = cst[:, :, D_B:].reshape(n_dec, CONV_W - 1, 2 * B_G, B_N)
    wxt = to_t(cw[:, :D_B], H_B, B_P)
    wbc = cw[:, D_B:].reshape(CONV_W, 2 * B_G, B_N)
    bxt = to_t(cb[:, :D_B], H_B, B_P)[0]
    bbc = cb[0, D_B:].reshape(2 * B_G, B_N)
    ssd_operands = (xt, bcr, zt, dt_s[:, None, :], cxt, cbc,
                    wxt, wbc, bxt, bbc, dt_bias[l][None], a_log[l][None], d_skip[l][None],
                    to_t(sn, H_B, B_P)[0])
    assert len(ssd_operands) + 1 == N_SSD_STEP_IN
    y_p, o_b_st, s_ssm_s = _mlp(x1, mod_p, False, t, g_pre_mlp, g_post_mlp, w_up[l], w_down[l],
                                *tiles["mlp_p"], ssd_step=(ssd_operands, state_ssm[l]))
    o_b_s = o_b_st.swapaxes(1, 2).reshape(n_dec, D_B)

    x1_s = _outproj(o_a_s.reshape(n_dec, D_A), o_b_s, w_out_b, xs, mod_s, True, 1,
                    g_post_mix, tiles["outproj_s"])
    y_s = _mlp(x1_s, mod_s, True, 1, g_pre_mlp, g_post_mlp, w_up[l], w_down[l], *tiles["mlp_s"])
    conv_s = jnp.concatenate([cst[:, 1:], xbc_s[:, None, :]], axis=1)

    return (y_p.reshape(n_seq, t, D_MODEL), y_s.reshape(n_dec, 1, D_MODEL),
            s_hgrn_p[None], s_ssm_p[None], conv_p[None],
            s_hgrn_s[None], s_ssm_s[None], conv_s[None])
```

```python
import functools

import jax
import jax.numpy as jnp
from jax import lax
from jax.experimental import pallas as pl
from jax.experimental.pallas import tpu as pltpu

F32 = jnp.float32
BF16 = jnp.bfloat16

D_MODEL = 2048
D_A = 2048
HA_D = 128
H_A = D_A // HA_D
D_B = 2048
B_P = 64
H_B = D_B // B_P
B_G = 8
HPG = H_B // B_G
GW = HPG * B_P
B_N = 128
CONV_W = 4
CONV_DIM = D_B + 2 * B_G * B_N
D_MAIN = 4 * D_A + D_B + CONV_DIM
D_FF = 4 * D_MODEL
N_MOD = 6
EPS = 1e-6

OFF_Q, OFF_F, OFF_I, OFF_G = 0, D_A, 2 * D_A, 3 * D_A
OFF_Z = 4 * D_A
OFF_X = OFF_Z + D_B
OFF_B = OFF_X + D_B
OFF_C = OFF_B + B_G * B_N
REGION_SILU = (0, 3, 4)
REGION_FORGET = 1

LANE = 128
SUBLANE = 8
VMEM_LIMIT = 56 * 1024 * 1024

CHUNK_A = 128
SUB_A = 16
CHUNK_B = 128
HGRN_HEADS_PER_STEP = 4
SSD_GROUPS_PER_STEP = 2


def _tiles(t, n_dec):
    return dict(
        inproj=(_pick_tile(t, 2048), HGRN_HEADS_PER_STEP * HA_D),
        outproj_p=_pick_tile(t, 256), outproj_s=n_dec,
        mlp_p=(_pick_tile(t, 1024), 512), mlp_s=(n_dec, 1024),
        ada_n=1024)


def _pick_tile(m, target):
    t = min(m, target)
    while m % t:
        t //= 2
    return t


def _silu(x):
    hx = 0.5 * x
    return hx * jnp.tanh(hx) + hx


def _softplus(x):
    return jnp.maximum(x, 0.0) + jnp.log1p(jnp.exp(-jnp.abs(x)))


def _idiv(x, d):
    assert d & (d - 1) == 0
    return jnp.right_shift(x, d.bit_length() - 1)


def _rms(x):
    return x * lax.rsqrt(jnp.mean(x * x, axis=-1, keepdims=True) + EPS)


def _cumsum_rows(x, n, row):
    s = 1
    while s < n:
        x = x + jnp.where(row >= s, pltpu.roll(x, s, 0), 0.0)
        s *= 2
    return x


def _cumsum_lanes(x, n, lane):
    s = 1
    while s < n:
        x = x + jnp.where(lane >= s, pltpu.roll(x, s, 1), 0.0)
        s *= 2
    return x


def _dot(a, b):
    return jnp.dot(a.astype(BF16), b.astype(BF16), preferred_element_type=F32)


def _dot_nt(a, b):
    return lax.dot_general(a.astype(BF16), b.astype(BF16), (((1,), (1,)), ((), ())),
                           preferred_element_type=F32)


def _exact_terms(x):
    hi = x.astype(BF16).astype(F32)
    mid = (x - hi).astype(BF16).astype(F32)
    lo = x - hi - mid
    return jnp.concatenate([hi, mid, lo, jnp.zeros_like(hi)], axis=0).astype(BF16)


def _transpose_bcast(x, sel):
    return lax.dot_general(_exact_terms(x), sel, (((0,), (0,)), ((), ())),
                           preferred_element_type=F32)


def _params(*sem):
    return pltpu.CompilerParams(dimension_semantics=sem, vmem_limit_bytes=VMEM_LIMIT)


def _ada_kernel(c_ref, w_ref, b_ref, o_ref):
    o_ref[...] = _dot(_silu(c_ref[...]), w_ref[...]) + b_ref[...]


def _ada(c_all, w_ada, b_ada, tn):
    m = c_all.shape[0]
    n = w_ada.shape[1]
    return pl.pallas_call(
        _ada_kernel,
        grid=(n // tn,),
        in_specs=[pl.BlockSpec((m, D_MODEL), lambda j: (0, 0)),
                  pl.BlockSpec((D_MODEL, tn), lambda j: (0, j)),
                  pl.BlockSpec((1, tn), lambda j: (0, j))],
        out_specs=pl.BlockSpec((m, tn), lambda j: (0, j)),
        out_shape=jax.ShapeDtypeStruct((m, n), F32),
        compiler_params=_params("parallel"),
        name="ada_mod",
    )(c_all, w_ada, b_ada)


def _mod_spec(per_row, rows_per_seq, tm, col):
    if per_row:
        return pl.BlockSpec((tm, D_MODEL), lambda i, j: (i, col))
    tiles_per_seq = rows_per_seq // tm
    return pl.BlockSpec((None, 1, D_MODEL), lambda i, j: (i // tiles_per_seq, 0, col))


def _lower_bound(logits):
    e = jnp.exp(logits - jnp.max(logits, axis=0, keepdims=True))
    return e[0:1] / jnp.sum(e, axis=0, keepdims=True)


def _inproj_kernel(x_ref, sh_ref, sc_ref, g_ref, w_ref, wdt_ref, lbl_ref, xs_ref, shs_ref, scs_ref,
                   o_ref, odt_ref, os_ref, odts_ref, h_scr, *, tn):
    i, j = pl.program_id(0), pl.program_id(1)
    first_rows = i == 0
    tm = x_ref.shape[0]

    def normed(x, sh, sc):
        return (_rms(x) * g_ref[...] * (1.0 + sc) + sh).astype(BF16)

    @pl.when(j == 0)
    def _():
        hb = normed(x_ref[...], sh_ref[...], sc_ref[...])
        h_scr[0:tm, :] = hb
        odt_ref[...] = _dot_nt(hb, wdt_ref[...])

    @pl.when(jnp.logical_and(first_rows, j == 0))
    def _():
        hb = normed(xs_ref[...], shs_ref[...], scs_ref[...])
        h_scr[tm:, :] = hb
        odts_ref[...] = _dot_nt(hb, wdt_ref[...])

    region = (j * tn) // D_A
    is_silu = functools.reduce(jnp.logical_or, [region == r for r in REGION_SILU])

    def both_groups(act):
        @pl.when(first_rows)
        def _():
            res = act(_dot_nt(h_scr[...], w_ref[...]))
            o_ref[...] = res[0:tm]
            os_ref[...] = res[tm:]

        @pl.when(jnp.logical_not(first_rows))
        def _():
            o_ref[...] = act(_dot_nt(h_scr[0:tm, :], w_ref[...]))

    def log_forget(p):
        lb = _lower_bound(lbl_ref[...])
        half = 0.5 * (1.0 - lb)
        return jnp.log((lb + half) + half * jnp.tanh(0.5 * p))

    pl.when(is_silu)(lambda: both_groups(_silu))
    pl.when(region == REGION_FORGET)(lambda: both_groups(log_forget))
    pl.when(jnp.logical_not(jnp.logical_or(is_silu, region == REGION_FORGET)))(lambda: both_groups(lambda p: p))


def _inproj(x, mod, rows_per_seq, xs, mod_s, g_pre, w_t, lb_logits, tm, tn, out_tile):
    m, ns = x.shape[0], xs.shape[0]
    assert D_MAIN % tn == 0 and D_MAIN % H_B == 0 and D_A % tn == 0
    f_tiles = D_A // tn
    n_j = D_MAIN // tn
    once = lambda shape, idx: pl.BlockSpec(shape, lambda i, j: idx, pipeline_mode=pl.Buffered(1))
    return pl.pallas_call(
        functools.partial(_inproj_kernel, tn=tn),
        grid=(m // tm, n_j),
        in_specs=[pl.BlockSpec((tm, D_MODEL), lambda i, j: (i, 0), pipeline_mode=pl.Buffered(1)),
                  _mod_spec(False, rows_per_seq, tm, 0),
                  _mod_spec(False, rows_per_seq, tm, 1),
                  pl.BlockSpec((1, D_MODEL), lambda i, j: (0, 0)),
                  pl.BlockSpec((tn, D_MODEL), lambda i, j: (j, 0)),
                  pl.BlockSpec((H_B, D_MODEL), lambda i, j: (D_MAIN // H_B, 0)),
                  pl.BlockSpec((lb_logits.shape[0], tn),
                               lambda i, j: (0, jnp.clip(j - REGION_FORGET * f_tiles, 0, f_tiles - 1))),
                  once((ns, D_MODEL), (0, 0)), once((ns, D_MODEL), (0, 0)), once((ns, D_MODEL), (0, 1))],
        out_specs=[pl.BlockSpec((tm, tn), lambda i, j: (i, out_tile(j))),
                   pl.BlockSpec((tm, H_B), lambda i, j: (i, 0)),
                   pl.BlockSpec((ns, tn), lambda i, j: (0, jnp.where(i == 0, j, n_j - 1))),
                   pl.BlockSpec((ns, H_B), lambda i, j: (0, 0))],
        out_shape=[jax.ShapeDtypeStruct((m, D_MAIN), F32),
                   jax.ShapeDtypeStruct((m, H_B), F32),
                   jax.ShapeDtypeStruct((ns, D_MAIN), F32),
                   jax.ShapeDtypeStruct((ns, H_B), F32)],
        scratch_shapes=[pltpu.VMEM((tm + ns, D_MODEL), BF16)],
        compiler_params=_params("arbitrary", "arbitrary"),
        name="in_proj",
    )(x, mod, mod, g_pre, w_t, w_t, lb_logits, xs, mod_s, mod_s)


def _hgrn_interleave(j, tn):
    n_hg = D_A // tn
    return jnp.where(j < 4 * n_hg, 4 * (j % n_hg) + j // n_hg, j)


def _hgrn_scores(qs, lf, row):
    c_len = CHUNK_A
    n_sub = c_len // SUB_A
    k = 1.0 - jnp.exp(lf)
    b = _cumsum_rows(lf, SUB_A, row & (SUB_A - 1))
    k_acc = []
    blocks, q_dec = [], []
    start = jnp.zeros((1, HA_D), F32)
    for i in range(n_sub):
        lo = i * SUB_A
        hi = lo + SUB_A
        b_i = b[lo:hi, :]
        b_sub = b[hi - 1:hi, :]
        k_i = k[lo:hi, :]
        qg = qs[lo:hi, :] * jnp.exp(b_i)
        k_diag = k_i * jnp.exp(-b_i)
        rhs = jnp.concatenate([p.astype(BF16) for p in k_acc] + [k_diag.astype(BF16)], axis=0)
        a_i = lax.dot_general(qg.astype(BF16), rhs, (((1,), (1,)), ((), ())),
                              preferred_element_type=F32)
        causal = (lax.broadcasted_iota(jnp.int32, (SUB_A, hi), 1)
                  <= lax.broadcasted_iota(jnp.int32, (SUB_A, hi), 0) + lo)
        a_i = jnp.where(causal, a_i, 0.0)
        blocks.append(jnp.pad(a_i, ((0, 0), (0, c_len - hi))).astype(BF16))
        q_dec.append((qg * jnp.exp(start)).astype(BF16))
        sub_decay = jnp.exp(b_sub)
        k_acc = [p * sub_decay for p in k_acc] + [k_i * jnp.exp(b_sub - b_i)]
        start = start + b_sub
    scores = jnp.concatenate(blocks, axis=0)
    k_end = jnp.concatenate([p.astype(BF16) for p in k_acc], axis=0)
    chunk_decay = jnp.exp(start)
    return scores, jnp.concatenate(q_dec, axis=0), k_end, chunk_decay


def _hgrn_apply(pend, v, gs, nw, st):
    scores, q_dec, k_end, chunk_decay = pend
    vb = v.astype(BF16)
    o = (jnp.dot(scores, vb, preferred_element_type=F32)
         + lax.dot_general(q_dec, st.astype(BF16), (((1,), (1,)), ((), ())),
                           preferred_element_type=F32))
    st = chunk_decay * st + lax.dot_general(vb, k_end, (((0,), (0,)), ((), ())),
                                            preferred_element_type=F32)
    return (_rms(o) * nw * gs).astype(BF16), st


def _hgrn_prompt_kernel(p_ref, nw_ref, o_ref, s_ref, *, n_chunks, n_heads):
    c_len = CHUNK_A
    w = n_heads * HA_D
    nw = nw_ref[...]
    row = lax.broadcasted_iota(jnp.int32, (c_len, HA_D), 0)
    heads = [slice(h * HA_D, (h + 1) * HA_D) for h in range(n_heads)]

    def part(k, sl, hs):
        return p_ref[sl, k * w + hs.start:k * w + hs.stop]

    def rows(c):
        return pl.ds(pl.multiple_of(c * c_len, c_len), c_len)

    def scores_of(c):
        sl = rows(c)
        return tuple(_hgrn_scores(part(0, sl, hs), part(1, sl, hs), row) for hs in heads)

    def apply_to(c, pends, sts):
        sl = rows(c)
        new = []
        for hs, pend, st in zip(heads, pends, sts):
            out, st = _hgrn_apply(pend, part(2, sl, hs), part(3, sl, hs), nw[:, hs], st)
            o_ref[sl, hs] = out
            new.append(st)
        return tuple(new)

    def step(c, carry):
        pends, sts = carry
        sts = apply_to(c - 1, pends, sts)
        return scores_of(c), sts

    zeros = tuple(jnp.zeros((HA_D, HA_D), F32) for _ in range(n_heads))
    pends, sts = lax.fori_loop(1, n_chunks, step, (scores_of(0), zeros))
    sts = apply_to(n_chunks - 1, pends, sts)
    for h in range(n_heads):
        s_ref[h] = sts[h].T


def _hgrn_prompt(proj, hgrn_norm, n_seq, t):
    m = proj.shape[0]
    nh = HGRN_HEADS_PER_STEP
    w = nh * HA_D
    return pl.pallas_call(
        functools.partial(_hgrn_prompt_kernel, n_chunks=t // CHUNK_A, n_heads=nh),
        grid=(n_seq, H_A // nh),
        in_specs=[pl.BlockSpec((t, 4 * w), lambda b, h: (b, h)),
                  pl.BlockSpec((1, w), lambda b, h: (0, h))],
        out_specs=[pl.BlockSpec((t, w), lambda b, h: (b, h)),
                   pl.BlockSpec((None, nh, HA_D, HA_D), lambda b, h: (b, h, 0, 0))],
        out_shape=[jax.ShapeDtypeStruct((m, D_A), BF16),
                   jax.ShapeDtypeStruct((n_seq, H_A, HA_D, HA_D), F32)],
        compiler_params=_params("parallel", "parallel"),
        name="hgrn_prompt",
    )(proj, hgrn_norm)


def _conv_chunk(tail, u, w, bias, n):
    full = jnp.concatenate([tail, u], axis=0)
    out = bias + w[CONV_W - 1:CONV_W, :] * u
    for i in range(CONV_W - 1):
        shifted = pltpu.roll(full, CONV_W - 1 - i, 0)[SUBLANE:SUBLANE + n, :]
        out = out + w[i:i + 1, :] * shifted
    return _silu(out)


def _ssd_prompt_kernel(z_ref, x_ref, b_ref, c_ref, dtr_ref, cpar_ref, dpar_ref,
                       o_ref, h_ref, row_scr, *, n_chunks, n_groups):
    n = CHUNK_B
    assert n == LANE
    tri = (lax.broadcasted_iota(jnp.int32, (n, n), 0)
           >= lax.broadcasted_iota(jnp.int32, (n, n), 1))
    lane_head = _idiv(lax.broadcasted_iota(jnp.int32, (1, GW), 1), B_P)
    row_head = _idiv(lax.broadcasted_iota(jnp.int32, (GW, 1), 0), B_P)
    nr = n_chunks * SUBLANE
    lane_nr = lax.broadcasted_iota(jnp.int32, (nr, n), 1)
    is_dt = (lax.broadcasted_iota(jnp.int32, (nr, n), 0) & (SUBLANE - 1)) < HPG
    is_dt8 = lax.broadcasted_iota(jnp.int32, (SUBLANE, n), 0) < HPG
    n_src = 2 * SUBLANE
    sel_shape = (4 * n_src, HPG * n + 2 * GW)
    sel_k = lax.broadcasted_iota(jnp.int32, sel_shape, 0)
    sel_l = lax.broadcasted_iota(jnp.int32, sel_shape, 1)
    want = jnp.where(sel_l < HPG * n, HPG + _idiv(sel_l, n),
                     2 * HPG + _idiv(sel_l - HPG * n, B_P))
    bcast_sel = jnp.where(((sel_k & (n_src - 1)) == want) & (sel_k < 3 * n_src), 1.0, 0.0).astype(BF16)

    groups = []
    for gi in range(n_groups):
        xs = slice(gi * GW, (gi + 1) * GW)
        bs = slice(gi * B_N, (gi + 1) * B_N)
        dpar = dpar_ref[gi]
        dt_all = _softplus(dtr_ref[gi] + dpar[:, 0:1])
        cum_all = _cumsum_lanes(dt_all * -jnp.exp(dpar[:, 1:2]), n, lane_nr)
        row_scr[gi] = jnp.where(is_dt, dt_all, cum_all).reshape(n_chunks, SUBLANE, n)
        bl = slice(n_groups * GW + gi * B_N, n_groups * GW + (gi + 1) * B_N)
        cl = slice(n_groups * (GW + B_N) + gi * B_N, n_groups * (GW + B_N) + (gi + 1) * B_N)
        taps, bias = slice(0, CONV_W), slice(CONV_W, CONV_W + 1)
        groups.append(dict(gi=gi, xs=xs, bs=bs, dskip=cpar_ref[6:7, xs], nw=cpar_ref[5:6, xs],
                           wx=cpar_ref[taps, xs], wb=cpar_ref[taps, bl], wc=cpar_ref[taps, cl],
                           bx=cpar_ref[bias, xs], bb=cpar_ref[bias, bl], bc=cpar_ref[bias, cl]))

    def rows(c):
        return pl.ds(pl.multiple_of(c * n, n), n)

    def scan_free(c, grp):
        xs, bs = grp["xs"], grp["bs"]
        sl = rows(c)
        if isinstance(c, int) and c == 0:
            tail = lambda ref, lanes: jnp.zeros((SUBLANE, lanes.stop - lanes.start), F32)
        else:
            prev = pl.ds(pl.multiple_of(c * n - SUBLANE, SUBLANE), SUBLANE)
            tail = lambda ref, lanes: ref[prev, lanes]
        xc = _conv_chunk(tail(x_ref, xs), x_ref[sl, xs], grp["wx"], grp["bx"], n)
        bcv = _conv_chunk(tail(b_ref, bs), b_ref[sl, bs], grp["wb"], grp["bb"], n)
        ccv = _conv_chunk(tail(c_ref, bs), c_ref[sl, bs], grp["wc"], grp["bc"], n)
        r8 = row_scr[grp["gi"], c]
        swapped = pltpu.roll(r8, HPG, 0)
        cum8 = jnp.where(is_dt8, swapped, r8)
        dt8 = jnp.where(is_dt8, r8, swapped)
        e8 = jnp.where(is_dt8, jnp.exp(cum8), jnp.exp(cum8[:, n - 1:n] - cum8) * dt8)
        cols = _transpose_bcast(jnp.concatenate([r8, e8], axis=0), bcast_sel)
        e_cum = cols[:, HPG * n:HPG * n + GW]
        w_end = cols[:, HPG * n + GW:]
        g = _dot_nt(ccv, bcv)
        xb = xc.astype(BF16)
        s_all, x_all = [], []
        for j in range(HPG):
            cc = cols[:, j * n:(j + 1) * n]
            seg = cc - r8[HPG + j:HPG + j + 1, :]
            s = g * jnp.exp(jnp.where(tri, seg, -jnp.inf)) * r8[j:j + 1, :]
            s_all.append(s.astype(BF16))
            x_all.append(jnp.where(lane_head == j, xb, jnp.zeros_like(xb)))
        y = grp["dskip"] * xc + jnp.dot(jnp.concatenate(s_all, axis=1), jnp.concatenate(x_all, axis=0),
                                        preferred_element_type=F32)
        return y, e_cum, ccv.astype(BF16), (xc * w_end).astype(BF16), bcv.astype(BF16), r8

    def apply_state(c, grp, pend, h):
        xs = grp["xs"]
        y, e_cum, ccb, xwb, bcb, r8 = pend
        sl = rows(c)
        ch = lax.dot_general(ccb, h.astype(BF16), (((1,), (1,)), ((), ())),
                             preferred_element_type=F32)
        yz = (y + ch * e_cum) * z_ref[sl, xs]
        o_ref[sl, xs] = (_rms(yz) * grp["nw"]).astype(o_ref.dtype)
        h_dec = jnp.zeros((GW, 1), F32)
        for j in range(HPG):
            h_dec = jnp.where(row_head == j, jnp.exp(r8[HPG + j:HPG + j + 1, n - 1:n]), h_dec)
        return h_dec * h + lax.dot_general(xwb, bcb, (((0,), (0,)), ((), ())),
                                           preferred_element_type=F32)

    def step(c, carry):
        pends, hs = carry
        hs = tuple(apply_state(c - 1, grp, pend, h) for grp, pend, h in zip(groups, pends, hs))
        return tuple(scan_free(c, grp) for grp in groups), hs

    init = (tuple(scan_free(0, grp) for grp in groups),
            tuple(jnp.zeros((GW, B_N), F32) for _ in groups))
    pends, hs = lax.fori_loop(1, n_chunks, step, init)
    for gi, (grp, pend, h) in enumerate(zip(groups, pends, hs)):
        h = apply_state(n_chunks - 1, grp, pend, h)
        h_ref[gi * HPG:(gi + 1) * HPG] = h.reshape(HPG, B_P, B_N)


def _ssd_channel_params(conv_w, conv_b, ssd_norm, d_skip, ng):
    zeros = jnp.zeros((1, 2 * B_G * B_N), F32)
    rows = jnp.concatenate([
        conv_w, conv_b,
        jnp.concatenate([ssd_norm, zeros], axis=1),
        jnp.concatenate([jnp.repeat(d_skip, B_P)[None], zeros], axis=1),
        jnp.zeros((SUBLANE - CONV_W - 3, CONV_DIM), F32)], axis=0)
    n_steps = B_G // ng
    part = lambda lo, width: rows[:, lo:lo + n_steps * width].reshape(SUBLANE, n_steps, width)
    slab = jnp.concatenate([part(0, ng * GW), part(D_B, ng * B_N), part(D_B + B_G * B_N, ng * B_N)], axis=2)
    return slab.transpose(1, 0, 2)


def _ssd_prompt(proj, dt_rows, cpar, dpar, n_seq, t):
    m = proj.shape[0]
    n_chunks = t // CHUNK_B
    nr = n_chunks * SUBLANE
    ng = SSD_GROUPS_PER_STEP
    xw, bw = ng * GW, ng * B_N
    wide = lambda off: pl.BlockSpec((t, xw), lambda b, g: (b, off // xw + g))
    narrow = lambda off: pl.BlockSpec((t, bw), lambda b, g: (b, off // bw + g))
    return pl.pallas_call(
        functools.partial(_ssd_prompt_kernel, n_chunks=n_chunks, n_groups=ng),
        grid=(n_seq, B_G // ng),
        in_specs=[wide(OFF_Z), wide(OFF_X), narrow(OFF_B), narrow(OFF_C),
                  pl.BlockSpec((ng, None, nr, CHUNK_B), lambda b, g: (g, b, 0, 0)),
                  pl.BlockSpec((None, SUBLANE, xw + 2 * bw), lambda b, g: (g, 0, 0)),
                  pl.BlockSpec((ng, nr, 2), lambda b, g: (g, 0, 0))],
        out_specs=[pl.BlockSpec((t, xw), lambda b, g: (b, g)),
                   pl.BlockSpec((None, ng * HPG, B_P, B_N), lambda b, g: (b, g, 0, 0))],
        out_shape=[jax.ShapeDtypeStruct((m, D_B), BF16),
                   jax.ShapeDtypeStruct((n_seq, H_B, B_P, B_N), F32)],
        scratch_shapes=[pltpu.VMEM((ng, n_chunks, SUBLANE, CHUNK_B), F32)],
        compiler_params=_params("parallel", "parallel"),
        name="ssd_prompt",
    )(proj, proj, proj, proj, dt_rows, cpar, dpar)


def _hgrn_step_rows(q_ref, f_ref, v_ref, g_ref, nw_ref, s_ref, o_ref, so_ref):
    nw = nw_ref[...]
    pad = jnp.zeros((HA_D - H_A, HA_D), F32)
    for b in range(q_ref.shape[0]):
        ft = jnp.concatenate([jnp.exp(f_ref[b]), pad], axis=0).T
        qt = jnp.concatenate([q_ref[b], pad], axis=0).T
        v = v_ref[b]
        rows = []
        for h in range(H_A):
            fc = ft[:, h:h + 1]
            s_new = fc * s_ref[b, h] + (1.0 - fc) * v[h:h + 1, :]
            so_ref[b, h] = s_new
            rows.append(jnp.sum(qt[:, h:h + 1] * s_new, axis=0, keepdims=True))
        o = jnp.concatenate(rows, axis=0)
        o_ref[b] = (_rms(o) * nw * g_ref[b]).astype(o_ref.dtype)


def _conv_step(buf_ref, b, u, w, bias):
    out = bias + w[CONV_W - 1] * u
    for i in range(CONV_W - 1):
        out = out + w[i] * buf_ref[b, i]
    return _silu(out)


N_SSD_STEP_LEAD = 6
N_SSD_STEP_IN = 15


def _ssd_step_rows(xt_ref, bcr_ref, zt_ref, dt_ref, cxt_ref, cbc_ref,
                   wxt_ref, wbc_ref, bxt_ref, bbc_ref, dtb_ref, al_ref, dsk_ref, nwt_ref, h_ref,
                   o_ref, ho_ref):
    wxt, wbc, bxt, bbc = wxt_ref[...], wbc_ref[...], bxt_ref[...], bbc_ref[...]
    a = -jnp.exp(al_ref[...])
    dtb, dsk, nwt = dtb_ref[...], dsk_ref[...], nwt_ref[...]
    lane = lax.broadcasted_iota(jnp.int32, (1, H_B), 1)
    lane_group = _idiv(lane, HPG)
    for b in range(xt_ref.shape[0]):
        xt = _conv_step(cxt_ref, b, xt_ref[b], wxt, bxt)
        bcv = _conv_step(cbc_ref, b, bcr_ref[b], wbc, bbc)
        dt = _softplus(dt_ref[b] + dtb)
        d_a = jnp.exp(dt * a)
        xdt = xt * dt
        yt = jnp.zeros((B_P, H_B), F32)
        for hh in range(H_B):
            g = hh // HPG
            h_new = d_a[:, hh:hh + 1] * h_ref[b, hh] + xdt[:, hh:hh + 1] * bcv[g:g + 1, :]
            ho_ref[b, hh] = h_new
            y_col = jnp.sum(h_new * bcv[B_G + g:B_G + g + 1, :], axis=1, keepdims=True)
            yt = jnp.where(lane == hh, y_col, yt)
        yz = (yt + dsk * xt) * zt_ref[b]
        col_sq = jnp.sum(yz * yz, axis=0, keepdims=True)
        ms = jnp.zeros((1, H_B), F32)
        for g in range(B_G):
            total = jnp.sum(col_sq[:, g * HPG:(g + 1) * HPG], axis=1, keepdims=True)
            ms = jnp.where(lane_group == g, total * (1.0 / GW), ms)
        o_ref[b] = (yz * lax.rsqrt(ms + EPS) * nwt).astype(o_ref.dtype)


def _ssd_step_specs(operands, state, bt, step_index):
    lead_ops, const_ops = operands[:N_SSD_STEP_LEAD], operands[N_SSD_STEP_LEAD:]
    lead = lambda a: pl.BlockSpec((bt,) + a.shape[1:], lambda *g: (step_index(*g),) + (0,) * (a.ndim - 1))
    full = lambda a: pl.BlockSpec(a.shape, lambda *g: (0,) * a.ndim)
    nb = state.shape[0]
    in_specs = [lead(a) for a in lead_ops] + [full(a) for a in const_ops] + [lead(state)]
    out_specs = [pl.BlockSpec((bt, B_P, H_B), lambda *g: (step_index(*g), 0, 0)), lead(state)]
    out_shape = [jax.ShapeDtypeStruct((nb, B_P, H_B), F32), jax.ShapeDtypeStruct(state.shape, F32)]
    return in_specs, out_specs, out_shape


def _outproj_kernel(oa_ref, ob_ref, wa_ref, wb_ref, x_ref, gt_ref, gp_ref, *rest):
    y_ref = rest[-3] if len(rest) > 1 else rest[0]
    mix = _dot(oa_ref[...], wa_ref[...]) + _dot(ob_ref[...], wb_ref[...])
    y_ref[...] = x_ref[...] + gt_ref[...] * (_rms(mix) * gp_ref[...])
    if len(rest) > 1:
        q_ref, f_ref, v_ref, g_ref, nw_ref, s_ref, _, o_ref, so_ref = rest
        _hgrn_step_rows(q_ref, f_ref, v_ref, g_ref, nw_ref, s_ref, o_ref, so_ref)


def _outproj(o_a, o_b, w_out, x, mod, per_row, rows_per_seq, g_post, tm, hgrn_step=None):
    m = x.shape[0]
    n_steps = m // tm
    row = lambda w: pl.BlockSpec((tm, w), lambda i, j: (i, 0))
    const = lambda shape, idx: pl.BlockSpec(shape, lambda i, j: idx, pipeline_mode=pl.Buffered(1))
    in_specs = [row(D_A), row(D_B),
                const((D_A, D_MODEL), (0, 0)), const((D_B, D_MODEL), (1, 0)),
                row(D_MODEL),
                _mod_spec(per_row, rows_per_seq, tm, 2),
                pl.BlockSpec((1, D_MODEL), lambda i, j: (0, 0))]
    out_specs = [row(D_MODEL)]
    out_shape = [jax.ShapeDtypeStruct((m, D_MODEL), F32)]
    args = [o_a, o_b, w_out, w_out, x, mod, g_post]
    if hgrn_step is not None:
        q, f, v, g, nw, state = hgrn_step
        nb = q.shape[0]
        assert nb % n_steps == 0
        bt = nb // n_steps
        rw = pl.BlockSpec((bt, H_A, HA_D), lambda i, j: (i, 0, 0))
        st = pl.BlockSpec((bt, H_A, HA_D, HA_D), lambda i, j: (i, 0, 0, 0))
        in_specs += [rw, rw, rw, rw, pl.BlockSpec((H_A, HA_D), lambda i, j: (0, 0)), st]
        out_specs += [rw, st]
        out_shape += [jax.ShapeDtypeStruct((nb, H_A, HA_D), BF16), jax.ShapeDtypeStruct(state.shape, F32)]
        args += [q, f, v, g, nw, state]
    out = pl.pallas_call(
        _outproj_kernel,
        grid=(n_steps, 1),
        in_specs=in_specs,
        out_specs=out_specs,
        out_shape=out_shape,
        compiler_params=_params("parallel", "arbitrary"),
        name="out_proj",
    )(*args)
    return out if hgrn_step is not None else out[0]


def _mlp_kernel(x_ref, sh_ref, sc_ref, gt_ref, gpre_ref, gpost_ref, wu_ref, wd_ref, *rest):
    ssd = len(rest) > 2
    if ssd:
        ssd_in, (y_ref, so_ref, sh_out_ref, h_scr) = rest[:N_SSD_STEP_IN], rest[N_SSD_STEP_IN:]
    else:
        y_ref, h_scr = rest
    f = pl.program_id(1)

    @pl.when(f == 0)
    def _():
        h = _rms(x_ref[...]) * gpre_ref[...] * (1.0 + sc_ref[...]) + sh_ref[...]
        h_scr[...] = h.astype(BF16)
        y_ref[...] = jnp.zeros_like(y_ref)

    if ssd:
        _ssd_step_rows(*ssd_in, so_ref, sh_out_ref)
    u = jnp.maximum(jnp.dot(h_scr[...], wu_ref[...].astype(BF16), preferred_element_type=F32), 0.0)
    y_ref[...] += jnp.dot((u * u).astype(BF16), wd_ref[...].astype(BF16), preferred_element_type=F32)

    @pl.when(f == pl.num_programs(1) - 1)
    def _():
        y_ref[...] = x_ref[...] + gt_ref[...] * (_rms(y_ref[...]) * gpost_ref[...])


def _mlp(x, mod, per_row, rows_per_seq, g_pre, g_post, w_up, w_down, tm, tf, ssd_step=None):
    m = x.shape[0]
    n_f = D_FF // tf
    n_steps = (m // tm) * n_f
    row = pl.BlockSpec((tm, D_MODEL), lambda i, j: (i, 0))
    vec = pl.BlockSpec((1, D_MODEL), lambda i, j: (0, 0))
    in_specs = [pl.BlockSpec((tm, D_MODEL), lambda i, j: (i, 0), pipeline_mode=pl.Buffered(1)),
                _mod_spec(per_row, rows_per_seq, tm, 3),
                _mod_spec(per_row, rows_per_seq, tm, 4),
                _mod_spec(per_row, rows_per_seq, tm, 5),
                vec, vec,
                pl.BlockSpec((D_MODEL, tf), lambda i, j: (0, j)),
                pl.BlockSpec((tf, D_MODEL), lambda i, j: (j, 0))]
    out_specs = [row]
    out_shape = [jax.ShapeDtypeStruct((m, D_MODEL), F32)]
    args = [x, mod, mod, mod, g_pre, g_post, w_up, w_down]
    if ssd_step is not None:
        out_specs = [pl.BlockSpec((tm, D_MODEL), lambda i, j: (i, 0), pipeline_mode=pl.Buffered(1))]
        operands, state = ssd_step
        nb = state.shape[0]
        assert nb % n_steps == 0
        s_in, s_out, s_shape = _ssd_step_specs(operands, state, nb // n_steps, lambda i, j: i * n_f + j)
        in_specs += s_in
        out_specs += s_out
        out_shape += s_shape
        args += list(operands) + [state]
    out = pl.pallas_call(
        _mlp_kernel,
        grid=(m // tm, n_f),
        in_specs=in_specs,
        out_specs=out_specs,
        out_shape=out_shape,
        scratch_shapes=[pltpu.VMEM((tm, D_MODEL), BF16)],
        compiler_params=_params("parallel", "arbitrary"),
        name="mlp",
    )(*args)
    return out if ssd_step is not None else out[0]


def kernel(x_prompt, x_sample, c_prompt, c_sample, state_hgrn, state_ssm, state_conv, w_ada, b_ada, norm_pre_mix, norm_post_mix, norm_pre_mlp, norm_post_mlp, w_in, hgrn_lb_logits, hgrn_norm, conv_w, conv_b, dt_bias, a_log, d_skip, ssd_norm, w_out, w_up, w_down):
    n_seq, t, _ = x_prompt.shape
    n_dec = x_sample.shape[0]
    assert x_sample.shape[1] == 1 and t % CHUNK_A == 0 and t % CHUNK_B == 0
    assert w_ada.shape[0] == 1, "one layer"
    l = 0
    tiles = _tiles(t, n_dec)

    w_in_t = w_in[l].T
    w_out_b = w_out[l].astype(BF16)
    g_pre_mix, g_post_mix = norm_pre_mix[l][None], norm_post_mix[l][None]
    g_pre_mlp, g_post_mlp = norm_pre_mlp[l][None], norm_post_mlp[l][None]
    cw, cb = conv_w[l], conv_b[l][None]
    hn = hgrn_norm[l][None]
    sn = ssd_norm[l][None]

    n_c = n_seq + n_dec
    pad = (-n_c) % SUBLANE
    c_all = jnp.concatenate([c_sample, c_prompt, jnp.zeros((pad, D_MODEL), F32)], axis=0)
    mod = _ada(c_all, w_ada[l], b_ada[l][None], tiles["ada_n"])
    mod_p = mod[n_dec:n_c].reshape(n_seq, 1, N_MOD * D_MODEL)
    mod_s = mod

    xp = x_prompt.reshape(n_seq * t, D_MODEL)
    xs = x_sample.reshape(n_dec, D_MODEL)
    tm, tn = tiles["inproj"]
    proj_p, dt_p, proj_s, dt_s = _inproj(xp, mod_p, t, xs, mod_s, g_pre_mix, w_in_t, hgrn_lb_logits, tm, tn,
                                         functools.partial(_hgrn_interleave, tn=tn))
    o_a, s_hgrn_p = _hgrn_prompt(proj_p, hn, n_seq, t)

    n_chunks = t // CHUNK_B
    rep = SUBLANE // HPG
    dt_rows = dt_p.reshape(n_seq, n_chunks, CHUNK_B, B_G, 1, HPG).transpose(3, 0, 1, 4, 5, 2)
    dt_rows = jnp.broadcast_to(dt_rows, (B_G, n_seq, n_chunks, rep, HPG, CHUNK_B))
    dt_rows = dt_rows.reshape(B_G, n_seq, n_chunks * SUBLANE, CHUNK_B)
    prow = lambda p: jnp.tile(p.reshape(B_G, 1, HPG), (1, n_chunks * rep, 1)).reshape(B_G, n_chunks * SUBLANE, 1)
    cpar = _ssd_channel_params(cw, cb, sn, d_skip[l], SSD_GROUPS_PER_STEP)
    dpar = jnp.concatenate([prow(dt_bias[l]), prow(a_log[l])], axis=2)
    o_b, s_ssm_p = _ssd_prompt(proj_p, dt_rows, cpar, dpar, n_seq, t)
    heads = lambda off: proj_s[:, off:off + D_A].reshape(n_dec, H_A, HA_D)
    x1, o_a_s, s_hgrn_s = _outproj(
        o_a, o_b, w_out_b, xp, mod_p, False, t, g_post_mix, tiles["outproj_p"],
        hgrn_step=(heads(OFF_Q), heads(OFF_F), heads(OFF_I), heads(OFF_G), hn.reshape(H_A, HA_D),
                   state_hgrn[l]))
    conv_p = proj_p.reshape(n_seq, t, D_MAIN)[:, t - (CONV_W - 1):, OFF_X:]

    to_t = lambda a, h, d: a.reshape(a.shape[:-1] + (h, d)).swapaxes(-1, -2)
    xbc_s = proj_s[:, OFF_X:]
    cst = state_conv[l]
    xt = to_t(xbc_s[:, :D_B], H_B, B_P)
    zt = to_t(proj_s[:, OFF_Z:OFF_Z + D_B], H_B, B_P)
    bcr = xbc_s[:, D_B:].reshape(n_dec, 2 * B_G, B_N)
    cxt = to_t(cst[:, :, :D_B], H_B, B_P)
    cbc = cst[:, :, D_B:].reshape(n_dec, CONV_W - 1, 2 * B_G, B_N)
    wxt = to_t(cw[:, :D_B], H_B, B_P)
    wbc = cw[:, D_B:].reshape(CONV_W, 2 * B_G, B_N)
    bxt = to_t(cb[:, :D_B], H_B, B_P)[0]
    bbc = cb[0, D_B:].reshape(2 * B_G, B_N)
    ssd_operands = (xt, bcr, zt, dt_s[:, None, :], cxt, cbc,
                    wxt, wbc, bxt, bbc, dt_bias[l][None], a_log[l][None], d_skip[l][None],
                    to_t(sn, H_B, B_P)[0])
    assert len(ssd_operands) + 1 == N_SSD_STEP_IN
    y_p, o_b_st, s_ssm_s = _mlp(x1, mod_p, False, t, g_pre_mlp, g_post_mlp, w_up[l], w_down[l],
                                *tiles["mlp_p"], ssd_step=(ssd_operands, state_ssm[l]))
    o_b_s = o_b_st.swapaxes(1, 2).reshape(n_dec, D_B)

    x1_s = _outproj(o_a_s.reshape(n_dec, D_A), o_b_s, w_out_b, xs, mod_s, True, 1,
                    g_post_mix, tiles["outproj_s"])
    y_s = _mlp(x1_s, mod_s, True, 1, g_pre_mlp, g_post_mlp, w_up[l], w_down[l], *tiles["mlp_s"])
    conv_s = jnp.concatenate([cst[:, 1:], xbc_s[:, None, :]], axis=1)

    return (y_p.reshape(n_seq, t, D_MODEL), y_s.reshape(n_dec, 1, D_MODEL),
            s_hgrn_p[None], s_ssm_p[None], conv_p[None],
            s_hgrn_s[None], s_ssm_s[None], conv_s[None])
```

```python
import functools

import jax
import jax.numpy as jnp
from jax import lax
from jax.experimental import pallas as pl
from jax.experimental.pallas import tpu as pltpu

F32 = jnp.float32
BF16 = jnp.bfloat16

D_MODEL = 2048
D_A = 2048
HA_D = 128
H_A = D_A // HA_D
D_B = 2048
B_P = 64
H_B = D_B // B_P
B_G = 8
HPG = H_B // B_G
GW = HPG * B_P
B_N = 128
CONV_W = 4
CONV_DIM = D_B + 2 * B_G * B_N
D_MAIN = 4 * D_A + D_B + CONV_DIM
D_FF = 4 * D_MODEL
N_MOD = 6
EPS = 1e-6

OFF_Q, OFF_F, OFF_I, OFF_G = 0, D_A, 2 * D_A, 3 * D_A
OFF_Z = 4 * D_A
OFF_X = OFF_Z + D_B
OFF_B = OFF_X + D_B
OFF_C = OFF_B + B_G * B_N
REGION_SILU = (0, 3, 4)
REGION_FORGET = 1

LANE = 128
SUBLANE = 8
VMEM_LIMIT = 56 * 1024 * 1024

CHUNK_A = 128
SUB_A = 16
CHUNK_B = 128
HGRN_HEADS_PER_STEP = 4
SSD_GROUPS_PER_STEP = 2


def _tiles(t, n_dec):
    return dict(
        inproj=(_pick_tile(t, 2048), HGRN_HEADS_PER_STEP * HA_D),
        outproj_p=_pick_tile(t, 256), outproj_s=n_dec,
        mlp_p=(_pick_tile(t, 1024), 512), mlp_s=(n_dec, 1024),
        ada_n=1024)


def _pick_tile(m, target):
    t = min(m, target)
    while m % t:
        t //= 2
    return t


def _silu(x):
    hx = 0.5 * x
    return hx * jnp.tanh(hx) + hx


def _softplus(x):
    return jnp.maximum(x, 0.0) + jnp.log1p(jnp.exp(-jnp.abs(x)))


def _idiv(x, d):
    assert d & (d - 1) == 0
    return jnp.right_shift(x, d.bit_length() - 1)


def _rms(x):
    return x * lax.rsqrt(jnp.mean(x * x, axis=-1, keepdims=True) + EPS)


def _cumsum_rows(x, n, row):
    s = 1
    while s < n:
        x = x + jnp.where(row >= s, pltpu.roll(x, s, 0), 0.0)
        s *= 2
    return x


def _cumsum_lanes(x, n, lane):
    s = 1
    while s < n:
        x = x + jnp.where(lane >= s, pltpu.roll(x, s, 1), 0.0)
        s *= 2
    return x


def _dot(a, b):
    return jnp.dot(a.astype(BF16), b.astype(BF16), preferred_element_type=F32)


def _dot_nt(a, b):
    return lax.dot_general(a.astype(BF16), b.astype(BF16), (((1,), (1,)), ((), ())),
                           preferred_element_type=F32)


def _exact_terms(x):
    hi = x.astype(BF16).astype(F32)
    mid = (x - hi).astype(BF16).astype(F32)
    lo = x - hi - mid
    return jnp.concatenate([hi, mid, lo, jnp.zeros_like(hi)], axis=0).astype(BF16)


def _transpose_bcast(x, sel):
    return lax.dot_general(_exact_terms(x), sel, (((0,), (0,)), ((), ())),
                           preferred_element_type=F32)


def _params(*sem):
    return pltpu.CompilerParams(dimension_semantics=sem, vmem_limit_bytes=VMEM_LIMIT)


def _ada_kernel(c_ref, w_ref, b_ref, o_ref):
    o_ref[...] = _dot(_silu(c_ref[...]), w_ref[...]) + b_ref[...]


def _ada(c_all, w_ada, b_ada, tn):
    m = c_all.shape[0]
    n = w_ada.shape[1]
    return pl.pallas_call(
        _ada_kernel,
        grid=(n // tn,),
        in_specs=[pl.BlockSpec((m, D_MODEL), lambda j: (0, 0)),
                  pl.BlockSpec((D_MODEL, tn), lambda j: (0, j)),
                  pl.BlockSpec((1, tn), lambda j: (0, j))],
        out_specs=pl.BlockSpec((m, tn), lambda j: (0, j)),
        out_shape=jax.ShapeDtypeStruct((m, n), F32),
        compiler_params=_params("parallel"),
        name="ada_mod",
    )(c_all, w_ada, b_ada)


def _mod_spec(per_row, rows_per_seq, tm, col):
    if per_row:
        return pl.BlockSpec((tm, D_MODEL), lambda i, j: (i, col))
    tiles_per_seq = rows_per_seq // tm
    return pl.BlockSpec((None, 1, D_MODEL), lambda i, j: (i // tiles_per_seq, 0, col))


def _lower_bound(logits):
    e = jnp.exp(logits - jnp.max(logits, axis=0, keepdims=True))
    return e[0:1] / jnp.sum(e, axis=0, keepdims=True)


def _inproj_kernel(x_ref, sh_ref, sc_ref, g_ref, w_ref, wdt_ref, lbl_ref, xs_ref, shs_ref, scs_ref,
                   o_ref, odt_ref, os_ref, odts_ref, h_scr, *, tn):
    i, j = pl.program_id(0), pl.program_id(1)
    first_rows = i == 0
    tm = x_ref.shape[0]

    def normed(x, sh, sc):
        return (_rms(x) * g_ref[...] * (1.0 + sc) + sh).astype(BF16)

    @pl.when(j == 0)
    def _():
        hb = normed(x_ref[...], sh_ref[...], sc_ref[...])
        h_scr[0:tm, :] = hb
        odt_ref[...] = _dot_nt(hb, wdt_ref[...])

    @pl.when(jnp.logical_and(first_rows, j == 0))
    def _():
        hb = normed(xs_ref[...], shs_ref[...], scs_ref[...])
        h_scr[tm:, :] = hb
        odts_ref[...] = _dot_nt(hb, wdt_ref[...])

    region = (j * tn) // D_A
    is_silu = functools.reduce(jnp.logical_or, [region == r for r in REGION_SILU])

    def both_groups(act):
        @pl.when(first_rows)
        def _():
            res = act(_dot_nt(h_scr[...], w_ref[...]))
            o_ref[...] = res[0:tm]
            os_ref[...] = res[tm:]

        @pl.when(jnp.logical_not(first_rows))
        def _():
            o_ref[...] = act(_dot_nt(h_scr[0:tm, :], w_ref[...]))

    def log_forget(p):
        lb = _lower_bound(lbl_ref[...])
        half = 0.5 * (1.0 - lb)
        return jnp.log((lb + half) + half * jnp.tanh(0.5 * p))

    pl.when(is_silu)(lambda: both_groups(_silu))
    pl.when(region == REGION_FORGET)(lambda: both_groups(log_forget))
    pl.when(jnp.logical_not(jnp.logical_or(is_silu, region == REGION_FORGET)))(lambda: both_groups(lambda p: p))


def _inproj(x, mod, rows_per_seq, xs, mod_s, g_pre, w_t, lb_logits, tm, tn, out_tile):
    m, ns = x.shape[0], xs.shape[0]
    assert D_MAIN % tn == 0 and D_MAIN % H_B == 0 and D_A % tn == 0
    f_tiles = D_A // tn
    n_j = D_MAIN // tn
    once = lambda shape, idx: pl.BlockSpec(shape, lambda i, j: idx, pipeline_mode=pl.Buffered(1))
    return pl.pallas_call(
        functools.partial(_inproj_kernel, tn=tn),
        grid=(m // tm, n_j),
        in_specs=[pl.BlockSpec((tm, D_MODEL), lambda i, j: (i, 0), pipeline_mode=pl.Buffered(1)),
                  _mod_spec(False, rows_per_seq, tm, 0),
                  _mod_spec(False, rows_per_seq, tm, 1),
                  pl.BlockSpec((1, D_MODEL), lambda i, j: (0, 0)),
                  pl.BlockSpec((tn, D_MODEL), lambda i, j: (j, 0)),
                  pl.BlockSpec((H_B, D_MODEL), lambda i, j: (D_MAIN // H_B, 0)),
                  pl.BlockSpec((lb_logits.shape[0], tn),
                               lambda i, j: (0, jnp.clip(j - REGION_FORGET * f_tiles, 0, f_tiles - 1))),
                  once((ns, D_MODEL), (0, 0)), once((ns, D_MODEL), (0, 0)), once((ns, D_MODEL), (0, 1))],
        out_specs=[pl.BlockSpec((tm, tn), lambda i, j: (i, out_tile(j))),
                   pl.BlockSpec((tm, H_B), lambda i, j: (i, 0)),
                   pl.BlockSpec((ns, tn), lambda i, j: (0, jnp.where(i == 0, j, n_j - 1))),
                   pl.BlockSpec((ns, H_B), lambda i, j: (0, 0))],
        out_shape=[jax.ShapeDtypeStruct((m, D_MAIN), F32),
                   jax.ShapeDtypeStruct((m, H_B), F32),
                   jax.ShapeDtypeStruct((ns, D_MAIN), F32),
                   jax.ShapeDtypeStruct((ns, H_B), F32)],
        scratch_shapes=[pltpu.VMEM((tm + ns, D_MODEL), BF16)],
        compiler_params=_params("arbitrary", "arbitrary"),
        name="in_proj",
    )(x, mod, mod, g_pre, w_t, w_t, lb_logits, xs, mod_s, mod_s)


def _hgrn_interleave(j, tn):
    n_hg = D_A // tn
    return jnp.where(j < 4 * n_hg, 4 * (j % n_hg) + j // n_hg, j)


def _hgrn_scores(qs, lf, row):
    c_len = CHUNK_A
    n_sub = c_len // SUB_A
    k = 1.0 - jnp.exp(lf)
    b = _cumsum_rows(lf, SUB_A, row & (SUB_A - 1))
    k_acc = []
    blocks, q_dec = [], []
    start = jnp.zeros((1, HA_D), F32)
    for i in range(n_sub):
        lo = i * SUB_A
        hi = lo + SUB_A
        b_i = b[lo:hi, :]
        b_sub = b[hi - 1:hi, :]
        k_i = k[lo:hi, :]
        qg = qs[lo:hi, :] * jnp.exp(b_i)
        k_diag = k_i * jnp.exp(-b_i)
        rhs = jnp.concatenate([p.astype(BF16) for p in k_acc] + [k_diag.astype(BF16)], axis=0)
        a_i = lax.dot_general(qg.astype(BF16), rhs, (((1,), (1,)), ((), ())),
                              preferred_element_type=F32)
        causal = (lax.broadcasted_iota(jnp.int32, (SUB_A, hi), 1)
                  <= lax.broadcasted_iota(jnp.int32, (SUB_A, hi), 0) + lo)
        a_i = jnp.where(causal, a_i, 0.0)
        blocks.append(jnp.pad(a_i, ((0, 0), (0, c_len - hi))).astype(BF16))
        q_dec.append((qg * jnp.exp(start)).astype(BF16))
        sub_decay = jnp.exp(b_sub)
        k_acc = [p * sub_decay for p in k_acc] + [k_i * jnp.exp(b_sub - b_i)]
        start = start + b_sub
    scores = jnp.concatenate(blocks, axis=0)
    k_end = jnp.concatenate([p.astype(BF16) for p in k_acc], axis=0)
    chunk_decay = jnp.exp(start)
    return scores, jnp.concatenate(q_dec, axis=0), k_end, chunk_decay


def _hgrn_apply(pend, v, gs, nw, st):
    scores, q_dec, k_end, chunk_decay = pend
    vb = v.astype(BF16)
    o = (jnp.dot(scores, vb, preferred_element_type=F32)
         + lax.dot_general(q_dec, st.astype(BF16), (((1,), (1,)), ((), ())),
                           preferred_element_type=F32))
    st = chunk_decay * st + lax.dot_general(vb, k_end, (((0,), (0,)), ((), ())),
                                            preferred_element_type=F32)
    return (_rms(o) * nw * gs).astype(BF16), st


def _hgrn_prompt_kernel(p_ref, nw_ref, o_ref, s_ref, *, n_chunks, n_heads):
    c_len = CHUNK_A
    w = n_heads * HA_D
    nw = nw_ref[...]
    row = lax.broadcasted_iota(jnp.int32, (c_len, HA_D), 0)
    heads = [slice(h * HA_D, (h + 1) * HA_D) for h in range(n_heads)]

    def part(k, sl, hs):
        return p_ref[sl, k * w + hs.start:k * w + hs.stop]

    def rows(c):
        return pl.ds(pl.multiple_of(c * c_len, c_len), c_len)

    def scores_of(c):
        sl = rows(c)
        return tuple(_hgrn_scores(part(0, sl, hs), part(1, sl, hs), row) for hs in heads)

    def apply_to(c, pends, sts):
        sl = rows(c)
        new = []
        for hs, pend, st in zip(heads, pends, sts):
            out, st = _hgrn_apply(pend, part(2, sl, hs), part(3, sl, hs), nw[:, hs], st)
            o_ref[sl, hs] = out
            new.append(st)
        return tuple(new)

    def step(c, carry):
        pends, sts = carry
        sts = apply_to(c - 1, pends, sts)
        return scores_of(c), sts

    zeros = tuple(jnp.zeros((HA_D, HA_D), F32) for _ in range(n_heads))
    pends, sts = lax.fori_loop(1, n_chunks, step, (scores_of(0), zeros))
    sts = apply_to(n_chunks - 1, pends, sts)
    for h in range(n_heads):
        s_ref[h] = sts[h].T


def _hgrn_prompt(proj, hgrn_norm, n_seq, t):
    m = proj.shape[0]
    nh = HGRN_HEADS_PER_STEP
    w = nh * HA_D
    return pl.pallas_call(
        functools.partial(_hgrn_prompt_kernel, n_chunks=t // CHUNK_A, n_heads=nh),
        grid=(n_seq, H_A // nh),
        in_specs=[pl.BlockSpec((t, 4 * w), lambda b, h: (b, h)),
                  pl.BlockSpec((1, w), lambda b, h: (0, h))],
        out_specs=[pl.BlockSpec((t, w), lambda b, h: (b, h)),
                   pl.BlockSpec((None, nh, HA_D, HA_D), lambda b, h: (b, h, 0, 0))],
        out_shape=[jax.ShapeDtypeStruct((m, D_A), BF16),
                   jax.ShapeDtypeStruct((n_seq, H_A, HA_D, HA_D), F32)],
        compiler_params=_params("parallel", "parallel"),
        name="hgrn_prompt",
    )(proj, hgrn_norm)


def _conv_chunk(tail, u, w, bias, n):
    full = jnp.concatenate([tail, u], axis=0)
    out = bias + w[CONV_W - 1:CONV_W, :] * u
    for i in range(CONV_W - 1):
        shifted = pltpu.roll(full, CONV_W - 1 - i, 0)[SUBLANE:SUBLANE + n, :]
        out = out + w[i:i + 1, :] * shifted
    return _silu(out)


def _ssd_prompt_kernel(z_ref, x_ref, b_ref, c_ref, dtr_ref, cpar_ref, dpar_ref,
                       o_ref, h_ref, row_scr, *, n_chunks, n_groups):
    n = CHUNK_B
    assert n == LANE
    tri = (lax.broadcasted_iota(jnp.int32, (n, n), 0)
           >= lax.broadcasted_iota(jnp.int32, (n, n), 1))
    lane_head = _idiv(lax.broadcasted_iota(jnp.int32, (1, GW), 1), B_P)
    row_head = _idiv(lax.broadcasted_iota(jnp.int32, (GW, 1), 0), B_P)
    nr = n_chunks * SUBLANE
    lane_nr = lax.broadcasted_iota(jnp.int32, (nr, n), 1)
    is_dt = (lax.broadcasted_iota(jnp.int32, (nr, n), 0) & (SUBLANE - 1)) < HPG
    is_dt8 = lax.broadcasted_iota(jnp.int32, (SUBLANE, n), 0) < HPG
    n_src = 2 * SUBLANE
    sel_shape = (4 * n_src, HPG * n + 2 * GW)
    sel_k = lax.broadcasted_iota(jnp.int32, sel_shape, 0)
    sel_l = lax.broadcasted_iota(jnp.int32, sel_shape, 1)
    want = jnp.where(sel_l < HPG * n, HPG + _idiv(sel_l, n),
                     2 * HPG + _idiv(sel_l - HPG * n, B_P))
    bcast_sel = jnp.where(((sel_k & (n_src - 1)) == want) & (sel_k < 3 * n_src), 1.0, 0.0).astype(BF16)

    groups = []
    for gi in range(n_groups):
        xs = slice(gi * GW, (gi + 1) * GW)
        bs = slice(gi * B_N, (gi + 1) * B_N)
        dpar = dpar_ref[gi]
        dt_all = _softplus(dtr_ref[gi] + dpar[:, 0:1])
        cum_all = _cumsum_lanes(dt_all * -jnp.exp(dpar[:, 1:2]), n, lane_nr)
        row_scr[gi] = jnp.where(is_dt, dt_all, cum_all).reshape(n_chunks, SUBLANE, n)
        bl = slice(n_groups * GW + gi * B_N, n_groups * GW + (gi + 1) * B_N)
        cl = slice(n_groups * (GW + B_N) + gi * B_N, n_groups * (GW + B_N) + (gi + 1) * B_N)
        taps, bias = slice(0, CONV_W), slice(CONV_W, CONV_W + 1)
        groups.append(dict(gi=gi, xs=xs, bs=bs, dskip=cpar_ref[6:7, xs], nw=cpar_ref[5:6, xs],
                           wx=cpar_ref[taps, xs], wb=cpar_ref[taps, bl], wc=cpar_ref[taps, cl],
                           bx=cpar_ref[bias, xs], bb=cpar_ref[bias, bl], bc=cpar_ref[bias, cl]))

    def rows(c):
        return pl.ds(pl.multiple_of(c * n, n), n)

    def scan_free(c, grp):
        xs, bs = grp["xs"], grp["bs"]
        sl = rows(c)
        if isinstance(c, int) and c == 0:
            tail = lambda ref, lanes: jnp.zeros((SUBLANE, lanes.stop - lanes.start), F32)
        else:
            prev = pl.ds(pl.multiple_of(c * n - SUBLANE, SUBLANE), SUBLANE)
            tail = lambda ref, lanes: ref[prev, lanes]
        xc = _conv_chunk(tail(x_ref, xs), x_ref[sl, xs], grp["wx"], grp["bx"], n)
        bcv = _conv_chunk(tail(b_ref, bs), b_ref[sl, bs], grp["wb"], grp["bb"], n)
        ccv = _conv_chunk(tail(c_ref, bs), c_ref[sl, bs], grp["wc"], grp["bc"], n)
        r8 = row_scr[grp["gi"], c]
        swapped = pltpu.roll(r8, HPG, 0)
        cum8 = jnp.where(is_dt8, swapped, r8)
        dt8 = jnp.where(is_dt8, r8, swapped)
        e8 = jnp.where(is_dt8, jnp.exp(cum8), jnp.exp(cum8[:, n - 1:n] - cum8) * dt8)
        cols = _transpose_bcast(jnp.concatenate([r8, e8], axis=0), bcast_sel)
        e_cum = cols[:, HPG * n:HPG * n + GW]
        w_end = cols[:, HPG * n + GW:]
        g = _dot_nt(ccv, bcv)
        xb = xc.astype(BF16)
        s_all, x_all = [], []
        for j in range(HPG):
            cc = cols[:, j * n:(j + 1) * n]
            seg = cc - r8[HPG + j:HPG + j + 1, :]
            s = g * jnp.exp(jnp.where(tri, seg, -jnp.inf)) * r8[j:j + 1, :]
            s_all.append(s.astype(BF16))
            x_all.append(jnp.where(lane_head == j, xb, jnp.zeros_like(xb)))
        y = grp["dskip"] * xc + jnp.dot(jnp.concatenate(s_all, axis=1), jnp.concatenate(x_all, axis=0),
                                        preferred_element_type=F32)
        return y, e_cum, ccv.astype(BF16), (xc * w_end).astype(BF16), bcv.astype(BF16), r8

    def apply_state(c, grp, pend, h):
        xs = grp["xs"]
        y, e_cum, ccb, xwb, bcb, r8 = pend
        sl = rows(c)
        ch = lax.dot_general(ccb, h.astype(BF16), (((1,), (1,)), ((), ())),
                             preferred_element_type=F32)
        yz = (y + ch * e_cum) * z_ref[sl, xs]
        o_ref[sl, xs] = (_rms(yz) * grp["nw"]).astype(o_ref.dtype)
        h_dec = jnp.zeros((GW, 1), F32)
        for j in range(HPG):
            h_dec = jnp.where(row_head == j, jnp.exp(r8[HPG + j:HPG + j + 1, n - 1:n]), h_dec)
        return h_dec * h + lax.dot_general(xwb, bcb, (((0,), (0,)), ((), ())),
                                           preferred_element_type=F32)

    def step(c, carry):
        pends, hs = carry
        hs = tuple(apply_state(c - 1, grp, pend, h) for grp, pend, h in zip(groups, pends, hs))
        return tuple(scan_free(c, grp) for grp in groups), hs

    init = (tuple(scan_free(0, grp) for grp in groups),
            tuple(jnp.zeros((GW, B_N), F32) for _ in groups))
    pends, hs = lax.fori_loop(1, n_chunks, step, init)
    for gi, (grp, pend, h) in enumerate(zip(groups, pends, hs)):
        h = apply_state(n_chunks - 1, grp, pend, h)
        h_ref[gi * HPG:(gi + 1) * HPG] = h.reshape(HPG, B_P, B_N)


def _ssd_channel_params(conv_w, conv_b, ssd_norm, d_skip, ng):
    zeros = jnp.zeros((1, 2 * B_G * B_N), F32)
    rows = jnp.concatenate([
        conv_w, conv_b,
        jnp.concatenate([ssd_norm, zeros], axis=1),
        jnp.concatenate([jnp.repeat(d_skip, B_P)[None], zeros], axis=1),
        jnp.zeros((SUBLANE - CONV_W - 3, CONV_DIM), F32)], axis=0)
    n_steps = B_G // ng
    part = lambda lo, width: rows[:, lo:lo + n_steps * width].reshape(SUBLANE, n_steps, width)
    slab = jnp.concatenate([part(0, ng * GW), part(D_B, ng * B_N), part(D_B + B_G * B_N, ng * B_N)], axis=2)
    return slab.transpose(1, 0, 2)


def _ssd_prompt(proj, dt_rows, cpar, dpar, n_seq, t):
    m = proj.shape[0]
    n_chunks = t // CHUNK_B
    nr = n_chunks * SUBLANE
    ng = SSD_GROUPS_PER_STEP
    xw, bw = ng * GW, ng * B_N
    wide = lambda off: pl.BlockSpec((t, xw), lambda b, g: (b, off // xw + g))
    narrow = lambda off: pl.BlockSpec((t, bw), lambda b, g: (b, off // bw + g))
    return pl.pallas_call(
        functools.partial(_ssd_prompt_kernel, n_chunks=n_chunks, n_groups=ng),
        grid=(n_seq, B_G // ng),
        in_specs=[wide(OFF_Z), wide(OFF_X), narrow(OFF_B), narrow(OFF_C),
                  pl.BlockSpec((ng, None, nr, CHUNK_B), lambda b, g: (g, b, 0, 0)),
                  pl.BlockSpec((None, SUBLANE, xw + 2 * bw), lambda b, g: (g, 0, 0)),
                  pl.BlockSpec((ng, nr, 2), lambda b, g: (g, 0, 0))],
        out_specs=[pl.BlockSpec((t, xw), lambda b, g: (b, g)),
                   pl.BlockSpec((None, ng * HPG, B_P, B_N), lambda b, g: (b, g, 0, 0))],
        out_shape=[jax.ShapeDtypeStruct((m, D_B), BF16),
                   jax.ShapeDtypeStruct((n_seq, H_B, B_P, B_N), F32)],
        scratch_shapes=[pltpu.VMEM((ng, n_chunks, SUBLANE, CHUNK_B), F32)],
        compiler_params=_params("parallel", "parallel"),
        name="ssd_prompt",
    )(proj, proj, proj, proj, dt_rows, cpar, dpar)


def _hgrn_step_rows(q_ref, f_ref, v_ref, g_ref, nw_ref, s_ref, o_ref, so_ref):
    nw = nw_ref[...]
    pad = jnp.zeros((HA_D - H_A, HA_D), F32)
    for b in range(q_ref.shape[0]):
        ft = jnp.concatenate([jnp.exp(f_ref[b]), pad], axis=0).T
        qt = jnp.concatenate([q_ref[b], pad], axis=0).T
        v = v_ref[b]
        rows = []
        for h in range(H_A):
            fc = ft[:, h:h + 1]
            s_new = fc * s_ref[b, h] + (1.0 - fc) * v[h:h + 1, :]
            so_ref[b, h] = s_new
            rows.append(jnp.sum(qt[:, h:h + 1] * s_new, axis=0, keepdims=True))
        o = jnp.concatenate(rows, axis=0)
        o_ref[b] = (_rms(o) * nw * g_ref[b]).astype(o_ref.dtype)


def _conv_step(buf_ref, b, u, w, bias):
    out = bias + w[CONV_W - 1] * u
    for i in range(CONV_W - 1):
        out = out + w[i] * buf_ref[b, i]
    return _silu(out)


N_SSD_STEP_LEAD = 6
N_SSD_STEP_IN = 15


def _ssd_step_rows(xt_ref, bcr_ref, zt_ref, dt_ref, cxt_ref, cbc_ref,
                   wxt_ref, wbc_ref, bxt_ref, bbc_ref, dtb_ref, al_ref, dsk_ref, nwt_ref, h_ref,
                   o_ref, ho_ref):
    wxt, wbc, bxt, bbc = wxt_ref[...], wbc_ref[...], bxt_ref[...], bbc_ref[...]
    a = -jnp.exp(al_ref[...])
    dtb, dsk, nwt = dtb_ref[...], dsk_ref[...], nwt_ref[...]
    lane = lax.broadcasted_iota(jnp.int32, (1, H_B), 1)
    lane_group = _idiv(lane, HPG)
    for b in range(xt_ref.shape[0]):
        xt = _conv_step(cxt_ref, b, xt_ref[b], wxt, bxt)
        bcv = _conv_step(cbc_ref, b, bcr_ref[b], wbc, bbc)
        dt = _softplus(dt_ref[b] + dtb)
        d_a = jnp.exp(dt * a)
        xdt = xt * dt
        yt = jnp.zeros((B_P, H_B), F32)
        for hh in range(H_B):
            g = hh // HPG
            h_new = d_a[:, hh:hh + 1] * h_ref[b, hh] + xdt[:, hh:hh + 1] * bcv[g:g + 1, :]
            ho_ref[b, hh] = h_new
            y_col = jnp.sum(h_new * bcv[B_G + g:B_G + g + 1, :], axis=1, keepdims=True)
            yt = jnp.where(lane == hh, y_col, yt)
        yz = (yt + dsk * xt) * zt_ref[b]
        col_sq = jnp.sum(yz * yz, axis=0, keepdims=True)
        ms = jnp.zeros((1, H_B), F32)
        for g in range(B_G):
            total = jnp.sum(col_sq[:, g * HPG:(g + 1) * HPG], axis=1, keepdims=True)
            ms = jnp.where(lane_group == g, total * (1.0 / GW), ms)
        o_ref[b] = (yz * lax.rsqrt(ms + EPS) * nwt).astype(o_ref.dtype)


def _ssd_step_specs(operands, state, bt, step_index):
    lead_ops, const_ops = operands[:N_SSD_STEP_LEAD], operands[N_SSD_STEP_LEAD:]
    lead = lambda a: pl.BlockSpec((bt,) + a.shape[1:], lambda *g: (step_index(*g),) + (0,) * (a.ndim - 1))
    full = lambda a: pl.BlockSpec(a.shape, lambda *g: (0,) * a.ndim)
    nb = state.shape[0]
    in_specs = [lead(a) for a in lead_ops] + [full(a) for a in const_ops] + [lead(state)]
    out_specs = [pl.BlockSpec((bt, B_P, H_B), lambda *g: (step_index(*g), 0, 0)), lead(state)]
    out_shape = [jax.ShapeDtypeStruct((nb, B_P, H_B), F32), jax.ShapeDtypeStruct(state.shape, F32)]
    return in_specs, out_specs, out_shape


def _outproj_kernel(oa_ref, ob_ref, wa_ref, wb_ref, x_ref, gt_ref, gp_ref, *rest):
    y_ref = rest[-3] if len(rest) > 1 else rest[0]
    mix = _dot(oa_ref[...], wa_ref[...]) + _dot(ob_ref[...], wb_ref[...])
    y_ref[...] = x_ref[...] + gt_ref[...] * (_rms(mix) * gp_ref[...])
    if len(rest) > 1:
        q_ref, f_ref, v_ref, g_ref, nw_ref, s_ref, _, o_ref, so_ref = rest
        _hgrn_step_rows(q_ref, f_ref, v_ref, g_ref, nw_ref, s_ref, o_ref, so_ref)


def _outproj(o_a, o_b, w_out, x, mod, per_row, rows_per_seq, g_post, tm, hgrn_step=None):
    m = x.shape[0]
    n_steps = m // tm
    row = lambda w: pl.BlockSpec((tm, w), lambda i, j: (i, 0))
    const = lambda shape, idx: pl.BlockSpec(shape, lambda i, j: idx, pipeline_mode=pl.Buffered(1))
    in_specs = [row(D_A), row(D_B),
                const((D_A, D_MODEL), (0, 0)), const((D_B, D_MODEL), (1, 0)),
                row(D_MODEL),
                _mod_spec(per_row, rows_per_seq, tm, 2),
                pl.BlockSpec((1, D_MODEL), lambda i, j: (0, 0))]
    out_specs = [row(D_MODEL)]
    out_shape = [jax.ShapeDtypeStruct((m, D_MODEL), F32)]
    args = [o_a, o_b, w_out, w_out, x, mod, g_post]
    if hgrn_step is not None:
        q, f, v, g, nw, state = hgrn_step
        nb = q.shape[0]
        assert nb % n_steps == 0
        bt = nb // n_steps
        rw = pl.BlockSpec((bt, H_A, HA_D), lambda i, j: (i, 0, 0))
        st = pl.BlockSpec((bt, H_A, HA_D, HA_D), lambda i, j: (i, 0, 0, 0))
        in_specs += [rw, rw, rw, rw, pl.BlockSpec((H_A, HA_D), lambda i, j: (0, 0)), st]
        out_specs += [rw, st]
        out_shape += [jax.ShapeDtypeStruct((nb, H_A, HA_D), BF16), jax.ShapeDtypeStruct(state.shape, F32)]
        args += [q, f, v, g, nw, state]
    out = pl.pallas_call(
        _outproj_kernel,
        grid=(n_steps, 1),
        in_specs=in_specs,
        out_specs=out_specs,
        out_shape=out_shape,
        compiler_params=_params("parallel", "arbitrary"),
        name="out_proj",
    )(*args)
    return out if hgrn_step is not None else out[0]


def _mlp_kernel(x_ref, sh_ref, sc_ref, gt_ref, gpre_ref, gpost_ref, wu_ref, wd_ref, *rest):
    ssd = len(rest) > 2
    if ssd:
        ssd_in, (y_ref, so_ref, sh_out_ref, h_scr) = rest[:N_SSD_STEP_IN], rest[N_SSD_STEP_IN:]
    else:
        y_ref, h_scr = rest
    f = pl.program_id(1)

    @pl.when(f == 0)
    def _():
        h = _rms(x_ref[...]) * gpre_ref[...] * (1.0 + sc_ref[...]) + sh_ref[...]
        h_scr[...] = h.astype(BF16)

    def tile_product():
        if ssd:
            _ssd_step_rows(*ssd_in, so_ref, sh_out_ref)
        u = jnp.maximum(jnp.dot(h_scr[...], wu_ref[...].astype(BF16), preferred_element_type=F32), 0.0)
        return jnp.dot((u * u).astype(BF16), wd_ref[...].astype(BF16), preferred_element_type=F32)

    @pl.when(f == 0)
    def _():
        y_ref[...] = tile_product()

    @pl.when(f > 0)
    def _():
        y_ref[...] += tile_product()

    @pl.when(f == pl.num_programs(1) - 1)
    def _():
        y_ref[...] = x_ref[...] + gt_ref[...] * (_rms(y_ref[...]) * gpost_ref[...])


def _mlp(x, mod, per_row, rows_per_seq, g_pre, g_post, w_up, w_down, tm, tf, ssd_step=None):
    m = x.shape[0]
    n_f = D_FF // tf
    n_steps = (m // tm) * n_f
    row = pl.BlockSpec((tm, D_MODEL), lambda i, j: (i, 0))
    vec = pl.BlockSpec((1, D_MODEL), lambda i, j: (0, 0))
    in_specs = [pl.BlockSpec((tm, D_MODEL), lambda i, j: (i, 0), pipeline_mode=pl.Buffered(1)),
                _mod_spec(per_row, rows_per_seq, tm, 3),
                _mod_spec(per_row, rows_per_seq, tm, 4),
                _mod_spec(per_row, rows_per_seq, tm, 5),
                vec, vec,
                pl.BlockSpec((D_MODEL, tf), lambda i, j: (0, j)),
                pl.BlockSpec((tf, D_MODEL), lambda i, j: (j, 0))]
    out_specs = [row]
    out_shape = [jax.ShapeDtypeStruct((m, D_MODEL), F32)]
    args = [x, mod, mod, mod, g_pre, g_post, w_up, w_down]
    if ssd_step is not None:
        out_specs = [pl.BlockSpec((tm, D_MODEL), lambda i, j: (i, 0), pipeline_mode=pl.Buffered(1))]
        operands, state = ssd_step
        nb = state.shape[0]
        assert nb % n_steps == 0
        s_in, s_out, s_shape = _ssd_step_specs(operands, state, nb // n_steps, lambda i, j: i * n_f + j)
        in_specs += s_in
        out_specs += s_out
        out_shape += s_shape
        args += list(operands) + [state]
    out = pl.pallas_call(
        _mlp_kernel,
        grid=(m // tm, n_f),
        in_specs=in_specs,
        out_specs=out_specs,
        out_shape=out_shape,
        scratch_shapes=[pltpu.VMEM((tm, D_MODEL), BF16)],
        compiler_params=_params("parallel", "arbitrary"),
        name="mlp",
    )(*args)
    return out if ssd_step is not None else out[0]


def kernel(x_prompt, x_sample, c_prompt, c_sample, state_hgrn, state_ssm, state_conv, w_ada, b_ada, norm_pre_mix, norm_post_mix, norm_pre_mlp, norm_post_mlp, w_in, hgrn_lb_logits, hgrn_norm, conv_w, conv_b, dt_bias, a_log, d_skip, ssd_norm, w_out, w_up, w_down):
    n_seq, t, _ = x_prompt.shape
    n_dec = x_sample.shape[0]
    assert x_sample.shape[1] == 1 and t % CHUNK_A == 0 and t % CHUNK_B == 0
    assert w_ada.shape[0] == 1, "one layer"
    l = 0
    tiles = _tiles(t, n_dec)

    w_in_t = w_in[l].T
    w_out_b = w_out[l].astype(BF16)
    g_pre_mix, g_post_mix = norm_pre_mix[l][None], norm_post_mix[l][None]
    g_pre_mlp, g_post_mlp = norm_pre_mlp[l][None], norm_post_mlp[l][None]
    cw, cb = conv_w[l], conv_b[l][None]
    hn = hgrn_norm[l][None]
    sn = ssd_norm[l][None]

    n_c = n_seq + n_dec
    pad = (-n_c) % SUBLANE
    c_all = jnp.concatenate([c_sample, c_prompt, jnp.zeros((pad, D_MODEL), F32)], axis=0)
    mod = _ada(c_all, w_ada[l], b_ada[l][None], tiles["ada_n"])
    mod_p = mod[n_dec:n_c].reshape(n_seq, 1, N_MOD * D_MODEL)
    mod_s = mod

    xp = x_prompt.reshape(n_seq * t, D_MODEL)
    xs = x_sample.reshape(n_dec, D_MODEL)
    tm, tn = tiles["inproj"]
    proj_p, dt_p, proj_s, dt_s = _inproj(xp, mod_p, t, xs, mod_s, g_pre_mix, w_in_t, hgrn_lb_logits, tm, tn,
                                         functools.partial(_hgrn_interleave, tn=tn))
    o_a, s_hgrn_p = _hgrn_prompt(proj_p, hn, n_seq, t)

    n_chunks = t // CHUNK_B
    rep = SUBLANE // HPG
    dt_rows = dt_p.reshape(n_seq, n_chunks, CHUNK_B, B_G, 1, HPG).transpose(3, 0, 1, 4, 5, 2)
    dt_rows = jnp.broadcast_to(dt_rows, (B_G, n_seq, n_chunks, rep, HPG, CHUNK_B))
    dt_rows = dt_rows.reshape(B_G, n_seq, n_chunks * SUBLANE, CHUNK_B)
    prow = lambda p: jnp.tile(p.reshape(B_G, 1, HPG), (1, n_chunks * rep, 1)).reshape(B_G, n_chunks * SUBLANE, 1)
    cpar = _ssd_channel_params(cw, cb, sn, d_skip[l], SSD_GROUPS_PER_STEP)
    dpar = jnp.concatenate([prow(dt_bias[l]), prow(a_log[l])], axis=2)
    o_b, s_ssm_p = _ssd_prompt(proj_p, dt_rows, cpar, dpar, n_seq, t)
    heads = lambda off: proj_s[:, off:off + D_A].reshape(n_dec, H_A, HA_D)
    x1, o_a_s, s_hgrn_s = _outproj(
        o_a, o_b, w_out_b, xp, mod_p, False, t, g_post_mix, tiles["outproj_p"],
        hgrn_step=(heads(OFF_Q), heads(OFF_F), heads(OFF_I), heads(OFF_G), hn.reshape(H_A, HA_D),
                   state_hgrn[l]))
    conv_p = proj_p.reshape(n_seq, t, D_MAIN)[:, t - (CONV_W - 1):, OFF_X:]

    to_t = lambda a, h, d: a.reshape(a.shape[:-1] + (h, d)).swapaxes(-1, -2)
    xbc_s = proj_s[:, OFF_X:]
    cst = state_conv[l]
    xt = to_t(xbc_s[:, :D_B], H_B, B_P)
    zt = to_t(proj_s[:, OFF_Z:OFF_Z + D_B], H_B, B_P)
    bcr = xbc_s[:, D_B:].reshape(n_dec, 2 * B_G, B_N)
    cxt = to_t(cst[:, :, :D_B], H_B, B_P)
    cbc = cst[:, :, D_B:].reshape(n_dec, CONV_W - 1, 2 * B_G, B_N)
    wxt = to_t(cw[:, :D_B], H_B, B_P)
    wbc = cw[:, D_B:].reshape(CONV_W, 2 * B_G, B_N)
    bxt = to_t(cb[:, :D_B], H_B, B_P)[0]
    bbc = cb[0, D_B:].reshape(2 * B_G, B_N)
    ssd_operands = (xt, bcr, zt, dt_s[:, None, :], cxt, cbc,
                    wxt, wbc, bxt, bbc, dt_bias[l][None], a_log[l][None], d_skip[l][None],
                    to_t(sn, H_B, B_P)[0])
    assert len(ssd_operands) + 1 == N_SSD_STEP_IN
    y_p, o_b_st, s_ssm_s = _mlp(x1, mod_p, False, t, g_pre_mlp, g_post_mlp, w_up[l], w_down[l],
                                *tiles["mlp_p"], ssd_step=(ssd_operands, state_ssm[l]))
    o_b_s = o_b_st.swapaxes(1, 2).reshape(n_dec, D_B)

    x1_s = _outproj(o_a_s.reshape(n_dec, D_A), o_b_s, w_out_b, xs, mod_s, True, 1,
                    g_post_mix, tiles["outproj_s"])
    y_s = _mlp(x1_s, mod_s, True, 1, g_pre_mlp, g_post_mlp, w_up[l], w_down[l], *tiles["mlp_s"])
    conv_s = jnp.concatenate([cst[:, 1:], xbc_s[:, None, :]], axis=1)

    return (y_p.reshape(n_seq, t, D_MODEL), y_s.reshape(n_dec, 1, D_MODEL),
            s_hgrn_p[None], s_ssm_p[None], conv_p[None],
            s_hgrn_s[None], s_ssm_s[None], conv_s[None])
```

```python
import functools

import jax
import jax.numpy as jnp
from jax import lax
from jax.experimental import pallas as pl
from jax.experimental.pallas import tpu as pltpu

F32 = jnp.float32
BF16 = jnp.bfloat16

D_MODEL = 2048
D_A = 2048
HA_D = 128
H_A = D_A // HA_D
D_B = 2048
B_P = 64
H_B = D_B // B_P
B_G = 8
HPG = H_B // B_G
GW = HPG * B_P
B_N = 128
CONV_W = 4
CONV_DIM = D_B + 2 * B_G * B_N
D_MAIN = 4 * D_A + D_B + CONV_DIM
D_FF = 4 * D_MODEL
N_MOD = 6
EPS = 1e-6

OFF_Q, OFF_F, OFF_I, OFF_G = 0, D_A, 2 * D_A, 3 * D_A
OFF_Z = 4 * D_A
OFF_X = OFF_Z + D_B
OFF_B = OFF_X + D_B
OFF_C = OFF_B + B_G * B_N
REGION_SILU = (0, 3, 4)
REGION_FORGET = 1

LANE = 128
SUBLANE = 8
VMEM_LIMIT = 56 * 1024 * 1024

CHUNK_A = 128
SUB_A = 16
CHUNK_B = 128
HGRN_HEADS_PER_STEP = 4
SSD_GROUPS_PER_STEP = 2


def _tiles(t, n_dec):
    return dict(
        inproj=(_pick_tile(t, 2048), HGRN_HEADS_PER_STEP * HA_D),
        outproj_p=_pick_tile(t, 256), outproj_s=n_dec,
        mlp_p=(_pick_tile(t, 1024), 512), mlp_s=(n_dec, 1024),
        ada_n=2048)


def _pick_tile(m, target):
    t = min(m, target)
    while m % t:
        t //= 2
    return t


def _silu(x):
    hx = 0.5 * x
    return hx * jnp.tanh(hx) + hx


def _softplus(x):
    return jnp.maximum(x, 0.0) + jnp.log1p(jnp.exp(-jnp.abs(x)))


def _idiv(x, d):
    assert d & (d - 1) == 0
    return jnp.right_shift(x, d.bit_length() - 1)


def _rms(x):
    return x * lax.rsqrt(jnp.mean(x * x, axis=-1, keepdims=True) + EPS)


def _cumsum_rows(x, n, row):
    s = 1
    while s < n:
        x = x + jnp.where(row >= s, pltpu.roll(x, s, 0), 0.0)
        s *= 2
    return x


def _cumsum_lanes(x, n, lane):
    s = 1
    while s < n:
        x = x + jnp.where(lane >= s, pltpu.roll(x, s, 1), 0.0)
        s *= 2
    return x


def _dot(a, b):
    return jnp.dot(a.astype(BF16), b.astype(BF16), preferred_element_type=F32)


def _dot_nt(a, b):
    return lax.dot_general(a.astype(BF16), b.astype(BF16), (((1,), (1,)), ((), ())),
                           preferred_element_type=F32)


def _exact_terms(x):
    hi = x.astype(BF16).astype(F32)
    mid = (x - hi).astype(BF16).astype(F32)
    lo = x - hi - mid
    return jnp.concatenate([hi, mid, lo, jnp.zeros_like(hi)], axis=0).astype(BF16)


def _transpose_bcast(x, sel):
    return lax.dot_general(_exact_terms(x), sel, (((0,), (0,)), ((), ())),
                           preferred_element_type=F32)


def _params(*sem):
    return pltpu.CompilerParams(dimension_semantics=sem, vmem_limit_bytes=VMEM_LIMIT)


def _ada_kernel(c_ref, w_ref, b_ref, o_ref):
    o_ref[...] = _dot(_silu(c_ref[...]), w_ref[...]) + b_ref[...]


def _ada(c_all, w_ada, b_ada, tn):
    m = c_all.shape[0]
    n = w_ada.shape[1]
    return pl.pallas_call(
        _ada_kernel,
        grid=(n // tn,),
        in_specs=[pl.BlockSpec((m, D_MODEL), lambda j: (0, 0)),
                  pl.BlockSpec((D_MODEL, tn), lambda j: (0, j)),
                  pl.BlockSpec((1, tn), lambda j: (0, j))],
        out_specs=pl.BlockSpec((m, tn), lambda j: (0, j)),
        out_shape=jax.ShapeDtypeStruct((m, n), F32),
        compiler_params=_params("parallel"),
        name="ada_mod",
    )(c_all, w_ada, b_ada)


def _mod_spec(per_row, rows_per_seq, tm, col):
    if per_row:
        return pl.BlockSpec((tm, D_MODEL), lambda i, j: (i, col))
    tiles_per_seq = rows_per_seq // tm
    return pl.BlockSpec((None, 1, D_MODEL), lambda i, j: (i // tiles_per_seq, 0, col))


def _lower_bound(logits):
    e = jnp.exp(logits - jnp.max(logits, axis=0, keepdims=True))
    return e[0:1] / jnp.sum(e, axis=0, keepdims=True)


def _inproj_kernel(x_ref, sh_ref, sc_ref, g_ref, w_ref, wdt_ref, lbl_ref, xs_ref, shs_ref, scs_ref,
                   o_ref, odt_ref, os_ref, odts_ref, h_scr, *, tn):
    i, j = pl.program_id(0), pl.program_id(1)
    first_rows = i == 0
    tm = x_ref.shape[0]

    def normed(x, sh, sc):
        return (_rms(x) * g_ref[...] * (1.0 + sc) + sh).astype(BF16)

    @pl.when(j == 0)
    def _():
        hb = normed(x_ref[...], sh_ref[...], sc_ref[...])
        h_scr[0:tm, :] = hb
        odt_ref[...] = _dot_nt(hb, wdt_ref[...])

    @pl.when(jnp.logical_and(first_rows, j == 0))
    def _():
        hb = normed(xs_ref[...], shs_ref[...], scs_ref[...])
        h_scr[tm:, :] = hb
        odts_ref[...] = _dot_nt(hb, wdt_ref[...])

    region = (j * tn) // D_A
    is_silu = functools.reduce(jnp.logical_or, [region == r for r in REGION_SILU])

    def both_groups(act):
        @pl.when(first_rows)
        def _():
            res = act(_dot_nt(h_scr[...], w_ref[...]))
            o_ref[...] = res[0:tm]
            os_ref[...] = res[tm:]

        @pl.when(jnp.logical_not(first_rows))
        def _():
            o_ref[...] = act(_dot_nt(h_scr[0:tm, :], w_ref[...]))

    def log_forget(p):
        lb = _lower_bound(lbl_ref[...])
        half = 0.5 * (1.0 - lb)
        return jnp.log((lb + half) + half * jnp.tanh(0.5 * p))

    pl.when(is_silu)(lambda: both_groups(_silu))
    pl.when(region == REGION_FORGET)(lambda: both_groups(log_forget))
    pl.when(jnp.logical_not(jnp.logical_or(is_silu, region == REGION_FORGET)))(lambda: both_groups(lambda p: p))


def _inproj(x, mod, rows_per_seq, xs, mod_s, g_pre, w_t, lb_logits, tm, tn, out_tile):
    m, ns = x.shape[0], xs.shape[0]
    assert D_MAIN % tn == 0 and D_MAIN % H_B == 0 and D_A % tn == 0
    f_tiles = D_A // tn
    n_j = D_MAIN // tn
    once = lambda shape, idx: pl.BlockSpec(shape, lambda i, j: idx, pipeline_mode=pl.Buffered(1))
    return pl.pallas_call(
        functools.partial(_inproj_kernel, tn=tn),
        grid=(m // tm, n_j),
        in_specs=[pl.BlockSpec((tm, D_MODEL), lambda i, j: (i, 0), pipeline_mode=pl.Buffered(1)),
                  _mod_spec(False, rows_per_seq, tm, 0),
                  _mod_spec(False, rows_per_seq, tm, 1),
                  pl.BlockSpec((1, D_MODEL), lambda i, j: (0, 0)),
                  pl.BlockSpec((tn, D_MODEL), lambda i, j: (j, 0)),
                  pl.BlockSpec((H_B, D_MODEL), lambda i, j: (D_MAIN // H_B, 0)),
                  pl.BlockSpec((lb_logits.shape[0], tn),
                               lambda i, j: (0, jnp.clip(j - REGION_FORGET * f_tiles, 0, f_tiles - 1))),
                  once((ns, D_MODEL), (0, 0)), once((ns, D_MODEL), (0, 0)), once((ns, D_MODEL), (0, 1))],
        out_specs=[pl.BlockSpec((tm, tn), lambda i, j: (i, out_tile(j))),
                   pl.BlockSpec((tm, H_B), lambda i, j: (i, 0)),
                   pl.BlockSpec((ns, tn), lambda i, j: (0, jnp.where(i == 0, j, n_j - 1))),
                   pl.BlockSpec((ns, H_B), lambda i, j: (0, 0))],
        out_shape=[jax.ShapeDtypeStruct((m, D_MAIN), F32),
                   jax.ShapeDtypeStruct((m, H_B), F32),
                   jax.ShapeDtypeStruct((ns, D_MAIN), F32),
                   jax.ShapeDtypeStruct((ns, H_B), F32)],
        scratch_shapes=[pltpu.VMEM((tm + ns, D_MODEL), BF16)],
        compiler_params=_params("arbitrary", "arbitrary"),
        name="in_proj",
    )(x, mod, mod, g_pre, w_t, w_t, lb_logits, xs, mod_s, mod_s)


def _hgrn_interleave(j, tn):
    n_hg = D_A // tn
    return jnp.where(j < 4 * n_hg, 4 * (j % n_hg) + j // n_hg, j)


def _hgrn_scores(qs, lf, row):
    c_len = CHUNK_A
    n_sub = c_len // SUB_A
    k = 1.0 - jnp.exp(lf)
    b = _cumsum_rows(lf, SUB_A, row & (SUB_A - 1))
    k_acc = []
    blocks, q_dec = [], []
    start = jnp.zeros((1, HA_D), F32)
    for i in range(n_sub):
        lo = i * SUB_A
        hi = lo + SUB_A
        b_i = b[lo:hi, :]
        b_sub = b[hi - 1:hi, :]
        k_i = k[lo:hi, :]
        qg = qs[lo:hi, :] * jnp.exp(b_i)
        k_diag = k_i * jnp.exp(-b_i)
        rhs = jnp.concatenate([p.astype(BF16) for p in k_acc] + [k_diag.astype(BF16)], axis=0)
        a_i = lax.dot_general(qg.astype(BF16), rhs, (((1,), (1,)), ((), ())),
                              preferred_element_type=F32)
        causal = (lax.broadcasted_iota(jnp.int32, (SUB_A, hi), 1)
                  <= lax.broadcasted_iota(jnp.int32, (SUB_A, hi), 0) + lo)
        a_i = jnp.where(causal, a_i, 0.0)
        blocks.append(jnp.pad(a_i, ((0, 0), (0, c_len - hi))).astype(BF16))
        q_dec.append((qg * jnp.exp(start)).astype(BF16))
        sub_decay = jnp.exp(b_sub)
        k_acc = [p * sub_decay for p in k_acc] + [k_i * jnp.exp(b_sub - b_i)]
        start = start + b_sub
    scores = jnp.concatenate(blocks, axis=0)
    k_end = jnp.concatenate([p.astype(BF16) for p in k_acc], axis=0)
    chunk_decay = jnp.exp(start)
    return scores, jnp.concatenate(q_dec, axis=0), k_end, chunk_decay


def _hgrn_apply(pend, v, gs, nw, st):
    scores, q_dec, k_end, chunk_decay = pend
    vb = v.astype(BF16)
    o = (jnp.dot(scores, vb, preferred_element_type=F32)
         + lax.dot_general(q_dec, st.astype(BF16), (((1,), (1,)), ((), ())),
                           preferred_element_type=F32))
    st = chunk_decay * st + lax.dot_general(vb, k_end, (((0,), (0,)), ((), ())),
                                            preferred_element_type=F32)
    return (_rms(o) * nw * gs).astype(BF16), st


def _hgrn_prompt_kernel(p_ref, nw_ref, o_ref, s_ref, *, n_chunks, n_heads):
    c_len = CHUNK_A
    w = n_heads * HA_D
    nw = nw_ref[...]
    row = lax.broadcasted_iota(jnp.int32, (c_len, HA_D), 0)
    heads = [slice(h * HA_D, (h + 1) * HA_D) for h in range(n_heads)]

    def part(k, sl, hs):
        return p_ref[sl, k * w + hs.start:k * w + hs.stop]

    def rows(c):
        return pl.ds(pl.multiple_of(c * c_len, c_len), c_len)

    def scores_of(c):
        sl = rows(c)
        return tuple(_hgrn_scores(part(0, sl, hs), part(1, sl, hs), row) for hs in heads)

    def apply_to(c, pends, sts):
        sl = rows(c)
        new = []
        for hs, pend, st in zip(heads, pends, sts):
            out, st = _hgrn_apply(pend, part(2, sl, hs), part(3, sl, hs), nw[:, hs], st)
            o_ref[sl, hs] = out
            new.append(st)
        return tuple(new)

    def step(c, carry):
        pends, sts = carry
        sts = apply_to(c - 1, pends, sts)
        return scores_of(c), sts

    zeros = tuple(jnp.zeros((HA_D, HA_D), F32) for _ in range(n_heads))
    pends, sts = lax.fori_loop(1, n_chunks, step, (scores_of(0), zeros))
    sts = apply_to(n_chunks - 1, pends, sts)
    for h in range(n_heads):
        s_ref[h] = sts[h].T


def _hgrn_prompt(proj, hgrn_norm, n_seq, t):
    m = proj.shape[0]
    nh = HGRN_HEADS_PER_STEP
    w = nh * HA_D
    return pl.pallas_call(
        functools.partial(_hgrn_prompt_kernel, n_chunks=t // CHUNK_A, n_heads=nh),
        grid=(n_seq, H_A // nh),
        in_specs=[pl.BlockSpec((t, 4 * w), lambda b, h: (b, h)),
                  pl.BlockSpec((1, w), lambda b, h: (0, h))],
        out_specs=[pl.BlockSpec((t, w), lambda b, h: (b, h)),
                   pl.BlockSpec((None, nh, HA_D, HA_D), lambda b, h: (b, h, 0, 0))],
        out_shape=[jax.ShapeDtypeStruct((m, D_A), BF16),
                   jax.ShapeDtypeStruct((n_seq, H_A, HA_D, HA_D), F32)],
        compiler_params=_params("parallel", "parallel"),
        name="hgrn_prompt",
    )(proj, hgrn_norm)


def _conv_chunk(tail, u, w, bias, n):
    full = jnp.concatenate([tail, u], axis=0)
    out = bias + w[CONV_W - 1:CONV_W, :] * u
    for i in range(CONV_W - 1):
        shifted = pltpu.roll(full, CONV_W - 1 - i, 0)[SUBLANE:SUBLANE + n, :]
        out = out + w[i:i + 1, :] * shifted
    return _silu(out)


def _ssd_prompt_kernel(z_ref, x_ref, b_ref, c_ref, dtr_ref, cpar_ref, dpar_ref,
                       o_ref, h_ref, row_scr, *, n_chunks, n_groups):
    n = CHUNK_B
    assert n == LANE
    tri = (lax.broadcasted_iota(jnp.int32, (n, n), 0)
           >= lax.broadcasted_iota(jnp.int32, (n, n), 1))
    lane_head = _idiv(lax.broadcasted_iota(jnp.int32, (1, GW), 1), B_P)
    row_head = _idiv(lax.broadcasted_iota(jnp.int32, (GW, 1), 0), B_P)
    nr = n_chunks * SUBLANE
    lane_nr = lax.broadcasted_iota(jnp.int32, (nr, n), 1)
    is_dt = (lax.broadcasted_iota(jnp.int32, (nr, n), 0) & (SUBLANE - 1)) < HPG
    is_dt8 = lax.broadcasted_iota(jnp.int32, (SUBLANE, n), 0) < HPG
    n_src = 2 * SUBLANE
    sel_shape = (4 * n_src, HPG * n + 2 * GW)
    sel_k = lax.broadcasted_iota(jnp.int32, sel_shape, 0)
    sel_l = lax.broadcasted_iota(jnp.int32, sel_shape, 1)
    want = jnp.where(sel_l < HPG * n, HPG + _idiv(sel_l, n),
                     2 * HPG + _idiv(sel_l - HPG * n, B_P))
    bcast_sel = jnp.where(((sel_k & (n_src - 1)) == want) & (sel_k < 3 * n_src), 1.0, 0.0).astype(BF16)

    groups = []
    for gi in range(n_groups):
        xs = slice(gi * GW, (gi + 1) * GW)
        bs = slice(gi * B_N, (gi + 1) * B_N)
        dpar = dpar_ref[gi]
        dt_all = _softplus(dtr_ref[gi] + dpar[:, 0:1])
        cum_all = _cumsum_lanes(dt_all * -jnp.exp(dpar[:, 1:2]), n, lane_nr)
        row_scr[gi] = jnp.where(is_dt, dt_all, cum_all).reshape(n_chunks, SUBLANE, n)
        bl = slice(n_groups * GW + gi * B_N, n_groups * GW + (gi + 1) * B_N)
        cl = slice(n_groups * (GW + B_N) + gi * B_N, n_groups * (GW + B_N) + (gi + 1) * B_N)
        taps, bias = slice(0, CONV_W), slice(CONV_W, CONV_W + 1)
        groups.append(dict(gi=gi, xs=xs, bs=bs, dskip=cpar_ref[6:7, xs], nw=cpar_ref[5:6, xs],
                           wx=cpar_ref[taps, xs], wb=cpar_ref[taps, bl], wc=cpar_ref[taps, cl],
                           bx=cpar_ref[bias, xs], bb=cpar_ref[bias, bl], bc=cpar_ref[bias, cl]))

    def rows(c):
        return pl.ds(pl.multiple_of(c * n, n), n)

    def scan_free(c, grp):
        xs, bs = grp["xs"], grp["bs"]
        sl = rows(c)
        if isinstance(c, int) and c == 0:
            tail = lambda ref, lanes: jnp.zeros((SUBLANE, lanes.stop - lanes.start), F32)
        else:
            prev = pl.ds(pl.multiple_of(c * n - SUBLANE, SUBLANE), SUBLANE)
            tail = lambda ref, lanes: ref[prev, lanes]
        xc = _conv_chunk(tail(x_ref, xs), x_ref[sl, xs], grp["wx"], grp["bx"], n)
        bcv = _conv_chunk(tail(b_ref, bs), b_ref[sl, bs], grp["wb"], grp["bb"], n)
        ccv = _conv_chunk(tail(c_ref, bs), c_ref[sl, bs], grp["wc"], grp["bc"], n)
        r8 = row_scr[grp["gi"], c]
        swapped = pltpu.roll(r8, HPG, 0)
        cum8 = jnp.where(is_dt8, swapped, r8)
        dt8 = jnp.where(is_dt8, r8, swapped)
        e8 = jnp.where(is_dt8, jnp.exp(cum8), jnp.exp(cum8[:, n - 1:n] - cum8) * dt8)
        cols = _transpose_bcast(jnp.concatenate([r8, e8], axis=0), bcast_sel)
        e_cum = cols[:, HPG * n:HPG * n + GW]
        w_end = cols[:, HPG * n + GW:]
        g = _dot_nt(ccv, bcv)
        xb = xc.astype(BF16)
        s_all, x_all = [], []
        for j in range(HPG):
            cc = cols[:, j * n:(j + 1) * n]
            seg = cc - r8[HPG + j:HPG + j + 1, :]
            s = g * jnp.exp(jnp.where(tri, seg, -jnp.inf)) * r8[j:j + 1, :]
            s_all.append(s.astype(BF16))
            x_all.append(jnp.where(lane_head == j, xb, jnp.zeros_like(xb)))
        y = grp["dskip"] * xc + jnp.dot(jnp.concatenate(s_all, axis=1), jnp.concatenate(x_all, axis=0),
                                        preferred_element_type=F32)
        return y, e_cum, ccv.astype(BF16), (xc * w_end).astype(BF16), bcv.astype(BF16), r8

    def apply_state(c, grp, pend, h):
        xs = grp["xs"]
        y, e_cum, ccb, xwb, bcb, r8 = pend
        sl = rows(c)
        ch = lax.dot_general(ccb, h.astype(BF16), (((1,), (1,)), ((), ())),
                             preferred_element_type=F32)
        yz = (y + ch * e_cum) * z_ref[sl, xs]
        o_ref[sl, xs] = (_rms(yz) * grp["nw"]).astype(o_ref.dtype)
        h_dec = jnp.zeros((GW, 1), F32)
        for j in range(HPG):
            h_dec = jnp.where(row_head == j, jnp.exp(r8[HPG + j:HPG + j + 1, n - 1:n]), h_dec)
        return h_dec * h + lax.dot_general(xwb, bcb, (((0,), (0,)), ((), ())),
                                           preferred_element_type=F32)

    def step(c, carry):
        pends, hs = carry
        hs = tuple(apply_state(c - 1, grp, pend, h) for grp, pend, h in zip(groups, pends, hs))
        return tuple(scan_free(c, grp) for grp in groups), hs

    init = (tuple(scan_free(0, grp) for grp in groups),
            tuple(jnp.zeros((GW, B_N), F32) for _ in groups))
    pends, hs = lax.fori_loop(1, n_chunks, step, init)
    for gi, (grp, pend, h) in enumerate(zip(groups, pends, hs)):
        h = apply_state(n_chunks - 1, grp, pend, h)
        h_ref[gi * HPG:(gi + 1) * HPG] = h.reshape(HPG, B_P, B_N)


def _ssd_channel_params(conv_w, conv_b, ssd_norm, d_skip, ng):
    zeros = jnp.zeros((1, 2 * B_G * B_N), F32)
    rows = jnp.concatenate([
        conv_w, conv_b,
        jnp.concatenate([ssd_norm, zeros], axis=1),
        jnp.concatenate([jnp.repeat(d_skip, B_P)[None], zeros], axis=1),
        jnp.zeros((SUBLANE - CONV_W - 3, CONV_DIM), F32)], axis=0)
    n_steps = B_G // ng
    part = lambda lo, width: rows[:, lo:lo + n_steps * width].reshape(SUBLANE, n_steps, width)
    slab = jnp.concatenate([part(0, ng * GW), part(D_B, ng * B_N), part(D_B + B_G * B_N, ng * B_N)], axis=2)
    return slab.transpose(1, 0, 2)


def _ssd_prompt(proj, dt_rows, cpar, dpar, n_seq, t):
    m = proj.shape[0]
    n_chunks = t // CHUNK_B
    nr = n_chunks * SUBLANE
    ng = SSD_GROUPS_PER_STEP
    xw, bw = ng * GW, ng * B_N
    wide = lambda off: pl.BlockSpec((t, xw), lambda b, g: (b, off // xw + g))
    narrow = lambda off: pl.BlockSpec((t, bw), lambda b, g: (b, off // bw + g))
    return pl.pallas_call(
        functools.partial(_ssd_prompt_kernel, n_chunks=n_chunks, n_groups=ng),
        grid=(n_seq, B_G // ng),
        in_specs=[wide(OFF_Z), wide(OFF_X), narrow(OFF_B), narrow(OFF_C),
                  pl.BlockSpec((ng, None, nr, CHUNK_B), lambda b, g: (g, b, 0, 0)),
                  pl.BlockSpec((None, SUBLANE, xw + 2 * bw), lambda b, g: (g, 0, 0)),
                  pl.BlockSpec((ng, nr, 2), lambda b, g: (g, 0, 0))],
        out_specs=[pl.BlockSpec((t, xw), lambda b, g: (b, g)),
                   pl.BlockSpec((None, ng * HPG, B_P, B_N), lambda b, g: (b, g, 0, 0))],
        out_shape=[jax.ShapeDtypeStruct((m, D_B), BF16),
                   jax.ShapeDtypeStruct((n_seq, H_B, B_P, B_N), F32)],
        scratch_shapes=[pltpu.VMEM((ng, n_chunks, SUBLANE, CHUNK_B), F32)],
        compiler_params=_params("parallel", "parallel"),
        name="ssd_prompt",
    )(proj, proj, proj, proj, dt_rows, cpar, dpar)


def _hgrn_step_rows(q_ref, f_ref, v_ref, g_ref, nw_ref, s_ref, o_ref, so_ref):
    nw = nw_ref[...]
    pad = jnp.zeros((HA_D - H_A, HA_D), F32)
    for b in range(q_ref.shape[0]):
        ft = jnp.concatenate([jnp.exp(f_ref[b]), pad], axis=0).T
        qt = jnp.concatenate([q_ref[b], pad], axis=0).T
        v = v_ref[b]
        rows = []
        for h in range(H_A):
            fc = ft[:, h:h + 1]
            s_new = fc * s_ref[b, h] + (1.0 - fc) * v[h:h + 1, :]
            so_ref[b, h] = s_new
            rows.append(jnp.sum(qt[:, h:h + 1] * s_new, axis=0, keepdims=True))
        o = jnp.concatenate(rows, axis=0)
        o_ref[b] = (_rms(o) * nw * g_ref[b]).astype(o_ref.dtype)


def _conv_step(buf_ref, b, u, w, bias):
    out = bias + w[CONV_W - 1] * u
    for i in range(CONV_W - 1):
        out = out + w[i] * buf_ref[b, i]
    return _silu(out)


N_SSD_STEP_LEAD = 6
N_SSD_STEP_IN = 15


def _ssd_step_rows(xt_ref, bcr_ref, zt_ref, dt_ref, cxt_ref, cbc_ref,
                   wxt_ref, wbc_ref, bxt_ref, bbc_ref, dtb_ref, al_ref, dsk_ref, nwt_ref, h_ref,
                   o_ref, ho_ref):
    wxt, wbc, bxt, bbc = wxt_ref[...], wbc_ref[...], bxt_ref[...], bbc_ref[...]
    a = -jnp.exp(al_ref[...])
    dtb, dsk, nwt = dtb_ref[...], dsk_ref[...], nwt_ref[...]
    lane = lax.broadcasted_iota(jnp.int32, (1, H_B), 1)
    lane_group = _idiv(lane, HPG)
    for b in range(xt_ref.shape[0]):
        xt = _conv_step(cxt_ref, b, xt_ref[b], wxt, bxt)
        bcv = _conv_step(cbc_ref, b, bcr_ref[b], wbc, bbc)
        dt = _softplus(dt_ref[b] + dtb)
        d_a = jnp.exp(dt * a)
        xdt = xt * dt
        yt = jnp.zeros((B_P, H_B), F32)
        for hh in range(H_B):
            g = hh // HPG
            h_new = d_a[:, hh:hh + 1] * h_ref[b, hh] + xdt[:, hh:hh + 1] * bcv[g:g + 1, :]
            ho_ref[b, hh] = h_new
            y_col = jnp.sum(h_new * bcv[B_G + g:B_G + g + 1, :], axis=1, keepdims=True)
            yt = jnp.where(lane == hh, y_col, yt)
        yz = (yt + dsk * xt) * zt_ref[b]
        col_sq = jnp.sum(yz * yz, axis=0, keepdims=True)
        ms = jnp.zeros((1, H_B), F32)
        for g in range(B_G):
            total = jnp.sum(col_sq[:, g * HPG:(g + 1) * HPG], axis=1, keepdims=True)
            ms = jnp.where(lane_group == g, total * (1.0 / GW), ms)
        o_ref[b] = (yz * lax.rsqrt(ms + EPS) * nwt).astype(o_ref.dtype)


def _ssd_step_specs(operands, state, bt, step_index):
    lead_ops, const_ops = operands[:N_SSD_STEP_LEAD], operands[N_SSD_STEP_LEAD:]
    lead = lambda a: pl.BlockSpec((bt,) + a.shape[1:], lambda *g: (step_index(*g),) + (0,) * (a.ndim - 1))
    full = lambda a: pl.BlockSpec(a.shape, lambda *g: (0,) * a.ndim)
    nb = state.shape[0]
    in_specs = [lead(a) for a in lead_ops] + [full(a) for a in const_ops] + [lead(state)]
    out_specs = [pl.BlockSpec((bt, B_P, H_B), lambda *g: (step_index(*g), 0, 0)), lead(state)]
    out_shape = [jax.ShapeDtypeStruct((nb, B_P, H_B), F32), jax.ShapeDtypeStruct(state.shape, F32)]
    return in_specs, out_specs, out_shape


def _outproj_kernel(oa_ref, ob_ref, wa_ref, wb_ref, x_ref, gt_ref, gp_ref, *rest):
    y_ref = rest[-3] if len(rest) > 1 else rest[0]
    mix = _dot(oa_ref[...], wa_ref[...]) + _dot(ob_ref[...], wb_ref[...])
    y_ref[...] = x_ref[...] + gt_ref[...] * (_rms(mix) * gp_ref[...])
    if len(rest) > 1:
        q_ref, f_ref, v_ref, g_ref, nw_ref, s_ref, _, o_ref, so_ref = rest
        _hgrn_step_rows(q_ref, f_ref, v_ref, g_ref, nw_ref, s_ref, o_ref, so_ref)


def _outproj(o_a, o_b, w_out, x, mod, per_row, rows_per_seq, g_post, tm, hgrn_step=None):
    m = x.shape[0]
    n_steps = m // tm
    row = lambda w: pl.BlockSpec((tm, w), lambda i, j: (i, 0))
    const = lambda shape, idx: pl.BlockSpec(shape, lambda i, j: idx, pipeline_mode=pl.Buffered(1))
    in_specs = [row(D_A), row(D_B),
                const((D_A, D_MODEL), (0, 0)), const((D_B, D_MODEL), (1, 0)),
                row(D_MODEL),
                _mod_spec(per_row, rows_per_seq, tm, 2),
                pl.BlockSpec((1, D_MODEL), lambda i, j: (0, 0))]
    out_specs = [row(D_MODEL)]
    out_shape = [jax.ShapeDtypeStruct((m, D_MODEL), F32)]
    args = [o_a, o_b, w_out, w_out, x, mod, g_post]
    if hgrn_step is not None:
        q, f, v, g, nw, state = hgrn_step
        nb = q.shape[0]
        assert nb % n_steps == 0
        bt = nb // n_steps
        rw = pl.BlockSpec((bt, H_A, HA_D), lambda i, j: (i, 0, 0))
        st = pl.BlockSpec((bt, H_A, HA_D, HA_D), lambda i, j: (i, 0, 0, 0))
        in_specs += [rw, rw, rw, rw, pl.BlockSpec((H_A, HA_D), lambda i, j: (0, 0)), st]
        out_specs += [rw, st]
        out_shape += [jax.ShapeDtypeStruct((nb, H_A, HA_D), BF16), jax.ShapeDtypeStruct(state.shape, F32)]
        args += [q, f, v, g, nw, state]
    out = pl.pallas_call(
        _outproj_kernel,
        grid=(n_steps, 1),
        in_specs=in_specs,
        out_specs=out_specs,
        out_shape=out_shape,
        compiler_params=_params("parallel", "arbitrary"),
        name="out_proj",
    )(*args)
    return out if hgrn_step is not None else out[0]


def _mlp_kernel(x_ref, sh_ref, sc_ref, gt_ref, gpre_ref, gpost_ref, wu_ref, wd_ref, *rest):
    ssd = len(rest) > 2
    if ssd:
        ssd_in, (y_ref, so_ref, sh_out_ref, h_scr) = rest[:N_SSD_STEP_IN], rest[N_SSD_STEP_IN:]
    else:
        y_ref, h_scr = rest
    f = pl.program_id(1)

    @pl.when(f == 0)
    def _():
        h = _rms(x_ref[...]) * gpre_ref[...] * (1.0 + sc_ref[...]) + sh_ref[...]
        h_scr[...] = h.astype(BF16)

    def tile_product():
        if ssd:
            _ssd_step_rows(*ssd_in, so_ref, sh_out_ref)
        u = jnp.maximum(jnp.dot(h_scr[...], wu_ref[...].astype(BF16), preferred_element_type=F32), 0.0)
        return jnp.dot((u * u).astype(BF16), wd_ref[...].astype(BF16), preferred_element_type=F32)

    @pl.when(f == 0)
    def _():
        y_ref[...] = tile_product()

    @pl.when(f > 0)
    def _():
        y_ref[...] += tile_product()

    @pl.when(f == pl.num_programs(1) - 1)
    def _():
        y_ref[...] = x_ref[...] + gt_ref[...] * (_rms(y_ref[...]) * gpost_ref[...])


def _mlp(x, mod, per_row, rows_per_seq, g_pre, g_post, w_up, w_down, tm, tf, ssd_step=None):
    m = x.shape[0]
    n_f = D_FF // tf
    n_steps = (m // tm) * n_f
    row = pl.BlockSpec((tm, D_MODEL), lambda i, j: (i, 0))
    vec = pl.BlockSpec((1, D_MODEL), lambda i, j: (0, 0))
    in_specs = [pl.BlockSpec((tm, D_MODEL), lambda i, j: (i, 0), pipeline_mode=pl.Buffered(1)),
                _mod_spec(per_row, rows_per_seq, tm, 3),
                _mod_spec(per_row, rows_per_seq, tm, 4),
                _mod_spec(per_row, rows_per_seq, tm, 5),
                vec, vec,
                pl.BlockSpec((D_MODEL, tf), lambda i, j: (0, j)),
                pl.BlockSpec((tf, D_MODEL), lambda i, j: (j, 0))]
    out_specs = [row]
    out_shape = [jax.ShapeDtypeStruct((m, D_MODEL), F32)]
    args = [x, mod, mod, mod, g_pre, g_post, w_up, w_down]
    if ssd_step is not None:
        out_specs = [pl.BlockSpec((tm, D_MODEL), lambda i, j: (i, 0), pipeline_mode=pl.Buffered(1))]
        operands, state = ssd_step
        nb = state.shape[0]
        assert nb % n_steps == 0
        s_in, s_out, s_shape = _ssd_step_specs(operands, state, nb // n_steps, lambda i, j: i * n_f + j)
        in_specs += s_in
        out_specs += s_out
        out_shape += s_shape
        args += list(operands) + [state]
    out = pl.pallas_call(
        _mlp_kernel,
        grid=(m // tm, n_f),
        in_specs=in_specs,
        out_specs=out_specs,
        out_shape=out_shape,
        scratch_shapes=[pltpu.VMEM((tm, D_MODEL), BF16)],
        compiler_params=_params("parallel", "arbitrary"),
        name="mlp",
    )(*args)
    return out if ssd_step is not None else out[0]


def kernel(x_prompt, x_sample, c_prompt, c_sample, state_hgrn, state_ssm, state_conv, w_ada, b_ada, norm_pre_mix, norm_post_mix, norm_pre_mlp, norm_post_mlp, w_in, hgrn_lb_logits, hgrn_norm, conv_w, conv_b, dt_bias, a_log, d_skip, ssd_norm, w_out, w_up, w_down):
    n_seq, t, _ = x_prompt.shape
    n_dec = x_sample.shape[0]
    assert x_sample.shape[1] == 1 and t % CHUNK_A == 0 and t % CHUNK_B == 0
    assert w_ada.shape[0] == 1, "one layer"
    l = 0
    tiles = _tiles(t, n_dec)

    w_in_t = w_in[l].T
    w_out_b = w_out[l].astype(BF16)
    g_pre_mix, g_post_mix = norm_pre_mix[l][None], norm_post_mix[l][None]
    g_pre_mlp, g_post_mlp = norm_pre_mlp[l][None], norm_post_mlp[l][None]
    cw, cb = conv_w[l], conv_b[l][None]
    hn = hgrn_norm[l][None]
    sn = ssd_norm[l][None]

    n_c = n_seq + n_dec
    pad = (-n_c) % SUBLANE
    c_all = jnp.concatenate([c_sample, c_prompt, jnp.zeros((pad, D_MODEL), F32)], axis=0)
    mod = _ada(c_all, w_ada[l], b_ada[l][None], tiles["ada_n"])
    mod_p = mod[n_dec:n_c].reshape(n_seq, 1, N_MOD * D_MODEL)
    mod_s = mod

    xp = x_prompt.reshape(n_seq * t, D_MODEL)
    xs = x_sample.reshape(n_dec, D_MODEL)
    tm, tn = tiles["inproj"]
    proj_p, dt_p, proj_s, dt_s = _inproj(xp, mod_p, t, xs, mod_s, g_pre_mix, w_in_t, hgrn_lb_logits, tm, tn,
                                         functools.partial(_hgrn_interleave, tn=tn))
    o_a, s_hgrn_p = _hgrn_prompt(proj_p, hn, n_seq, t)

    n_chunks = t // CHUNK_B
    rep = SUBLANE // HPG
    dt_rows = dt_p.reshape(n_seq, n_chunks, CHUNK_B, B_G, 1, HPG).transpose(3, 0, 1, 4, 5, 2)
    dt_rows = jnp.broadcast_to(dt_rows, (B_G, n_seq, n_chunks, rep, HPG, CHUNK_B))
    dt_rows = dt_rows.reshape(B_G, n_seq, n_chunks * SUBLANE, CHUNK_B)
    prow = lambda p: jnp.tile(p.reshape(B_G, 1, HPG), (1, n_chunks * rep, 1)).reshape(B_G, n_chunks * SUBLANE, 1)
    cpar = _ssd_channel_params(cw, cb, sn, d_skip[l], SSD_GROUPS_PER_STEP)
    dpar = jnp.concatenate([prow(dt_bias[l]), prow(a_log[l])], axis=2)
    o_b, s_ssm_p = _ssd_prompt(proj_p, dt_rows, cpar, dpar, n_seq, t)
    heads = lambda off: proj_s[:, off:off + D_A].reshape(n_dec, H_A, HA_D)
    x1, o_a_s, s_hgrn_s = _outproj(
        o_a, o_b, w_out_b, xp, mod_p, False, t, g_post_mix, tiles["outproj_p"],
        hgrn_step=(heads(OFF_Q), heads(OFF_F), heads(OFF_I), heads(OFF_G), hn.reshape(H_A, HA_D),
                   state_hgrn[l]))
    conv_p = proj_p.reshape(n_seq, t, D_MAIN)[:, t - (CONV_W - 1):, OFF_X:]

    to_t = lambda a, h, d: a.reshape(a.shape[:-1] + (h, d)).swapaxes(-1, -2)
    xbc_s = proj_s[:, OFF_X:]
    cst = state_conv[l]
    xt = to_t(xbc_s[:, :D_B], H_B, B_P)
    zt = to_t(proj_s[:, OFF_Z:OFF_Z + D_B], H_B, B_P)
    bcr = xbc_s[:, D_B:].reshape(n_dec, 2 * B_G, B_N)
    cxt = to_t(cst[:, :, :D_B], H_B, B_P)
    cbc = cst[:, :, D_B:].reshape(n_dec, CONV_W - 1, 2 * B_G, B_N)
    wxt = to_t(cw[:, :D_B], H_B, B_P)
    wbc = cw[:, D_B:].reshape(CONV_W, 2 * B_G, B_N)
    bxt = to_t(cb[:, :D_B], H_B, B_P)[0]
    bbc = cb[0, D_B:].reshape(2 * B_G, B_N)
    ssd_operands = (xt, bcr, zt, dt_s[:, None, :], cxt, cbc,
                    wxt, wbc, bxt, bbc, dt_bias[l][None], a_log[l][None], d_skip[l][None],
                    to_t(sn, H_B, B_P)[0])
    assert len(ssd_operands) + 1 == N_SSD_STEP_IN
    y_p, o_b_st, s_ssm_s = _mlp(x1, mod_p, False, t, g_pre_mlp, g_post_mlp, w_up[l], w_down[l],
                                *tiles["mlp_p"], ssd_step=(ssd_operands, state_ssm[l]))
    o_b_s = o_b_st.swapaxes(1, 2).reshape(n_dec, D_B)

    x1_s = _outproj(o_a_s.reshape(n_dec, D_A), o_b_s, w_out_b, xs, mod_s, True, 1,
                    g_post_mix, tiles["outproj_s"])
    y_s = _mlp(x1_s, mod_s, True, 1, g_pre_mlp, g_post_mlp, w_up[l], w_down[l], *tiles["mlp_s"])
    conv_s = jnp.concatenate([cst[:, 1:], xbc_s[:, None, :]], axis=1)

    return (y_p.reshape(n_seq, t, D_MODEL), y_s.reshape(n_dec, 1, D_MODEL),
            s_hgrn_p[None], s_ssm_p[None], conv_p[None],
            s_hgrn_s[None], s_ssm_s[None], conv_s[None])
```

```python
import functools

import jax
import jax.numpy as jnp
from jax import lax
from jax.experimental import pallas as pl
from jax.experimental.pallas import tpu as pltpu

F32 = jnp.float32
BF16 = jnp.bfloat16

D_MODEL = 2048
D_A = 2048
HA_D = 128
H_A = D_A // HA_D
D_B = 2048
B_P = 64
H_B = D_B // B_P
B_G = 8
HPG = H_B // B_G
GW = HPG * B_P
B_N = 128
CONV_W = 4
CONV_DIM = D_B + 2 * B_G * B_N
D_MAIN = 4 * D_A + D_B + CONV_DIM
D_FF = 4 * D_MODEL
N_MOD = 6
EPS = 1e-6

OFF_Q, OFF_F, OFF_I, OFF_G = 0, D_A, 2 * D_A, 3 * D_A
OFF_Z = 4 * D_A
OFF_X = OFF_Z + D_B
OFF_B = OFF_X + D_B
OFF_C = OFF_B + B_G * B_N
REGION_SILU = (0, 3, 4)
REGION_FORGET = 1

LANE = 128
SUBLANE = 8
VMEM_LIMIT = 56 * 1024 * 1024

CHUNK_A = 128
SUB_A = 16
CHUNK_B = 128
HGRN_HEADS_PER_STEP = 4
SSD_GROUPS_PER_STEP = 2


def _tiles(t, n_dec):
    return dict(
        inproj=(_pick_tile(t, 2048), HGRN_HEADS_PER_STEP * HA_D),
        outproj_p=_pick_tile(t, 256), outproj_s=n_dec,
        mlp_p=(_pick_tile(t, 1024), 512), mlp_s=(n_dec, 1024),
        ada_n=1024)


def _pick_tile(m, target):
    t = min(m, target)
    while m % t:
        t //= 2
    return t


def _silu(x):
    hx = 0.5 * x
    return hx * jnp.tanh(hx) + hx


def _softplus(x):
    return jnp.maximum(x, 0.0) + jnp.log1p(jnp.exp(-jnp.abs(x)))


def _idiv(x, d):
    assert d & (d - 1) == 0
    return jnp.right_shift(x, d.bit_length() - 1)


def _rms(x):
    return x * lax.rsqrt(jnp.mean(x * x, axis=-1, keepdims=True) + EPS)


def _cumsum_rows(x, n, row):
    s = 1
    while s < n:
        x = x + jnp.where(row >= s, pltpu.roll(x, s, 0), 0.0)
        s *= 2
    return x


def _cumsum_lanes(x, n, lane):
    s = 1
    while s < n:
        x = x + jnp.where(lane >= s, pltpu.roll(x, s, 1), 0.0)
        s *= 2
    return x


def _dot(a, b):
    return jnp.dot(a.astype(BF16), b.astype(BF16), preferred_element_type=F32)


def _dot_nt(a, b):
    return lax.dot_general(a.astype(BF16), b.astype(BF16), (((1,), (1,)), ((), ())),
                           preferred_element_type=F32)


def _exact_terms(x):
    hi = x.astype(BF16).astype(F32)
    mid = (x - hi).astype(BF16).astype(F32)
    lo = x - hi - mid
    return jnp.concatenate([hi, mid, lo, jnp.zeros_like(hi)], axis=0).astype(BF16)


def _transpose_bcast(x, sel):
    return lax.dot_general(_exact_terms(x), sel, (((0,), (0,)), ((), ())),
                           preferred_element_type=F32)


def _params(*sem):
    return pltpu.CompilerParams(dimension_semantics=sem, vmem_limit_bytes=VMEM_LIMIT)


def _ada_kernel(c_ref, w_ref, b_ref, o_ref):
    o_ref[...] = _dot(_silu(c_ref[...]), w_ref[...]) + b_ref[...]


def _ada(c_all, w_ada, b_ada, tn):
    m = c_all.shape[0]
    n = w_ada.shape[1]
    return pl.pallas_call(
        _ada_kernel,
        grid=(n // tn,),
        in_specs=[pl.BlockSpec((m, D_MODEL), lambda j: (0, 0)),
                  pl.BlockSpec((D_MODEL, tn), lambda j: (0, j)),
                  pl.BlockSpec((1, tn), lambda j: (0, j))],
        out_specs=pl.BlockSpec((m, tn), lambda j: (0, j)),
        out_shape=jax.ShapeDtypeStruct((m, n), F32),
        compiler_params=_params("parallel"),
        name="ada_mod",
    )(c_all, w_ada, b_ada)


def _mod_spec(per_row, rows_per_seq, tm, col):
    if per_row:
        return pl.BlockSpec((tm, D_MODEL), lambda i, j: (i, col))
    tiles_per_seq = rows_per_seq // tm
    return pl.BlockSpec((None, 1, D_MODEL), lambda i, j: (i // tiles_per_seq, 0, col))


def _lower_bound(logits):
    e = jnp.exp(logits - jnp.max(logits, axis=0, keepdims=True))
    return e[0:1] / jnp.sum(e, axis=0, keepdims=True)


def _inproj_kernel(x_ref, sh_ref, sc_ref, g_ref, w_ref, wdt_ref, lbl_ref, xs_ref, shs_ref, scs_ref,
                   o_ref, odt_ref, os_ref, odts_ref, h_scr, *, tn):
    i, j = pl.program_id(0), pl.program_id(1)
    first_rows = i == 0
    tm = x_ref.shape[0]

    def normed(x, sh, sc):
        return (_rms(x) * g_ref[...] * (1.0 + sc) + sh).astype(BF16)

    @pl.when(j == 0)
    def _():
        hb = normed(x_ref[...], sh_ref[...], sc_ref[...])
        h_scr[0:tm, :] = hb
        odt_ref[...] = _dot_nt(hb, wdt_ref[...])

    @pl.when(jnp.logical_and(first_rows, j == 0))
    def _():
        hb = normed(xs_ref[...], shs_ref[...], scs_ref[...])
        h_scr[tm:, :] = hb
        odts_ref[...] = _dot_nt(hb, wdt_ref[...])

    region = (j * tn) // D_A
    is_silu = functools.reduce(jnp.logical_or, [region == r for r in REGION_SILU])

    def both_groups(act):
        @pl.when(first_rows)
        def _():
            res = act(_dot_nt(h_scr[...], w_ref[...]))
            o_ref[...] = res[0:tm]
            os_ref[...] = res[tm:]

        @pl.when(jnp.logical_not(first_rows))
        def _():
            o_ref[...] = act(_dot_nt(h_scr[0:tm, :], w_ref[...]))

    def log_forget(p):
        lb = _lower_bound(lbl_ref[...])
        half = 0.5 * (1.0 - lb)
        return jnp.log((lb + half) + half * jnp.tanh(0.5 * p))

    pl.when(is_silu)(lambda: both_groups(_silu))
    pl.when(region == REGION_FORGET)(lambda: both_groups(log_forget))
    pl.when(jnp.logical_not(jnp.logical_or(is_silu, region == REGION_FORGET)))(lambda: both_groups(lambda p: p))


def _inproj(x, mod, rows_per_seq, xs, mod_s, g_pre, w_t, lb_logits, tm, tn, out_tile):
    m, ns = x.shape[0], xs.shape[0]
    assert D_MAIN % tn == 0 and D_MAIN % H_B == 0 and D_A % tn == 0
    f_tiles = D_A // tn
    n_j = D_MAIN // tn
    once = lambda shape, idx: pl.BlockSpec(shape, lambda i, j: idx, pipeline_mode=pl.Buffered(1))
    return pl.pallas_call(
        functools.partial(_inproj_kernel, tn=tn),
        grid=(m // tm, n_j),
        in_specs=[pl.BlockSpec((tm, D_MODEL), lambda i, j: (i, 0), pipeline_mode=pl.Buffered(1)),
                  _mod_spec(False, rows_per_seq, tm, 0),
                  _mod_spec(False, rows_per_seq, tm, 1),
                  pl.BlockSpec((1, D_MODEL), lambda i, j: (0, 0)),
                  pl.BlockSpec((tn, D_MODEL), lambda i, j: (j, 0)),
                  pl.BlockSpec((H_B, D_MODEL), lambda i, j: (D_MAIN // H_B, 0)),
                  pl.BlockSpec((lb_logits.shape[0], tn),
                               lambda i, j: (0, jnp.clip(j - REGION_FORGET * f_tiles, 0, f_tiles - 1))),
                  once((ns, D_MODEL), (0, 0)), once((ns, D_MODEL), (0, 0)), once((ns, D_MODEL), (0, 1))],
        out_specs=[pl.BlockSpec((tm, tn), lambda i, j: (i, out_tile(j))),
                   pl.BlockSpec((tm, H_B), lambda i, j: (i, 0)),
                   pl.BlockSpec((ns, tn), lambda i, j: (0, jnp.where(i == 0, j, n_j - 1))),
                   pl.BlockSpec((ns, H_B), lambda i, j: (0, 0))],
        out_shape=[jax.ShapeDtypeStruct((m, D_MAIN), F32),
                   jax.ShapeDtypeStruct((m, H_B), F32),
                   jax.ShapeDtypeStruct((ns, D_MAIN), F32),
                   jax.ShapeDtypeStruct((ns, H_B), F32)],
        scratch_shapes=[pltpu.VMEM((tm + ns, D_MODEL), BF16)],
        compiler_params=_params("arbitrary", "arbitrary"),
        name="in_proj",
    )(x, mod, mod, g_pre, w_t, w_t, lb_logits, xs, mod_s, mod_s)


def _hgrn_interleave(j, tn):
    n_hg = D_A // tn
    return jnp.where(j < 4 * n_hg, 4 * (j % n_hg) + j // n_hg, j)


def _hgrn_scores(qs, lf, row):
    c_len = CHUNK_A
    n_sub = c_len // SUB_A
    k = 1.0 - jnp.exp(lf)
    b = _cumsum_rows(lf, SUB_A, row & (SUB_A - 1))
    k_acc = []
    blocks, q_dec = [], []
    start = jnp.zeros((1, HA_D), F32)
    for i in range(n_sub):
        lo = i * SUB_A
        hi = lo + SUB_A
        b_i = b[lo:hi, :]
        b_sub = b[hi - 1:hi, :]
        k_i = k[lo:hi, :]
        qg = qs[lo:hi, :] * jnp.exp(b_i)
        k_diag = k_i * jnp.exp(-b_i)
        rhs = jnp.concatenate([p.astype(BF16) for p in k_acc] + [k_diag.astype(BF16)], axis=0)
        a_i = lax.dot_general(qg.astype(BF16), rhs, (((1,), (1,)), ((), ())),
                              preferred_element_type=F32)
        causal = (lax.broadcasted_iota(jnp.int32, (SUB_A, hi), 1)
                  <= lax.broadcasted_iota(jnp.int32, (SUB_A, hi), 0) + lo)
        a_i = jnp.where(causal, a_i, 0.0)
        blocks.append(jnp.pad(a_i, ((0, 0), (0, c_len - hi))).astype(BF16))
        q_dec.append((qg * jnp.exp(start)).astype(BF16))
        sub_decay = jnp.exp(b_sub)
        k_acc = [p * sub_decay for p in k_acc] + [k_i * jnp.exp(b_sub - b_i)]
        start = start + b_sub
    scores = jnp.concatenate(blocks, axis=0)
    k_end = jnp.concatenate([p.astype(BF16) for p in k_acc], axis=0)
    chunk_decay = jnp.exp(start)
    return scores, jnp.concatenate(q_dec, axis=0), k_end, chunk_decay


def _hgrn_apply(pend, v, gs, nw, st):
    scores, q_dec, k_end, chunk_decay = pend
    vb = v.astype(BF16)
    o = (jnp.dot(scores, vb, preferred_element_type=F32)
         + lax.dot_general(q_dec, st.astype(BF16), (((1,), (1,)), ((), ())),
                           preferred_element_type=F32))
    st = chunk_decay * st + lax.dot_general(vb, k_end, (((0,), (0,)), ((), ())),
                                            preferred_element_type=F32)
    return (_rms(o) * nw * gs).astype(BF16), st


def _hgrn_prompt_kernel(p_ref, nw_ref, o_ref, s_ref, *, n_chunks, n_heads):
    c_len = CHUNK_A
    w = n_heads * HA_D
    nw = nw_ref[...]
    row = lax.broadcasted_iota(jnp.int32, (c_len, HA_D), 0)
    heads = [slice(h * HA_D, (h + 1) * HA_D) for h in range(n_heads)]

    def part(k, sl, hs):
        return p_ref[sl, k * w + hs.start:k * w + hs.stop]

    def rows(c):
        return pl.ds(pl.multiple_of(c * c_len, c_len), c_len)

    def scores_of(c):
        sl = rows(c)
        return tuple(_hgrn_scores(part(0, sl, hs), part(1, sl, hs), row) for hs in heads)

    def apply_to(c, pends, sts):
        sl = rows(c)
        new = []
        for hs, pend, st in zip(heads, pends, sts):
            out, st = _hgrn_apply(pend, part(2, sl, hs), part(3, sl, hs), nw[:, hs], st)
            o_ref[sl, hs] = out
            new.append(st)
        return tuple(new)

    def step(c, carry):
        pends, sts = carry
        sts = apply_to(c - 1, pends, sts)
        return scores_of(c), sts

    zeros = tuple(jnp.zeros((HA_D, HA_D), F32) for _ in range(n_heads))
    pends, sts = lax.fori_loop(1, n_chunks, step, (scores_of(0), zeros))
    sts = apply_to(n_chunks - 1, pends, sts)
    for h in range(n_heads):
        s_ref[h] = sts[h].T


def _hgrn_prompt(proj, hgrn_norm, n_seq, t):
    m = proj.shape[0]
    nh = HGRN_HEADS_PER_STEP
    w = nh * HA_D
    return pl.pallas_call(
        functools.partial(_hgrn_prompt_kernel, n_chunks=t // CHUNK_A, n_heads=nh),
        grid=(n_seq, H_A // nh),
        in_specs=[pl.BlockSpec((t, 4 * w), lambda b, h: (b, h)),
                  pl.BlockSpec((1, w), lambda b, h: (0, h))],
        out_specs=[pl.BlockSpec((t, w), lambda b, h: (b, h)),
                   pl.BlockSpec((None, nh, HA_D, HA_D), lambda b, h: (b, h, 0, 0))],
        out_shape=[jax.ShapeDtypeStruct((m, D_A), BF16),
                   jax.ShapeDtypeStruct((n_seq, H_A, HA_D, HA_D), F32)],
        compiler_params=_params("parallel", "parallel"),
        name="hgrn_prompt",
    )(proj, hgrn_norm)


def _conv_chunk(tail, u, w, bias, n):
    full = jnp.concatenate([tail, u], axis=0)
    out = bias + w[CONV_W - 1:CONV_W, :] * u
    for i in range(CONV_W - 1):
        shifted = pltpu.roll(full, CONV_W - 1 - i, 0)[SUBLANE:SUBLANE + n, :]
        out = out + w[i:i + 1, :] * shifted
    return _silu(out)


def _ssd_prompt_kernel(z_ref, x_ref, b_ref, c_ref, dtr_ref, cpar_ref, dpar_ref,
                       o_ref, h_ref, row_scr, *, n_chunks, n_groups):
    n = CHUNK_B
    assert n == LANE
    tri = (lax.broadcasted_iota(jnp.int32, (n, n), 0)
           >= lax.broadcasted_iota(jnp.int32, (n, n), 1))
    lane_head = _idiv(lax.broadcasted_iota(jnp.int32, (1, GW), 1), B_P)
    row_head = _idiv(lax.broadcasted_iota(jnp.int32, (GW, 1), 0), B_P)
    nr = n_chunks * SUBLANE
    lane_nr = lax.broadcasted_iota(jnp.int32, (nr, n), 1)
    is_dt = (lax.broadcasted_iota(jnp.int32, (nr, n), 0) & (SUBLANE - 1)) < HPG
    is_dt8 = lax.broadcasted_iota(jnp.int32, (SUBLANE, n), 0) < HPG
    n_src = 2 * SUBLANE
    sel_shape = (4 * n_src, HPG * n + 2 * GW)
    sel_k = lax.broadcasted_iota(jnp.int32, sel_shape, 0)
    sel_l = lax.broadcasted_iota(jnp.int32, sel_shape, 1)
    want = jnp.where(sel_l < HPG * n, HPG + _idiv(sel_l, n),
                     2 * HPG + _idiv(sel_l - HPG * n, B_P))
    bcast_sel = jnp.where(((sel_k & (n_src - 1)) == want) & (sel_k < 3 * n_src), 1.0, 0.0).astype(BF16)

    groups = []
    for gi in range(n_groups):
        xs = slice(gi * GW, (gi + 1) * GW)
        bs = slice(gi * B_N, (gi + 1) * B_N)
        dpar = dpar_ref[gi]
        dt_all = _softplus(dtr_ref[gi] + dpar[:, 0:1])
        cum_all = _cumsum_lanes(dt_all * -jnp.exp(dpar[:, 1:2]), n, lane_nr)
        row_scr[gi] = jnp.where(is_dt, dt_all, cum_all).reshape(n_chunks, SUBLANE, n)
        bl = slice(n_groups * GW + gi * B_N, n_groups * GW + (gi + 1) * B_N)
        cl = slice(n_groups * (GW + B_N) + gi * B_N, n_groups * (GW + B_N) + (gi + 1) * B_N)
        taps, bias = slice(0, CONV_W), slice(CONV_W, CONV_W + 1)
        groups.append(dict(gi=gi, xs=xs, bs=bs, dskip=cpar_ref[6:7, xs], nw=cpar_ref[5:6, xs],
                           wx=cpar_ref[taps, xs], wb=cpar_ref[taps, bl], wc=cpar_ref[taps, cl],
                           bx=cpar_ref[bias, xs], bb=cpar_ref[bias, bl], bc=cpar_ref[bias, cl]))

    def rows(c):
        return pl.ds(pl.multiple_of(c * n, n), n)

    def scan_free(c, grp):
        xs, bs = grp["xs"], grp["bs"]
        sl = rows(c)
        if isinstance(c, int) and c == 0:
            tail = lambda ref, lanes: jnp.zeros((SUBLANE, lanes.stop - lanes.start), F32)
        else:
            prev = pl.ds(pl.multiple_of(c * n - SUBLANE, SUBLANE), SUBLANE)
            tail = lambda ref, lanes: ref[prev, lanes]
        xc = _conv_chunk(tail(x_ref, xs), x_ref[sl, xs], grp["wx"], grp["bx"], n)
        bcv = _conv_chunk(tail(b_ref, bs), b_ref[sl, bs], grp["wb"], grp["bb"], n)
        ccv = _conv_chunk(tail(c_ref, bs), c_ref[sl, bs], grp["wc"], grp["bc"], n)
        r8 = row_scr[grp["gi"], c]
        swapped = pltpu.roll(r8, HPG, 0)
        cum8 = jnp.where(is_dt8, swapped, r8)
        dt8 = jnp.where(is_dt8, r8, swapped)
        e8 = jnp.where(is_dt8, jnp.exp(cum8), jnp.exp(cum8[:, n - 1:n] - cum8) * dt8)
        cols = _transpose_bcast(jnp.concatenate([r8, e8], axis=0), bcast_sel)
        e_cum = cols[:, HPG * n:HPG * n + GW]
        w_end = cols[:, HPG * n + GW:]
        g = _dot_nt(ccv, bcv)
        xb = xc.astype(BF16)
        s_all, x_all = [], []
        for j in range(HPG):
            cc = cols[:, j * n:(j + 1) * n]
            seg = cc - r8[HPG + j:HPG + j + 1, :]
            s = g * jnp.exp(jnp.where(tri, seg, -jnp.inf)) * r8[j:j + 1, :]
            s_all.append(s.astype(BF16))
            x_all.append(jnp.where(lane_head == j, xb, jnp.zeros_like(xb)))
        y = grp["dskip"] * xc + jnp.dot(jnp.concatenate(s_all, axis=1), jnp.concatenate(x_all, axis=0),
                                        preferred_element_type=F32)
        return y, e_cum, ccv.astype(BF16), (xc * w_end).astype(BF16), bcv.astype(BF16), r8

    def apply_state(c, grp, pend, h):
        xs = grp["xs"]
        y, e_cum, ccb, xwb, bcb, r8 = pend
        sl = rows(c)
        ch = lax.dot_general(ccb, h.astype(BF16), (((1,), (1,)), ((), ())),
                             preferred_element_type=F32)
        yz = (y + ch * e_cum) * z_ref[sl, xs]
        o_ref[sl, xs] = (_rms(yz) * grp["nw"]).astype(o_ref.dtype)
        h_dec = jnp.zeros((GW, 1), F32)
        for j in range(HPG):
            h_dec = jnp.where(row_head == j, jnp.exp(r8[HPG + j:HPG + j + 1, n - 1:n]), h_dec)
        return h_dec * h + lax.dot_general(xwb, bcb, (((0,), (0,)), ((), ())),
                                           preferred_element_type=F32)

    def step(c, carry):
        pends, hs = carry
        hs = tuple(apply_state(c - 1, grp, pend, h) for grp, pend, h in zip(groups, pends, hs))
        return tuple(scan_free(c, grp) for grp in groups), hs

    init = (tuple(scan_free(0, grp) for grp in groups),
            tuple(jnp.zeros((GW, B_N), F32) for _ in groups))
    pends, hs = lax.fori_loop(1, n_chunks, step, init)
    for gi, (grp, pend, h) in enumerate(zip(groups, pends, hs)):
        h = apply_state(n_chunks - 1, grp, pend, h)
        h_ref[gi * HPG:(gi + 1) * HPG] = h.reshape(HPG, B_P, B_N)


def _ssd_channel_params(conv_w, conv_b, ssd_norm, d_skip, ng):
    zeros = jnp.zeros((1, 2 * B_G * B_N), F32)
    rows = jnp.concatenate([
        conv_w, conv_b,
        jnp.concatenate([ssd_norm, zeros], axis=1),
        jnp.concatenate([jnp.repeat(d_skip, B_P)[None], zeros], axis=1),
        jnp.zeros((SUBLANE - CONV_W - 3, CONV_DIM), F32)], axis=0)
    n_steps = B_G // ng
    part = lambda lo, width: rows[:, lo:lo + n_steps * width].reshape(SUBLANE, n_steps, width)
    slab = jnp.concatenate([part(0, ng * GW), part(D_B, ng * B_N), part(D_B + B_G * B_N, ng * B_N)], axis=2)
    return slab.transpose(1, 0, 2)


def _ssd_prompt(proj, dt_rows, cpar, dpar, n_seq, t):
    m = proj.shape[0]
    n_chunks = t // CHUNK_B
    nr = n_chunks * SUBLANE
    ng = SSD_GROUPS_PER_STEP
    xw, bw = ng * GW, ng * B_N
    wide = lambda off: pl.BlockSpec((t, xw), lambda b, g: (b, off // xw + g))
    narrow = lambda off: pl.BlockSpec((t, bw), lambda b, g: (b, off // bw + g))
    return pl.pallas_call(
        functools.partial(_ssd_prompt_kernel, n_chunks=n_chunks, n_groups=ng),
        grid=(n_seq, B_G // ng),
        in_specs=[wide(OFF_Z), wide(OFF_X), narrow(OFF_B), narrow(OFF_C),
                  pl.BlockSpec((ng, None, nr, CHUNK_B), lambda b, g: (g, b, 0, 0)),
                  pl.BlockSpec((None, SUBLANE, xw + 2 * bw), lambda b, g: (g, 0, 0)),
                  pl.BlockSpec((ng, nr, 2), lambda b, g: (g, 0, 0))],
        out_specs=[pl.BlockSpec((t, xw), lambda b, g: (b, g)),
                   pl.BlockSpec((None, ng * HPG, B_P, B_N), lambda b, g: (b, g, 0, 0))],
        out_shape=[jax.ShapeDtypeStruct((m, D_B), BF16),
                   jax.ShapeDtypeStruct((n_seq, H_B, B_P, B_N), F32)],
        scratch_shapes=[pltpu.VMEM((ng, n_chunks, SUBLANE, CHUNK_B), F32)],
        compiler_params=_params("parallel", "parallel"),
        name="ssd_prompt",
    )(proj, proj, proj, proj, dt_rows, cpar, dpar)


def _hgrn_step_rows(q_ref, f_ref, v_ref, g_ref, nw_ref, s_ref, o_ref, so_ref):
    nw = nw_ref[...]
    pad = jnp.zeros((HA_D - H_A, HA_D), F32)
    for b in range(q_ref.shape[0]):
        ft = jnp.concatenate([jnp.exp(f_ref[b]), pad], axis=0).T
        qt = jnp.concatenate([q_ref[b], pad], axis=0).T
        v = v_ref[b]
        rows = []
        for h in range(H_A):
            fc = ft[:, h:h + 1]
            s_new = fc * s_ref[b, h] + (1.0 - fc) * v[h:h + 1, :]
            so_ref[b, h] = s_new
            rows.append(jnp.sum(qt[:, h:h + 1] * s_new, axis=0, keepdims=True))
        o = jnp.concatenate(rows, axis=0)
        o_ref[b] = (_rms(o) * nw * g_ref[b]).astype(o_ref.dtype)


def _conv_step(buf_ref, b, u, w, bias):
    out = bias + w[CONV_W - 1] * u
    for i in range(CONV_W - 1):
        out = out + w[i] * buf_ref[b, i]
    return _silu(out)


N_SSD_STEP_LEAD = 6
N_SSD_STEP_IN = 15


def _ssd_step_rows(xt_ref, bcr_ref, zt_ref, dt_ref, cxt_ref, cbc_ref,
                   wxt_ref, wbc_ref, bxt_ref, bbc_ref, dtb_ref, al_ref, dsk_ref, nwt_ref, h_ref,
                   o_ref, ho_ref):
    wxt, wbc, bxt, bbc = wxt_ref[...], wbc_ref[...], bxt_ref[...], bbc_ref[...]
    a = -jnp.exp(al_ref[...])
    dtb, dsk, nwt = dtb_ref[...], dsk_ref[...], nwt_ref[...]
    lane = lax.broadcasted_iota(jnp.int32, (1, H_B), 1)
    lane_group = _idiv(lane, HPG)
    for b in range(xt_ref.shape[0]):
        xt = _conv_step(cxt_ref, b, xt_ref[b], wxt, bxt)
        bcv = _conv_step(cbc_ref, b, bcr_ref[b], wbc, bbc)
        dt = _softplus(dt_ref[b] + dtb)
        d_a = jnp.exp(dt * a)
        xdt = xt * dt
        yt = jnp.zeros((B_P, H_B), F32)
        for hh in range(H_B):
            g = hh // HPG
            h_new = d_a[:, hh:hh + 1] * h_ref[b, hh] + xdt[:, hh:hh + 1] * bcv[g:g + 1, :]
            ho_ref[b, hh] = h_new
            y_col = jnp.sum(h_new * bcv[B_G + g:B_G + g + 1, :], axis=1, keepdims=True)
            yt = jnp.where(lane == hh, y_col, yt)
        yz = (yt + dsk * xt) * zt_ref[b]
        col_sq = jnp.sum(yz * yz, axis=0, keepdims=True)
        ms = jnp.zeros((1, H_B), F32)
        for g in range(B_G):
            total = jnp.sum(col_sq[:, g * HPG:(g + 1) * HPG], axis=1, keepdims=True)
            ms = jnp.where(lane_group == g, total * (1.0 / GW), ms)
        o_ref[b] = (yz * lax.rsqrt(ms + EPS) * nwt).astype(o_ref.dtype)


def _ssd_step_specs(operands, state, bt, step_index):
    lead_ops, const_ops = operands[:N_SSD_STEP_LEAD], operands[N_SSD_STEP_LEAD:]
    lead = lambda a: pl.BlockSpec((bt,) + a.shape[1:], lambda *g: (step_index(*g),) + (0,) * (a.ndim - 1))
    full = lambda a: pl.BlockSpec(a.shape, lambda *g: (0,) * a.ndim)
    nb = state.shape[0]
    in_specs = [lead(a) for a in lead_ops] + [full(a) for a in const_ops] + [lead(state)]
    out_specs = [pl.BlockSpec((bt, B_P, H_B), lambda *g: (step_index(*g), 0, 0)), lead(state)]
    out_shape = [jax.ShapeDtypeStruct((nb, B_P, H_B), F32), jax.ShapeDtypeStruct(state.shape, F32)]
    return in_specs, out_specs, out_shape


def _outproj_kernel(oa_ref, ob_ref, wa_ref, wb_ref, x_ref, gt_ref, gp_ref, *rest):
    y_ref = rest[-3] if len(rest) > 1 else rest[0]
    mix = _dot(oa_ref[...], wa_ref[...]) + _dot(ob_ref[...], wb_ref[...])
    y_ref[...] = x_ref[...] + gt_ref[...] * (_rms(mix) * gp_ref[...])
    if len(rest) > 1:
        q_ref, f_ref, v_ref, g_ref, nw_ref, s_ref, _, o_ref, so_ref = rest
        _hgrn_step_rows(q_ref, f_ref, v_ref, g_ref, nw_ref, s_ref, o_ref, so_ref)


def _outproj(o_a, o_b, w_out, x, mod, per_row, rows_per_seq, g_post, tm, hgrn_step=None):
    m = x.shape[0]
    n_steps = m // tm
    row = lambda w: pl.BlockSpec((tm, w), lambda i, j: (i, 0))
    const = lambda shape, idx: pl.BlockSpec(shape, lambda i, j: idx, pipeline_mode=pl.Buffered(1))
    in_specs = [row(D_A), row(D_B),
                const((D_A, D_MODEL), (0, 0)), const((D_B, D_MODEL), (1, 0)),
                row(D_MODEL),
                _mod_spec(per_row, rows_per_seq, tm, 2),
                pl.BlockSpec((1, D_MODEL), lambda i, j: (0, 0))]
    out_specs = [row(D_MODEL)]
    out_shape = [jax.ShapeDtypeStruct((m, D_MODEL), F32)]
    args = [o_a, o_b, w_out, w_out, x, mod, g_post]
    if hgrn_step is not None:
        q, f, v, g, nw, state = hgrn_step
        nb = q.shape[0]
        assert nb % n_steps == 0
        bt = nb // n_steps
        rw = pl.BlockSpec((bt, H_A, HA_D), lambda i, j: (i, 0, 0))
        st = pl.BlockSpec((bt, H_A, HA_D, HA_D), lambda i, j: (i, 0, 0, 0))
        in_specs += [rw, rw, rw, rw, pl.BlockSpec((H_A, HA_D), lambda i, j: (0, 0)), st]
        out_specs += [rw, st]
        out_shape += [jax.ShapeDtypeStruct((nb, H_A, HA_D), BF16), jax.ShapeDtypeStruct(state.shape, F32)]
        args += [q, f, v, g, nw, state]
    out = pl.pallas_call(
        _outproj_kernel,
        grid=(n_steps, 1),
        in_specs=in_specs,
        out_specs=out_specs,
        out_shape=out_shape,
        compiler_params=_params("parallel", "arbitrary"),
        name="out_proj",
    )(*args)
    return out if hgrn_step is not None else out[0]


def _mlp_kernel(x_ref, sh_ref, sc_ref, gt_ref, gpre_ref, gpost_ref, wu_ref, wd_ref, *rest):
    ssd = len(rest) > 2
    if ssd:
        ssd_in, (y_ref, so_ref, sh_out_ref, h_scr) = rest[:N_SSD_STEP_IN], rest[N_SSD_STEP_IN:]
    else:
        y_ref, h_scr = rest
    f = pl.program_id(1)

    @pl.when(f == 0)
    def _():
        h = _rms(x_ref[...]) * gpre_ref[...] * (1.0 + sc_ref[...]) + sh_ref[...]
        h_scr[...] = h.astype(BF16)

    def tile_product():
        if ssd:
            _ssd_step_rows(*ssd_in, so_ref, sh_out_ref)
        u = jnp.maximum(jnp.dot(h_scr[...], wu_ref[...].astype(BF16), preferred_element_type=F32), 0.0)
        return jnp.dot((u * u).astype(BF16), wd_ref[...].astype(BF16), preferred_element_type=F32)

    @pl.when(f == 0)
    def _():
        y_ref[...] = tile_product()

    @pl.when(f > 0)
    def _():
        y_ref[...] += tile_product()

    @pl.when(f == pl.num_programs(1) - 1)
    def _():
        y_ref[...] = x_ref[...] + gt_ref[...] * (_rms(y_ref[...]) * gpost_ref[...])


def _mlp(x, mod, per_row, rows_per_seq, g_pre, g_post, w_up, w_down, tm, tf, ssd_step=None):
    m = x.shape[0]
    n_f = D_FF // tf
    n_steps = (m // tm) * n_f
    row = pl.BlockSpec((tm, D_MODEL), lambda i, j: (i, 0))
    vec = pl.BlockSpec((1, D_MODEL), lambda i, j: (0, 0))
    in_specs = [pl.BlockSpec((tm, D_MODEL), lambda i, j: (i, 0), pipeline_mode=pl.Buffered(1)),
                _mod_spec(per_row, rows_per_seq, tm, 3),
                _mod_spec(per_row, rows_per_seq, tm, 4),
                _mod_spec(per_row, rows_per_seq, tm, 5),
                vec, vec,
                pl.BlockSpec((D_MODEL, tf), lambda i, j: (0, j)),
                pl.BlockSpec((tf, D_MODEL), lambda i, j: (j, 0))]
    out_specs = [row]
    out_shape = [jax.ShapeDtypeStruct((m, D_MODEL), F32)]
    args = [x, mod, mod, mod, g_pre, g_post, w_up, w_down]
    if ssd_step is not None:
        out_specs = [pl.BlockSpec((tm, D_MODEL), lambda i, j: (i, 0), pipeline_mode=pl.Buffered(1))]
        operands, state = ssd_step
        nb = state.shape[0]
        assert nb % n_steps == 0
        s_in, s_out, s_shape = _ssd_step_specs(operands, state, nb // n_steps, lambda i, j: i * n_f + j)
        in_specs += s_in
        out_specs += s_out
        out_shape += s_shape
        args += list(operands) + [state]
    out = pl.pallas_call(
        _mlp_kernel,
        grid=(m // tm, n_f),
        in_specs=in_specs,
        out_specs=out_specs,
        out_shape=out_shape,
        scratch_shapes=[pltpu.VMEM((tm, D_MODEL), BF16)],
        compiler_params=_params("parallel", "arbitrary"),
        name="mlp",
    )(*args)
    return out if ssd_step is not None else out[0]


def kernel(x_prompt, x_sample, c_prompt, c_sample, state_hgrn, state_ssm, state_conv, w_ada, b_ada, norm_pre_mix, norm_post_mix, norm_pre_mlp, norm_post_mlp, w_in, hgrn_lb_logits, hgrn_norm, conv_w, conv_b, dt_bias, a_log, d_skip, ssd_norm, w_out, w_up, w_down):
    n_seq, t, _ = x_prompt.shape
    n_dec = x_sample.shape[0]
    assert x_sample.shape[1] == 1 and t % CHUNK_A == 0 and t % CHUNK_B == 0
    assert w_ada.shape[0] == 1, "one layer"
    l = 0
    tiles = _tiles(t, n_dec)

    w_in_t = w_in[l].T
    w_out_b = w_out[l].astype(BF16)
    g_pre_mix, g_post_mix = norm_pre_mix[l][None], norm_post_mix[l][None]
    g_pre_mlp, g_post_mlp = norm_pre_mlp[l][None], norm_post_mlp[l][None]
    cw, cb = conv_w[l], conv_b[l][None]
    hn = hgrn_norm[l][None]
    sn = ssd_norm[l][None]

    n_c = n_seq + n_dec
    pad = (-n_c) % SUBLANE
    c_all = jnp.concatenate([c_sample, c_prompt, jnp.zeros((pad, D_MODEL), F32)], axis=0)
    mod = _ada(c_all, w_ada[l], b_ada[l][None], tiles["ada_n"])
    mod_p = mod[n_dec:n_c].reshape(n_seq, 1, N_MOD * D_MODEL)
    mod_s = mod

    xp = x_prompt.reshape(n_seq * t, D_MODEL)
    xs = x_sample.reshape(n_dec, D_MODEL)
    tm, tn = tiles["inproj"]
    proj_p, dt_p, proj_s, dt_s = _inproj(xp, mod_p, t, xs, mod_s, g_pre_mix, w_in_t, hgrn_lb_logits, tm, tn,
                                         functools.partial(_hgrn_interleave, tn=tn))
    o_a, s_hgrn_p = _hgrn_prompt(proj_p, hn, n_seq, t)

    n_chunks = t // CHUNK_B
    rep = SUBLANE // HPG
    dt_rows = dt_p.reshape(n_seq, n_chunks, CHUNK_B, B_G, 1, HPG).transpose(3, 0, 1, 4, 5, 2)
    dt_rows = jnp.broadcast_to(dt_rows, (B_G, n_seq, n_chunks, rep, HPG, CHUNK_B))
    dt_rows = dt_rows.reshape(B_G, n_seq, n_chunks * SUBLANE, CHUNK_B)
    prow = lambda p: jnp.tile(p.reshape(B_G, 1, HPG), (1, n_chunks * rep, 1)).reshape(B_G, n_chunks * SUBLANE, 1)
    cpar = _ssd_channel_params(cw, cb, sn, d_skip[l], SSD_GROUPS_PER_STEP)
    dpar = jnp.concatenate([prow(dt_bias[l]), prow(a_log[l])], axis=2)
    o_b, s_ssm_p = _ssd_prompt(proj_p, dt_rows, cpar, dpar, n_seq, t)
    heads = lambda off: proj_s[:, off:off + D_A].reshape(n_dec, H_A, HA_D)
    x1, o_a_s, s_hgrn_s = _outproj(
        o_a, o_b, w_out_b, xp, mod_p, False, t, g_post_mix, tiles["outproj_p"],
        hgrn_step=(heads(OFF_Q), heads(OFF_F), heads(OFF_I), heads(OFF_G), hn.reshape(H_A, HA_D),
                   state_hgrn[l]))
    conv_p = proj_p.reshape(n_seq, t, D_MAIN)[:, t - (CONV_W - 1):, OFF_X:]

    to_t = lambda a, h, d: a.reshape(a.shape[:-1] + (h, d)).swapaxes(-1, -2)
    xbc_s = proj_s[:, OFF_X:]
    cst = state_conv[l]
    xt = to_t(xbc_s[:, :D_B], H_B, B_P)
    zt = to_t(proj_s[:, OFF_Z:OFF_Z + D_B], H_B, B_P)
    bcr = xbc_s[:, D_B:].reshape(n_dec, 2 * B_G, B_N)
    cxt = to_t(cst[:, :, :D_B], H_B, B_P)
    cbc = cst[:, :, D_B:].reshape(n_dec, CONV_W - 1, 2 * B_G, B_N)
    wxt = to_t(cw[:, :D_B], H_B, B_P)
    wbc = cw[:, D_B:].reshape(CONV_W, 2 * B_G, B_N)
    bxt = to_t(cb[:, :D_B], H_B, B_P)[0]
    bbc = cb[0, D_B:].reshape(2 * B_G, B_N)
    ssd_operands = (xt, bcr, zt, dt_s[:, None, :], cxt, cbc,
                    wxt, wbc, bxt, bbc, dt_bias[l][None], a_log[l][None], d_skip[l][None],
                    to_t(sn, H_B, B_P)[0])
    assert len(ssd_operands) + 1 == N_SSD_STEP_IN
    y_p, o_b_st, s_ssm_s = _mlp(x1, mod_p, False, t, g_pre_mlp, g_post_mlp, w_up[l], w_down[l],
                                *tiles["mlp_p"], ssd_step=(ssd_operands, state_ssm[l]))
    o_b_s = o_b_st.swapaxes(1, 2).reshape(n_dec, D_B)

    x1_s = _outproj(o_a_s.reshape(n_dec, D_A), o_b_s, w_out_b, xs, mod_s, True, 1,
                    g_post_mix, tiles["outproj_s"])
    y_s = _mlp(x1_s, mod_s, True, 1, g_pre_mlp, g_post_mlp, w_up[l], w_down[l], *tiles["mlp_s"])
    conv_s = jnp.concatenate([cst[:, 1:], xbc_s[:, None, :]], axis=1)

    return (y_p.reshape(n_seq, t, D_MODEL), y_s.reshape(n_dec, 1, D_MODEL),
            s_hgrn_p[None], s_ssm_p[None], conv_p[None],
            s_hgrn_s[None], s_ssm_s[None], conv_s[None])
```

```python
import functools

import jax
import jax.numpy as jnp
from jax import lax
from jax.experimental import pallas as pl
from jax.experimental.pallas import tpu as pltpu

F32 = jnp.float32
BF16 = jnp.bfloat16

D_MODEL = 2048
D_A = 2048
HA_D = 128
H_A = D_A // HA_D
D_B = 2048
B_P = 64
H_B = D_B // B_P
B_G = 8
HPG = H_B // B_G
GW = HPG * B_P
B_N = 128
CONV_W = 4
CONV_DIM = D_B + 2 * B_G * B_N
D_MAIN = 4 * D_A + D_B + CONV_DIM
D_FF = 4 * D_MODEL
N_MOD = 6
EPS = 1e-6

OFF_Q, OFF_F, OFF_I, OFF_G = 0, D_A, 2 * D_A, 3 * D_A
OFF_Z = 4 * D_A
OFF_X = OFF_Z + D_B
OFF_B = OFF_X + D_B
OFF_C = OFF_B + B_G * B_N
REGION_SILU = (0, 3, 4)
REGION_FORGET = 1

LANE = 128
SUBLANE = 8
VMEM_LIMIT = 56 * 1024 * 1024

CHUNK_A = 128
SUB_A = 16
CHUNK_B = 128
HGRN_HEADS_PER_STEP = 4
SSD_GROUPS_PER_STEP = 2


def _tiles(t, n_dec):
    return dict(
        inproj=(_pick_tile(t, 2048), HGRN_HEADS_PER_STEP * HA_D),
        outproj_p=_pick_tile(t, 256), outproj_s=n_dec,
        mlp_p=(_pick_tile(t, 1024), 512), mlp_s=(n_dec, 1024),
        ada_n=1024)


def _pick_tile(m, target):
    t = min(m, target)
    while m % t:
        t //= 2
    return t


def _silu(x):
    hx = 0.5 * x
    return hx * jnp.tanh(hx) + hx


def _softplus(x):
    return jnp.maximum(x, 0.0) + jnp.log1p(jnp.exp(-jnp.abs(x)))


def _idiv(x, d):
    assert d & (d - 1) == 0
    return jnp.right_shift(x, d.bit_length() - 1)


def _rms(x):
    return x * lax.rsqrt(jnp.mean(x * x, axis=-1, keepdims=True) + EPS)


def _cumsum_rows(x, n, row):
    s = 1
    while s < n:
        x = x + jnp.where(row >= s, pltpu.roll(x, s, 0), 0.0)
        s *= 2
    return x


def _cumsum_lanes(x, n, lane):
    s = 1
    while s < n:
        x = x + jnp.where(lane >= s, pltpu.roll(x, s, 1), 0.0)
        s *= 2
    return x


def _dot(a, b):
    return jnp.dot(a.astype(BF16), b.astype(BF16), preferred_element_type=F32)


def _dot_nt(a, b):
    return lax.dot_general(a.astype(BF16), b.astype(BF16), (((1,), (1,)), ((), ())),
                           preferred_element_type=F32)


def _exact_terms(x):
    hi = x.astype(BF16).astype(F32)
    mid = (x - hi).astype(BF16).astype(F32)
    lo = x - hi - mid
    return jnp.concatenate([hi, mid, lo, jnp.zeros_like(hi)], axis=0).astype(BF16)


def _transpose_bcast(x, sel):
    return lax.dot_general(_exact_terms(x), sel, (((0,), (0,)), ((), ())),
                           preferred_element_type=F32)


def _params(*sem):
    return pltpu.CompilerParams(dimension_semantics=sem, vmem_limit_bytes=VMEM_LIMIT)


def _ada_kernel(c_ref, w_ref, b_ref, o_ref):
    o_ref[...] = _dot(_silu(c_ref[...]), w_ref[...]) + b_ref[...]


def _ada(c_all, w_ada, b_ada, tn):
    m = c_all.shape[0]
    n = w_ada.shape[1]
    return pl.pallas_call(
        _ada_kernel,
        grid=(n // tn,),
        in_specs=[pl.BlockSpec((m, D_MODEL), lambda j: (0, 0)),
                  pl.BlockSpec((D_MODEL, tn), lambda j: (0, j)),
                  pl.BlockSpec((1, tn), lambda j: (0, j))],
        out_specs=pl.BlockSpec((m, tn), lambda j: (0, j)),
        out_shape=jax.ShapeDtypeStruct((m, n), F32),
        compiler_params=_params("parallel"),
        name="ada_mod",
    )(c_all, w_ada, b_ada)


def _mod_spec(per_row, rows_per_seq, tm, col):
    if per_row:
        return pl.BlockSpec((tm, D_MODEL), lambda i, j: (i, col))
    tiles_per_seq = rows_per_seq // tm
    return pl.BlockSpec((None, 1, D_MODEL), lambda i, j: (i // tiles_per_seq, 0, col))


def _lower_bound(logits):
    e = jnp.exp(logits - jnp.max(logits, axis=0, keepdims=True))
    return e[0:1] / jnp.sum(e, axis=0, keepdims=True)


def _inproj_kernel(x_ref, sh_ref, sc_ref, g_ref, w_ref, wdt_ref, lbl_ref, xs_ref, shs_ref, scs_ref,
                   o_ref, odt_ref, os_ref, odts_ref, h_scr, *, tn):
    i, j = pl.program_id(0), pl.program_id(1)
    first_rows = i == 0
    tm = x_ref.shape[0]

    def normed(x, sh, sc):
        return (_rms(x) * g_ref[...] * (1.0 + sc) + sh).astype(BF16)

    @pl.when(j == 0)
    def _():
        hb = normed(x_ref[...], sh_ref[...], sc_ref[...])
        h_scr[0:tm, :] = hb
        odt_ref[...] = _dot_nt(hb, wdt_ref[...])

    @pl.when(jnp.logical_and(first_rows, j == 0))
    def _():
        hb = normed(xs_ref[...], shs_ref[...], scs_ref[...])
        h_scr[tm:, :] = hb
        odts_ref[...] = _dot_nt(hb, wdt_ref[...])

    region = (j * tn) // D_A
    is_silu = functools.reduce(jnp.logical_or, [region == r for r in REGION_SILU])

    def both_groups(act):
        @pl.when(first_rows)
        def _():
            res = act(_dot_nt(h_scr[...], w_ref[...]))
            o_ref[...] = res[0:tm]
            os_ref[...] = res[tm:]

        @pl.when(jnp.logical_not(first_rows))
        def _():
            o_ref[...] = act(_dot_nt(h_scr[0:tm, :], w_ref[...]))

    def log_forget(p):
        lb = _lower_bound(lbl_ref[...])
        half = 0.5 * (1.0 - lb)
        return jnp.log((lb + half) + half * jnp.tanh(0.5 * p))

    pl.when(is_silu)(lambda: both_groups(_silu))
    pl.when(region == REGION_FORGET)(lambda: both_groups(log_forget))
    pl.when(jnp.logical_not(jnp.logical_or(is_silu, region == REGION_FORGET)))(lambda: both_groups(lambda p: p))


def _inproj(x, mod, rows_per_seq, xs, mod_s, g_pre, w_t, lb_logits, tm, tn, out_tile):
    m, ns = x.shape[0], xs.shape[0]
    assert D_MAIN % tn == 0 and D_MAIN % H_B == 0 and D_A % tn == 0
    f_tiles = D_A // tn
    n_j = D_MAIN // tn
    once = lambda shape, idx: pl.BlockSpec(shape, lambda i, j: idx, pipeline_mode=pl.Buffered(1))
    return pl.pallas_call(
        functools.partial(_inproj_kernel, tn=tn),
        grid=(m // tm, n_j),
        in_specs=[pl.BlockSpec((tm, D_MODEL), lambda i, j: (i, 0), pipeline_mode=pl.Buffered(1)),
                  _mod_spec(False, rows_per_seq, tm, 0),
                  _mod_spec(False, rows_per_seq, tm, 1),
                  pl.BlockSpec((1, D_MODEL), lambda i, j: (0, 0)),
                  pl.BlockSpec((tn, D_MODEL), lambda i, j: (j, 0)),
                  pl.BlockSpec((H_B, D_MODEL), lambda i, j: (D_MAIN // H_B, 0)),
                  pl.BlockSpec((lb_logits.shape[0], tn),
                               lambda i, j: (0, jnp.clip(j - REGION_FORGET * f_tiles, 0, f_tiles - 1))),
                  once((ns, D_MODEL), (0, 0)), once((ns, D_MODEL), (0, 0)), once((ns, D_MODEL), (0, 1))],
        out_specs=[pl.BlockSpec((tm, tn), lambda i, j: (i, out_tile(j))),
                   pl.BlockSpec((tm, H_B), lambda i, j: (i, 0)),
                   pl.BlockSpec((ns, tn), lambda i, j: (0, jnp.where(i == 0, j, n_j - 1))),
                   pl.BlockSpec((ns, H_B), lambda i, j: (0, 0))],
        out_shape=[jax.ShapeDtypeStruct((m, D_MAIN), F32),
                   jax.ShapeDtypeStruct((m, H_B), F32),
                   jax.ShapeDtypeStruct((ns, D_MAIN), F32),
                   jax.ShapeDtypeStruct((ns, H_B), F32)],
        scratch_shapes=[pltpu.VMEM((tm + ns, D_MODEL), BF16)],
        compiler_params=_params("arbitrary", "arbitrary"),
        name="in_proj",
    )(x, mod, mod, g_pre, w_t, w_t, lb_logits, xs, mod_s, mod_s)


def _hgrn_interleave(j, tn):
    n_hg = D_A // tn
    return jnp.where(j < 4 * n_hg, 4 * (j % n_hg) + j // n_hg, j)


def _hgrn_scores(qs, lf, row):
    c_len = CHUNK_A
    n_sub = c_len // SUB_A
    k = 1.0 - jnp.exp(lf)
    b = _cumsum_rows(lf, SUB_A, row & (SUB_A - 1))
    k_acc = []
    blocks, q_dec = [], []
    start = jnp.zeros((1, HA_D), F32)
    for i in range(n_sub):
        lo = i * SUB_A
        hi = lo + SUB_A
        b_i = b[lo:hi, :]
        b_sub = b[hi - 1:hi, :]
        k_i = k[lo:hi, :]
        qg = qs[lo:hi, :] * jnp.exp(b_i)
        k_diag = k_i * jnp.exp(-b_i)
        rhs = jnp.concatenate([p.astype(BF16) for p in k_acc] + [k_diag.astype(BF16)], axis=0)
        a_i = lax.dot_general(qg.astype(BF16), rhs, (((1,), (1,)), ((), ())),
                              preferred_element_type=F32)
        causal = (lax.broadcasted_iota(jnp.int32, (SUB_A, hi), 1)
                  <= lax.broadcasted_iota(jnp.int32, (SUB_A, hi), 0) + lo)
        a_i = jnp.where(causal, a_i, 0.0)
        blocks.append(jnp.pad(a_i, ((0, 0), (0, c_len - hi))).astype(BF16))
        q_dec.append((qg * jnp.exp(start)).astype(BF16))
        sub_decay = jnp.exp(b_sub)
        k_acc = [p * sub_decay for p in k_acc] + [k_i * jnp.exp(b_sub - b_i)]
        start = start + b_sub
    scores = jnp.concatenate(blocks, axis=0)
    k_end = jnp.concatenate([p.astype(BF16) for p in k_acc], axis=0)
    chunk_decay = jnp.exp(start)
    return scores, jnp.concatenate(q_dec, axis=0), k_end, chunk_decay


def _hgrn_apply(pend, v, gs, nw, st):
    scores, q_dec, k_end, chunk_decay = pend
    vb = v.astype(BF16)
    o = (jnp.dot(scores, vb, preferred_element_type=F32)
         + lax.dot_general(q_dec, st.astype(BF16), (((1,), (1,)), ((), ())),
                           preferred_element_type=F32))
    st = chunk_decay * st + lax.dot_general(vb, k_end, (((0,), (0,)), ((), ())),
                                            preferred_element_type=F32)
    return (_rms(o) * nw * gs).astype(BF16), st


def _hgrn_prompt_kernel(p_ref, nw_ref, wc_ref, o_ref, s_ref, wco_ref, *, n_chunks, n_heads):
    wco_ref[...] = wc_ref[...].astype(BF16)
    c_len = CHUNK_A
    w = n_heads * HA_D
    nw = nw_ref[...]
    row = lax.broadcasted_iota(jnp.int32, (c_len, HA_D), 0)
    heads = [slice(h * HA_D, (h + 1) * HA_D) for h in range(n_heads)]

    def part(k, sl, hs):
        return p_ref[sl, k * w + hs.start:k * w + hs.stop]

    def rows(c):
        return pl.ds(pl.multiple_of(c * c_len, c_len), c_len)

    def scores_of(c):
        sl = rows(c)
        return tuple(_hgrn_scores(part(0, sl, hs), part(1, sl, hs), row) for hs in heads)

    def apply_to(c, pends, sts):
        sl = rows(c)
        new = []
        for hs, pend, st in zip(heads, pends, sts):
            out, st = _hgrn_apply(pend, part(2, sl, hs), part(3, sl, hs), nw[:, hs], st)
            o_ref[sl, hs] = out
            new.append(st)
        return tuple(new)

    def step(c, carry):
        pends, sts = carry
        sts = apply_to(c - 1, pends, sts)
        return scores_of(c), sts

    zeros = tuple(jnp.zeros((HA_D, HA_D), F32) for _ in range(n_heads))
    pends, sts = lax.fori_loop(1, n_chunks, step, (scores_of(0), zeros))
    sts = apply_to(n_chunks - 1, pends, sts)
    for h in range(n_heads):
        s_ref[h] = sts[h].T


def _hgrn_prompt(proj, hgrn_norm, w_cast, n_seq, t):
    m = proj.shape[0]
    nh = HGRN_HEADS_PER_STEP
    w = nh * HA_D
    n_hg = H_A // nh
    rows = w_cast.shape[0] // (n_seq * n_hg)
    assert rows * n_seq * n_hg == w_cast.shape[0] and rows % (2 * SUBLANE) == 0
    wc_spec = pl.BlockSpec((rows, w_cast.shape[1]), lambda b, h: (b * n_hg + h, 0))
    return pl.pallas_call(
        functools.partial(_hgrn_prompt_kernel, n_chunks=t // CHUNK_A, n_heads=nh),
        grid=(n_seq, n_hg),
        in_specs=[pl.BlockSpec((t, 4 * w), lambda b, h: (b, h)),
                  pl.BlockSpec((1, w), lambda b, h: (0, h)),
                  wc_spec],
        out_specs=[pl.BlockSpec((t, w), lambda b, h: (b, h)),
                   pl.BlockSpec((None, nh, HA_D, HA_D), lambda b, h: (b, h, 0, 0)),
                   wc_spec],
        out_shape=[jax.ShapeDtypeStruct((m, D_A), BF16),
                   jax.ShapeDtypeStruct((n_seq, H_A, HA_D, HA_D), F32),
                   jax.ShapeDtypeStruct(w_cast.shape, BF16)],
        compiler_params=_params("parallel", "parallel"),
        name="hgrn_prompt",
    )(proj, hgrn_norm, w_cast)


def _conv_chunk(tail, u, w, bias, n):
    full = jnp.concatenate([tail, u], axis=0)
    out = bias + w[CONV_W - 1:CONV_W, :] * u
    for i in range(CONV_W - 1):
        shifted = pltpu.roll(full, CONV_W - 1 - i, 0)[SUBLANE:SUBLANE + n, :]
        out = out + w[i:i + 1, :] * shifted
    return _silu(out)


def _ssd_prompt_kernel(z_ref, x_ref, b_ref, c_ref, dtr_ref, cpar_ref, dpar_ref,
                       o_ref, h_ref, row_scr, *, n_chunks, n_groups):
    n = CHUNK_B
    assert n == LANE
    tri = (lax.broadcasted_iota(jnp.int32, (n, n), 0)
           >= lax.broadcasted_iota(jnp.int32, (n, n), 1))
    lane_head = _idiv(lax.broadcasted_iota(jnp.int32, (1, GW), 1), B_P)
    row_head = _idiv(lax.broadcasted_iota(jnp.int32, (GW, 1), 0), B_P)
    nr = n_chunks * SUBLANE
    lane_nr = lax.broadcasted_iota(jnp.int32, (nr, n), 1)
    is_dt = (lax.broadcasted_iota(jnp.int32, (nr, n), 0) & (SUBLANE - 1)) < HPG
    is_dt8 = lax.broadcasted_iota(jnp.int32, (SUBLANE, n), 0) < HPG
    n_src = 2 * SUBLANE
    sel_shape = (4 * n_src, HPG * n + 2 * GW)
    sel_k = lax.broadcasted_iota(jnp.int32, sel_shape, 0)
    sel_l = lax.broadcasted_iota(jnp.int32, sel_shape, 1)
    want = jnp.where(sel_l < HPG * n, HPG + _idiv(sel_l, n),
                     2 * HPG + _idiv(sel_l - HPG * n, B_P))
    bcast_sel = jnp.where(((sel_k & (n_src - 1)) == want) & (sel_k < 3 * n_src), 1.0, 0.0).astype(BF16)

    groups = []
    for gi in range(n_groups):
        xs = slice(gi * GW, (gi + 1) * GW)
        bs = slice(gi * B_N, (gi + 1) * B_N)
        dpar = dpar_ref[gi]
        dt_all = _softplus(dtr_ref[gi] + dpar[:, 0:1])
        cum_all = _cumsum_lanes(dt_all * -jnp.exp(dpar[:, 1:2]), n, lane_nr)
        row_scr[gi] = jnp.where(is_dt, dt_all, cum_all).reshape(n_chunks, SUBLANE, n)
        bl = slice(n_groups * GW + gi * B_N, n_groups * GW + (gi + 1) * B_N)
        cl = slice(n_groups * (GW + B_N) + gi * B_N, n_groups * (GW + B_N) + (gi + 1) * B_N)
        taps, bias = slice(0, CONV_W), slice(CONV_W, CONV_W + 1)
        groups.append(dict(gi=gi, xs=xs, bs=bs, dskip=cpar_ref[6:7, xs], nw=cpar_ref[5:6, xs],
                           wx=cpar_ref[taps, xs], wb=cpar_ref[taps, bl], wc=cpar_ref[taps, cl],
                           bx=cpar_ref[bias, xs], bb=cpar_ref[bias, bl], bc=cpar_ref[bias, cl]))

    def rows(c):
        return pl.ds(pl.multiple_of(c * n, n), n)

    def scan_free(c, grp):
        xs, bs = grp["xs"], grp["bs"]
        sl = rows(c)
        if isinstance(c, int) and c == 0:
            tail = lambda ref, lanes: jnp.zeros((SUBLANE, lanes.stop - lanes.start), F32)
        else:
            prev = pl.ds(pl.multiple_of(c * n - SUBLANE, SUBLANE), SUBLANE)
            tail = lambda ref, lanes: ref[prev, lanes]
        xc = _conv_chunk(tail(x_ref, xs), x_ref[sl, xs], grp["wx"], grp["bx"], n)
        bcv = _conv_chunk(tail(b_ref, bs), b_ref[sl, bs], grp["wb"], grp["bb"], n)
        ccv = _conv_chunk(tail(c_ref, bs), c_ref[sl, bs], grp["wc"], grp["bc"], n)
        r8 = row_scr[grp["gi"], c]
        swapped = pltpu.roll(r8, HPG, 0)
        cum8 = jnp.where(is_dt8, swapped, r8)
        dt8 = jnp.where(is_dt8, r8, swapped)
        e8 = jnp.where(is_dt8, jnp.exp(cum8), jnp.exp(cum8[:, n - 1:n] - cum8) * dt8)
        cols = _transpose_bcast(jnp.concatenate([r8, e8], axis=0), bcast_sel)
        e_cum = cols[:, HPG * n:HPG * n + GW]
        w_end = cols[:, HPG * n + GW:]
        g = _dot_nt(ccv, bcv)
        xb = xc.astype(BF16)
        s_all, x_all = [], []
        for j in range(HPG):
            cc = cols[:, j * n:(j + 1) * n]
            seg = cc - r8[HPG + j:HPG + j + 1, :]
            s = g * jnp.exp(jnp.where(tri, seg, -jnp.inf)) * r8[j:j + 1, :]
            s_all.append(s.astype(BF16))
            x_all.append(jnp.where(lane_head == j, xb, jnp.zeros_like(xb)))
        y = grp["dskip"] * xc + jnp.dot(jnp.concatenate(s_all, axis=1), jnp.concatenate(x_all, axis=0),
                                        preferred_element_type=F32)
        return y, e_cum, ccv.astype(BF16), (xc * w_end).astype(BF16), bcv.astype(BF16), r8

    def apply_state(c, grp, pend, h):
        xs = grp["xs"]
        y, e_cum, ccb, xwb, bcb, r8 = pend
        sl = rows(c)
        ch = lax.dot_general(ccb, h.astype(BF16), (((1,), (1,)), ((), ())),
                             preferred_element_type=F32)
        yz = (y + ch * e_cum) * z_ref[sl, xs]
        o_ref[sl, xs] = (_rms(yz) * grp["nw"]).astype(o_ref.dtype)
        h_dec = jnp.zeros((GW, 1), F32)
        for j in range(HPG):
            h_dec = jnp.where(row_head == j, jnp.exp(r8[HPG + j:HPG + j + 1, n - 1:n]), h_dec)
        return h_dec * h + lax.dot_general(xwb, bcb, (((0,), (0,)), ((), ())),
                                           preferred_element_type=F32)

    def step(c, carry):
        pends, hs = carry
        hs = tuple(apply_state(c - 1, grp, pend, h) for grp, pend, h in zip(groups, pends, hs))
        return tuple(scan_free(c, grp) for grp in groups), hs

    init = (tuple(scan_free(0, grp) for grp in groups),
            tuple(jnp.zeros((GW, B_N), F32) for _ in groups))
    pends, hs = lax.fori_loop(1, n_chunks, step, init)
    for gi, (grp, pend, h) in enumerate(zip(groups, pends, hs)):
        h = apply_state(n_chunks - 1, grp, pend, h)
        h_ref[gi * HPG:(gi + 1) * HPG] = h.reshape(HPG, B_P, B_N)


def _ssd_channel_params(conv_w, conv_b, ssd_norm, d_skip, ng):
    zeros = jnp.zeros((1, 2 * B_G * B_N), F32)
    rows = jnp.concatenate([
        conv_w, conv_b,
        jnp.concatenate([ssd_norm, zeros], axis=1),
        jnp.concatenate([jnp.repeat(d_skip, B_P)[None], zeros], axis=1),
        jnp.zeros((SUBLANE - CONV_W - 3, CONV_DIM), F32)], axis=0)
    n_steps = B_G // ng
    part = lambda lo, width: rows[:, lo:lo + n_steps * width].reshape(SUBLANE, n_steps, width)
    slab = jnp.concatenate([part(0, ng * GW), part(D_B, ng * B_N), part(D_B + B_G * B_N, ng * B_N)], axis=2)
    return slab.transpose(1, 0, 2)


def _ssd_prompt(proj, dt_rows, cpar, dpar, n_seq, t):
    m = proj.shape[0]
    n_chunks = t // CHUNK_B
    nr = n_chunks * SUBLANE
    ng = SSD_GROUPS_PER_STEP
    xw, bw = ng * GW, ng * B_N
    wide = lambda off: pl.BlockSpec((t, xw), lambda b, g: (b, off // xw + g))
    narrow = lambda off: pl.BlockSpec((t, bw), lambda b, g: (b, off // bw + g))
    return pl.pallas_call(
        functools.partial(_ssd_prompt_kernel, n_chunks=n_chunks, n_groups=ng),
        grid=(n_seq, B_G // ng),
        in_specs=[wide(OFF_Z), wide(OFF_X), narrow(OFF_B), narrow(OFF_C),
                  pl.BlockSpec((ng, None, nr, CHUNK_B), lambda b, g: (g, b, 0, 0)),
                  pl.BlockSpec((None, SUBLANE, xw + 2 * bw), lambda b, g: (g, 0, 0)),
                  pl.BlockSpec((ng, nr, 2), lambda b, g: (g, 0, 0))],
        out_specs=[pl.BlockSpec((t, xw), lambda b, g: (b, g)),
                   pl.BlockSpec((None, ng * HPG, B_P, B_N), lambda b, g: (b, g, 0, 0))],
        out_shape=[jax.ShapeDtypeStruct((m, D_B), BF16),
                   jax.ShapeDtypeStruct((n_seq, H_B, B_P, B_N), F32)],
        scratch_shapes=[pltpu.VMEM((ng, n_chunks, SUBLANE, CHUNK_B), F32)],
        compiler_params=_params("parallel", "parallel"),
        name="ssd_prompt",
    )(proj, proj, proj, proj, dt_rows, cpar, dpar)


def _hgrn_step_rows(q_ref, f_ref, v_ref, g_ref, nw_ref, s_ref, o_ref, so_ref):
    nw = nw_ref[...]
    pad = jnp.zeros((HA_D - H_A, HA_D), F32)
    for b in range(q_ref.shape[0]):
        ft = jnp.concatenate([jnp.exp(f_ref[b]), pad], axis=0).T
        qt = jnp.concatenate([q_ref[b], pad], axis=0).T
        v = v_ref[b]
        rows = []
        for h in range(H_A):
            fc = ft[:, h:h + 1]
            s_new = fc * s_ref[b, h] + (1.0 - fc) * v[h:h + 1, :]
            so_ref[b, h] = s_new
            rows.append(jnp.sum(qt[:, h:h + 1] * s_new, axis=0, keepdims=True))
        o = jnp.concatenate(rows, axis=0)
        o_ref[b] = (_rms(o) * nw * g_ref[b]).astype(o_ref.dtype)


def _conv_step(buf_ref, b, u, w, bias):
    out = bias + w[CONV_W - 1] * u
    for i in range(CONV_W - 1):
        out = out + w[i] * buf_ref[b, i]
    return _silu(out)


N_SSD_STEP_LEAD = 6
N_SSD_STEP_IN = 15


def _ssd_step_rows(xt_ref, bcr_ref, zt_ref, dt_ref, cxt_ref, cbc_ref,
                   wxt_ref, wbc_ref, bxt_ref, bbc_ref, dtb_ref, al_ref, dsk_ref, nwt_ref, h_ref,
                   o_ref, ho_ref):
    wxt, wbc, bxt, bbc = wxt_ref[...], wbc_ref[...], bxt_ref[...], bbc_ref[...]
    a = -jnp.exp(al_ref[...])
    dtb, dsk, nwt = dtb_ref[...], dsk_ref[...], nwt_ref[...]
    lane = lax.broadcasted_iota(jnp.int32, (1, H_B), 1)
    lane_group = _idiv(lane, HPG)
    for b in range(xt_ref.shape[0]):
        xt = _conv_step(cxt_ref, b, xt_ref[b], wxt, bxt)
        bcv = _conv_step(cbc_ref, b, bcr_ref[b], wbc, bbc)
        dt = _softplus(dt_ref[b] + dtb)
        d_a = jnp.exp(dt * a)
        xdt = xt * dt
        yt = jnp.zeros((B_P, H_B), F32)
        for hh in range(H_B):
            g = hh // HPG
            h_new = d_a[:, hh:hh + 1] * h_ref[b, hh] + xdt[:, hh:hh + 1] * bcv[g:g + 1, :]
            ho_ref[b, hh] = h_new
            y_col = jnp.sum(h_new * bcv[B_G + g:B_G + g + 1, :], axis=1, keepdims=True)
            yt = jnp.where(lane == hh, y_col, yt)
        yz = (yt + dsk * xt) * zt_ref[b]
        col_sq = jnp.sum(yz * yz, axis=0, keepdims=True)
        ms = jnp.zeros((1, H_B), F32)
        for g in range(B_G):
            total = jnp.sum(col_sq[:, g * HPG:(g + 1) * HPG], axis=1, keepdims=True)
            ms = jnp.where(lane_group == g, total * (1.0 / GW), ms)
        o_ref[b] = (yz * lax.rsqrt(ms + EPS) * nwt).astype(o_ref.dtype)


def _ssd_step_specs(operands, state, bt, step_index):
    lead_ops, const_ops = operands[:N_SSD_STEP_LEAD], operands[N_SSD_STEP_LEAD:]
    lead = lambda a: pl.BlockSpec((bt,) + a.shape[1:], lambda *g: (step_index(*g),) + (0,) * (a.ndim - 1))
    full = lambda a: pl.BlockSpec(a.shape, lambda *g: (0,) * a.ndim)
    nb = state.shape[0]
    in_specs = [lead(a) for a in lead_ops] + [full(a) for a in const_ops] + [lead(state)]
    out_specs = [pl.BlockSpec((bt, B_P, H_B), lambda *g: (step_index(*g), 0, 0)), lead(state)]
    out_shape = [jax.ShapeDtypeStruct((nb, B_P, H_B), F32), jax.ShapeDtypeStruct(state.shape, F32)]
    return in_specs, out_specs, out_shape


def _outproj_kernel(oa_ref, ob_ref, wa_ref, wb_ref, x_ref, gt_ref, gp_ref, *rest):
    y_ref = rest[-3] if len(rest) > 1 else rest[0]
    mix = _dot(oa_ref[...], wa_ref[...]) + _dot(ob_ref[...], wb_ref[...])
    y_ref[...] = x_ref[...] + gt_ref[...] * (_rms(mix) * gp_ref[...])
    if len(rest) > 1:
        q_ref, f_ref, v_ref, g_ref, nw_ref, s_ref, _, o_ref, so_ref = rest
        _hgrn_step_rows(q_ref, f_ref, v_ref, g_ref, nw_ref, s_ref, o_ref, so_ref)


def _outproj(o_a, o_b, w_out, x, mod, per_row, rows_per_seq, g_post, tm, hgrn_step=None):
    m = x.shape[0]
    n_steps = m // tm
    row = lambda w: pl.BlockSpec((tm, w), lambda i, j: (i, 0))
    const = lambda shape, idx: pl.BlockSpec(shape, lambda i, j: idx, pipeline_mode=pl.Buffered(1))
    in_specs = [row(D_A), row(D_B),
                const((D_A, D_MODEL), (0, 0)), const((D_B, D_MODEL), (1, 0)),
                row(D_MODEL),
                _mod_spec(per_row, rows_per_seq, tm, 2),
                pl.BlockSpec((1, D_MODEL), lambda i, j: (0, 0))]
    out_specs = [row(D_MODEL)]
    out_shape = [jax.ShapeDtypeStruct((m, D_MODEL), F32)]
    args = [o_a, o_b, w_out, w_out, x, mod, g_post]
    if hgrn_step is not None:
        q, f, v, g, nw, state = hgrn_step
        nb = q.shape[0]
        assert nb % n_steps == 0
        bt = nb // n_steps
        rw = pl.BlockSpec((bt, H_A, HA_D), lambda i, j: (i, 0, 0))
        st = pl.BlockSpec((bt, H_A, HA_D, HA_D), lambda i, j: (i, 0, 0, 0))
        in_specs += [rw, rw, rw, rw, pl.BlockSpec((H_A, HA_D), lambda i, j: (0, 0)), st]
        out_specs += [rw, st]
        out_shape += [jax.ShapeDtypeStruct((nb, H_A, HA_D), BF16), jax.ShapeDtypeStruct(state.shape, F32)]
        args += [q, f, v, g, nw, state]
    out = pl.pallas_call(
        _outproj_kernel,
        grid=(n_steps, 1),
        in_specs=in_specs,
        out_specs=out_specs,
        out_shape=out_shape,
        compiler_params=_params("parallel", "arbitrary"),
        name="out_proj",
    )(*args)
    return out if hgrn_step is not None else out[0]


def _mlp_kernel(x_ref, sh_ref, sc_ref, gt_ref, gpre_ref, gpost_ref, wu_ref, wd_ref, *rest):
    ssd = len(rest) > 2
    if ssd:
        ssd_in, (y_ref, so_ref, sh_out_ref, h_scr) = rest[:N_SSD_STEP_IN], rest[N_SSD_STEP_IN:]
    else:
        y_ref, h_scr = rest
    f = pl.program_id(1)

    @pl.when(f == 0)
    def _():
        h = _rms(x_ref[...]) * gpre_ref[...] * (1.0 + sc_ref[...]) + sh_ref[...]
        h_scr[...] = h.astype(BF16)

    def tile_product():
        if ssd:
            _ssd_step_rows(*ssd_in, so_ref, sh_out_ref)
        u = jnp.maximum(jnp.dot(h_scr[...], wu_ref[...].astype(BF16), preferred_element_type=F32), 0.0)
        return jnp.dot((u * u).astype(BF16), wd_ref[...].astype(BF16), preferred_element_type=F32)

    @pl.when(f == 0)
    def _():
        y_ref[...] = tile_product()

    @pl.when(f > 0)
    def _():
        y_ref[...] += tile_product()

    @pl.when(f == pl.num_programs(1) - 1)
    def _():
        y_ref[...] = x_ref[...] + gt_ref[...] * (_rms(y_ref[...]) * gpost_ref[...])


def _mlp(x, mod, per_row, rows_per_seq, g_pre, g_post, w_up, w_down, tm, tf, ssd_step=None):
    m = x.shape[0]
    n_f = D_FF // tf
    n_steps = (m // tm) * n_f
    row = pl.BlockSpec((tm, D_MODEL), lambda i, j: (i, 0))
    vec = pl.BlockSpec((1, D_MODEL), lambda i, j: (0, 0))
    in_specs = [pl.BlockSpec((tm, D_MODEL), lambda i, j: (i, 0), pipeline_mode=pl.Buffered(1)),
                _mod_spec(per_row, rows_per_seq, tm, 3),
                _mod_spec(per_row, rows_per_seq, tm, 4),
                _mod_spec(per_row, rows_per_seq, tm, 5),
                vec, vec,
                pl.BlockSpec((D_MODEL, tf), lambda i, j: (0, j)),
                pl.BlockSpec((tf, D_MODEL), lambda i, j: (j, 0))]
    out_specs = [row]
    out_shape = [jax.ShapeDtypeStruct((m, D_MODEL), F32)]
    args = [x, mod, mod, mod, g_pre, g_post, w_up, w_down]
    if ssd_step is not None:
        out_specs = [pl.BlockSpec((tm, D_MODEL), lambda i, j: (i, 0), pipeline_mode=pl.Buffered(1))]
        operands, state = ssd_step
        nb = state.shape[0]
        assert nb % n_steps == 0
        s_in, s_out, s_shape = _ssd_step_specs(operands, state, nb // n_steps, lambda i, j: i * n_f + j)
        in_specs += s_in
        out_specs += s_out
        out_shape += s_shape
        args += list(operands) + [state]
    out = pl.pallas_call(
        _mlp_kernel,
        grid=(m // tm, n_f),
        in_specs=in_specs,
        out_specs=out_specs,
        out_shape=out_shape,
        scratch_shapes=[pltpu.VMEM((tm, D_MODEL), BF16)],
        compiler_params=_params("parallel", "arbitrary"),
        name="mlp",
    )(*args)
    return out if ssd_step is not None else out[0]


def kernel(x_prompt, x_sample, c_prompt, c_sample, state_hgrn, state_ssm, state_conv, w_ada, b_ada, norm_pre_mix, norm_post_mix, norm_pre_mlp, norm_post_mlp, w_in, hgrn_lb_logits, hgrn_norm, conv_w, conv_b, dt_bias, a_log, d_skip, ssd_norm, w_out, w_up, w_down):
    n_seq, t, _ = x_prompt.shape
    n_dec = x_sample.shape[0]
    assert x_sample.shape[1] == 1 and t % CHUNK_A == 0 and t % CHUNK_B == 0
    assert w_ada.shape[0] == 1, "one layer"
    l = 0
    tiles = _tiles(t, n_dec)

    w_in_t = w_in[l].T
    g_pre_mix, g_post_mix = norm_pre_mix[l][None], norm_post_mix[l][None]
    g_pre_mlp, g_post_mlp = norm_pre_mlp[l][None], norm_post_mlp[l][None]
    cw, cb = conv_w[l], conv_b[l][None]
    hn = hgrn_norm[l][None]
    sn = ssd_norm[l][None]

    n_c = n_seq + n_dec
    pad = (-n_c) % SUBLANE
    c_all = jnp.concatenate([c_sample, c_prompt, jnp.zeros((pad, D_MODEL), F32)], axis=0)
    mod = _ada(c_all, w_ada[l], b_ada[l][None], tiles["ada_n"])
    mod_p = mod[n_dec:n_c].reshape(n_seq, 1, N_MOD * D_MODEL)
    mod_s = mod

    xp = x_prompt.reshape(n_seq * t, D_MODEL)
    xs = x_sample.reshape(n_dec, D_MODEL)
    tm, tn = tiles["inproj"]
    proj_p, dt_p, proj_s, dt_s = _inproj(xp, mod_p, t, xs, mod_s, g_pre_mix, w_in_t, hgrn_lb_logits, tm, tn,
                                         functools.partial(_hgrn_interleave, tn=tn))
    o_a, s_hgrn_p, w_out_b = _hgrn_prompt(proj_p, hn, w_out[l], n_seq, t)

    n_chunks = t // CHUNK_B
    rep = SUBLANE // HPG
    dt_rows = dt_p.reshape(n_seq, n_chunks, CHUNK_B, B_G, 1, HPG).transpose(3, 0, 1, 4, 5, 2)
    dt_rows = jnp.broadcast_to(dt_rows, (B_G, n_seq, n_chunks, rep, HPG, CHUNK_B))
    dt_rows = dt_rows.reshape(B_G, n_seq, n_chunks * SUBLANE, CHUNK_B)
    prow = lambda p: jnp.tile(p.reshape(B_G, 1, HPG), (1, n_chunks * rep, 1)).reshape(B_G, n_chunks * SUBLANE, 1)
    cpar = _ssd_channel_params(cw, cb, sn, d_skip[l], SSD_GROUPS_PER_STEP)
    dpar = jnp.concatenate([prow(dt_bias[l]), prow(a_log[l])], axis=2)
    o_b, s_ssm_p = _ssd_prompt(proj_p, dt_rows, cpar, dpar, n_seq, t)
    heads = lambda off: proj_s[:, off:off + D_A].reshape(n_dec, H_A, HA_D)
    x1, o_a_s, s_hgrn_s = _outproj(
        o_a, o_b, w_out_b, xp, mod_p, False, t, g_post_mix, tiles["outproj_p"],
        hgrn_step=(heads(OFF_Q), heads(OFF_F), heads(OFF_I), heads(OFF_G), hn.reshape(H_A, HA_D),
                   state_hgrn[l]))
    conv_p = proj_p.reshape(n_seq, t, D_MAIN)[:, t - (CONV_W - 1):, OFF_X:]

    to_t = lambda a, h, d: a.reshape(a.shape[:-1] + (h, d)).swapaxes(-1, -2)
    xbc_s = proj_s[:, OFF_X:]
    cst = state_conv[l]
    xt = to_t(xbc_s[:, :D_B], H_B, B_P)
    zt = to_t(proj_s[:, OFF_Z:OFF_Z + D_B], H_B, B_P)
    bcr = xbc_s[:, D_B:].reshape(n_dec, 2 * B_G, B_N)
    cxt = to_t(cst[:, :, :D_B], H_B, B_P)
    cbc = cst[:, :, D_B:].reshape(n_dec, CONV_W - 1, 2 * B_G, B_N)
    wxt = to_t(cw[:, :D_B], H_B, B_P)
    wbc = cw[:, D_B:].reshape(CONV_W, 2 * B_G, B_N)
    bxt = to_t(cb[:, :D_B], H_B, B_P)[0]
    bbc = cb[0, D_B:].reshape(2 * B_G, B_N)
    ssd_operands = (xt, bcr, zt, dt_s[:, None, :], cxt, cbc,
                    wxt, wbc, bxt, bbc, dt_bias[l][None], a_log[l][None], d_skip[l][None],
                    to_t(sn, H_B, B_P)[0])
    assert len(ssd_operands) + 1 == N_SSD_STEP_IN
    y_p, o_b_st, s_ssm_s = _mlp(x1, mod_p, False, t, g_pre_mlp, g_post_mlp, w_up[l], w_down[l],
                                *tiles["mlp_p"], ssd_step=(ssd_operands, state_ssm[l]))
    o_b_s = o_b_st.swapaxes(1, 2).reshape(n_dec, D_B)

    x1_s = _outproj(o_a_s.reshape(n_dec, D_A), o_b_s, w_out_b, xs, mod_s, True, 1,
                    g_post_mix, tiles["outproj_s"])
    y_s = _mlp(x1_s, mod_s, True, 1, g_pre_mlp, g_post_mlp, w_up[l], w_down[l], *tiles["mlp_s"])
    conv_s = jnp.concatenate([cst[:, 1:], xbc_s[:, None, :]], axis=1)

    return (y_p.reshape(n_seq, t, D_MODEL), y_s.reshape(n_dec, 1, D_MODEL),
            s_hgrn_p[None], s_ssm_p[None], conv_p[None],
            s_hgrn_s[None], s_ssm_s[None], conv_s[None])
```

```python
import functools

import jax
import jax.numpy as jnp
from jax import lax
from jax.experimental import pallas as pl
from jax.experimental.pallas import tpu as pltpu

F32 = jnp.float32
BF16 = jnp.bfloat16

D_MODEL = 2048
D_A = 2048
HA_D = 128
H_A = D_A // HA_D
D_B = 2048
B_P = 64
H_B = D_B // B_P
B_G = 8
HPG = H_B // B_G
GW = HPG * B_P
B_N = 128
CONV_W = 4
CONV_DIM = D_B + 2 * B_G * B_N
D_MAIN = 4 * D_A + D_B + CONV_DIM
D_FF = 4 * D_MODEL
N_MOD = 6
EPS = 1e-6

OFF_Q, OFF_F, OFF_I, OFF_G = 0, D_A, 2 * D_A, 3 * D_A
OFF_Z = 4 * D_A
OFF_X = OFF_Z + D_B
OFF_B = OFF_X + D_B
OFF_C = OFF_B + B_G * B_N
REGION_SILU = (0, 3, 4)
REGION_FORGET = 1

LANE = 128
SUBLANE = 8
VMEM_LIMIT = 56 * 1024 * 1024

CHUNK_A = 128
SUB_A = 16
CHUNK_B = 128
HGRN_HEADS_PER_STEP = 4
SSD_GROUPS_PER_STEP = 2


def _tiles(t, n_dec):
    return dict(
        inproj=(_pick_tile(t, 2048), HGRN_HEADS_PER_STEP * HA_D),
        outproj_p=_pick_tile(t, 256), outproj_s=n_dec,
        mlp_p=(_pick_tile(t, 1024), 512), mlp_s=(n_dec, 1024),
        ada_n=1024)


def _pick_tile(m, target):
    t = min(m, target)
    while m % t:
        t //= 2
    return t


def _silu(x):
    hx = 0.5 * x
    return hx * jnp.tanh(hx) + hx


def _softplus(x):
    return jnp.maximum(x, 0.0) + jnp.log1p(jnp.exp(-jnp.abs(x)))


def _idiv(x, d):
    assert d & (d - 1) == 0
    return jnp.right_shift(x, d.bit_length() - 1)


def _rms(x):
    return x * lax.rsqrt(jnp.mean(x * x, axis=-1, keepdims=True) + EPS)


def _cumsum_rows(x, n, row):
    s = 1
    while s < n:
        x = x + jnp.where(row >= s, pltpu.roll(x, s, 0), 0.0)
        s *= 2
    return x


def _cumsum_lanes(x, n, lane):
    s = 1
    while s < n:
        x = x + jnp.where(lane >= s, pltpu.roll(x, s, 1), 0.0)
        s *= 2
    return x


def _dot(a, b):
    return jnp.dot(a.astype(BF16), b.astype(BF16), preferred_element_type=F32)


def _dot_nt(a, b):
    return lax.dot_general(a.astype(BF16), b.astype(BF16), (((1,), (1,)), ((), ())),
                           preferred_element_type=F32)


def _exact_terms(x):
    hi = x.astype(BF16).astype(F32)
    mid = (x - hi).astype(BF16).astype(F32)
    lo = x - hi - mid
    return jnp.concatenate([hi, mid, lo, jnp.zeros_like(hi)], axis=0).astype(BF16)


def _transpose_bcast(x, sel):
    return lax.dot_general(_exact_terms(x), sel, (((0,), (0,)), ((), ())),
                           preferred_element_type=F32)


def _params(*sem):
    return pltpu.CompilerParams(dimension_semantics=sem, vmem_limit_bytes=VMEM_LIMIT)


def _ada_kernel(c_ref, w_ref, b_ref, o_ref):
    o_ref[...] = _dot(_silu(c_ref[...]), w_ref[...]) + b_ref[...]


def _ada(c_all, w_ada, b_ada, tn):
    m = c_all.shape[0]
    n = w_ada.shape[1]
    return pl.pallas_call(
        _ada_kernel,
        grid=(n // tn,),
        in_specs=[pl.BlockSpec((m, D_MODEL), lambda j: (0, 0)),
                  pl.BlockSpec((D_MODEL, tn), lambda j: (0, j)),
                  pl.BlockSpec((1, tn), lambda j: (0, j))],
        out_specs=pl.BlockSpec((m, tn), lambda j: (0, j)),
        out_shape=jax.ShapeDtypeStruct((m, n), F32),
        compiler_params=_params("parallel"),
        name="ada_mod",
    )(c_all, w_ada, b_ada)


def _mod_spec(per_row, rows_per_seq, tm, col):
    if per_row:
        return pl.BlockSpec((tm, D_MODEL), lambda i, j: (i, col))
    tiles_per_seq = rows_per_seq // tm
    return pl.BlockSpec((None, 1, D_MODEL), lambda i, j: (i // tiles_per_seq, 0, col))


def _lower_bound(logits):
    e = jnp.exp(logits - jnp.max(logits, axis=0, keepdims=True))
    return e[0:1] / jnp.sum(e, axis=0, keepdims=True)


def _inproj_kernel(x_ref, sh_ref, sc_ref, g_ref, w_ref, wdt_ref, lbl_ref, xs_ref, shs_ref, scs_ref,
                   o_ref, odt_ref, os_ref, odts_ref, h_scr, *, tn):
    i, j = pl.program_id(0), pl.program_id(1)
    first_rows = i == 0
    tm = x_ref.shape[0]

    def normed(x, sh, sc):
        return (_rms(x) * g_ref[...] * (1.0 + sc) + sh).astype(BF16)

    @pl.when(j == 0)
    def _():
        hb = normed(x_ref[...], sh_ref[...], sc_ref[...])
        h_scr[0:tm, :] = hb
        odt_ref[...] = _dot_nt(hb, wdt_ref[...])

    @pl.when(jnp.logical_and(first_rows, j == 0))
    def _():
        hb = normed(xs_ref[...], shs_ref[...], scs_ref[...])
        h_scr[tm:, :] = hb
        odts_ref[...] = _dot_nt(hb, wdt_ref[...])

    region = (j * tn) // D_A
    is_silu = functools.reduce(jnp.logical_or, [region == r for r in REGION_SILU])

    def both_groups(act):
        @pl.when(first_rows)
        def _():
            res = act(_dot_nt(h_scr[...], w_ref[...]))
            o_ref[...] = res[0:tm]
            os_ref[...] = res[tm:]

        @pl.when(jnp.logical_not(first_rows))
        def _():
            o_ref[...] = act(_dot_nt(h_scr[0:tm, :], w_ref[...]))

    def log_forget(p):
        lb = _lower_bound(lbl_ref[...])
        half = 0.5 * (1.0 - lb)
        return jnp.log((lb + half) + half * jnp.tanh(0.5 * p))

    pl.when(is_silu)(lambda: both_groups(_silu))
    pl.when(region == REGION_FORGET)(lambda: both_groups(log_forget))
    pl.when(jnp.logical_not(jnp.logical_or(is_silu, region == REGION_FORGET)))(lambda: both_groups(lambda p: p))


def _inproj(x, mod, rows_per_seq, xs, mod_s, g_pre, w_t, lb_logits, tm, tn, out_tile):
    m, ns = x.shape[0], xs.shape[0]
    assert D_MAIN % tn == 0 and D_MAIN % H_B == 0 and D_A % tn == 0
    f_tiles = D_A // tn
    n_j = D_MAIN // tn
    once = lambda shape, idx: pl.BlockSpec(shape, lambda i, j: idx, pipeline_mode=pl.Buffered(1))
    return pl.pallas_call(
        functools.partial(_inproj_kernel, tn=tn),
        grid=(m // tm, n_j),
        in_specs=[pl.BlockSpec((tm, D_MODEL), lambda i, j: (i, 0), pipeline_mode=pl.Buffered(1)),
                  _mod_spec(False, rows_per_seq, tm, 0),
                  _mod_spec(False, rows_per_seq, tm, 1),
                  pl.BlockSpec((1, D_MODEL), lambda i, j: (0, 0)),
                  pl.BlockSpec((tn, D_MODEL), lambda i, j: (j, 0)),
                  pl.BlockSpec((H_B, D_MODEL), lambda i, j: (D_MAIN // H_B, 0)),
                  pl.BlockSpec((lb_logits.shape[0], tn),
                               lambda i, j: (0, jnp.clip(j - REGION_FORGET * f_tiles, 0, f_tiles - 1))),
                  once((ns, D_MODEL), (0, 0)), once((ns, D_MODEL), (0, 0)), once((ns, D_MODEL), (0, 1))],
        out_specs=[pl.BlockSpec((tm, tn), lambda i, j: (i, out_tile(j))),
                   pl.BlockSpec((tm, H_B), lambda i, j: (i, 0)),
                   pl.BlockSpec((ns, tn), lambda i, j: (0, jnp.where(i == 0, j, n_j - 1))),
                   pl.BlockSpec((ns, H_B), lambda i, j: (0, 0))],
        out_shape=[jax.ShapeDtypeStruct((m, D_MAIN), F32),
                   jax.ShapeDtypeStruct((m, H_B), F32),
                   jax.ShapeDtypeStruct((ns, D_MAIN), F32),
                   jax.ShapeDtypeStruct((ns, H_B), F32)],
        scratch_shapes=[pltpu.VMEM((tm + ns, D_MODEL), BF16)],
        compiler_params=_params("arbitrary", "arbitrary"),
        name="in_proj",
    )(x, mod, mod, g_pre, w_t, w_t, lb_logits, xs, mod_s, mod_s)


def _hgrn_interleave(j, tn):
    n_hg = D_A // tn
    return jnp.where(j < 4 * n_hg, 4 * (j % n_hg) + j // n_hg, j)


def _hgrn_scores(qs, lf, row):
    c_len = CHUNK_A
    n_sub = c_len // SUB_A
    k = 1.0 - jnp.exp(lf)
    b = _cumsum_rows(lf, SUB_A, row & (SUB_A - 1))
    k_acc = []
    blocks, q_dec = [], []
    start = jnp.zeros((1, HA_D), F32)
    for i in range(n_sub):
        lo = i * SUB_A
        hi = lo + SUB_A
        b_i = b[lo:hi, :]
        b_sub = b[hi - 1:hi, :]
        k_i = k[lo:hi, :]
        qg = qs[lo:hi, :] * jnp.exp(b_i)
        k_diag = k_i * jnp.exp(-b_i)
        rhs = jnp.concatenate([p.astype(BF16) for p in k_acc] + [k_diag.astype(BF16)], axis=0)
        a_i = lax.dot_general(qg.astype(BF16), rhs, (((1,), (1,)), ((), ())),
                              preferred_element_type=F32)
        causal = (lax.broadcasted_iota(jnp.int32, (SUB_A, hi), 1)
                  <= lax.broadcasted_iota(jnp.int32, (SUB_A, hi), 0) + lo)
        a_i = jnp.where(causal, a_i, 0.0)
        blocks.append(jnp.pad(a_i, ((0, 0), (0, c_len - hi))).astype(BF16))
        q_dec.append((qg * jnp.exp(start)).astype(BF16))
        sub_decay = jnp.exp(b_sub)
        k_acc = [p * sub_decay for p in k_acc] + [k_i * jnp.exp(b_sub - b_i)]
        start = start + b_sub
    scores = jnp.concatenate(blocks, axis=0)
    k_end = jnp.concatenate([p.astype(BF16) for p in k_acc], axis=0)
    chunk_decay = jnp.exp(start)
    return scores, jnp.concatenate(q_dec, axis=0), k_end, chunk_decay


def _hgrn_apply(pend, v, gs, nw, st):
    scores, q_dec, k_end, chunk_decay = pend
    vb = v.astype(BF16)
    o = (jnp.dot(scores, vb, preferred_element_type=F32)
         + lax.dot_general(q_dec, st.astype(BF16), (((1,), (1,)), ((), ())),
                           preferred_element_type=F32))
    st = chunk_decay * st + lax.dot_general(vb, k_end, (((0,), (0,)), ((), ())),
                                            preferred_element_type=F32)
    return (_rms(o) * nw * gs).astype(BF16), st


def _hgrn_prompt_kernel(p_ref, nw_ref, wc_ref, o_ref, s_ref, wco_ref, *, n_chunks, n_heads):
    wco_ref[...] = wc_ref[...].astype(BF16)
    c_len = CHUNK_A
    w = n_heads * HA_D
    nw = nw_ref[...]
    row = lax.broadcasted_iota(jnp.int32, (c_len, HA_D), 0)
    heads = [slice(h * HA_D, (h + 1) * HA_D) for h in range(n_heads)]

    def part(k, sl, hs):
        return p_ref[sl, k * w + hs.start:k * w + hs.stop]

    def rows(c):
        return pl.ds(pl.multiple_of(c * c_len, c_len), c_len)

    def scores_of(c):
        sl = rows(c)
        return tuple(_hgrn_scores(part(0, sl, hs), part(1, sl, hs), row) for hs in heads)

    def apply_to(c, pends, sts):
        sl = rows(c)
        new = []
        for hs, pend, st in zip(heads, pends, sts):
            out, st = _hgrn_apply(pend, part(2, sl, hs), part(3, sl, hs), nw[:, hs], st)
            o_ref[sl, hs] = out
            new.append(st)
        return tuple(new)

    def step(c, carry):
        pends, sts = carry
        sts = apply_to(c - 1, pends, sts)
        return scores_of(c), sts

    zeros = tuple(jnp.zeros((HA_D, HA_D), F32) for _ in range(n_heads))
    pends, sts = lax.fori_loop(1, n_chunks, step, (scores_of(0), zeros))
    sts = apply_to(n_chunks - 1, pends, sts)
    for h in range(n_heads):
        s_ref[h] = sts[h].T


def _hgrn_prompt(proj, hgrn_norm, w_cast, n_seq, t):
    m = proj.shape[0]
    nh = HGRN_HEADS_PER_STEP
    w = nh * HA_D
    n_hg = H_A // nh
    rows = w_cast.shape[0] // (n_seq * n_hg)
    assert rows * n_seq * n_hg == w_cast.shape[0] and rows % (2 * SUBLANE) == 0
    wc_spec = pl.BlockSpec((rows, w_cast.shape[1]), lambda b, h: (b * n_hg + h, 0))
    return pl.pallas_call(
        functools.partial(_hgrn_prompt_kernel, n_chunks=t // CHUNK_A, n_heads=nh),
        grid=(n_seq, n_hg),
        in_specs=[pl.BlockSpec((t, 4 * w), lambda b, h: (b, h)),
                  pl.BlockSpec((1, w), lambda b, h: (0, h)),
                  wc_spec],
        out_specs=[pl.BlockSpec((t, w), lambda b, h: (b, h)),
                   pl.BlockSpec((None, nh, HA_D, HA_D), lambda b, h: (b, h, 0, 0)),
                   wc_spec],
        out_shape=[jax.ShapeDtypeStruct((m, D_A), BF16),
                   jax.ShapeDtypeStruct((n_seq, H_A, HA_D, HA_D), F32),
                   jax.ShapeDtypeStruct(w_cast.shape, BF16)],
        compiler_params=_params("parallel", "parallel"),
        name="hgrn_prompt",
    )(proj, hgrn_norm, w_cast)


def _conv_chunk(tail, u, w, bias, n):
    full = jnp.concatenate([tail, u], axis=0)
    out = bias + w[CONV_W - 1:CONV_W, :] * u
    for i in range(CONV_W - 1):
        shifted = pltpu.roll(full, CONV_W - 1 - i, 0)[SUBLANE:SUBLANE + n, :]
        out = out + w[i:i + 1, :] * shifted
    return _silu(out)


def _ssd_prompt_kernel(z_ref, x_ref, b_ref, c_ref, dtr_ref, cpar_ref, dpar_ref, wc_ref,
                       o_ref, h_ref, wco_ref, row_scr, *, n_chunks, n_groups):
    wco_ref[...] = wc_ref[...].astype(BF16)
    n = CHUNK_B
    assert n == LANE
    tri = (lax.broadcasted_iota(jnp.int32, (n, n), 0)
           >= lax.broadcasted_iota(jnp.int32, (n, n), 1))
    lane_head = _idiv(lax.broadcasted_iota(jnp.int32, (1, GW), 1), B_P)
    row_head = _idiv(lax.broadcasted_iota(jnp.int32, (GW, 1), 0), B_P)
    nr = n_chunks * SUBLANE
    lane_nr = lax.broadcasted_iota(jnp.int32, (nr, n), 1)
    is_dt = (lax.broadcasted_iota(jnp.int32, (nr, n), 0) & (SUBLANE - 1)) < HPG
    is_dt8 = lax.broadcasted_iota(jnp.int32, (SUBLANE, n), 0) < HPG
    n_src = 2 * SUBLANE
    sel_shape = (4 * n_src, HPG * n + 2 * GW)
    sel_k = lax.broadcasted_iota(jnp.int32, sel_shape, 0)
    sel_l = lax.broadcasted_iota(jnp.int32, sel_shape, 1)
    want = jnp.where(sel_l < HPG * n, HPG + _idiv(sel_l, n),
                     2 * HPG + _idiv(sel_l - HPG * n, B_P))
    bcast_sel = jnp.where(((sel_k & (n_src - 1)) == want) & (sel_k < 3 * n_src), 1.0, 0.0).astype(BF16)

    groups = []
    for gi in range(n_groups):
        xs = slice(gi * GW, (gi + 1) * GW)
        bs = slice(gi * B_N, (gi + 1) * B_N)
        dpar = dpar_ref[gi]
        dt_all = _softplus(dtr_ref[gi] + dpar[:, 0:1])
        cum_all = _cumsum_lanes(dt_all * -jnp.exp(dpar[:, 1:2]), n, lane_nr)
        row_scr[gi] = jnp.where(is_dt, dt_all, cum_all).reshape(n_chunks, SUBLANE, n)
        bl = slice(n_groups * GW + gi * B_N, n_groups * GW + (gi + 1) * B_N)
        cl = slice(n_groups * (GW + B_N) + gi * B_N, n_groups * (GW + B_N) + (gi + 1) * B_N)
        taps, bias = slice(0, CONV_W), slice(CONV_W, CONV_W + 1)
        groups.append(dict(gi=gi, xs=xs, bs=bs, dskip=cpar_ref[6:7, xs], nw=cpar_ref[5:6, xs],
                           wx=cpar_ref[taps, xs], wb=cpar_ref[taps, bl], wc=cpar_ref[taps, cl],
                           bx=cpar_ref[bias, xs], bb=cpar_ref[bias, bl], bc=cpar_ref[bias, cl]))

    def rows(c):
        return pl.ds(pl.multiple_of(c * n, n), n)

    def scan_free(c, grp):
        xs, bs = grp["xs"], grp["bs"]
        sl = rows(c)
        if isinstance(c, int) and c == 0:
            tail = lambda ref, lanes: jnp.zeros((SUBLANE, lanes.stop - lanes.start), F32)
        else:
            prev = pl.ds(pl.multiple_of(c * n - SUBLANE, SUBLANE), SUBLANE)
            tail = lambda ref, lanes: ref[prev, lanes]
        xc = _conv_chunk(tail(x_ref, xs), x_ref[sl, xs], grp["wx"], grp["bx"], n)
        bcv = _conv_chunk(tail(b_ref, bs), b_ref[sl, bs], grp["wb"], grp["bb"], n)
        ccv = _conv_chunk(tail(c_ref, bs), c_ref[sl, bs], grp["wc"], grp["bc"], n)
        r8 = row_scr[grp["gi"], c]
        swapped = pltpu.roll(r8, HPG, 0)
        cum8 = jnp.where(is_dt8, swapped, r8)
        dt8 = jnp.where(is_dt8, r8, swapped)
        e8 = jnp.where(is_dt8, jnp.exp(cum8), jnp.exp(cum8[:, n - 1:n] - cum8) * dt8)
        cols = _transpose_bcast(jnp.concatenate([r8, e8], axis=0), bcast_sel)
        e_cum = cols[:, HPG * n:HPG * n + GW]
        w_end = cols[:, HPG * n + GW:]
        g = _dot_nt(ccv, bcv)
        xb = xc.astype(BF16)
        s_all, x_all = [], []
        for j in range(HPG):
            cc = cols[:, j * n:(j + 1) * n]
            seg = cc - r8[HPG + j:HPG + j + 1, :]
            s = g * jnp.exp(jnp.where(tri, seg, -jnp.inf)) * r8[j:j + 1, :]
            s_all.append(s.astype(BF16))
            x_all.append(jnp.where(lane_head == j, xb, jnp.zeros_like(xb)))
        y = grp["dskip"] * xc + jnp.dot(jnp.concatenate(s_all, axis=1), jnp.concatenate(x_all, axis=0),
                                        preferred_element_type=F32)
        return y, e_cum, ccv.astype(BF16), (xc * w_end).astype(BF16), bcv.astype(BF16), r8

    def apply_state(c, grp, pend, h):
        xs = grp["xs"]
        y, e_cum, ccb, xwb, bcb, r8 = pend
        sl = rows(c)
        ch = lax.dot_general(ccb, h.astype(BF16), (((1,), (1,)), ((), ())),
                             preferred_element_type=F32)
        yz = (y + ch * e_cum) * z_ref[sl, xs]
        o_ref[sl, xs] = (_rms(yz) * grp["nw"]).astype(o_ref.dtype)
        h_dec = jnp.zeros((GW, 1), F32)
        for j in range(HPG):
            h_dec = jnp.where(row_head == j, jnp.exp(r8[HPG + j:HPG + j + 1, n - 1:n]), h_dec)
        return h_dec * h + lax.dot_general(xwb, bcb, (((0,), (0,)), ((), ())),
                                           preferred_element_type=F32)

    def step(c, carry):
        pends, hs = carry
        hs = tuple(apply_state(c - 1, grp, pend, h) for grp, pend, h in zip(groups, pends, hs))
        return tuple(scan_free(c, grp) for grp in groups), hs

    init = (tuple(scan_free(0, grp) for grp in groups),
            tuple(jnp.zeros((GW, B_N), F32) for _ in groups))
    pends, hs = lax.fori_loop(1, n_chunks, step, init)
    for gi, (grp, pend, h) in enumerate(zip(groups, pends, hs)):
        h = apply_state(n_chunks - 1, grp, pend, h)
        h_ref[gi * HPG:(gi + 1) * HPG] = h.reshape(HPG, B_P, B_N)


def _ssd_channel_params(conv_w, conv_b, ssd_norm, d_skip, ng):
    zeros = jnp.zeros((1, 2 * B_G * B_N), F32)
    rows = jnp.concatenate([
        conv_w, conv_b,
        jnp.concatenate([ssd_norm, zeros], axis=1),
        jnp.concatenate([jnp.repeat(d_skip, B_P)[None], zeros], axis=1),
        jnp.zeros((SUBLANE - CONV_W - 3, CONV_DIM), F32)], axis=0)
    n_steps = B_G // ng
    part = lambda lo, width: rows[:, lo:lo + n_steps * width].reshape(SUBLANE, n_steps, width)
    slab = jnp.concatenate([part(0, ng * GW), part(D_B, ng * B_N), part(D_B + B_G * B_N, ng * B_N)], axis=2)
    return slab.transpose(1, 0, 2)


def _ssd_prompt(proj, dt_rows, cpar, dpar, w_cast, n_seq, t):
    m = proj.shape[0]
    n_chunks = t // CHUNK_B
    nr = n_chunks * SUBLANE
    ng = SSD_GROUPS_PER_STEP
    n_gs = B_G // ng
    xw, bw = ng * GW, ng * B_N
    wide = lambda off: pl.BlockSpec((t, xw), lambda b, g: (b, off // xw + g))
    narrow = lambda off: pl.BlockSpec((t, bw), lambda b, g: (b, off // bw + g))
    rows = w_cast.shape[0] // (n_seq * n_gs)
    assert rows * n_seq * n_gs == w_cast.shape[0] and rows % (2 * SUBLANE) == 0
    wc_spec = pl.BlockSpec((rows, w_cast.shape[1]), lambda b, g: (b * n_gs + g, 0))
    return pl.pallas_call(
        functools.partial(_ssd_prompt_kernel, n_chunks=n_chunks, n_groups=ng),
        grid=(n_seq, n_gs),
        in_specs=[wide(OFF_Z), wide(OFF_X), narrow(OFF_B), narrow(OFF_C),
                  pl.BlockSpec((ng, None, nr, CHUNK_B), lambda b, g: (g, b, 0, 0)),
                  pl.BlockSpec((None, SUBLANE, xw + 2 * bw), lambda b, g: (g, 0, 0)),
                  pl.BlockSpec((ng, nr, 2), lambda b, g: (g, 0, 0)),
                  wc_spec],
        out_specs=[pl.BlockSpec((t, xw), lambda b, g: (b, g)),
                   pl.BlockSpec((None, ng * HPG, B_P, B_N), lambda b, g: (b, g, 0, 0)),
                   wc_spec],
        out_shape=[jax.ShapeDtypeStruct((m, D_B), BF16),
                   jax.ShapeDtypeStruct((n_seq, H_B, B_P, B_N), F32),
                   jax.ShapeDtypeStruct(w_cast.shape, BF16)],
        scratch_shapes=[pltpu.VMEM((ng, n_chunks, SUBLANE, CHUNK_B), F32)],
        compiler_params=_params("parallel", "parallel"),
        name="ssd_prompt",
    )(proj, proj, proj, proj, dt_rows, cpar, dpar, w_cast)


def _hgrn_step_rows(q_ref, f_ref, v_ref, g_ref, nw_ref, s_ref, o_ref, so_ref):
    nw = nw_ref[...]
    pad = jnp.zeros((HA_D - H_A, HA_D), F32)
    for b in range(q_ref.shape[0]):
        ft = jnp.concatenate([jnp.exp(f_ref[b]), pad], axis=0).T
        qt = jnp.concatenate([q_ref[b], pad], axis=0).T
        v = v_ref[b]
        rows = []
        for h in range(H_A):
            fc = ft[:, h:h + 1]
            s_new = fc * s_ref[b, h] + (1.0 - fc) * v[h:h + 1, :]
            so_ref[b, h] = s_new
            rows.append(jnp.sum(qt[:, h:h + 1] * s_new, axis=0, keepdims=True))
        o = jnp.concatenate(rows, axis=0)
        o_ref[b] = (_rms(o) * nw * g_ref[b]).astype(o_ref.dtype)


def _conv_step(buf_ref, b, u, w, bias):
    out = bias + w[CONV_W - 1] * u
    for i in range(CONV_W - 1):
        out = out + w[i] * buf_ref[b, i]
    return _silu(out)


N_SSD_STEP_LEAD = 6
N_SSD_STEP_IN = 15


def _ssd_step_rows(xt_ref, bcr_ref, zt_ref, dt_ref, cxt_ref, cbc_ref,
                   wxt_ref, wbc_ref, bxt_ref, bbc_ref, dtb_ref, al_ref, dsk_ref, nwt_ref, h_ref,
                   o_ref, ho_ref):
    wxt, wbc, bxt, bbc = wxt_ref[...], wbc_ref[...], bxt_ref[...], bbc_ref[...]
    a = -jnp.exp(al_ref[...])
    dtb, dsk, nwt = dtb_ref[...], dsk_ref[...], nwt_ref[...]
    lane = lax.broadcasted_iota(jnp.int32, (1, H_B), 1)
    lane_group = _idiv(lane, HPG)
    for b in range(xt_ref.shape[0]):
        xt = _conv_step(cxt_ref, b, xt_ref[b], wxt, bxt)
        bcv = _conv_step(cbc_ref, b, bcr_ref[b], wbc, bbc)
        dt = _softplus(dt_ref[b] + dtb)
        d_a = jnp.exp(dt * a)
        xdt = xt * dt
        yt = jnp.zeros((B_P, H_B), F32)
        for hh in range(H_B):
            g = hh // HPG
            h_new = d_a[:, hh:hh + 1] * h_ref[b, hh] + xdt[:, hh:hh + 1] * bcv[g:g + 1, :]
            ho_ref[b, hh] = h_new
            y_col = jnp.sum(h_new * bcv[B_G + g:B_G + g + 1, :], axis=1, keepdims=True)
            yt = jnp.where(lane == hh, y_col, yt)
        yz = (yt + dsk * xt) * zt_ref[b]
        col_sq = jnp.sum(yz * yz, axis=0, keepdims=True)
        ms = jnp.zeros((1, H_B), F32)
        for g in range(B_G):
            total = jnp.sum(col_sq[:, g * HPG:(g + 1) * HPG], axis=1, keepdims=True)
            ms = jnp.where(lane_group == g, total * (1.0 / GW), ms)
        o_ref[b] = (yz * lax.rsqrt(ms + EPS) * nwt).astype(o_ref.dtype)


def _ssd_step_specs(operands, state, bt, step_index):
    lead_ops, const_ops = operands[:N_SSD_STEP_LEAD], operands[N_SSD_STEP_LEAD:]
    lead = lambda a: pl.BlockSpec((bt,) + a.shape[1:], lambda *g: (step_index(*g),) + (0,) * (a.ndim - 1))
    full = lambda a: pl.BlockSpec(a.shape, lambda *g: (0,) * a.ndim)
    nb = state.shape[0]
    in_specs = [lead(a) for a in lead_ops] + [full(a) for a in const_ops] + [lead(state)]
    out_specs = [pl.BlockSpec((bt, B_P, H_B), lambda *g: (step_index(*g), 0, 0)), lead(state)]
    out_shape = [jax.ShapeDtypeStruct((nb, B_P, H_B), F32), jax.ShapeDtypeStruct(state.shape, F32)]
    return in_specs, out_specs, out_shape


def _outproj_kernel(oa_ref, ob_ref, wa_ref, wb_ref, x_ref, gt_ref, gp_ref, *rest):
    y_ref = rest[-3] if len(rest) > 1 else rest[0]
    mix = _dot(oa_ref[...], wa_ref[...]) + _dot(ob_ref[...], wb_ref[...])
    y_ref[...] = x_ref[...] + gt_ref[...] * (_rms(mix) * gp_ref[...])
    if len(rest) > 1:
        q_ref, f_ref, v_ref, g_ref, nw_ref, s_ref, _, o_ref, so_ref = rest
        _hgrn_step_rows(q_ref, f_ref, v_ref, g_ref, nw_ref, s_ref, o_ref, so_ref)


def _outproj(o_a, o_b, w_out, x, mod, per_row, rows_per_seq, g_post, tm, hgrn_step=None):
    m = x.shape[0]
    n_steps = m // tm
    row = lambda w: pl.BlockSpec((tm, w), lambda i, j: (i, 0))
    const = lambda shape, idx: pl.BlockSpec(shape, lambda i, j: idx, pipeline_mode=pl.Buffered(1))
    in_specs = [row(D_A), row(D_B),
                const((D_A, D_MODEL), (0, 0)), const((D_B, D_MODEL), (1, 0)),
                row(D_MODEL),
                _mod_spec(per_row, rows_per_seq, tm, 2),
                pl.BlockSpec((1, D_MODEL), lambda i, j: (0, 0))]
    out_specs = [row(D_MODEL)]
    out_shape = [jax.ShapeDtypeStruct((m, D_MODEL), F32)]
    args = [o_a, o_b, w_out, w_out, x, mod, g_post]
    if hgrn_step is not None:
        q, f, v, g, nw, state = hgrn_step
        nb = q.shape[0]
        assert nb % n_steps == 0
        bt = nb // n_steps
        rw = pl.BlockSpec((bt, H_A, HA_D), lambda i, j: (i, 0, 0))
        st = pl.BlockSpec((bt, H_A, HA_D, HA_D), lambda i, j: (i, 0, 0, 0))
        in_specs += [rw, rw, rw, rw, pl.BlockSpec((H_A, HA_D), lambda i, j: (0, 0)), st]
        out_specs += [rw, st]
        out_shape += [jax.ShapeDtypeStruct((nb, H_A, HA_D), BF16), jax.ShapeDtypeStruct(state.shape, F32)]
        args += [q, f, v, g, nw, state]
    out = pl.pallas_call(
        _outproj_kernel,
        grid=(n_steps, 1),
        in_specs=in_specs,
        out_specs=out_specs,
        out_shape=out_shape,
        compiler_params=_params("parallel", "arbitrary"),
        name="out_proj",
    )(*args)
    return out if hgrn_step is not None else out[0]


def _mlp_kernel(x_ref, sh_ref, sc_ref, gt_ref, gpre_ref, gpost_ref, wu_ref, wd_ref, *rest):
    ssd = len(rest) > 2
    if ssd:
        ssd_in, (y_ref, so_ref, sh_out_ref, h_scr) = rest[:N_SSD_STEP_IN], rest[N_SSD_STEP_IN:]
    else:
        y_ref, h_scr = rest
    f = pl.program_id(1)

    @pl.when(f == 0)
    def _():
        h = _rms(x_ref[...]) * gpre_ref[...] * (1.0 + sc_ref[...]) + sh_ref[...]
        h_scr[...] = h.astype(BF16)

    def tile_product():
        if ssd:
            _ssd_step_rows(*ssd_in, so_ref, sh_out_ref)
        u = jnp.maximum(jnp.dot(h_scr[...], wu_ref[...].astype(BF16), preferred_element_type=F32), 0.0)
        return jnp.dot((u * u).astype(BF16), wd_ref[...].astype(BF16), preferred_element_type=F32)

    @pl.when(f == 0)
    def _():
        y_ref[...] = tile_product()

    @pl.when(f > 0)
    def _():
        y_ref[...] += tile_product()

    @pl.when(f == pl.num_programs(1) - 1)
    def _():
        y_ref[...] = x_ref[...] + gt_ref[...] * (_rms(y_ref[...]) * gpost_ref[...])


def _mlp(x, mod, per_row, rows_per_seq, g_pre, g_post, w_up, w_down, tm, tf, ssd_step=None):
    m = x.shape[0]
    n_f = D_FF // tf
    n_steps = (m // tm) * n_f
    row = pl.BlockSpec((tm, D_MODEL), lambda i, j: (i, 0))
    vec = pl.BlockSpec((1, D_MODEL), lambda i, j: (0, 0))
    in_specs = [pl.BlockSpec((tm, D_MODEL), lambda i, j: (i, 0), pipeline_mode=pl.Buffered(1)),
                _mod_spec(per_row, rows_per_seq, tm, 3),
                _mod_spec(per_row, rows_per_seq, tm, 4),
                _mod_spec(per_row, rows_per_seq, tm, 5),
                vec, vec,
                pl.BlockSpec((D_MODEL, tf), lambda i, j: (0, j)),
                pl.BlockSpec((tf, D_MODEL), lambda i, j: (j, 0))]
    out_specs = [row]
    out_shape = [jax.ShapeDtypeStruct((m, D_MODEL), F32)]
    args = [x, mod, mod, mod, g_pre, g_post, w_up, w_down]
    if ssd_step is not None:
        out_specs = [pl.BlockSpec((tm, D_MODEL), lambda i, j: (i, 0), pipeline_mode=pl.Buffered(1))]
        operands, state = ssd_step
        nb = state.shape[0]
        assert nb % n_steps == 0
        s_in, s_out, s_shape = _ssd_step_specs(operands, state, nb // n_steps, lambda i, j: i * n_f + j)
        in_specs += s_in
        out_specs += s_out
        out_shape += s_shape
        args += list(operands) + [state]
    out = pl.pallas_call(
        _mlp_kernel,
        grid=(m // tm, n_f),
        in_specs=in_specs,
        out_specs=out_specs,
        out_shape=out_shape,
        scratch_shapes=[pltpu.VMEM((tm, D_MODEL), BF16)],
        compiler_params=_params("parallel", "arbitrary"),
        name="mlp",
    )(*args)
    return out if ssd_step is not None else out[0]


def kernel(x_prompt, x_sample, c_prompt, c_sample, state_hgrn, state_ssm, state_conv, w_ada, b_ada, norm_pre_mix, norm_post_mix, norm_pre_mlp, norm_post_mlp, w_in, hgrn_lb_logits, hgrn_norm, conv_w, conv_b, dt_bias, a_log, d_skip, ssd_norm, w_out, w_up, w_down):
    n_seq, t, _ = x_prompt.shape
    n_dec = x_sample.shape[0]
    assert x_sample.shape[1] == 1 and t % CHUNK_A == 0 and t % CHUNK_B == 0
    assert w_ada.shape[0] == 1, "one layer"
    l = 0
    tiles = _tiles(t, n_dec)

    w_in_t = w_in[l].T
    g_pre_mix, g_post_mix = norm_pre_mix[l][None], norm_post_mix[l][None]
    g_pre_mlp, g_post_mlp = norm_pre_mlp[l][None], norm_post_mlp[l][None]
    cw, cb = conv_w[l], conv_b[l][None]
    hn = hgrn_norm[l][None]
    sn = ssd_norm[l][None]

    n_c = n_seq + n_dec
    pad = (-n_c) % SUBLANE
    c_all = jnp.concatenate([c_sample, c_prompt, jnp.zeros((pad, D_MODEL), F32)], axis=0)
    mod = _ada(c_all, w_ada[l], b_ada[l][None], tiles["ada_n"])
    mod_p = mod[n_dec:n_c].reshape(n_seq, 1, N_MOD * D_MODEL)
    mod_s = mod

    xp = x_prompt.reshape(n_seq * t, D_MODEL)
    xs = x_sample.reshape(n_dec, D_MODEL)
    tm, tn = tiles["inproj"]
    proj_p, dt_p, proj_s, dt_s = _inproj(xp, mod_p, t, xs, mod_s, g_pre_mix, w_in_t, hgrn_lb_logits, tm, tn,
                                         functools.partial(_hgrn_interleave, tn=tn))
    o_a, s_hgrn_p, w_out_b = _hgrn_prompt(proj_p, hn, w_out[l], n_seq, t)

    n_chunks = t // CHUNK_B
    rep = SUBLANE // HPG
    dt_rows = dt_p.reshape(n_seq, n_chunks, CHUNK_B, B_G, 1, HPG).transpose(3, 0, 1, 4, 5, 2)
    dt_rows = jnp.broadcast_to(dt_rows, (B_G, n_seq, n_chunks, rep, HPG, CHUNK_B))
    dt_rows = dt_rows.reshape(B_G, n_seq, n_chunks * SUBLANE, CHUNK_B)
    prow = lambda p: jnp.tile(p.reshape(B_G, 1, HPG), (1, n_chunks * rep, 1)).reshape(B_G, n_chunks * SUBLANE, 1)
    cpar = _ssd_channel_params(cw, cb, sn, d_skip[l], SSD_GROUPS_PER_STEP)
    dpar = jnp.concatenate([prow(dt_bias[l]), prow(a_log[l])], axis=2)
    o_b, s_ssm_p, w_down_b = _ssd_prompt(proj_p, dt_rows, cpar, dpar, w_down[l], n_seq, t)
    heads = lambda off: proj_s[:, off:off + D_A].reshape(n_dec, H_A, HA_D)
    x1, o_a_s, s_hgrn_s = _outproj(
        o_a, o_b, w_out_b, xp, mod_p, False, t, g_post_mix, tiles["outproj_p"],
        hgrn_step=(heads(OFF_Q), heads(OFF_F), heads(OFF_I), heads(OFF_G), hn.reshape(H_A, HA_D),
                   state_hgrn[l]))
    conv_p = proj_p.reshape(n_seq, t, D_MAIN)[:, t - (CONV_W - 1):, OFF_X:]

    to_t = lambda a, h, d: a.reshape(a.shape[:-1] + (h, d)).swapaxes(-1, -2)
    xbc_s = proj_s[:, OFF_X:]
    cst = state_conv[l]
    xt = to_t(xbc_s[:, :D_B], H_B, B_P)
    zt = to_t(proj_s[:, OFF_Z:OFF_Z + D_B], H_B, B_P)
    bcr = xbc_s[:, D_B:].reshape(n_dec, 2 * B_G, B_N)
    cxt = to_t(cst[:, :, :D_B], H_B, B_P)
    cbc = cst[:, :, D_B:].reshape(n_dec, CONV_W - 1, 2 * B_G, B_N)
    wxt = to_t(cw[:, :D_B], H_B, B_P)
    wbc = cw[:, D_B:].reshape(CONV_W, 2 * B_G, B_N)
    bxt = to_t(cb[:, :D_B], H_B, B_P)[0]
    bbc = cb[0, D_B:].reshape(2 * B_G, B_N)
    ssd_operands = (xt, bcr, zt, dt_s[:, None, :], cxt, cbc,
                    wxt, wbc, bxt, bbc, dt_bias[l][None], a_log[l][None], d_skip[l][None],
                    to_t(sn, H_B, B_P)[0])
    assert len(ssd_operands) + 1 == N_SSD_STEP_IN
    y_p, o_b_st, s_ssm_s = _mlp(x1, mod_p, False, t, g_pre_mlp, g_post_mlp, w_up[l], w_down_b,
                                *tiles["mlp_p"], ssd_step=(ssd_operands, state_ssm[l]))
    o_b_s = o_b_st.swapaxes(1, 2).reshape(n_dec, D_B)

    x1_s = _outproj(o_a_s.reshape(n_dec, D_A), o_b_s, w_out_b, xs, mod_s, True, 1,
                    g_post_mix, tiles["outproj_s"])
    y_s = _mlp(x1_s, mod_s, True, 1, g_pre_mlp, g_post_mlp, w_up[l], w_down_b, *tiles["mlp_s"])
    conv_s = jnp.concatenate([cst[:, 1:], xbc_s[:, None, :]], axis=1)

    return (y_p.reshape(n_seq, t, D_MODEL), y_s.reshape(n_dec, 1, D_MODEL),
            s_hgrn_p[None], s_ssm_p[None], conv_p[None],
            s_hgrn_s[None], s_ssm_s[None], conv_s[None])
```

```python
import functools

import jax
import jax.numpy as jnp
from jax import lax
from jax.experimental import pallas as pl
from jax.experimental.pallas import tpu as pltpu

F32 = jnp.float32
BF16 = jnp.bfloat16

D_MODEL = 2048
D_A = 2048
HA_D = 128
H_A = D_A // HA_D
D_B = 2048
B_P = 64
H_B = D_B // B_P
B_G = 8
HPG = H_B // B_G
GW = HPG * B_P
B_N = 128
CONV_W = 4
CONV_DIM = D_B + 2 * B_G * B_N
D_MAIN = 4 * D_A + D_B + CONV_DIM
D_FF = 4 * D_MODEL
N_MOD = 6
EPS = 1e-6

OFF_Q, OFF_F, OFF_I, OFF_G = 0, D_A, 2 * D_A, 3 * D_A
OFF_Z = 4 * D_A
OFF_X = OFF_Z + D_B
OFF_B = OFF_X + D_B
OFF_C = OFF_B + B_G * B_N
REGION_SILU = (0, 3, 4)
REGION_FORGET = 1

LANE = 128
SUBLANE = 8
VMEM_LIMIT = 56 * 1024 * 1024

CHUNK_A = 128
SUB_A = 16
CHUNK_B = 128
HGRN_HEADS_PER_STEP = 4
SSD_GROUPS_PER_STEP = 2


def _tiles(t, n_dec):
    return dict(
        inproj=(_pick_tile(t, 2048), HGRN_HEADS_PER_STEP * HA_D),
        outproj_p=_pick_tile(t, 256), outproj_s=n_dec,
        mlp_p=(_pick_tile(t, 1024), 512), mlp_s=(n_dec, 1024),
        ada_n=1024)


def _pick_tile(m, target):
    t = min(m, target)
    while m % t:
        t //= 2
    return t


def _silu(x):
    hx = 0.5 * x
    return hx * jnp.tanh(hx) + hx


def _softplus(x):
    return jnp.maximum(x, 0.0) + jnp.log1p(jnp.exp(-jnp.abs(x)))


def _idiv(x, d):
    assert d & (d - 1) == 0
    return jnp.right_shift(x, d.bit_length() - 1)


def _rms(x):
    return x * lax.rsqrt(jnp.mean(x * x, axis=-1, keepdims=True) + EPS)


def _cumsum_rows(x, n, row):
    s = 1
    while s < n:
        x = x + jnp.where(row >= s, pltpu.roll(x, s, 0), 0.0)
        s *= 2
    return x


def _cumsum_lanes(x, n, lane):
    s = 1
    while s < n:
        x = x + jnp.where(lane >= s, pltpu.roll(x, s, 1), 0.0)
        s *= 2
    return x


def _dot(a, b):
    return jnp.dot(a.astype(BF16), b.astype(BF16), preferred_element_type=F32)


def _dot_nt(a, b):
    return lax.dot_general(a.astype(BF16), b.astype(BF16), (((1,), (1,)), ((), ())),
                           preferred_element_type=F32)


def _exact_terms(x):
    hi = x.astype(BF16).astype(F32)
    mid = (x - hi).astype(BF16).astype(F32)
    lo = x - hi - mid
    return jnp.concatenate([hi, mid, lo, jnp.zeros_like(hi)], axis=0).astype(BF16)


def _transpose_bcast(x, sel):
    return lax.dot_general(_exact_terms(x), sel, (((0,), (0,)), ((), ())),
                           preferred_element_type=F32)


def _params(*sem, n_fusible_inputs=None):
    fuse = None if n_fusible_inputs is None else [True] * n_fusible_inputs
    return pltpu.CompilerParams(dimension_semantics=sem, vmem_limit_bytes=VMEM_LIMIT, allow_input_fusion=fuse)


def _ada_kernel(c_ref, w_ref, b_ref, o_ref):
    o_ref[...] = _dot(_silu(c_ref[...]), w_ref[...]) + b_ref[...]


def _ada(c_all, w_ada, b_ada, tn):
    m = c_all.shape[0]
    n = w_ada.shape[1]
    return pl.pallas_call(
        _ada_kernel,
        grid=(n // tn,),
        in_specs=[pl.BlockSpec((m, D_MODEL), lambda j: (0, 0)),
                  pl.BlockSpec((D_MODEL, tn), lambda j: (0, j)),
                  pl.BlockSpec((1, tn), lambda j: (0, j))],
        out_specs=pl.BlockSpec((m, tn), lambda j: (0, j)),
        out_shape=jax.ShapeDtypeStruct((m, n), F32),
        compiler_params=_params("parallel"),
        name="ada_mod",
    )(c_all, w_ada, b_ada)


def _mod_spec(per_row, rows_per_seq, tm, col):
    if per_row:
        return pl.BlockSpec((tm, D_MODEL), lambda i, j: (i, col))
    tiles_per_seq = rows_per_seq // tm
    return pl.BlockSpec((None, 1, D_MODEL), lambda i, j: (i // tiles_per_seq, 0, col))


def _lower_bound(logits):
    e = jnp.exp(logits - jnp.max(logits, axis=0, keepdims=True))
    return e[0:1] / jnp.sum(e, axis=0, keepdims=True)


def _inproj_kernel(x_ref, sh_ref, sc_ref, g_ref, w_ref, wdt_ref, lbl_ref, xs_ref, shs_ref, scs_ref,
                   o_ref, odt_ref, os_ref, odts_ref, h_scr, *, tn):
    i, j = pl.program_id(0), pl.program_id(1)
    first_rows = i == 0
    tm = x_ref.shape[0]

    def normed(x, sh, sc):
        return (_rms(x) * g_ref[...] * (1.0 + sc) + sh).astype(BF16)

    @pl.when(j == 0)
    def _():
        hb = normed(x_ref[...], sh_ref[...], sc_ref[...])
        h_scr[0:tm, :] = hb
        odt_ref[...] = _dot_nt(hb, wdt_ref[...])

    @pl.when(jnp.logical_and(first_rows, j == 0))
    def _():
        hb = normed(xs_ref[...], shs_ref[...], scs_ref[...])
        h_scr[tm:, :] = hb
        odts_ref[...] = _dot_nt(hb, wdt_ref[...])

    region = (j * tn) // D_A
    is_silu = functools.reduce(jnp.logical_or, [region == r for r in REGION_SILU])

    def both_groups(act):
        @pl.when(first_rows)
        def _():
            res = act(_dot_nt(h_scr[...], w_ref[...]))
            o_ref[...] = res[0:tm]
            os_ref[...] = res[tm:]

        @pl.when(jnp.logical_not(first_rows))
        def _():
            o_ref[...] = act(_dot_nt(h_scr[0:tm, :], w_ref[...]))

    def log_forget(p):
        lb = _lower_bound(lbl_ref[...])
        half = 0.5 * (1.0 - lb)
        return jnp.log((lb + half) + half * jnp.tanh(0.5 * p))

    pl.when(is_silu)(lambda: both_groups(_silu))
    pl.when(region == REGION_FORGET)(lambda: both_groups(log_forget))
    pl.when(jnp.logical_not(jnp.logical_or(is_silu, region == REGION_FORGET)))(lambda: both_groups(lambda p: p))


def _inproj(x, mod, rows_per_seq, xs, mod_s, g_pre, w_t, lb_logits, tm, tn, out_tile):
    m, ns = x.shape[0], xs.shape[0]
    assert D_MAIN % tn == 0 and D_MAIN % H_B == 0 and D_A % tn == 0
    f_tiles = D_A // tn
    n_j = D_MAIN // tn
    once = lambda shape, idx: pl.BlockSpec(shape, lambda i, j: idx, pipeline_mode=pl.Buffered(1))
    return pl.pallas_call(
        functools.partial(_inproj_kernel, tn=tn),
        grid=(m // tm, n_j),
        in_specs=[pl.BlockSpec((tm, D_MODEL), lambda i, j: (i, 0), pipeline_mode=pl.Buffered(1)),
                  _mod_spec(False, rows_per_seq, tm, 0),
                  _mod_spec(False, rows_per_seq, tm, 1),
                  pl.BlockSpec((1, D_MODEL), lambda i, j: (0, 0)),
                  pl.BlockSpec((tn, D_MODEL), lambda i, j: (j, 0)),
                  pl.BlockSpec((H_B, D_MODEL), lambda i, j: (D_MAIN // H_B, 0)),
                  pl.BlockSpec((lb_logits.shape[0], tn),
                               lambda i, j: (0, jnp.clip(j - REGION_FORGET * f_tiles, 0, f_tiles - 1))),
                  once((ns, D_MODEL), (0, 0)), once((ns, D_MODEL), (0, 0)), once((ns, D_MODEL), (0, 1))],
        out_specs=[pl.BlockSpec((tm, tn), lambda i, j: (i, out_tile(j))),
                   pl.BlockSpec((tm, H_B), lambda i, j: (i, 0)),
                   pl.BlockSpec((ns, tn), lambda i, j: (0, jnp.where(i == 0, j, n_j - 1))),
                   pl.BlockSpec((ns, H_B), lambda i, j: (0, 0))],
        out_shape=[jax.ShapeDtypeStruct((m, D_MAIN), F32),
                   jax.ShapeDtypeStruct((m, H_B), F32),
                   jax.ShapeDtypeStruct((ns, D_MAIN), F32),
                   jax.ShapeDtypeStruct((ns, H_B), F32)],
        scratch_shapes=[pltpu.VMEM((tm + ns, D_MODEL), BF16)],
        compiler_params=_params("arbitrary", "arbitrary"),
        name="in_proj",
    )(x, mod, mod, g_pre, w_t, w_t, lb_logits, xs, mod_s, mod_s)


def _hgrn_interleave(j, tn):
    n_hg = D_A // tn
    return jnp.where(j < 4 * n_hg, 4 * (j % n_hg) + j // n_hg, j)


def _hgrn_scores(qs, lf, row):
    c_len = CHUNK_A
    n_sub = c_len // SUB_A
    k = 1.0 - jnp.exp(lf)
    b = _cumsum_rows(lf, SUB_A, row & (SUB_A - 1))
    k_acc = []
    blocks, q_dec = [], []
    start = jnp.zeros((1, HA_D), F32)
    for i in range(n_sub):
        lo = i * SUB_A
        hi = lo + SUB_A
        b_i = b[lo:hi, :]
        b_sub = b[hi - 1:hi, :]
        k_i = k[lo:hi, :]
        qg = qs[lo:hi, :] * jnp.exp(b_i)
        k_diag = k_i * jnp.exp(-b_i)
        rhs = jnp.concatenate([p.astype(BF16) for p in k_acc] + [k_diag.astype(BF16)], axis=0)
        a_i = lax.dot_general(qg.astype(BF16), rhs, (((1,), (1,)), ((), ())),
                              preferred_element_type=F32)
        causal = (lax.broadcasted_iota(jnp.int32, (SUB_A, hi), 1)
                  <= lax.broadcasted_iota(jnp.int32, (SUB_A, hi), 0) + lo)
        a_i = jnp.where(causal, a_i, 0.0)
        blocks.append(jnp.pad(a_i, ((0, 0), (0, c_len - hi))).astype(BF16))
        q_dec.append((qg * jnp.exp(start)).astype(BF16))
        sub_decay = jnp.exp(b_sub)
        k_acc = [p * sub_decay for p in k_acc] + [k_i * jnp.exp(b_sub - b_i)]
        start = start + b_sub
    scores = jnp.concatenate(blocks, axis=0)
    k_end = jnp.concatenate([p.astype(BF16) for p in k_acc], axis=0)
    chunk_decay = jnp.exp(start)
    return scores, jnp.concatenate(q_dec, axis=0), k_end, chunk_decay


def _hgrn_apply(pend, v, gs, nw, st):
    scores, q_dec, k_end, chunk_decay = pend
    vb = v.astype(BF16)
    o = (jnp.dot(scores, vb, preferred_element_type=F32)
         + lax.dot_general(q_dec, st.astype(BF16), (((1,), (1,)), ((), ())),
                           preferred_element_type=F32))
    st = chunk_decay * st + lax.dot_general(vb, k_end, (((0,), (0,)), ((), ())),
                                            preferred_element_type=F32)
    return (_rms(o) * nw * gs).astype(BF16), st


def _hgrn_prompt_kernel(p_ref, nw_ref, wc_ref, o_ref, s_ref, wco_ref, *, n_chunks, n_heads):
    wco_ref[...] = wc_ref[...].astype(BF16)
    c_len = CHUNK_A
    w = n_heads * HA_D
    nw = nw_ref[...]
    row = lax.broadcasted_iota(jnp.int32, (c_len, HA_D), 0)
    heads = [slice(h * HA_D, (h + 1) * HA_D) for h in range(n_heads)]

    def part(k, sl, hs):
        return p_ref[sl, k * w + hs.start:k * w + hs.stop]

    def rows(c):
        return pl.ds(pl.multiple_of(c * c_len, c_len), c_len)

    def scores_of(c):
        sl = rows(c)
        return tuple(_hgrn_scores(part(0, sl, hs), part(1, sl, hs), row) for hs in heads)

    def apply_to(c, pends, sts):
        sl = rows(c)
        new = []
        for hs, pend, st in zip(heads, pends, sts):
            out, st = _hgrn_apply(pend, part(2, sl, hs), part(3, sl, hs), nw[:, hs], st)
            o_ref[sl, hs] = out
            new.append(st)
        return tuple(new)

    def step(c, carry):
        pends, sts = carry
        sts = apply_to(c - 1, pends, sts)
        return scores_of(c), sts

    zeros = tuple(jnp.zeros((HA_D, HA_D), F32) for _ in range(n_heads))
    pends, sts = lax.fori_loop(1, n_chunks, step, (scores_of(0), zeros))
    sts = apply_to(n_chunks - 1, pends, sts)
    for h in range(n_heads):
        s_ref[h] = sts[h].T


def _hgrn_prompt(proj, hgrn_norm, w_cast, n_seq, t):
    m = proj.shape[0]
    nh = HGRN_HEADS_PER_STEP
    w = nh * HA_D
    n_hg = H_A // nh
    rows = w_cast.shape[0] // (n_seq * n_hg)
    assert rows * n_seq * n_hg == w_cast.shape[0] and rows % (2 * SUBLANE) == 0
    wc_spec = pl.BlockSpec((rows, w_cast.shape[1]), lambda b, h: (b * n_hg + h, 0))
    return pl.pallas_call(
        functools.partial(_hgrn_prompt_kernel, n_chunks=t // CHUNK_A, n_heads=nh),
        grid=(n_seq, n_hg),
        in_specs=[pl.BlockSpec((t, 4 * w), lambda b, h: (b, h)),
                  pl.BlockSpec((1, w), lambda b, h: (0, h)),
                  wc_spec],
        out_specs=[pl.BlockSpec((t, w), lambda b, h: (b, h)),
                   pl.BlockSpec((None, nh, HA_D, HA_D), lambda b, h: (b, h, 0, 0)),
                   wc_spec],
        out_shape=[jax.ShapeDtypeStruct((m, D_A), BF16),
                   jax.ShapeDtypeStruct((n_seq, H_A, HA_D, HA_D), F32),
                   jax.ShapeDtypeStruct(w_cast.shape, BF16)],
        compiler_params=_params("parallel", "parallel"),
        name="hgrn_prompt",
    )(proj, hgrn_norm, w_cast)


def _conv_chunk(tail, u, w, bias, n):
    full = jnp.concatenate([tail, u], axis=0)
    out = bias + w[CONV_W - 1:CONV_W, :] * u
    for i in range(CONV_W - 1):
        shifted = pltpu.roll(full, CONV_W - 1 - i, 0)[SUBLANE:SUBLANE + n, :]
        out = out + w[i:i + 1, :] * shifted
    return _silu(out)


def _ssd_prompt_kernel(z_ref, x_ref, b_ref, c_ref, dtr_ref, cpar_ref, dpar_ref,
                       o_ref, h_ref, row_scr, *, n_chunks, n_groups):
    n = CHUNK_B
    assert n == LANE
    tri = (lax.broadcasted_iota(jnp.int32, (n, n), 0)
           >= lax.broadcasted_iota(jnp.int32, (n, n), 1))
    lane_head = _idiv(lax.broadcasted_iota(jnp.int32, (1, GW), 1), B_P)
    row_head = _idiv(lax.broadcasted_iota(jnp.int32, (GW, 1), 0), B_P)
    nr = n_chunks * SUBLANE
    lane_nr = lax.broadcasted_iota(jnp.int32, (nr, n), 1)
    is_dt = (lax.broadcasted_iota(jnp.int32, (nr, n), 0) & (SUBLANE - 1)) < HPG
    is_dt8 = lax.broadcasted_iota(jnp.int32, (SUBLANE, n), 0) < HPG
    n_src = 2 * SUBLANE
    sel_shape = (4 * n_src, HPG * n + 2 * GW)
    sel_k = lax.broadcasted_iota(jnp.int32, sel_shape, 0)
    sel_l = lax.broadcasted_iota(jnp.int32, sel_shape, 1)
    want = jnp.where(sel_l < HPG * n, HPG + _idiv(sel_l, n),
                     2 * HPG + _idiv(sel_l - HPG * n, B_P))
    bcast_sel = jnp.where(((sel_k & (n_src - 1)) == want) & (sel_k < 3 * n_src), 1.0, 0.0).astype(BF16)

    groups = []
    for gi in range(n_groups):
        xs = slice(gi * GW, (gi + 1) * GW)
        bs = slice(gi * B_N, (gi + 1) * B_N)
        dpar = dpar_ref[gi]
        dt_all = _softplus(dtr_ref[gi] + dpar[:, 0:1])
        cum_all = _cumsum_lanes(dt_all * -jnp.exp(dpar[:, 1:2]), n, lane_nr)
        row_scr[gi] = jnp.where(is_dt, dt_all, cum_all).reshape(n_chunks, SUBLANE, n)
        bl = slice(n_groups * GW + gi * B_N, n_groups * GW + (gi + 1) * B_N)
        cl = slice(n_groups * (GW + B_N) + gi * B_N, n_groups * (GW + B_N) + (gi + 1) * B_N)
        taps, bias = slice(0, CONV_W), slice(CONV_W, CONV_W + 1)
        groups.append(dict(gi=gi, xs=xs, bs=bs, dskip=cpar_ref[6:7, xs], nw=cpar_ref[5:6, xs],
                           wx=cpar_ref[taps, xs], wb=cpar_ref[taps, bl], wc=cpar_ref[taps, cl],
                           bx=cpar_ref[bias, xs], bb=cpar_ref[bias, bl], bc=cpar_ref[bias, cl]))

    def rows(c):
        return pl.ds(pl.multiple_of(c * n, n), n)

    def scan_free(c, grp):
        xs, bs = grp["xs"], grp["bs"]
        sl = rows(c)
        if isinstance(c, int) and c == 0:
            tail = lambda ref, lanes: jnp.zeros((SUBLANE, lanes.stop - lanes.start), F32)
        else:
            prev = pl.ds(pl.multiple_of(c * n - SUBLANE, SUBLANE), SUBLANE)
            tail = lambda ref, lanes: ref[prev, lanes]
        xc = _conv_chunk(tail(x_ref, xs), x_ref[sl, xs], grp["wx"], grp["bx"], n)
        bcv = _conv_chunk(tail(b_ref, bs), b_ref[sl, bs], grp["wb"], grp["bb"], n)
        ccv = _conv_chunk(tail(c_ref, bs), c_ref[sl, bs], grp["wc"], grp["bc"], n)
        r8 = row_scr[grp["gi"], c]
        swapped = pltpu.roll(r8, HPG, 0)
        cum8 = jnp.where(is_dt8, swapped, r8)
        dt8 = jnp.where(is_dt8, r8, swapped)
        e8 = jnp.where(is_dt8, jnp.exp(cum8), jnp.exp(cum8[:, n - 1:n] - cum8) * dt8)
        cols = _transpose_bcast(jnp.concatenate([r8, e8], axis=0), bcast_sel)
        e_cum = cols[:, HPG * n:HPG * n + GW]
        w_end = cols[:, HPG * n + GW:]
        g = _dot_nt(ccv, bcv)
        xb = xc.astype(BF16)
        s_all, x_all = [], []
        for j in range(HPG):
            cc = cols[:, j * n:(j + 1) * n]
            seg = cc - r8[HPG + j:HPG + j + 1, :]
            s = g * jnp.exp(jnp.where(tri, seg, -jnp.inf)) * r8[j:j + 1, :]
            s_all.append(s.astype(BF16))
            x_all.append(jnp.where(lane_head == j, xb, jnp.zeros_like(xb)))
        y = grp["dskip"] * xc + jnp.dot(jnp.concatenate(s_all, axis=1), jnp.concatenate(x_all, axis=0),
                                        preferred_element_type=F32)
        return y, e_cum, ccv.astype(BF16), (xc * w_end).astype(BF16), bcv.astype(BF16), r8

    def apply_state(c, grp, pend, h):
        xs = grp["xs"]
        y, e_cum, ccb, xwb, bcb, r8 = pend
        sl = rows(c)
        ch = lax.dot_general(ccb, h.astype(BF16), (((1,), (1,)), ((), ())),
                             preferred_element_type=F32)
        yz = (y + ch * e_cum) * z_ref[sl, xs]
        o_ref[sl, xs] = (_rms(yz) * grp["nw"]).astype(o_ref.dtype)
        h_dec = jnp.zeros((GW, 1), F32)
        for j in range(HPG):
            h_dec = jnp.where(row_head == j, jnp.exp(r8[HPG + j:HPG + j + 1, n - 1:n]), h_dec)
        return h_dec * h + lax.dot_general(xwb, bcb, (((0,), (0,)), ((), ())),
                                           preferred_element_type=F32)

    def step(c, carry):
        pends, hs = carry
        hs = tuple(apply_state(c - 1, grp, pend, h) for grp, pend, h in zip(groups, pends, hs))
        return tuple(scan_free(c, grp) for grp in groups), hs

    init = (tuple(scan_free(0, grp) for grp in groups),
            tuple(jnp.zeros((GW, B_N), F32) for _ in groups))
    pends, hs = lax.fori_loop(1, n_chunks, step, init)
    for gi, (grp, pend, h) in enumerate(zip(groups, pends, hs)):
        h = apply_state(n_chunks - 1, grp, pend, h)
        h_ref[gi * HPG:(gi + 1) * HPG] = h.reshape(HPG, B_P, B_N)


def _ssd_channel_params(conv_w, conv_b, ssd_norm, d_skip, ng):
    zeros = jnp.zeros((1, 2 * B_G * B_N), F32)
    rows = jnp.concatenate([
        conv_w, conv_b,
        jnp.concatenate([ssd_norm, zeros], axis=1),
        jnp.concatenate([jnp.repeat(d_skip, B_P)[None], zeros], axis=1),
        jnp.zeros((SUBLANE - CONV_W - 3, CONV_DIM), F32)], axis=0)
    n_steps = B_G // ng
    part = lambda lo, width: rows[:, lo:lo + n_steps * width].reshape(SUBLANE, n_steps, width)
    slab = jnp.concatenate([part(0, ng * GW), part(D_B, ng * B_N), part(D_B + B_G * B_N, ng * B_N)], axis=2)
    return slab.transpose(1, 0, 2)


def _ssd_prompt(proj, dt_rows, cpar, dpar, n_seq, t):
    m = proj.shape[0]
    n_chunks = t // CHUNK_B
    nr = n_chunks * SUBLANE
    ng = SSD_GROUPS_PER_STEP
    xw, bw = ng * GW, ng * B_N
    wide = lambda off: pl.BlockSpec((t, xw), lambda b, g: (b, off // xw + g))
    narrow = lambda off: pl.BlockSpec((t, bw), lambda b, g: (b, off // bw + g))
    return pl.pallas_call(
        functools.partial(_ssd_prompt_kernel, n_chunks=n_chunks, n_groups=ng),
        grid=(n_seq, B_G // ng),
        in_specs=[wide(OFF_Z), wide(OFF_X), narrow(OFF_B), narrow(OFF_C),
                  pl.BlockSpec((ng, None, nr, CHUNK_B), lambda b, g: (g, b, 0, 0)),
                  pl.BlockSpec((None, SUBLANE, xw + 2 * bw), lambda b, g: (g, 0, 0)),
                  pl.BlockSpec((ng, nr, 2), lambda b, g: (g, 0, 0))],
        out_specs=[pl.BlockSpec((t, xw), lambda b, g: (b, g)),
                   pl.BlockSpec((None, ng * HPG, B_P, B_N), lambda b, g: (b, g, 0, 0))],
        out_shape=[jax.ShapeDtypeStruct((m, D_B), BF16),
                   jax.ShapeDtypeStruct((n_seq, H_B, B_P, B_N), F32)],
        scratch_shapes=[pltpu.VMEM((ng, n_chunks, SUBLANE, CHUNK_B), F32)],
        compiler_params=_params("parallel", "parallel", n_fusible_inputs=7),
        name="ssd_prompt",
    )(proj, proj, proj, proj, dt_rows, cpar, dpar)


def _hgrn_step_rows(q_ref, f_ref, v_ref, g_ref, nw_ref, s_ref, o_ref, so_ref):
    nw = nw_ref[...]
    pad = jnp.zeros((HA_D - H_A, HA_D), F32)
    for b in range(q_ref.shape[0]):
        ft = jnp.concatenate([jnp.exp(f_ref[b]), pad], axis=0).T
        qt = jnp.concatenate([q_ref[b], pad], axis=0).T
        v = v_ref[b]
        rows = []
        for h in range(H_A):
            fc = ft[:, h:h + 1]
            s_new = fc * s_ref[b, h] + (1.0 - fc) * v[h:h + 1, :]
            so_ref[b, h] = s_new
            rows.append(jnp.sum(qt[:, h:h + 1] * s_new, axis=0, keepdims=True))
        o = jnp.concatenate(rows, axis=0)
        o_ref[b] = (_rms(o) * nw * g_ref[b]).astype(o_ref.dtype)


def _conv_step(buf_ref, b, u, w, bias):
    out = bias + w[CONV_W - 1] * u
    for i in range(CONV_W - 1):
        out = out + w[i] * buf_ref[b, i]
    return _silu(out)


N_SSD_STEP_LEAD = 6
N_SSD_STEP_IN = 15


def _ssd_step_rows(xt_ref, bcr_ref, zt_ref, dt_ref, cxt_ref, cbc_ref,
                   wxt_ref, wbc_ref, bxt_ref, bbc_ref, dtb_ref, al_ref, dsk_ref, nwt_ref, h_ref,
                   o_ref, ho_ref):
    wxt, wbc, bxt, bbc = wxt_ref[...], wbc_ref[...], bxt_ref[...], bbc_ref[...]
    a = -jnp.exp(al_ref[...])
    dtb, dsk, nwt = dtb_ref[...], dsk_ref[...], nwt_ref[...]
    lane = lax.broadcasted_iota(jnp.int32, (1, H_B), 1)
    lane_group = _idiv(lane, HPG)
    for b in range(xt_ref.shape[0]):
        xt = _conv_step(cxt_ref, b, xt_ref[b], wxt, bxt)
        bcv = _conv_step(cbc_ref, b, bcr_ref[b], wbc, bbc)
        dt = _softplus(dt_ref[b] + dtb)
        d_a = jnp.exp(dt * a)
        xdt = xt * dt
        yt = jnp.zeros((B_P, H_B), F32)
        for hh in range(H_B):
            g = hh // HPG
            h_new = d_a[:, hh:hh + 1] * h_ref[b, hh] + xdt[:, hh:hh + 1] * bcv[g:g + 1, :]
            ho_ref[b, hh] = h_new
            y_col = jnp.sum(h_new * bcv[B_G + g:B_G + g + 1, :], axis=1, keepdims=True)
            yt = jnp.where(lane == hh, y_col, yt)
        yz = (yt + dsk * xt) * zt_ref[b]
        col_sq = jnp.sum(yz * yz, axis=0, keepdims=True)
        ms = jnp.zeros((1, H_B), F32)
        for g in range(B_G):
            total = jnp.sum(col_sq[:, g * HPG:(g + 1) * HPG], axis=1, keepdims=True)
            ms = jnp.where(lane_group == g, total * (1.0 / GW), ms)
        o_ref[b] = (yz * lax.rsqrt(ms + EPS) * nwt).astype(o_ref.dtype)


def _ssd_step_specs(operands, state, bt, step_index):
    lead_ops, const_ops = operands[:N_SSD_STEP_LEAD], operands[N_SSD_STEP_LEAD:]
    lead = lambda a: pl.BlockSpec((bt,) + a.shape[1:], lambda *g: (step_index(*g),) + (0,) * (a.ndim - 1))
    full = lambda a: pl.BlockSpec(a.shape, lambda *g: (0,) * a.ndim)
    nb = state.shape[0]
    in_specs = [lead(a) for a in lead_ops] + [full(a) for a in const_ops] + [lead(state)]
    out_specs = [pl.BlockSpec((bt, B_P, H_B), lambda *g: (step_index(*g), 0, 0)), lead(state)]
    out_shape = [jax.ShapeDtypeStruct((nb, B_P, H_B), F32), jax.ShapeDtypeStruct(state.shape, F32)]
    return in_specs, out_specs, out_shape


def _outproj_kernel(oa_ref, ob_ref, wa_ref, wb_ref, x_ref, gt_ref, gp_ref, *rest):
    y_ref = rest[-3] if len(rest) > 1 else rest[0]
    mix = _dot(oa_ref[...], wa_ref[...]) + _dot(ob_ref[...], wb_ref[...])
    y_ref[...] = x_ref[...] + gt_ref[...] * (_rms(mix) * gp_ref[...])
    if len(rest) > 1:
        q_ref, f_ref, v_ref, g_ref, nw_ref, s_ref, _, o_ref, so_ref = rest
        _hgrn_step_rows(q_ref, f_ref, v_ref, g_ref, nw_ref, s_ref, o_ref, so_ref)


def _outproj(o_a, o_b, w_out, x, mod, per_row, rows_per_seq, g_post, tm, hgrn_step=None):
    m = x.shape[0]
    n_steps = m // tm
    row = lambda w: pl.BlockSpec((tm, w), lambda i, j: (i, 0))
    const = lambda shape, idx: pl.BlockSpec(shape, lambda i, j: idx, pipeline_mode=pl.Buffered(1))
    in_specs = [row(D_A), row(D_B),
                const((D_A, D_MODEL), (0, 0)), const((D_B, D_MODEL), (1, 0)),
                row(D_MODEL),
                _mod_spec(per_row, rows_per_seq, tm, 2),
                pl.BlockSpec((1, D_MODEL), lambda i, j: (0, 0))]
    out_specs = [row(D_MODEL)]
    out_shape = [jax.ShapeDtypeStruct((m, D_MODEL), F32)]
    args = [o_a, o_b, w_out, w_out, x, mod, g_post]
    if hgrn_step is not None:
        q, f, v, g, nw, state = hgrn_step
        nb = q.shape[0]
        assert nb % n_steps == 0
        bt = nb // n_steps
        rw = pl.BlockSpec((bt, H_A, HA_D), lambda i, j: (i, 0, 0))
        st = pl.BlockSpec((bt, H_A, HA_D, HA_D), lambda i, j: (i, 0, 0, 0))
        in_specs += [rw, rw, rw, rw, pl.BlockSpec((H_A, HA_D), lambda i, j: (0, 0)), st]
        out_specs += [rw, st]
        out_shape += [jax.ShapeDtypeStruct((nb, H_A, HA_D), BF16), jax.ShapeDtypeStruct(state.shape, F32)]
        args += [q, f, v, g, nw, state]
    out = pl.pallas_call(
        _outproj_kernel,
        grid=(n_steps, 1),
        in_specs=in_specs,
        out_specs=out_specs,
        out_shape=out_shape,
        compiler_params=_params("parallel", "arbitrary", n_fusible_inputs=len(args)),
        name="out_proj",
    )(*args)
    return out if hgrn_step is not None else out[0]


def _mlp_kernel(x_ref, sh_ref, sc_ref, gt_ref, gpre_ref, gpost_ref, wu_ref, wd_ref, *rest):
    ssd = len(rest) > 2
    if ssd:
        ssd_in, (y_ref, so_ref, sh_out_ref, h_scr) = rest[:N_SSD_STEP_IN], rest[N_SSD_STEP_IN:]
    else:
        y_ref, h_scr = rest
    f = pl.program_id(1)

    @pl.when(f == 0)
    def _():
        h = _rms(x_ref[...]) * gpre_ref[...] * (1.0 + sc_ref[...]) + sh_ref[...]
        h_scr[...] = h.astype(BF16)

    def tile_product():
        if ssd:
            _ssd_step_rows(*ssd_in, so_ref, sh_out_ref)
        u = jnp.maximum(jnp.dot(h_scr[...], wu_ref[...].astype(BF16), preferred_element_type=F32), 0.0)
        return jnp.dot((u * u).astype(BF16), wd_ref[...].astype(BF16), preferred_element_type=F32)

    @pl.when(f == 0)
    def _():
        y_ref[...] = tile_product()

    @pl.when(f > 0)
    def _():
        y_ref[...] += tile_product()

    @pl.when(f == pl.num_programs(1) - 1)
    def _():
        y_ref[...] = x_ref[...] + gt_ref[...] * (_rms(y_ref[...]) * gpost_ref[...])


def _mlp(x, mod, per_row, rows_per_seq, g_pre, g_post, w_up, w_down, tm, tf, ssd_step=None):
    m = x.shape[0]
    n_f = D_FF // tf
    n_steps = (m // tm) * n_f
    row = pl.BlockSpec((tm, D_MODEL), lambda i, j: (i, 0))
    vec = pl.BlockSpec((1, D_MODEL), lambda i, j: (0, 0))
    in_specs = [pl.BlockSpec((tm, D_MODEL), lambda i, j: (i, 0), pipeline_mode=pl.Buffered(1)),
                _mod_spec(per_row, rows_per_seq, tm, 3),
                _mod_spec(per_row, rows_per_seq, tm, 4),
                _mod_spec(per_row, rows_per_seq, tm, 5),
                vec, vec,
                pl.BlockSpec((D_MODEL, tf), lambda i, j: (0, j)),
                pl.BlockSpec((tf, D_MODEL), lambda i, j: (j, 0))]
    out_specs = [row]
    out_shape = [jax.ShapeDtypeStruct((m, D_MODEL), F32)]
    args = [x, mod, mod, mod, g_pre, g_post, w_up, w_down]
    if ssd_step is not None:
        out_specs = [pl.BlockSpec((tm, D_MODEL), lambda i, j: (i, 0), pipeline_mode=pl.Buffered(1))]
        operands, state = ssd_step
        nb = state.shape[0]
        assert nb % n_steps == 0
        s_in, s_out, s_shape = _ssd_step_specs(operands, state, nb // n_steps, lambda i, j: i * n_f + j)
        in_specs += s_in
        out_specs += s_out
        out_shape += s_shape
        args += list(operands) + [state]
    out = pl.pallas_call(
        _mlp_kernel,
        grid=(m // tm, n_f),
        in_specs=in_specs,
        out_specs=out_specs,
        out_shape=out_shape,
        scratch_shapes=[pltpu.VMEM((tm, D_MODEL), BF16)],
        compiler_params=_params("parallel", "arbitrary", n_fusible_inputs=len(args)),
        name="mlp",
    )(*args)
    return out if ssd_step is not None else out[0]


def kernel(x_prompt, x_sample, c_prompt, c_sample, state_hgrn, state_ssm, state_conv, w_ada, b_ada, norm_pre_mix, norm_post_mix, norm_pre_mlp, norm_post_mlp, w_in, hgrn_lb_logits, hgrn_norm, conv_w, conv_b, dt_bias, a_log, d_skip, ssd_norm, w_out, w_up, w_down):
    n_seq, t, _ = x_prompt.shape
    n_dec = x_sample.shape[0]
    assert x_sample.shape[1] == 1 and t % CHUNK_A == 0 and t % CHUNK_B == 0
    assert w_ada.shape[0] == 1, "one layer"
    l = 0
    tiles = _tiles(t, n_dec)

    w_in_t = w_in[l].T
    g_pre_mix, g_post_mix = norm_pre_mix[l][None], norm_post_mix[l][None]
    g_pre_mlp, g_post_mlp = norm_pre_mlp[l][None], norm_post_mlp[l][None]
    cw, cb = conv_w[l], conv_b[l][None]
    hn = hgrn_norm[l][None]
    sn = ssd_norm[l][None]

    n_c = n_seq + n_dec
    pad = (-n_c) % SUBLANE
    c_all = jnp.concatenate([c_sample, c_prompt, jnp.zeros((pad, D_MODEL), F32)], axis=0)
    mod = _ada(c_all, w_ada[l], b_ada[l][None], tiles["ada_n"])
    mod_p = mod[n_dec:n_c].reshape(n_seq, 1, N_MOD * D_MODEL)
    mod_s = mod

    xp = x_prompt.reshape(n_seq * t, D_MODEL)
    xs = x_sample.reshape(n_dec, D_MODEL)
    tm, tn = tiles["inproj"]
    proj_p, dt_p, proj_s, dt_s = _inproj(xp, mod_p, t, xs, mod_s, g_pre_mix, w_in_t, hgrn_lb_logits, tm, tn,
                                         functools.partial(_hgrn_interleave, tn=tn))
    o_a, s_hgrn_p, w_out_b = _hgrn_prompt(proj_p, hn, w_out[l], n_seq, t)

    n_chunks = t // CHUNK_B
    rep = SUBLANE // HPG
    dt_rows = dt_p.reshape(n_seq, n_chunks, CHUNK_B, B_G, 1, HPG).transpose(3, 0, 1, 4, 5, 2)
    dt_rows = jnp.broadcast_to(dt_rows, (B_G, n_seq, n_chunks, rep, HPG, CHUNK_B))
    dt_rows = dt_rows.reshape(B_G, n_seq, n_chunks * SUBLANE, CHUNK_B)
    prow = lambda p: jnp.tile(p.reshape(B_G, 1, HPG), (1, n_chunks * rep, 1)).reshape(B_G, n_chunks * SUBLANE, 1)
    cpar = _ssd_channel_params(cw, cb, sn, d_skip[l], SSD_GROUPS_PER_STEP)
    dpar = jnp.concatenate([prow(dt_bias[l]), prow(a_log[l])], axis=2)
    o_b, s_ssm_p = _ssd_prompt(proj_p, dt_rows, cpar, dpar, n_seq, t)
    heads = lambda off: proj_s[:, off:off + D_A].reshape(n_dec, H_A, HA_D)
    x1, o_a_s, s_hgrn_s = _outproj(
        o_a, o_b, w_out_b, xp, mod_p, False, t, g_post_mix, tiles["outproj_p"],
        hgrn_step=(heads(OFF_Q), heads(OFF_F), heads(OFF_I), heads(OFF_G), hn.reshape(H_A, HA_D),
                   state_hgrn[l]))
    conv_p = proj_p.reshape(n_seq, t, D_MAIN)[:, t - (CONV_W - 1):, OFF_X:]

    to_t = lambda a, h, d: a.reshape(a.shape[:-1] + (h, d)).swapaxes(-1, -2)
    xbc_s = proj_s[:, OFF_X:]
    cst = state_conv[l]
    xt = to_t(xbc_s[:, :D_B], H_B, B_P)
    zt = to_t(proj_s[:, OFF_Z:OFF_Z + D_B], H_B, B_P)
    bcr = xbc_s[:, D_B:].reshape(n_dec, 2 * B_G, B_N)
    cxt = to_t(cst[:, :, :D_B], H_B, B_P)
    cbc = cst[:, :, D_B:].reshape(n_dec, CONV_W - 1, 2 * B_G, B_N)
    wxt = to_t(cw[:, :D_B], H_B, B_P)
    wbc = cw[:, D_B:].reshape(CONV_W, 2 * B_G, B_N)
    bxt = to_t(cb[:, :D_B], H_B, B_P)[0]
    bbc = cb[0, D_B:].reshape(2 * B_G, B_N)
    ssd_operands = (xt, bcr, zt, dt_s[:, None, :], cxt, cbc,
                    wxt, wbc, bxt, bbc, dt_bias[l][None], a_log[l][None], d_skip[l][None],
                    to_t(sn, H_B, B_P)[0])
    assert len(ssd_operands) + 1 == N_SSD_STEP_IN
    y_p, o_b_st, s_ssm_s = _mlp(x1, mod_p, False, t, g_pre_mlp, g_post_mlp, w_up[l], w_down[l],
                                *tiles["mlp_p"], ssd_step=(ssd_operands, state_ssm[l]))
    o_b_s = o_b_st.swapaxes(1, 2).reshape(n_dec, D_B)

    x1_s = _outproj(o_a_s.reshape(n_dec, D_A), o_b_s, w_out_b, xs, mod_s, True, 1,
                    g_post_mix, tiles["outproj_s"])
    y_s = _mlp(x1_s, mod_s, True, 1, g_pre_mlp, g_post_mlp, w_up[l], w_down[l], *tiles["mlp_s"])
    conv_s = jnp.concatenate([cst[:, 1:], xbc_s[:, None, :]], axis=1)

    return (y_p.reshape(n_seq, t, D_MODEL), y_s.reshape(n_dec, 1, D_MODEL),
            s_hgrn_p[None], s_ssm_p[None], conv_p[None],
            s_hgrn_s[None], s_ssm_s[None], conv_s[None])
```
